```python
import jax, jax.numpy as jnp
from jax import lax
import numpy as np

D_MODEL = 2048
BATCH = 8
SEQ = 2048
DEPTH = 2

MEM_LEN = 256
HEAD_DIM = 128
CHUNK = 128
GMLP_WIDTH = D_MODEL // 2
POOL_WIDTH = D_MODEL // 4
CONV_WIDTH = D_MODEL - GMLP_WIDTH - POOL_WIDTH
MIX_WIDTH = GMLP_WIDTH + POOL_WIDTH + CONV_WIDTH
GMLP_HEADS = GMLP_WIDTH // HEAD_DIM
POOL_WINDOWS = (2, 4, 8, 16)
POOL_GROUPS = len(POOL_WINDOWS)
POOL_GROUP_WIDTH = POOL_WIDTH // POOL_GROUPS
MAX_WINDOW = max(POOL_WINDOWS)
CONV_K = 31
IN_COLS = 2 * GMLP_WIDTH + POOL_WIDTH + 2 * CONV_WIDTH
XATTN_HEADS = 4
XATTN_HEAD_DIM = D_MODEL // XATTN_HEADS
D_FF = 4 * D_MODEL
RMS_EPS = 1e-6
LN_EPS = 1e-5

kernel_name = "hybrid_gmlp_pool_conformer_block"


def rms_norm(x, g):
    xf = x.astype(jnp.float32)
    y = xf * lax.rsqrt(jnp.mean(xf * xf, axis=-1, keepdims=True) + RMS_EPS)
    return (y * g.astype(jnp.float32)).astype(x.dtype)


def layer_norm(x, g, b=None):
    xf = x.astype(jnp.float32)
    mu = jnp.mean(xf, axis=-1, keepdims=True)
    var = jnp.mean(jnp.square(xf - mu), axis=-1, keepdims=True)
    y = (xf - mu) * lax.rsqrt(var + LN_EPS) * g.astype(jnp.float32)
    if b is not None:
        y = y + b.astype(jnp.float32)
    return y.astype(x.dtype)


def spatial_gating(u, v, g_v, w_s, b_s):
    B, S, _ = u.shape
    u = u.reshape(B, S, GMLP_HEADS, HEAD_DIM)
    v = layer_norm(v.reshape(B, S, GMLP_HEADS, HEAD_DIM), g_v)
    mask = jnp.tril(jnp.ones((CHUNK, CHUNK), w_s.dtype))
    vc = v.reshape(B, S // CHUNK, CHUNK, GMLP_HEADS, HEAD_DIM)
    mixed = jnp.einsum('hts,bcshd->bcthd', w_s * mask, vc) + b_s.T[None, None, :, :, None]
    return (u * mixed.reshape(B, S, GMLP_HEADS, HEAD_DIM)).reshape(B, S, GMLP_WIDTH)


def multiscale_pool(p, w_pool, s_pool):
    B, S, _ = p.shape
    pf = p.astype(jnp.float32).reshape(B, S, POOL_GROUPS, POOL_GROUP_WIDTH)
    cs = jnp.cumsum(pf, axis=1)
    cs = jnp.pad(cs, ((0, 0), (MAX_WINDOW, 0), (0, 0), (0, 0)))
    pos = jnp.arange(S, dtype=jnp.float32)
    means = []
    for g, w in enumerate(POOL_WINDOWS):
        win = cs[:, MAX_WINDOW:, g] - cs[:, MAX_WINDOW - w:MAX_WINDOW - w + S, g]
        cnt = jnp.minimum(pos + 1.0, float(w))[None, :, None]
        means.append(win / cnt)
    pooled = jnp.stack(means, axis=2) - pf
    out = jnp.einsum('bsgc,gcd->bsgd', pooled, w_pool.astype(jnp.float32))
    out = out * s_pool.astype(jnp.float32).reshape(POOL_GROUPS, POOL_GROUP_WIDTH)
    return out.reshape(B, S, POOL_WIDTH).astype(p.dtype)


def conformer_conv(c_val, c_gate, w_dw, b_dw, ln_g, ln_b):
    h = c_val * jax.nn.sigmoid(c_gate)
    h = lax.conv_general_dilated(
        h, w_dw[:, None, :], window_strides=(1,), padding=[(CONV_K - 1, 0)],
        dimension_numbers=('NWC', 'WIO', 'NWC'), feature_group_count=CONV_WIDTH) + b_dw
    return jax.nn.silu(layer_norm(h, ln_g, ln_b))


def hybrid_mixer(h, w_in, w_out, g_v, w_s, b_s, w_pool, s_pool, w_dw, b_dw, ln_g, ln_b):
    z = h @ w_in
    cuts = (GMLP_WIDTH, 2 * GMLP_WIDTH, 2 * GMLP_WIDTH + POOL_WIDTH,
            2 * GMLP_WIDTH + POOL_WIDTH + CONV_WIDTH)
    z_a, p_b, c_val, c_gate = (z[..., :cuts[1]], z[..., cuts[1]:cuts[2]],
                               z[..., cuts[2]:cuts[3]], z[..., cuts[3]:])
    z_a = jax.nn.gelu(z_a)
    y_a = spatial_gating(z_a[..., :GMLP_WIDTH], z_a[..., GMLP_WIDTH:], g_v, w_s, b_s)
    y_b = multiscale_pool(p_b, w_pool, s_pool)
    y_c = conformer_conv(c_val, c_gate, w_dw, b_dw, ln_g, ln_b)
    return jnp.concatenate([y_a, y_b, y_c], axis=-1) @ w_out


def cross_attention(h, m, w_q, w_k, w_v, w_o):
    B, S, _ = h.shape
    q = (h @ w_q).reshape(B, S, XATTN_HEADS, XATTN_HEAD_DIM)
    k = (m @ w_k).reshape(B, MEM_LEN, XATTN_HEADS, XATTN_HEAD_DIM)
    v = (m @ w_v).reshape(B, MEM_LEN, XATTN_HEADS, XATTN_HEAD_DIM)
    scores = jnp.einsum('bshd,bmhd->bhsm', q.astype(jnp.float32), k.astype(jnp.float32))
    probs = jax.nn.softmax(scores * (XATTN_HEAD_DIM ** -0.5), axis=-1).astype(v.dtype)
    out = jnp.einsum('bhsm,bmhd->bshd', probs, v).reshape(B, S, D_MODEL)
    return out @ w_o


def _fwd_setup_inputs(seed: int = 0) -> dict:
    key = jax.random.key(seed)
    ks = iter(jax.random.split(key, 32))
    f32 = jnp.float32

    def nrm(shape, scale):
        return jax.random.normal(next(ks), shape, f32) * scale

    def gain(shape):
        return 1.0 + 0.05 * jax.random.normal(next(ks), shape, f32)

    L = DEPTH
    return {
        "x": jax.random.normal(next(ks), (BATCH, SEQ, D_MODEL), f32),
        "mem": jax.random.normal(next(ks), (BATCH, MEM_LEN, D_MODEL), f32),
        "norm_mix_pre": gain((L, D_MODEL)),
        "norm_mix_post": gain((L, D_MODEL)),
        "w_in": nrm((L, D_MODEL, IN_COLS), D_MODEL ** -0.5),
        "w_out": nrm((L, MIX_WIDTH, D_MODEL), MIX_WIDTH ** -0.5),
        "gmlp_v_gain": gain((L, GMLP_HEADS, HEAD_DIM)),
        "w_spatial": nrm((L, GMLP_HEADS, CHUNK, CHUNK), CHUNK ** -0.5),
        "b_spatial": gain((L, GMLP_HEADS, CHUNK)),
        "w_pool": nrm((L, POOL_GROUPS, POOL_GROUP_WIDTH, POOL_GROUP_WIDTH), POOL_GROUP_WIDTH ** -0.5),
        "s_pool": gain((L, POOL_WIDTH)),
        "w_dw": nrm((L, CONV_K, CONV_WIDTH), CONV_K ** -0.5),
        "b_dw": nrm((L, CONV_WIDTH), 0.02),
        "conv_ln_g": gain((L, CONV_WIDTH)),
        "conv_ln_b": nrm((L, CONV_WIDTH), 0.02),
        "norm_xattn_pre": gain((L, D_MODEL)),
        "norm_mem": gain((L, D_MODEL)),
        "norm_xattn_post": gain((L, D_MODEL)),
        "w_q": nrm((L, D_MODEL, D_MODEL), D_MODEL ** -0.5),
        "w_k": nrm((L, D_MODEL, D_MODEL), D_MODEL ** -0.5),
        "w_v": nrm((L, D_MODEL, D_MODEL), D_MODEL ** -0.5),
        "w_o": nrm((L, D_MODEL, D_MODEL), D_MODEL ** -0.5),
        "norm_ffn_pre": gain((L, D_MODEL)),
        "norm_ffn_post": gain((L, D_MODEL)),
        "w_up": nrm((L, D_MODEL, D_FF), D_MODEL ** -0.5),
        "w_down": nrm((L, D_FF, D_MODEL), D_FF ** -0.5),
    }


def _fwd_reference(x, mem, norm_mix_pre, norm_mix_post, w_in, w_out, gmlp_v_gain, w_spatial,
              b_spatial, w_pool, s_pool, w_dw, b_dw, conv_ln_g, conv_ln_b, norm_xattn_pre,
              norm_mem, norm_xattn_post, w_q, w_k, w_v, w_o, norm_ffn_pre, norm_ffn_post,
              w_up, w_down):
    for l in range(DEPTH):
        h = rms_norm(x, norm_mix_pre[l])
        h = hybrid_mixer(h, w_in[l], w_out[l], gmlp_v_gain[l], w_spatial[l], b_spatial[l],
                         w_pool[l], s_pool[l], w_dw[l], b_dw[l], conv_ln_g[l], conv_ln_b[l])
        x = x + rms_norm(h, norm_mix_post[l])
        h = rms_norm(x, norm_xattn_pre[l])
        m = rms_norm(mem, norm_mem[l])
        h = cross_attention(h, m, w_q[l], w_k[l], w_v[l], w_o[l])
        x = x + rms_norm(h, norm_xattn_post[l])
        h = rms_norm(x, norm_ffn_pre[l])
        h = jnp.square(jax.nn.relu(h @ w_up[l])) @ w_down[l]
        x = x + rms_norm(h, norm_ffn_post[l])
    return x


import jax as _jax
import jax.numpy as _jnp

TWIN_FORMAT = 'train_step'
FWD_PARAMS = ['x', 'mem', 'norm_mix_pre', 'norm_mix_post', 'w_in', 'w_out', 'gmlp_v_gain', 'w_spatial', 'b_spatial', 'w_pool', 's_pool', 'w_dw', 'b_dw', 'conv_ln_g', 'conv_ln_b', 'norm_xattn_pre', 'norm_mem', 'norm_xattn_post', 'w_q', 'w_k', 'w_v', 'w_o', 'norm_ffn_pre', 'norm_ffn_post', 'w_up', 'w_down']
TWIN_WEIGHTS = ['norm_mix_pre', 'norm_mix_post', 'w_in', 'w_out', 'gmlp_v_gain', 'w_spatial', 'b_spatial', 'w_pool', 's_pool', 'w_dw', 'b_dw', 'conv_ln_g', 'conv_ln_b', 'norm_xattn_pre', 'norm_mem', 'norm_xattn_post', 'w_q', 'w_k', 'w_v', 'w_o', 'norm_ffn_pre', 'norm_ffn_post', 'w_up', 'w_down']
TWIN_DIFF_INPUT = 'x'
TWIN_INPUTS = ['x', 'mem', 'norm_mix_pre', 'norm_mix_post', 'w_in', 'w_out', 'gmlp_v_gain', 'w_spatial', 'b_spatial', 'w_pool', 's_pool', 'w_dw', 'b_dw', 'conv_ln_g', 'conv_ln_b', 'norm_xattn_pre', 'norm_mem', 'norm_xattn_post', 'w_q', 'w_k', 'w_v', 'w_o', 'norm_ffn_pre', 'norm_ffn_post', 'w_up', 'w_down', 'loss_target', 'm_norm_mix_pre', 'm_norm_mix_post', 'm_w_in', 'm_w_out', 'm_gmlp_v_gain', 'm_w_spatial', 'm_b_spatial', 'm_w_pool', 'm_s_pool', 'm_w_dw', 'm_b_dw', 'm_conv_ln_g', 'm_conv_ln_b', 'm_norm_xattn_pre', 'm_norm_mem', 'm_norm_xattn_post', 'm_w_q', 'm_w_k', 'm_w_v', 'm_w_o', 'm_norm_ffn_pre', 'm_norm_ffn_post', 'm_w_up', 'm_w_down', 'v_norm_mix_pre', 'v_norm_mix_post', 'v_w_in', 'v_w_out', 'v_gmlp_v_gain', 'v_w_spatial', 'v_b_spatial', 'v_w_pool', 'v_s_pool', 'v_w_dw', 'v_b_dw', 'v_conv_ln_g', 'v_conv_ln_b', 'v_norm_xattn_pre', 'v_norm_mem', 'v_norm_xattn_post', 'v_w_q', 'v_w_k', 'v_w_v', 'v_w_o', 'v_norm_ffn_pre', 'v_norm_ffn_post', 'v_w_up', 'v_w_down']
TWIN_OUTPUTS = ['loss', 'grad_x', 'grad_norm_mix_pre', 'grad_norm_mix_post', 'grad_w_in', 'grad_w_out', 'grad_gmlp_v_gain', 'grad_w_spatial', 'grad_b_spatial', 'grad_w_pool', 'grad_s_pool', 'grad_w_dw', 'grad_b_dw', 'grad_conv_ln_g', 'grad_conv_ln_b', 'grad_norm_xattn_pre', 'grad_norm_mem', 'grad_norm_xattn_post', 'grad_w_q', 'grad_w_k', 'grad_w_v', 'grad_w_o', 'grad_norm_ffn_pre', 'grad_norm_ffn_post', 'grad_w_up', 'grad_w_down', 'delta_norm_mix_pre', 'delta_norm_mix_post', 'delta_w_in', 'delta_w_out', 'delta_gmlp_v_gain', 'delta_w_spatial', 'delta_b_spatial', 'delta_w_pool', 'delta_s_pool', 'delta_w_dw', 'delta_b_dw', 'delta_conv_ln_g', 'delta_conv_ln_b', 'delta_norm_xattn_pre', 'delta_norm_mem', 'delta_norm_xattn_post', 'delta_w_q', 'delta_w_k', 'delta_w_v', 'delta_w_o', 'delta_norm_ffn_pre', 'delta_norm_ffn_post', 'delta_w_up', 'delta_w_down', 'new_m_norm_mix_pre', 'new_m_norm_mix_post', 'new_m_w_in', 'new_m_w_out', 'new_m_gmlp_v_gain', 'new_m_w_spatial', 'new_m_b_spatial', 'new_m_w_pool', 'new_m_s_pool', 'new_m_w_dw', 'new_m_b_dw', 'new_m_conv_ln_g', 'new_m_conv_ln_b', 'new_m_norm_xattn_pre', 'new_m_norm_mem', 'new_m_norm_xattn_post', 'new_m_w_q', 'new_m_w_k', 'new_m_w_v', 'new_m_w_o', 'new_m_norm_ffn_pre', 'new_m_norm_ffn_post', 'new_m_w_up', 'new_m_w_down', 'new_v_norm_mix_pre', 'new_v_norm_mix_post', 'new_v_w_in', 'new_v_w_out', 'new_v_gmlp_v_gain', 'new_v_w_spatial', 'new_v_b_spatial', 'new_v_w_pool', 'new_v_s_pool', 'new_v_w_dw', 'new_v_b_dw', 'new_v_conv_ln_g', 'new_v_conv_ln_b', 'new_v_norm_xattn_pre', 'new_v_norm_mem', 'new_v_norm_xattn_post', 'new_v_w_q', 'new_v_w_k', 'new_v_w_v', 'new_v_w_o', 'new_v_norm_ffn_pre', 'new_v_norm_ffn_post', 'new_v_w_up', 'new_v_w_down']
TWIN_LEAF_KINDS = {'loss': 'loss', 'grad_x': 'grad_x', 'grad_norm_mix_pre': 'grad_w', 'grad_norm_mix_post': 'grad_w', 'grad_w_in': 'grad_w', 'grad_w_out': 'grad_w', 'grad_gmlp_v_gain': 'grad_w', 'grad_w_spatial': 'grad_w', 'grad_b_spatial': 'grad_w', 'grad_w_pool': 'grad_w', 'grad_s_pool': 'grad_w', 'grad_w_dw': 'grad_w', 'grad_b_dw': 'grad_w', 'grad_conv_ln_g': 'grad_w', 'grad_conv_ln_b': 'grad_w', 'grad_norm_xattn_pre': 'grad_w', 'grad_norm_mem': 'grad_w', 'grad_norm_xattn_post': 'grad_w', 'grad_w_q': 'grad_w', 'grad_w_k': 'grad_w', 'grad_w_v': 'grad_w', 'grad_w_o': 'grad_w', 'grad_norm_ffn_pre': 'grad_w', 'grad_norm_ffn_post': 'grad_w', 'grad_w_up': 'grad_w', 'grad_w_down': 'grad_w', 'delta_norm_mix_pre': 'delta_w', 'delta_norm_mix_post': 'delta_w', 'delta_w_in': 'delta_w', 'delta_w_out': 'delta_w', 'delta_gmlp_v_gain': 'delta_w', 'delta_w_spatial': 'delta_w', 'delta_b_spatial': 'delta_w', 'delta_w_pool': 'delta_w', 'delta_s_pool': 'delta_w', 'delta_w_dw': 'delta_w', 'delta_b_dw': 'delta_w', 'delta_conv_ln_g': 'delta_w', 'delta_conv_ln_b': 'delta_w', 'delta_norm_xattn_pre': 'delta_w', 'delta_norm_mem': 'delta_w', 'delta_norm_xattn_post': 'delta_w', 'delta_w_q': 'delta_w', 'delta_w_k': 'delta_w', 'delta_w_v': 'delta_w', 'delta_w_o': 'delta_w', 'delta_norm_ffn_pre': 'delta_w', 'delta_norm_ffn_post': 'delta_w', 'delta_w_up': 'delta_w', 'delta_w_down': 'delta_w', 'new_m_norm_mix_pre': 'new_m', 'new_m_norm_mix_post': 'new_m', 'new_m_w_in': 'new_m', 'new_m_w_out': 'new_m', 'new_m_gmlp_v_gain': 'new_m', 'new_m_w_spatial': 'new_m', 'new_m_b_spatial': 'new_m', 'new_m_w_pool': 'new_m', 'new_m_s_pool': 'new_m', 'new_m_w_dw': 'new_m', 'new_m_b_dw': 'new_m', 'new_m_conv_ln_g': 'new_m', 'new_m_conv_ln_b': 'new_m', 'new_m_norm_xattn_pre': 'new_m', 'new_m_norm_mem': 'new_m', 'new_m_norm_xattn_post': 'new_m', 'new_m_w_q': 'new_m', 'new_m_w_k': 'new_m', 'new_m_w_v': 'new_m', 'new_m_w_o': 'new_m', 'new_m_norm_ffn_pre': 'new_m', 'new_m_norm_ffn_post': 'new_m', 'new_m_w_up': 'new_m', 'new_m_w_down': 'new_m', 'new_v_norm_mix_pre': 'new_v', 'new_v_norm_mix_post': 'new_v', 'new_v_w_in': 'new_v', 'new_v_w_out': 'new_v', 'new_v_gmlp_v_gain': 'new_v', 'new_v_w_spatial': 'new_v', 'new_v_b_spatial': 'new_v', 'new_v_w_pool': 'new_v', 'new_v_s_pool': 'new_v', 'new_v_w_dw': 'new_v', 'new_v_b_dw': 'new_v', 'new_v_conv_ln_g': 'new_v', 'new_v_conv_ln_b': 'new_v', 'new_v_norm_xattn_pre': 'new_v', 'new_v_norm_mem': 'new_v', 'new_v_norm_xattn_post': 'new_v', 'new_v_w_q': 'new_v', 'new_v_w_k': 'new_v', 'new_v_w_v': 'new_v', 'new_v_w_o': 'new_v', 'new_v_norm_ffn_pre': 'new_v', 'new_v_norm_ffn_post': 'new_v', 'new_v_w_up': 'new_v', 'new_v_w_down': 'new_v'}


def _forward(args):
    return _fwd_reference(*[args[k] for k in FWD_PARAMS])


def _output_shape():
    out = _jax.eval_shape(lambda: _forward(_fwd_setup_inputs(0)))
    return out.shape, out.dtype

N_MICROBATCH = 1
ADAM_LR = 0.001
ADAM_B1 = 0.9
ADAM_B2 = 0.999
ADAM_EPS = 1e-08
ADAM_WD = 0.01
ADAM_STEP = 10
PER_EXAMPLE_BATCH_AXIS = {'x': 0, 'mem': 0, 'loss_target': 0}
SHARED_INPUTS = []
_WEIGHT_DTYPES = {'norm_mix_pre': _jnp.float32, 'norm_mix_post': _jnp.float32, 'w_in': _jnp.float32, 'w_out': _jnp.float32, 'gmlp_v_gain': _jnp.float32, 'w_spatial': _jnp.float32, 'b_spatial': _jnp.float32, 'w_pool': _jnp.float32, 's_pool': _jnp.float32, 'w_dw': _jnp.float32, 'b_dw': _jnp.float32, 'conv_ln_g': _jnp.float32, 'conv_ln_b': _jnp.float32, 'norm_xattn_pre': _jnp.float32, 'norm_mem': _jnp.float32, 'norm_xattn_post': _jnp.float32, 'w_q': _jnp.float32, 'w_k': _jnp.float32, 'w_v': _jnp.float32, 'w_o': _jnp.float32, 'norm_ffn_pre': _jnp.float32, 'norm_ffn_post': _jnp.float32, 'w_up': _jnp.float32, 'w_down': _jnp.float32}
MOMENT_SCALE = {'norm_mix_pre': 1.062830e+00, 'norm_mix_post': 8.516674e+00, 'w_in': 7.652971e-01, 'w_out': 2.498980e+00, 'gmlp_v_gain': 1.783112e-01, 'w_spatial': 1.631704e-01, 'b_spatial': 3.060885e-01, 'w_pool': 4.439344e-01, 's_pool': 5.651698e-01, 'w_dw': 1.324117e+00, 'b_dw': 9.018982e+00, 'conv_ln_g': 3.812552e+00, 'conv_ln_b': 5.239860e+00, 'norm_xattn_pre': 9.577755e-01, 'norm_mem': 3.354517e+00, 'norm_xattn_post': 8.962357e+00, 'w_q': 9.923443e-01, 'w_k': 1.015649e+00, 'w_v': 3.150406e+00, 'w_o': 3.231226e+00, 'norm_ffn_pre': 1.719175e+00, 'norm_ffn_post': 8.881772e+00, 'w_up': 8.435689e-01, 'w_down': 3.221354e+00}


def _to_microbatches(a, axis):
    t = _jnp.moveaxis(a, axis, 0)
    t = t.reshape((N_MICROBATCH, t.shape[0] // N_MICROBATCH) + t.shape[1:])
    return _jnp.moveaxis(t, 1, axis + 1)


def setup_inputs(seed: int = 0) -> dict:
    inp = _fwd_setup_inputs(seed)
    key = _jax.random.fold_in(_jax.random.key(seed), 7919)
    shape, _ = _output_shape()
    out = dict(inp)
    out["loss_target"] = _jax.random.normal(_jax.random.fold_in(key, 0), shape, _jnp.float32)
    for i, name in enumerate(TWIN_WEIGHTS):
        w = inp[name].astype(_jnp.float32)
        if MOMENT_SCALE is None:
            s = _jnp.sqrt(_jnp.mean(_jnp.square(w)) + 1e-30)
        else:
            s = MOMENT_SCALE[name]
        km, kv = _jax.random.split(_jax.random.fold_in(key, i + 1))
        out[name] = w
        out["m_" + name] = s * _jax.random.normal(km, w.shape, _jnp.float32)
        out["v_" + name] = (s * s) * _jax.random.uniform(kv, w.shape, _jnp.float32, 0.5, 1.5)
    if N_MICROBATCH > 1:
        for name, axis in PER_EXAMPLE_BATCH_AXIS.items():
            out[name] = _to_microbatches(out[name], axis)
    return {'x': out['x'], 'mem': out['mem'], 'norm_mix_pre': out['norm_mix_pre'], 'norm_mix_post': out['norm_mix_post'], 'w_in': out['w_in'], 'w_out': out['w_out'], 'gmlp_v_gain': out['gmlp_v_gain'], 'w_spatial': out['w_spatial'], 'b_spatial': out['b_spatial'], 'w_pool': out['w_pool'], 's_pool': out['s_pool'], 'w_dw': out['w_dw'], 'b_dw': out['b_dw'], 'conv_ln_g': out['conv_ln_g'], 'conv_ln_b': out['conv_ln_b'], 'norm_xattn_pre': out['norm_xattn_pre'], 'norm_mem': out['norm_mem'], 'norm_xattn_post': out['norm_xattn_post'], 'w_q': out['w_q'], 'w_k': out['w_k'], 'w_v': out['w_v'], 'w_o': out['w_o'], 'norm_ffn_pre': out['norm_ffn_pre'], 'norm_ffn_post': out['norm_ffn_post'], 'w_up': out['w_up'], 'w_down': out['w_down'], 'loss_target': out['loss_target'], 'm_norm_mix_pre': out['m_norm_mix_pre'], 'm_norm_mix_post': out['m_norm_mix_post'], 'm_w_in': out['m_w_in'], 'm_w_out': out['m_w_out'], 'm_gmlp_v_gain': out['m_gmlp_v_gain'], 'm_w_spatial': out['m_w_spatial'], 'm_b_spatial': out['m_b_spatial'], 'm_w_pool': out['m_w_pool'], 'm_s_pool': out['m_s_pool'], 'm_w_dw': out['m_w_dw'], 'm_b_dw': out['m_b_dw'], 'm_conv_ln_g': out['m_conv_ln_g'], 'm_conv_ln_b': out['m_conv_ln_b'], 'm_norm_xattn_pre': out['m_norm_xattn_pre'], 'm_norm_mem': out['m_norm_mem'], 'm_norm_xattn_post': out['m_norm_xattn_post'], 'm_w_q': out['m_w_q'], 'm_w_k': out['m_w_k'], 'm_w_v': out['m_w_v'], 'm_w_o': out['m_w_o'], 'm_norm_ffn_pre': out['m_norm_ffn_pre'], 'm_norm_ffn_post': out['m_norm_ffn_post'], 'm_w_up': out['m_w_up'], 'm_w_down': out['m_w_down'], 'v_norm_mix_pre': out['v_norm_mix_pre'], 'v_norm_mix_post': out['v_norm_mix_post'], 'v_w_in': out['v_w_in'], 'v_w_out': out['v_w_out'], 'v_gmlp_v_gain': out['v_gmlp_v_gain'], 'v_w_spatial': out['v_w_spatial'], 'v_b_spatial': out['v_b_spatial'], 'v_w_pool': out['v_w_pool'], 'v_s_pool': out['v_s_pool'], 'v_w_dw': out['v_w_dw'], 'v_b_dw': out['v_b_dw'], 'v_conv_ln_g': out['v_conv_ln_g'], 'v_conv_ln_b': out['v_conv_ln_b'], 'v_norm_xattn_pre': out['v_norm_xattn_pre'], 'v_norm_mem': out['v_norm_mem'], 'v_norm_xattn_post': out['v_norm_xattn_post'], 'v_w_q': out['v_w_q'], 'v_w_k': out['v_w_k'], 'v_w_v': out['v_w_v'], 'v_w_o': out['v_w_o'], 'v_norm_ffn_pre': out['v_norm_ffn_pre'], 'v_norm_ffn_post': out['v_norm_ffn_post'], 'v_w_up': out['v_w_up'], 'v_w_down': out['v_w_down']}


def _loss(weights, diff, rest, loss_target):
    with _jax.named_scope("forward"):
        args = {**rest, TWIN_DIFF_INPUT: diff, **{k: w.astype(_WEIGHT_DTYPES[k]) for k, w in weights.items()}}
        y = _forward(args)
    with _jax.named_scope("loss_head"):
        err = _jnp.square(y.astype(_jnp.float32) - loss_target)
        return 0.5 * _jnp.sum(_jnp.mean(err, axis=-1)) if err.ndim else 0.5 * err


def _adamw(w, g, m, v):
    m = ADAM_B1 * m + (1.0 - ADAM_B1) * g
    v = ADAM_B2 * v + (1.0 - ADAM_B2) * _jnp.square(g)
    m_hat = m / (1.0 - ADAM_B1 ** ADAM_STEP)
    v_hat = v / (1.0 - ADAM_B2 ** ADAM_STEP)
    delta = -ADAM_LR * (m_hat / (_jnp.sqrt(v_hat) + ADAM_EPS) + ADAM_WD * w)
    return delta, m, v


def reference(x, mem, norm_mix_pre, norm_mix_post, w_in, w_out, gmlp_v_gain, w_spatial, b_spatial, w_pool, s_pool, w_dw, b_dw, conv_ln_g, conv_ln_b, norm_xattn_pre, norm_mem, norm_xattn_post, w_q, w_k, w_v, w_o, norm_ffn_pre, norm_ffn_post, w_up, w_down, loss_target, m_norm_mix_pre, m_norm_mix_post, m_w_in, m_w_out, m_gmlp_v_gain, m_w_spatial, m_b_spatial, m_w_pool, m_s_pool, m_w_dw, m_b_dw, m_conv_ln_g, m_conv_ln_b, m_norm_xattn_pre, m_norm_mem, m_norm_xattn_post, m_w_q, m_w_k, m_w_v, m_w_o, m_norm_ffn_pre, m_norm_ffn_post, m_w_up, m_w_down, v_norm_mix_pre, v_norm_mix_post, v_w_in, v_w_out, v_gmlp_v_gain, v_w_spatial, v_b_spatial, v_w_pool, v_s_pool, v_w_dw, v_b_dw, v_conv_ln_g, v_conv_ln_b, v_norm_xattn_pre, v_norm_mem, v_norm_xattn_post, v_w_q, v_w_k, v_w_v, v_w_o, v_norm_ffn_pre, v_norm_ffn_post, v_w_up, v_w_down):
    given = dict(x=x, mem=mem, norm_mix_pre=norm_mix_pre, norm_mix_post=norm_mix_post, w_in=w_in, w_out=w_out, gmlp_v_gain=gmlp_v_gain, w_spatial=w_spatial, b_spatial=b_spatial, w_pool=w_pool, s_pool=s_pool, w_dw=w_dw, b_dw=b_dw, conv_ln_g=conv_ln_g, conv_ln_b=conv_ln_b, norm_xattn_pre=norm_xattn_pre, norm_mem=norm_mem, norm_xattn_post=norm_xattn_post, w_q=w_q, w_k=w_k, w_v=w_v, w_o=w_o, norm_ffn_pre=norm_ffn_pre, norm_ffn_post=norm_ffn_post, w_up=w_up, w_down=w_down, loss_target=loss_target, m_norm_mix_pre=m_norm_mix_pre, m_norm_mix_post=m_norm_mix_post, m_w_in=m_w_in, m_w_out=m_w_out, m_gmlp_v_gain=m_gmlp_v_gain, m_w_spatial=m_w_spatial, m_b_spatial=m_b_spatial, m_w_pool=m_w_pool, m_s_pool=m_s_pool, m_w_dw=m_w_dw, m_b_dw=m_b_dw, m_conv_ln_g=m_conv_ln_g, m_conv_ln_b=m_conv_ln_b, m_norm_xattn_pre=m_norm_xattn_pre, m_norm_mem=m_norm_mem, m_norm_xattn_post=m_norm_xattn_post, m_w_q=m_w_q, m_w_k=m_w_k, m_w_v=m_w_v, m_w_o=m_w_o, m_norm_ffn_pre=m_norm_ffn_pre, m_norm_ffn_post=m_norm_ffn_post, m_w_up=m_w_up, m_w_down=m_w_down, v_norm_mix_pre=v_norm_mix_pre, v_norm_mix_post=v_norm_mix_post, v_w_in=v_w_in, v_w_out=v_w_out, v_gmlp_v_gain=v_gmlp_v_gain, v_w_spatial=v_w_spatial, v_b_spatial=v_b_spatial, v_w_pool=v_w_pool, v_s_pool=v_s_pool, v_w_dw=v_w_dw, v_b_dw=v_b_dw, v_conv_ln_g=v_conv_ln_g, v_conv_ln_b=v_conv_ln_b, v_norm_xattn_pre=v_norm_xattn_pre, v_norm_mem=v_norm_mem, v_norm_xattn_post=v_norm_xattn_post, v_w_q=v_w_q, v_w_k=v_w_k, v_w_v=v_w_v, v_w_o=v_w_o, v_norm_ffn_pre=v_norm_ffn_pre, v_norm_ffn_post=v_norm_ffn_post, v_w_up=v_w_up, v_w_down=v_w_down)
    weights = {n: given[n] for n in TWIN_WEIGHTS}
    shared = {n: given[n] for n in SHARED_INPUTS}
    per_example = {n: given[n] for n in ['x', 'mem']}
    grad_fn = _jax.value_and_grad(_loss, argnums=(0, 1))

    def one_microbatch(ex, loss_target):
        ex = dict(ex)
        diff = ex.pop(TWIN_DIFF_INPUT)
        return grad_fn(weights, diff, {**shared, **ex}, loss_target)

    if N_MICROBATCH == 1:
        loss, (grad_w, grad_x) = one_microbatch(per_example, given["loss_target"])
    else:
        def body(carry, xs):
            loss_sum, grad_sum = carry
            l_k, (gw_k, gx_k) = one_microbatch(xs[0], xs[1])
            with _jax.named_scope("update"):
                return (loss_sum + l_k, _jax.tree.map(_jnp.add, grad_sum, gw_k)), gx_k

        init = (_jnp.zeros((), _jnp.float32), _jax.tree.map(_jnp.zeros_like, weights))
        (loss, grad_w), grad_x = _jax.lax.scan(body, init, (per_example, given["loss_target"]))
    with _jax.named_scope("update"):
        delta_w, new_m, new_v = {}, {}, {}
        for n in TWIN_WEIGHTS:
            delta_w[n], new_m[n], new_v[n] = _adamw(weights[n], grad_w[n], given["m_" + n], given["v_" + n])
    return (loss, grad_x, *[grad_w[n] for n in TWIN_WEIGHTS], *[delta_w[n] for n in TWIN_WEIGHTS],
            *[new_m[n] for n in TWIN_WEIGHTS], *[new_v[n] for n in TWIN_WEIGHTS])
```

```python
import functools

import jax
import jax.numpy as jnp
from jax import lax
from jax.experimental import pallas as pl
from jax.experimental.pallas import tpu as pltpu

F32 = jnp.float32
BF16 = jnp.bfloat16
MESH = pl.DeviceIdType.MESH

N_CHIPS = 4
N_DEVICES = 8
XATTN_HEADS = 4
POOL_WINDOWS = (2, 4, 8, 16)
RMS_EPS = 1e-6
LN_EPS = 1e-5
ADAM_LR, ADAM_B1, ADAM_B2, ADAM_EPS, ADAM_WD, ADAM_STEP = 0.001, 0.9, 0.999, 1e-08, 0.01, 10

V7X_LANES = 128
V7X_VMEM_LIMIT = 56 * 1024 * 1024
ROW_TILE = 256
MM_TILE_M, MM_TILE_N, MM_TILE_K = 1024, 1024, 1024

ANY = pl.BlockSpec(memory_space=pl.ANY)


def _tile(dim, pref):
    if dim <= pref:
        return dim
    t = (pref // V7X_LANES) * V7X_LANES
    while t >= V7X_LANES:
        if dim % t == 0:
            return t
        t -= V7X_LANES
    return dim


def _params(sem=None):
    return pltpu.CompilerParams(dimension_semantics=sem, vmem_limit_bytes=V7X_VMEM_LIMIT)


NN = (((1,), (0,)), ((), ()))
NT = (((1,), (1,)), ((), ()))
TN = (((0,), (0,)), ((), ()))


def _mm(name, a, b, *, dn, grid, a_spec, b_spec, o_specs, out_shapes, acc_shape, epi=None, extra=None,
        extra_spec=None, prev=None):
    nk = grid[2]
    n_out = len(out_shapes)
    has_extra = extra is not None

    def body(*refs):
        a_ref, b_ref = refs[0], refs[1]
        pos = 2
        e_ref = None
        if has_extra:
            e_ref = refs[pos]
            pos += 1
        if prev is not None:
            pos += 1
        o_refs = refs[pos:pos + n_out]
        acc = refs[pos + n_out]
        k = pl.program_id(2)

        @pl.when(k == 0)
        def _():
            acc[...] = jnp.zeros_like(acc)

        acc[...] += lax.dot_general(a_ref[...].astype(BF16), b_ref[...].astype(BF16), dn,
                                    preferred_element_type=F32)

        @pl.when(k == nk - 1)
        def _():
            if epi is None:
                vals = (acc[...],)
            elif has_extra:
                vals = epi(acc[...], e_ref[...])
            else:
                vals = epi(acc[...])
            for o, v in zip(o_refs, vals):
                o[...] = v.astype(o.dtype)

    ins, in_specs = [a, b], [a_spec, b_spec]
    if has_extra:
        ins.append(extra)
        in_specs.append(extra_spec)
    aliases = {}
    if prev is not None:
        aliases = {len(ins): 0}
        ins.append(prev)
        in_specs.append(ANY)
    outs = pl.pallas_call(
        body, name=name, grid=grid, in_specs=in_specs, out_specs=list(o_specs), out_shape=list(out_shapes),
        scratch_shapes=[pltpu.VMEM(acc_shape, F32)], input_output_aliases=aliases,
        compiler_params=_params(("parallel", "parallel", "arbitrary")))(*ins)
    return outs


def _mm_nn_row(name, a, w, l, out_dtypes, epi=None):
    m, k = a.shape
    n = w.shape[2]
    tm, tn, tk = _tile(m, MM_TILE_M), _tile(n, MM_TILE_N), _tile(k, MM_TILE_K)
    o_spec = pl.BlockSpec((tm, tn), lambda i, j, kk: (i, j))
    return _mm(name, a, w, dn=NN, grid=(m // tm, n // tn, k // tk),
               a_spec=pl.BlockSpec((tm, tk), lambda i, j, kk: (i, kk)),
               b_spec=pl.BlockSpec((None, tk, tn), lambda i, j, kk: (l, kk, j)),
               o_specs=[o_spec] * len(out_dtypes),
               out_shapes=[jax.ShapeDtypeStruct((m, n), d) for d in out_dtypes], acc_shape=(tm, tn), epi=epi)


def _mm_nn_col(name, a, w, l, out_dtypes, epi=None):
    m, k = a.shape
    c = w.shape[3]
    tm, tn, tk = _tile(m, MM_TILE_M), _tile(c, MM_TILE_N), _tile(k, MM_TILE_K)
    nb = c // tn
    o_spec = pl.BlockSpec((tm, tn), lambda i, j, kk: (i, j))
    return _mm(name, a, w, dn=NN, grid=(m // tm, N_CHIPS * nb, k // tk),
               a_spec=pl.BlockSpec((tm, tk), lambda i, j, kk: (i, kk)),
               b_spec=pl.BlockSpec((None, None, tk, tn), lambda i, j, kk: (l, j // nb, kk, j % nb)),
               o_specs=[o_spec] * len(out_dtypes),
               out_shapes=[jax.ShapeDtypeStruct((m, N_CHIPS * c), d) for d in out_dtypes], acc_shape=(tm, tn),
               epi=epi)


def _mm_nt_row(name, dy, w, l, out_dtype, epi=None, extra=None):
    m, n = dy.shape
    k = w.shape[1]
    tm, tn, tk = _tile(m, MM_TILE_M), _tile(k, MM_TILE_N), _tile(n, MM_TILE_K)
    o_spec = pl.BlockSpec((tm, tn), lambda i, j, kk: (i, j))
    return _mm(name, dy, w, dn=NT, grid=(m // tm, k // tn, n // tk),
               a_spec=pl.BlockSpec((tm, tk), lambda i, j, kk: (i, kk)),
               b_spec=pl.BlockSpec((None, tn, tk), lambda i, j, kk: (l, j, kk)),
               o_specs=[o_spec], out_shapes=[jax.ShapeDtypeStruct((m, k), out_dtype)], acc_shape=(tm, tn),
               epi=epi, extra=extra, extra_spec=o_spec)[0]


def _mm_nt_col(name, dy, w, l, out_dtype):
    m = dy.shape[0]
    k, c = w.shape[2], w.shape[3]
    tm, tn, tk = _tile(m, MM_TILE_M), _tile(k, MM_TILE_N), _tile(c, MM_TILE_K)
    kb = c // tk
    return _mm(name, dy, w, dn=NT, grid=(m // tm, k // tn, N_CHIPS * kb),
               a_spec=pl.BlockSpec((tm, tk), lambda i, j, kk: (i, kk)),
               b_spec=pl.BlockSpec((None, None, tn, tk), lambda i, j, kk: (l, kk // kb, j, kk % kb)),
               o_specs=[pl.BlockSpec((tm, tn), lambda i, j, kk: (i, j))],
               out_shapes=[jax.ShapeDtypeStruct((m, k), out_dtype)], acc_shape=(tm, tn))[0]


def _mm_tn_row(name, a, dy, l, n_layers, prev):
    t, m = a.shape
    n = dy.shape[1]
    tm, tn, tk = _tile(m, MM_TILE_M), _tile(n, MM_TILE_N), _tile(t, MM_TILE_K)
    return _mm(name, a, dy, dn=TN, grid=(m // tm, n // tn, t // tk),
               a_spec=pl.BlockSpec((tk, tm), lambda i, j, kk: (kk, i)),
               b_spec=pl.BlockSpec((tk, tn), lambda i, j, kk: (kk, j)),
               o_specs=[pl.BlockSpec((None, tm, tn), lambda i, j, kk: (l, i, j))],
               out_shapes=[jax.ShapeDtypeStruct((n_layers, m, n), F32)], acc_shape=(tm, tn), prev=prev)[0]


def _mm_tn_col(name, a, dy, l, n_layers, prev):
    t, m = a.shape
    c = dy.shape[1] // N_CHIPS
    tm, tn, tk = _tile(m, MM_TILE_M), _tile(c, MM_TILE_N), _tile(t, MM_TILE_K)
    nb = c // tn
    return _mm(name, a, dy, dn=TN, grid=(m // tm, N_CHIPS * nb, t // tk),
               a_spec=pl.BlockSpec((tk, tm), lambda i, j, kk: (kk, i)),
               b_spec=pl.BlockSpec((tk, tn), lambda i, j, kk: (kk, j)),
               o_specs=[pl.BlockSpec((None, None, tm, tn), lambda i, j, kk: (l, j // nb, i, j % nb))],
               out_shapes=[jax.ShapeDtypeStruct((n_layers, N_CHIPS, m, c), F32)], acc_shape=(tm, tn),
               prev=prev)[0]


def _rms(x, g):
    r = lax.rsqrt(jnp.mean(x * x, axis=-1, keepdims=True) + RMS_EPS)
    return x * r * g


def _rms_bwd(x, g, dy):
    r = lax.rsqrt(jnp.mean(x * x, axis=-1, keepdims=True) + RMS_EPS)
    xr = x * r
    dyg = dy * g
    dx = r * (dyg - xr * jnp.mean(dyg * xr, axis=-1, keepdims=True))
    return dx, jnp.sum(dy * xr, axis=0, keepdims=True)


def _norm_fwd(name, x, o, g_post, g_next):
    s, d = x.shape
    tr = _tile(s, ROW_TILE)
    has_prev, has_next = o is not None, g_next is not None
    row = pl.BlockSpec((tr, d), lambda i: (i, 0))
    vec = pl.BlockSpec((1, d), lambda i: (0, 0))

    def body(*refs):
        refs = list(refs)
        xn = refs.pop(0)[...]
        if has_prev:
            o_ref, gp_ref = refs.pop(0), refs.pop(0)
            xn = xn + _rms(o_ref[...], gp_ref[...])
        gn_ref = refs.pop(0) if has_next else None
        if has_prev:
            refs.pop(0)[...] = xn
        if has_next:
            refs.pop(0)[...] = _rms(xn, gn_ref[...]).astype(BF16)

    ins, in_specs = [x], [row]
    if has_prev:
        ins += [o, g_post]
        in_specs += [row, vec]
    if has_next:
        ins.append(g_next)
        in_specs.append(vec)
    out_shapes, out_specs = [], []
    if has_prev:
        out_shapes.append(jax.ShapeDtypeStruct((s, d), F32))
        out_specs.append(row)
    if has_next:
        out_shapes.append(jax.ShapeDtypeStruct((s, d), BF16))
        out_specs.append(row)
    outs = pl.pallas_call(body, name=name, grid=(s // tr,), in_specs=in_specs, out_specs=out_specs,
                          out_shape=out_shapes, compiler_params=_params(("parallel",)))(*ins)
    outs = list(outs)
    x_new = outs.pop(0) if has_prev else x
    h = outs.pop(0) if has_next else None
    return x_new, h


def _norm_bwd(name, dxn, dh, xn, o, g_post, g_next):
    s, d = xn.shape
    tr = _tile(s, ROW_TILE)
    has_prev, has_next, has_dxn = o is not None, dh is not None, dxn is not None
    row = pl.BlockSpec((tr, d), lambda i: (i, 0))
    vec = pl.BlockSpec((1, d), lambda i: (0, 0))

    def body(*refs):
        refs = list(refs)
        first = pl.program_id(0) == 0
        dxn_ref = refs.pop(0) if has_dxn else None
        dh_ref = refs.pop(0) if has_next else None
        xn_ref = refs.pop(0)
        if has_prev:
            o_ref, gp_ref = refs.pop(0), refs.pop(0)
        gn_ref = refs.pop(0) if has_next else None
        dx_ref = refs.pop(0)
        if has_prev:
            do_ref, dgp_ref = refs.pop(0), refs.pop(0)
        dgn_ref = refs.pop(0) if has_next else None

        def accumulate(ref, val):
            @pl.when(first)
            def _():
                ref[...] = val

            @pl.when(jnp.logical_not(first))
            def _():
                ref[...] += val

        dx = dxn_ref[...] if has_dxn else None
        if has_next:
            dxh, dgn = _rms_bwd(xn_ref[...], gn_ref[...], dh_ref[...].astype(F32))
            dx = dxh if dx is None else dx + dxh
            accumulate(dgn_ref, dgn)
        dx_ref[...] = dx
        if has_prev:
            do, dgp = _rms_bwd(o_ref[...], gp_ref[...], dx)
            do_ref[...] = do.astype(BF16)
            accumulate(dgp_ref, dgp)

    ins, in_specs = [], []
    if has_dxn:
        ins.append(dxn)
        in_specs.append(row)
    if has_next:
        ins.append(dh)
        in_specs.append(row)
    ins.append(xn)
    in_specs.append(row)
    if has_prev:
        ins += [o, g_post]
        in_specs += [row, vec]
    if has_next:
        ins.append(g_next)
        in_specs.append(vec)
    out_shapes, out_specs = [jax.ShapeDtypeStruct((s, d), F32)], [row]
    if has_prev:
        out_shapes += [jax.ShapeDtypeStruct((s, d), BF16), jax.ShapeDtypeStruct((1, d), F32)]
        out_specs += [row, vec]
    if has_next:
        out_shapes.append(jax.ShapeDtypeStruct((1, d), F32))
        out_specs.append(vec)
    outs = list(pl.pallas_call(body, name=name, grid=(s // tr,), in_specs=in_specs, out_specs=out_specs,
                               out_shape=out_shapes, compiler_params=_params(("arbitrary",)))(*ins))
    dx = outs.pop(0)
    do, dgp = (outs.pop(0), outs.pop(0)) if has_prev else (None, None)
    dgn = outs.pop(0) if has_next else None
    return dx, do, dgp, dgn


def _loss_head(y, target):
    s, d = y.shape
    tr = _tile(s, ROW_TILE)
    row = pl.BlockSpec((tr, d), lambda i: (i, 0))
    vec = pl.BlockSpec((1, d), lambda i: (0, 0))

    def body(y_ref, t_ref, dy_ref, l_ref):
        err = y_ref[...] - t_ref[...]
        dy_ref[...] = err * (1.0 / d)
        part = jnp.sum(err * err, axis=0, keepdims=True) * (0.5 / d)

        @pl.when(pl.program_id(0) == 0)
        def _():
            l_ref[...] = part

        @pl.when(pl.program_id(0) != 0)
        def _():
            l_ref[...] += part

    return pl.pallas_call(body, name="loss_head", grid=(s // tr,), in_specs=[row, row], out_specs=[row, vec],
                          out_shape=[jax.ShapeDtypeStruct((s, d), F32), jax.ShapeDtypeStruct((1, d), F32)],
                          compiler_params=_params(("arbitrary",)))(y, target)


@jax.custom_vjp
def _bdot(a, b):
    return jnp.dot(a.astype(BF16), b.astype(BF16), preferred_element_type=F32)


def _bdot_fwd(a, b):
    return _bdot(a, b), (a, b)


def _bdot_bwd(res, ct):
    a, b = res
    ctb = ct.astype(BF16)
    da = lax.dot_general(ctb, b.astype(BF16), NT, preferred_element_type=F32)
    db = lax.dot_general(a.astype(BF16), ctb, TN, preferred_element_type=F32)
    return da, db


_bdot.defvjp(_bdot_fwd, _bdot_bwd)


@functools.partial(jax.custom_vjp, nondiff_argnums=(1,))
def _shift(x, k):
    n = x.shape[0]
    if k == 0:
        return x
    rolled = pltpu.roll(x, k % n, 0)
    t = lax.broadcasted_iota(jnp.int32, x.shape, 0)
    keep = (t >= k) if k > 0 else (t < n + k)
    return jnp.where(keep, rolled, 0.0)


def _shift_fwd(x, k):
    return _shift(x, k), None


def _shift_bwd(k, _, ct):
    return (_shift(ct, -k),)


_shift.defvjp(_shift_fwd, _shift_bwd)


def _sigmoid(x):
    return 1.0 / (1.0 + jnp.exp(-x))


def _layer_norm(x, g, b=None):
    mu = jnp.mean(x, axis=-1, keepdims=True)
    xc = x - mu
    var = jnp.mean(xc * xc, axis=-1, keepdims=True)
    y = xc * lax.rsqrt(var + LN_EPS) * g
    return y if b is None else y + b


def _gmlp_chunk(zu, zv, gv, w, bcol):
    ch = w.shape[0]
    u = jax.nn.gelu(zu)
    vn = _layer_norm(jax.nn.gelu(zv), gv)
    t = lax.broadcasted_iota(jnp.int32, (ch, ch), 0)
    s = lax.broadcasted_iota(jnp.int32, (ch, ch), 1)
    wm = jnp.where(t >= s, w, 0.0)
    return u * (_bdot(wm, vn) + bcol)


def _gmlp_specs(seq, heads, hd, ch, u_off, v_off):
    col = lambda off: pl.BlockSpec((seq, hd), lambda h: (0, off + h))
    return (col(u_off), col(v_off), pl.BlockSpec((None, 1, hd), lambda h: (h, 0, 0)),
            pl.BlockSpec((None, ch, ch), lambda h: (h, 0, 0)), pl.BlockSpec((None, ch, 1), lambda h: (h, 0, 0)))


def _gmlp_fwd(z, gv, ws, bcol):
    seq = z.shape[0]
    heads, _, hd = gv.shape
    ch = ws.shape[-1]
    zu_s, zv_s, gv_s, w_s, b_s = _gmlp_specs(seq, heads, hd, ch, 0, heads)

    def body(zu_ref, zv_ref, gv_ref, w_ref, b_ref, y_ref):
        gvv, w, bc = gv_ref[...], w_ref[...], b_ref[...]

        def step(c, carry):
            rows = pl.ds(pl.multiple_of(c * ch, ch), ch)
            y_ref[rows, :] = _gmlp_chunk(zu_ref[rows, :], zv_ref[rows, :], gvv, w, bc).astype(BF16)
            return carry

        lax.fori_loop(0, seq // ch, step, 0)

    return pl.pallas_call(body, name="gmlp_fwd", grid=(heads,), in_specs=[zu_s, zv_s, gv_s, w_s, b_s],
                          out_specs=pl.BlockSpec((seq, hd), lambda h: (0, h)),
                          out_shape=jax.ShapeDtypeStruct((seq, heads * hd), BF16),
                          compiler_params=_params(("parallel",)))(z, z, gv, ws, bcol)


def _gmlp_bwd(z, dy, gv, ws, bcol):
    seq = z.shape[0]
    heads, _, hd = gv.shape
    ch = ws.shape[-1]
    zu_s, zv_s, gv_s, w_s, b_s = _gmlp_specs(seq, heads, hd, ch, 0, heads)
    col = pl.BlockSpec((seq, hd), lambda h: (0, h))

    def body(zu_ref, zv_ref, dy_ref, gv_ref, w_ref, b_ref, dzu_ref, dzv_ref, dgv_ref, dw_ref, db_ref):
        gvv, w, bc = gv_ref[...], w_ref[...], b_ref[...]

        def step(c, carry):
            dgv, dw, db = carry
            rows = pl.ds(pl.multiple_of(c * ch, ch), ch)
            _, vjp = jax.vjp(_gmlp_chunk, zu_ref[rows, :], zv_ref[rows, :], gvv, w, bc)
            dzu, dzv, dgv_c, dw_c, db_c = vjp(dy_ref[rows, :])
            dzu_ref[rows, :] = dzu.astype(BF16)
            dzv_ref[rows, :] = dzv.astype(BF16)
            return dgv + dgv_c, dw + dw_c, db + db_c

        zero = (jnp.zeros((1, hd), F32), jnp.zeros((ch, ch), F32), jnp.zeros((ch, 1), F32))
        dgv, dw, db = lax.fori_loop(0, seq // ch, step, zero)
        dgv_ref[...] = dgv
        dw_ref[...] = dw
        db_ref[...] = db

    return pl.pallas_call(
        body, name="gmlp_bwd", grid=(heads,), in_specs=[zu_s, zv_s, col, gv_s, w_s, b_s],
        out_specs=[col, col, gv_s, w_s, b_s],
        out_shape=[jax.ShapeDtypeStruct((seq, heads * hd), BF16), jax.ShapeDtypeStruct((seq, heads * hd), BF16),
                   jax.ShapeDtypeStruct(gv.shape, F32), jax.ShapeDtypeStruct(ws.shape, F32),
                   jax.ShapeDtypeStruct(bcol.shape, F32)],
        compiler_params=_params(("parallel",)))(z, z, dy, gv, ws, bcol)


def _pool_group(p, w, s, window):
    win, span = p, 1
    while span < window:
        win = win + _shift(win, span)
        span *= 2
    t = lax.broadcasted_iota(jnp.int32, (p.shape[0], 1), 0).astype(F32)
    cnt = jnp.minimum(t + 1.0, float(window))
    return _bdot(win / cnt - p, w) * s


def _pool_fwd(z, w_pool, s_pool, col_block):
    seq = z.shape[0]
    groups, gw, _ = w_pool.shape
    pw = groups * gw

    def body(p_ref, w_ref, s_ref, y_ref):
        for g in range(groups):
            cols = slice(g * gw, (g + 1) * gw)
            y_ref[:, cols] = _pool_group(p_ref[:, cols], w_ref[g], s_ref[:, cols], POOL_WINDOWS[g]).astype(BF16)

    return pl.pallas_call(
        body, name="pool_fwd", grid=(1,),
        in_specs=[pl.BlockSpec((seq, pw), lambda i: (0, col_block)),
                  pl.BlockSpec((groups, gw, gw), lambda i: (0, 0, 0)), pl.BlockSpec((1, pw), lambda i: (0, 0))],
        out_specs=pl.BlockSpec((seq, pw), lambda i: (0, 0)), out_shape=jax.ShapeDtypeStruct((seq, pw), BF16),
        compiler_params=_params(("arbitrary",)))(z, w_pool, s_pool)


def _pool_bwd(z, dy, w_pool, s_pool, col_block, dy_block):
    seq = z.shape[0]
    groups, gw, _ = w_pool.shape
    pw = groups * gw

    def body(p_ref, dy_ref, w_ref, s_ref, dp_ref, dw_ref, ds_ref):
        for g in range(groups):
            cols = slice(g * gw, (g + 1) * gw)
            _, vjp = jax.vjp(functools.partial(_pool_group, window=POOL_WINDOWS[g]), p_ref[:, cols], w_ref[g],
                             s_ref[:, cols])
            dp, dw, ds = vjp(dy_ref[:, cols])
            dp_ref[:, cols] = dp.astype(BF16)
            dw_ref[g] = dw
            ds_ref[:, cols] = ds

    return pl.pallas_call(
        body, name="pool_bwd", grid=(1,),
        in_specs=[pl.BlockSpec((seq, pw), lambda i: (0, col_block)),
                  pl.BlockSpec((seq, pw), lambda i: (0, dy_block)),
                  pl.BlockSpec((groups, gw, gw), lambda i: (0, 0, 0)), pl.BlockSpec((1, pw), lambda i: (0, 0))],
        out_specs=[pl.BlockSpec((seq, pw), lambda i: (0, 0)), pl.BlockSpec((groups, gw, gw), lambda i: (0, 0, 0)),
                   pl.BlockSpec((1, pw), lambda i: (0, 0))],
        out_shape=[jax.ShapeDtypeStruct((seq, pw), BF16), jax.ShapeDtypeStruct(w_pool.shape, F32),
                   jax.ShapeDtypeStruct((1, pw), F32)],
        compiler_params=_params(("arbitrary",)))(z, dy, w_pool, s_pool)


def _conv_fwd(z, w_dw, l, b_dw, val_block, gate_block):
    seq = z.shape[0]
    taps, cb = w_dw.shape[2], w_dw.shape[3]

    def body(val_ref, gate_ref, w_ref, b_ref, out_ref):
        h = val_ref[...] * _sigmoid(gate_ref[...])
        acc = jnp.broadcast_to(b_ref[...], h.shape)
        for d in range(taps):
            acc = acc + w_ref[pl.ds(taps - 1 - d, 1), :] * _shift(h, d)
        out_ref[...] = acc

    return pl.pallas_call(
        body, name="conv_fwd", grid=(N_CHIPS,),
        in_specs=[pl.BlockSpec((seq, cb), lambda j: (0, val_block + j)),
                  pl.BlockSpec((seq, cb), lambda j: (0, gate_block + j)),
                  pl.BlockSpec((None, None, taps, cb), lambda j: (l, j, 0, 0)),
                  pl.BlockSpec((1, cb), lambda j: (0, j))],
        out_specs=pl.BlockSpec((seq, cb), lambda j: (0, j)),
        out_shape=jax.ShapeDtypeStruct((seq, N_CHIPS * cb), F32),
        compiler_params=_params(("parallel",)))(z, z, w_dw, b_dw)


def _conv_bwd(z, dout, w_dw, l, val_block, gate_block):
    seq = z.shape[0]
    taps, cb = w_dw.shape[2], w_dw.shape[3]
    col = pl.BlockSpec((seq, cb), lambda j: (0, j))

    def body(val_ref, gate_ref, do_ref, w_ref, dval_ref, dgate_ref, dw_ref, db_ref):
        val, sg, do = val_ref[...], _sigmoid(gate_ref[...]), do_ref[...]
        h = val * sg
        db_ref[...] = jnp.sum(do, axis=0, keepdims=True)
        dh = jnp.zeros_like(h)
        for d in range(taps):
            k = taps - 1 - d
            dw_ref[pl.ds(k, 1), :] = jnp.sum(do * _shift(h, d), axis=0, keepdims=True)
            dh = dh + w_ref[pl.ds(k, 1), :] * _shift(do, -d)
        dval_ref[...] = (dh * sg).astype(BF16)
        dgate_ref[...] = (dh * val * sg * (1.0 - sg)).astype(BF16)

    return pl.pallas_call(
        body, name="conv_bwd", grid=(N_CHIPS,),
        in_specs=[pl.BlockSpec((seq, cb), lambda j: (0, val_block + j)),
                  pl.BlockSpec((seq, cb), lambda j: (0, gate_block + j)), col,
                  pl.BlockSpec((None, None, taps, cb), lambda j: (l, j, 0, 0))],
        out_specs=[col, col, pl.BlockSpec((taps, cb), lambda j: (0, j)), pl.BlockSpec((1, cb), lambda j: (0, j))],
        out_shape=[jax.ShapeDtypeStruct((seq, N_CHIPS * cb), BF16), jax.ShapeDtypeStruct((seq, N_CHIPS * cb), BF16),
                   jax.ShapeDtypeStruct((taps, N_CHIPS * cb), F32), jax.ShapeDtypeStruct((1, N_CHIPS * cb), F32)],
        compiler_params=_params(("parallel",)))(z, z, dout, w_dw)


def _ln_swish(hc, g, b):
    y = _layer_norm(hc, g, b)
    return y * _sigmoid(y)


def _ln_swish_fwd(hc, g, b):
    s, cw = hc.shape
    tr = _tile(s, ROW_TILE)
    row = pl.BlockSpec((tr, cw), lambda i: (i, 0))
    vec = pl.BlockSpec((1, cw), lambda i: (0, 0))

    def body(h_ref, g_ref, b_ref, y_ref):
        y_ref[...] = _ln_swish(h_ref[...], g_ref[...], b_ref[...]).astype(BF16)

    return pl.pallas_call(body, name="ln_swish_fwd", grid=(s // tr,), in_specs=[row, vec, vec], out_specs=row,
                          out_shape=jax.ShapeDtypeStruct((s, cw), BF16),
                          compiler_params=_params(("parallel",)))(hc, g, b)


def _ln_swish_bwd(hc, dy, g, b, dy_block):
    s, cw = hc.shape
    tr = _tile(s, ROW_TILE)
    row = pl.BlockSpec((tr, cw), lambda i: (i, 0))
    vec = pl.BlockSpec((1, cw), lambda i: (0, 0))

    def body(h_ref, dy_ref, g_ref, b_ref, dh_ref, dg_ref, db_ref):
        _, vjp = jax.vjp(_ln_swish, h_ref[...], g_ref[...], b_ref[...])
        dh, dg, db = vjp(dy_ref[...])
        dh_ref[...] = dh

        @pl.when(pl.program_id(0) == 0)
        def _():
            dg_ref[...] = dg
            db_ref[...] = db

        @pl.when(pl.program_id(0) != 0)
        def _():
            dg_ref[...] += dg
            db_ref[...] += db

    return pl.pallas_call(
        body, name="ln_swish_bwd", grid=(s // tr,),
        in_specs=[row, pl.BlockSpec((tr, cw), lambda i: (i, dy_block)), vec, vec], out_specs=[row, vec, vec],
        out_shape=[jax.ShapeDtypeStruct((s, cw), F32), jax.ShapeDtypeStruct((1, cw), F32),
                   jax.ShapeDtypeStruct((1, cw), F32)],
        compiler_params=_params(("arbitrary",)))(hc, dy, g, b)


def _attn_probs(q, k, scale):
    s = lax.dot_general(q, k, NT, preferred_element_type=F32) * scale
    e = jnp.exp(s - jnp.max(s, axis=-1, keepdims=True))
    return e / jnp.sum(e, axis=-1, keepdims=True)


def _attn_fwd(q, k, v):
    seq, d = q.shape
    mem = k.shape[0]
    hd = d // XATTN_HEADS
    scale = hd ** -0.5
    qs = pl.BlockSpec((seq, hd), lambda h: (0, h))
    ms = pl.BlockSpec((mem, hd), lambda h: (0, h))

    def body(q_ref, k_ref, v_ref, a_ref):
        p = _attn_probs(q_ref[...], k_ref[...], scale)
        a_ref[...] = jnp.dot(p.astype(BF16), v_ref[...], preferred_element_type=F32).astype(BF16)

    return pl.pallas_call(body, name="attn_fwd", grid=(XATTN_HEADS,), in_specs=[qs, ms, ms], out_specs=qs,
                          out_shape=jax.ShapeDtypeStruct((seq, d), BF16),
                          compiler_params=_params(("parallel",)))(q, k, v)


def _attn_bwd(q, k, v, da):
    seq, d = q.shape
    mem = k.shape[0]
    hd = d // XATTN_HEADS
    scale = hd ** -0.5
    qs = pl.BlockSpec((seq, hd), lambda h: (0, h))
    ms = pl.BlockSpec((mem, hd), lambda h: (0, h))

    def body(q_ref, k_ref, v_ref, da_ref, dq_ref, dk_ref, dv_ref):
        q_, k_, v_, da_ = q_ref[...], k_ref[...], v_ref[...], da_ref[...]
        p = _attn_probs(q_, k_, scale)
        dv_ref[...] = lax.dot_general(p.astype(BF16), da_, TN, preferred_element_type=F32).astype(BF16)
        dp = lax.dot_general(da_, v_, NT, preferred_element_type=F32)
        ds = (p * (dp - jnp.sum(dp * p, axis=-1, keepdims=True)) * scale).astype(BF16)
        dq_ref[...] = jnp.dot(ds, k_, preferred_element_type=F32).astype(BF16)
        dk_ref[...] = lax.dot_general(ds, q_, TN, preferred_element_type=F32).astype(BF16)

    return pl.pallas_call(
        body, name="attn_bwd", grid=(XATTN_HEADS,), in_specs=[qs, ms, ms, qs], out_specs=[qs, ms, ms],
        out_shape=[jax.ShapeDtypeStruct((seq, d), BF16), jax.ShapeDtypeStruct((mem, d), BF16),
                   jax.ShapeDtypeStruct((mem, d), BF16)],
        compiler_params=_params(("parallel",)))(q, k, v, da)


def _cast_bf16(name, w):
    n_l, r, c = w.shape
    tr = _tile(r, 2 * ROW_TILE)
    spec = pl.BlockSpec((None, tr, c), lambda l, i: (l, i, 0))

    def body(w_ref, o_ref):
        o_ref[...] = w_ref[...].astype(BF16)

    return pl.pallas_call(body, name=name, grid=(n_l, r // tr), in_specs=[spec], out_specs=spec,
                          out_shape=jax.ShapeDtypeStruct(w.shape, BF16),
                          compiler_params=_params(("parallel", "parallel")))(w)


def _pair_sum(name, place, dw, got):
    _, _, r, c = dw.shape
    tr = _tile(r, ROW_TILE)

    def body(place_ref, own_ref, got_ref, o_ref):
        o_ref[...] = (own_ref[...] + got_ref[...]).astype(BF16)

    return pl.pallas_call(
        body, name=name,
        grid_spec=pltpu.PrefetchScalarGridSpec(
            num_scalar_prefetch=1, grid=(N_CHIPS, r // tr),
            in_specs=[pl.BlockSpec((None, None, tr, c), lambda j, i, p: (p[0], j, i, 0)),
                      pl.BlockSpec((None, tr, c), lambda j, i, p: (j, i, 0))],
            out_specs=pl.BlockSpec((None, tr, c), lambda j, i, p: (j, i, 0))),
        out_shape=jax.ShapeDtypeStruct((N_CHIPS, r, c), BF16),
        compiler_params=_params(("parallel", "parallel")))(place, dw, got)


def _chip_sum(name, place, parts, n_layers):
    _, r, c = parts.shape
    tr = _tile(r, ROW_TILE)

    def body(place_ref, *refs):
        o_ref = refs[N_CHIPS]
        acc = refs[0][...].astype(F32)
        for j in range(1, N_CHIPS):
            acc = acc + refs[j][...].astype(F32)
        o_ref[...] = acc

    part = lambda j: pl.BlockSpec((None, tr, c), lambda i, p: (j, i, 0))
    return pl.pallas_call(
        body, name=name,
        grid_spec=pltpu.PrefetchScalarGridSpec(
            num_scalar_prefetch=1, grid=(r // tr,), in_specs=[part(j) for j in range(N_CHIPS)],
            out_specs=pl.BlockSpec((None, tr, c), lambda i, p: (p[0], i, 0))),
        out_shape=jax.ShapeDtypeStruct((n_layers, r, c), F32),
        compiler_params=_params(("parallel",)))(place, *([parts] * N_CHIPS))


def _device_sum(parts):
    n, rows, lanes = parts.shape
    tr = _tile(rows, 4 * ROW_TILE) if rows % 8 == 0 else rows

    def body(p_ref, o_ref):
        acc = p_ref[0]
        for j in range(1, n):
            acc = acc + p_ref[j]
        o_ref[...] = acc

    return pl.pallas_call(
        body, name="device_sum", grid=(rows // tr,), in_specs=[pl.BlockSpec((n, tr, lanes), lambda i: (0, i, 0))],
        out_specs=pl.BlockSpec((tr, lanes), lambda i: (i, 0)), out_shape=jax.ShapeDtypeStruct((rows, lanes), F32),
        compiler_params=_params(("parallel",)))(parts)


def _adamw(name, w, g, m, v):
    rows, cols = w.shape
    tr = _tile(rows, ROW_TILE) if rows % 8 == 0 else rows
    spec = pl.BlockSpec((tr, cols), lambda i: (i, 0))
    c1 = 1.0 - ADAM_B1 ** ADAM_STEP
    c2 = 1.0 - ADAM_B2 ** ADAM_STEP

    def body(w_ref, g_ref, m_ref, v_ref, d_ref, nm_ref, nv_ref):
        g_ = g_ref[...]
        nm = ADAM_B1 * m_ref[...] + (1.0 - ADAM_B1) * g_
        nv = ADAM_B2 * v_ref[...] + (1.0 - ADAM_B2) * (g_ * g_)
        d_ref[...] = -ADAM_LR * ((nm / c1) / (jnp.sqrt(nv / c2) + ADAM_EPS) + ADAM_WD * w_ref[...])
        nm_ref[...] = nm
        nv_ref[...] = nv

    sds = jax.ShapeDtypeStruct((rows, cols), F32)
    return pl.pallas_call(body, name=name, grid=(rows // tr,), in_specs=[spec] * 4, out_specs=[spec] * 3,
                          out_shape=[sds] * 3, compiler_params=_params(("parallel",)))(w, g, m, v)


def _exchange(name, inputs, out_shapes, plan, n_copies, aliases=None):
    n_in, n_out = len(inputs), len(out_shapes)

    def body(*refs):
        in_refs, out_refs = refs[:n_in], refs[n_in:n_in + n_out]
        send_sems, recv_sems = refs[n_in + n_out:]
        me = (lax.axis_index("x"), lax.axis_index("y"), lax.axis_index("c"))
        i = 0
        sent = []
        for phase in plan(in_refs, out_refs, me):
            started = []
            for src, dst, target in phase:
                if target is None:
                    cp = pltpu.make_async_copy(src, dst, send_sems.at[i])
                else:
                    cp = pltpu.make_async_remote_copy(src_ref=src, dst_ref=dst, send_sem=send_sems.at[i],
                                                      recv_sem=recv_sems.at[i], device_id=target,
                                                      device_id_type=MESH)
                cp.start()
                started.append((cp, target))
                i += 1
            for cp, target in started:
                if target is None:
                    cp.wait()
                else:
                    cp.wait_recv()
                    sent.append(cp)
        assert i == n_copies
        for cp in sent:
            cp.wait_send()

    return pl.pallas_call(
        body, name=name, in_specs=[ANY] * n_in, out_specs=[ANY] * n_out, out_shape=list(out_shapes),
        scratch_shapes=[pltpu.SemaphoreType.DMA((n_copies,)), pltpu.SemaphoreType.DMA((n_copies,))],
        input_output_aliases=aliases or {},
        compiler_params=pltpu.CompilerParams(has_side_effects=True))(*inputs)


def _other_chips(x, y):
    return [(1 - x, y, 2 * (1 - x) + y), (x, 1 - y, 2 * x + 1 - y), (1 - x, 1 - y, 2 * (1 - x) + 1 - y)]


def _gather_weights(shards):
    def plan(in_refs, out_refs, me):
        x, y, c = me
        mine = 2 * x + y
        own, first, second = [], [], []
        for src, dst in zip(in_refs, out_refs):
            for l in range(2):
                own.append((src.at[l], dst.at[l, mine], None))
            for px, py, _ in _other_chips(x, y):
                first.append((src.at[c], dst.at[c, mine], (px, py, c)))
            for _, _, chip in _other_chips(x, y):
                second.append((dst.at[c, chip], dst.at[c, chip], (x, y, 1 - c)))
        return [own + first, second]

    outs = [jax.ShapeDtypeStruct((s.shape[0], N_CHIPS) + s.shape[1:], s.dtype) for s in shards]
    return _exchange("gather_weights", shards, outs, plan, 8 * len(shards))


def _swap_layers(grads):
    def plan(in_refs, out_refs, me):
        x, y, c = me
        return [[(src.at[1 - c], dst, (x, y, 1 - c)) for src, dst in zip(in_refs, out_refs)]]

    outs = [jax.ShapeDtypeStruct(g.shape[1:], g.dtype) for g in grads]
    return _exchange("swap_layers", grads, outs, plan, len(grads))


def _scatter_chips(sums):
    def plan(in_refs, out_refs, me):
        x, y, c = me
        mine = 2 * x + y
        copies = []
        for src, dst in zip(in_refs, out_refs):
            copies.append((src.at[mine], dst.at[mine], None))
            for px, py, chip in _other_chips(x, y):
                copies.append((src.at[chip], dst.at[mine], (px, py, c)))
        return [copies]

    outs = [jax.ShapeDtypeStruct(s.shape, s.dtype) for s in sums]
    return _exchange("scatter_chips", sums, outs, plan, 4 * len(sums))


def _share_layers(grads):
    def plan(in_refs, out_refs, me):
        x, y, c = me
        return [[(dst.at[c], dst.at[c], (x, y, 1 - c)) for dst in out_refs]]

    outs = [jax.ShapeDtypeStruct(g.shape, g.dtype) for g in grads]
    return _exchange("share_layers", grads, outs, plan, len(grads), aliases={i: i for i in range(len(grads))})


def _gather_devices(flat):
    def plan(in_refs, out_refs, me):
        x, y, c = me
        mine = 4 * x + 2 * y + c
        src, dst = in_refs[0], out_refs[0]
        copies = [(src, dst.at[mine], None)]
        for fx, fy, fc in [(0, 0, 1), (0, 1, 0), (0, 1, 1), (1, 0, 0), (1, 0, 1), (1, 1, 0), (1, 1, 1)]:
            peer = (x + fx - 2 * fx * x, y + fy - 2 * fy * y, c + fc - 2 * fc * c)
            copies.append((src, dst.at[mine], peer))
        return [copies]

    out = jax.ShapeDtypeStruct((N_DEVICES,) + flat.shape, flat.dtype)
    return _exchange("gather_devices", [flat], [out], plan, N_DEVICES)[0]


BIG = ("w_in", "w_out", "w_q", "w_k", "w_v", "w_o", "w_up", "w_down")
COLUMN_SPLIT = ("w_in", "w_up")
WEIGHTS = ("norm_mix_pre", "norm_mix_post", "w_in", "w_out", "gmlp_v_gain", "w_spatial", "b_spatial", "w_pool",
           "s_pool", "w_dw", "b_dw", "conv_ln_g", "conv_ln_b", "norm_xattn_pre", "norm_mem", "norm_xattn_post",
           "w_q", "w_k", "w_v", "w_o", "norm_ffn_pre", "norm_ffn_post", "w_up", "w_down")
SMALL = tuple(n for n in WEIGHTS if n not in BIG)


def _relu2(acc):
    r = jnp.maximum(acc, 0.0)
    return acc, r * r


def _relu2_bwd(acc, up):
    return (acc * (2.0 * jnp.maximum(up, 0.0)),)


def _pack(arrays):
    flat = jnp.concatenate([a.reshape(-1) for a in arrays])
    tile = 8 * V7X_LANES
    pad = (-flat.shape[0]) % tile
    return jnp.pad(flat, (0, pad)).reshape(-1, V7X_LANES)


def _unpack(packed, like):
    flat = packed.reshape(-1)
    out, pos = [], 0
    for a in like:
        out.append(flat[pos:pos + a.size].reshape(a.shape))
        pos += a.size
    return out


def _step(x, mem, target, w, m, v):
    n_layers = w["w_in"].shape[0]
    seq, d = x.shape
    heads, hd = w["gmlp_v_gain"].shape[1:]
    gw = heads * hd
    groups, pgw = w["w_pool"].shape[1:3]
    pw = groups * pgw
    cw = w["b_dw"].shape[1]
    cb = cw // N_CHIPS
    cx, cy, cc = lax.axis_index("x"), lax.axis_index("y"), lax.axis_index("c")
    chip = 2 * cx + cy
    place = jnp.stack([cc, chip]).astype(jnp.int32)

    shards = [_cast_bf16("cast_" + n, w[n]) for n in BIG] + [w["w_dw"]]
    gathered = dict(zip(BIG + ("w_dw",), _gather_weights(shards)))
    full = {n: (gathered[n] if n in COLUMN_SPLIT else
                gathered[n].reshape((n_layers, N_CHIPS * gathered[n].shape[2], gathered[n].shape[3])))
            for n in BIG}
    w_dw = gathered["w_dw"]

    vec = lambda name, l: w[name][l].reshape(1, -1)

    saved = []
    _, h1 = _norm_fwd("norm_first", x, None, None, vec("norm_mix_pre", 0))
    for l in range(n_layers):
        gv = w["gmlp_v_gain"][l].reshape(heads, 1, hd)
        ws = w["w_spatial"][l]
        bcol = w["b_spatial"][l].reshape(heads, -1, 1)
        z = _mm_nn_col("mm_in", h1, full["w_in"], l, [F32])[0]
        ya = _gmlp_fwd(z, gv, ws, bcol)
        yb = _pool_fwd(z, w["w_pool"][l], vec("s_pool", l), (2 * gw) // pw)
        hc = _conv_fwd(z, w_dw, l, vec("b_dw", l), (2 * gw + pw) // cb, (2 * gw + pw + cw) // cb)
        yc = _ln_swish_fwd(hc, vec("conv_ln_g", l), vec("conv_ln_b", l))
        y = jnp.concatenate([ya, yb, yc], axis=1)
        o = _mm_nn_row("mm_out", y, full["w_out"], l, [F32])[0]
        x1, h2 = _norm_fwd("norm_mix", x, o, vec("norm_mix_post", l), vec("norm_xattn_pre", l))
        _, mn = _norm_fwd("norm_mem", mem, None, None, vec("norm_mem", l))
        q = _mm_nn_row("mm_q", h2, full["w_q"], l, [BF16])[0]
        k = _mm_nn_row("mm_k", mn, full["w_k"], l, [BF16])[0]
        vv = _mm_nn_row("mm_v", mn, full["w_v"], l, [BF16])[0]
        a = _attn_fwd(q, k, vv)
        o2 = _mm_nn_row("mm_o", a, full["w_o"], l, [F32])[0]
        x2, h3 = _norm_fwd("norm_xattn", x1, o2, vec("norm_xattn_post", l), vec("norm_ffn_pre", l))
        up, r = _mm_nn_col("mm_up", h3, full["w_up"], l, [F32, BF16], epi=_relu2)
        o3 = _mm_nn_row("mm_down", r, full["w_down"], l, [F32])[0]
        g_next = vec("norm_mix_pre", l + 1) if l + 1 < n_layers else None
        x3, h_next = _norm_fwd("norm_ffn", x2, o3, vec("norm_ffn_post", l), g_next)
        saved.append(dict(x=x, h1=h1, z=z, hc=hc, y=y, o=o, x1=x1, h2=h2, mn=mn, q=q, k=k, v=vv, a=a, o2=o2, x2=x2,
                          h3=h3, up=up, r=r, o3=o3, x3=x3, gv=gv, ws=ws, bcol=bcol))
        x, h1 = x3, h_next

    dx, loss_parts = _loss_head(x, target)
    loss = lax.psum(jnp.sum(loss_parts), ("x", "y", "c"))

    big = {n: None for n in BIG}
    small = {n: [None] * n_layers for n in SMALL}
    dh = None
    for l in reversed(range(n_layers)):
        t = saved[l]
        g_next = vec("norm_mix_pre", l + 1) if l + 1 < n_layers else None
        dx, do3, dgp, dgn = _norm_bwd("norm_ffn_bwd", dx, dh, t["x3"], t["o3"], vec("norm_ffn_post", l), g_next)
        small["norm_ffn_post"][l] = dgp
        if dgn is not None:
            small["norm_mix_pre"][l + 1] = dgn
        big["w_down"] = _mm_tn_row("mm_down_dw", t["r"], do3, l, n_layers, big["w_down"])
        dup = _mm_nt_row("mm_down_dx", do3, full["w_down"], l, BF16, epi=_relu2_bwd, extra=t["up"])
        big["w_up"] = _mm_tn_col("mm_up_dw", t["h3"], dup, l, n_layers, big["w_up"])
        dh3 = _mm_nt_col("mm_up_dx", dup, full["w_up"], l, F32)
        dx, do2, dgp, dgn = _norm_bwd("norm_xattn_bwd", dx, dh3, t["x2"], t["o2"], vec("norm_xattn_post", l),
                                      vec("norm_ffn_pre", l))
        small["norm_xattn_post"][l], small["norm_ffn_pre"][l] = dgp, dgn
        big["w_o"] = _mm_tn_row("mm_o_dw", t["a"], do2, l, n_layers, big["w_o"])
        da = _mm_nt_row("mm_o_dx", do2, full["w_o"], l, BF16)
        dq, dk, dv = _attn_bwd(t["q"], t["k"], t["v"], da)
        big["w_q"] = _mm_tn_row("mm_q_dw", t["h2"], dq, l, n_layers, big["w_q"])
        big["w_k"] = _mm_tn_row("mm_k_dw", t["mn"], dk, l, n_layers, big["w_k"])
        big["w_v"] = _mm_tn_row("mm_v_dw", t["mn"], dv, l, n_layers, big["w_v"])
        dh2 = _mm_nt_row("mm_q_dx", dq, full["w_q"], l, F32)
        dmn = _mm_nt_row("mm_k_dx", dk, full["w_k"], l, F32) + _mm_nt_row("mm_v_dx", dv, full["w_v"], l, F32)
        _, _, _, small["norm_mem"][l] = _norm_bwd("norm_mem_bwd", None, dmn, mem, None, None, vec("norm_mem", l))
        dx, do, dgp, dgn = _norm_bwd("norm_mix_bwd", dx, dh2, t["x1"], t["o"], vec("norm_mix_post", l),
                                     vec("norm_xattn_pre", l))
        small["norm_mix_post"][l], small["norm_xattn_pre"][l] = dgp, dgn
        big["w_out"] = _mm_tn_row("mm_out_dw", t["y"], do, l, n_layers, big["w_out"])
        dy = _mm_nt_row("mm_out_dx", do, full["w_out"], l, F32)
        dzu, dzv, dgv, dws, dbcol = _gmlp_bwd(t["z"], dy, t["gv"], t["ws"], t["bcol"])
        small["gmlp_v_gain"][l] = dgv.reshape(heads, hd)
        small["w_spatial"][l] = dws
        small["b_spatial"][l] = dbcol.reshape(heads, -1)
        dzp, dwp, dsp = _pool_bwd(t["z"], dy, w["w_pool"][l], vec("s_pool", l), (2 * gw) // pw, gw // pw)
        small["w_pool"][l], small["s_pool"][l] = dwp, dsp.reshape(-1)
        dhc, dlg, dlb = _ln_swish_bwd(t["hc"], dy, vec("conv_ln_g", l), vec("conv_ln_b", l), (gw + pw) // cw)
        small["conv_ln_g"][l], small["conv_ln_b"][l] = dlg.reshape(-1), dlb.reshape(-1)
        dval, dgate, dwd, dbd = _conv_bwd(t["z"], dhc, w_dw, l, (2 * gw + pw) // cb, (2 * gw + pw + cw) // cb)
        small["w_dw"][l], small["b_dw"][l] = dwd, dbd.reshape(-1)
        dz = jnp.concatenate([dzu, dzv, dzp, dval, dgate], axis=1)
        big["w_in"] = _mm_tn_col("mm_in_dw", t["h1"], dz, l, n_layers, big["w_in"])
        dh = _mm_nt_col("mm_in_dx", dz, full["w_in"], l, F32)
    grad_x, _, _, dgn = _norm_bwd("norm_first_bwd", dx, dh, saved[0]["x"], None, None, vec("norm_mix_pre", 0))
    small["norm_mix_pre"][0] = dgn

    names = list(BIG)
    grads4 = [big[n] if n in COLUMN_SPLIT else
              big[n].reshape((n_layers, N_CHIPS, big[n].shape[1] // N_CHIPS, big[n].shape[2])) for n in names]
    got = _swap_layers(grads4)
    sums = [_pair_sum("pair_sum_" + n, place, g4, gt) for n, g4, gt in zip(names, grads4, got)]
    parts = _scatter_chips(sums)
    halves = [_chip_sum("chip_sum_" + n, place, p, n_layers) for n, p in zip(names, parts)]
    grad = dict(zip(names, _share_layers(halves)))

    small_full = [jnp.stack([g.reshape(w[n].shape[1:]) if n != "w_dw" else g for g in small[n]]) for n in SMALL]
    total = _device_sum(_gather_devices(_pack(small_full)))
    for n, g in zip(SMALL, _unpack(total, small_full)):
        grad[n] = lax.dynamic_slice_in_dim(g, chip * cb, cb, axis=2) if n == "w_dw" else g

    delta, new_m, new_v = {}, {}, {}
    for n in BIG:
        view = lambda a: a.reshape(-1, a.shape[-1])
        dl, nm, nv = _adamw("adamw_" + n, view(w[n]), view(grad[n]), view(m[n]), view(v[n]))
        delta[n], new_m[n], new_v[n] = (a.reshape(w[n].shape) for a in (dl, nm, nv))
    like = [w[n] for n in SMALL]
    dl, nm, nv = _adamw("adamw_small", _pack(like), _pack([grad[n] for n in SMALL]), _pack([m[n] for n in SMALL]),
                        _pack([v[n] for n in SMALL]))
    for out, packed in ((delta, dl), (new_m, nm), (new_v, nv)):
        out.update(zip(SMALL, _unpack(packed, like)))

    return (loss, grad_x[None], *[grad[n] for n in WEIGHTS], *[delta[n] for n in WEIGHTS],
            *[new_m[n] for n in WEIGHTS], *[new_v[n] for n in WEIGHTS])


def kernel(x, mem, norm_mix_pre, norm_mix_post, w_in, w_out, gmlp_v_gain, w_spatial, b_spatial, w_pool, s_pool, w_dw, b_dw, conv_ln_g, conv_ln_b, norm_xattn_pre, norm_mem, norm_xattn_post, w_q, w_k, w_v, w_o, norm_ffn_pre, norm_ffn_post, w_up, w_down, loss_target, m_norm_mix_pre, m_norm_mix_post, m_w_in, m_w_out, m_gmlp_v_gain, m_w_spatial, m_b_spatial, m_w_pool, m_s_pool, m_w_dw, m_b_dw, m_conv_ln_g, m_conv_ln_b, m_norm_xattn_pre, m_norm_mem, m_norm_xattn_post, m_w_q, m_w_k, m_w_v, m_w_o, m_norm_ffn_pre, m_norm_ffn_post, m_w_up, m_w_down, v_norm_mix_pre, v_norm_mix_post, v_w_in, v_w_out, v_gmlp_v_gain, v_w_spatial, v_b_spatial, v_w_pool, v_s_pool, v_w_dw, v_b_dw, v_conv_ln_g, v_conv_ln_b, v_norm_xattn_pre, v_norm_mem, v_norm_xattn_post, v_w_q, v_w_k, v_w_v, v_w_o, v_norm_ffn_pre, v_norm_ffn_post, v_w_up, v_w_down):
    given = dict(locals())
    w = {n: given[n] for n in WEIGHTS}
    m = {n: given["m_" + n] for n in WEIGHTS}
    v = {n: given["v_" + n] for n in WEIGHTS}
    return _step(x[0], mem[0], loss_target[0], w, m, v)
```

```python
import functools

import jax
import jax.numpy as jnp
from jax import lax
from jax.experimental import pallas as pl
from jax.experimental.pallas import tpu as pltpu

F32 = jnp.float32
BF16 = jnp.bfloat16
MESH = pl.DeviceIdType.MESH

N_CHIPS = 4
N_DEVICES = 8
XATTN_HEADS = 4
POOL_WINDOWS = (2, 4, 8, 16)
RMS_EPS = 1e-6
LN_EPS = 1e-5
ADAM_LR, ADAM_B1, ADAM_B2, ADAM_EPS, ADAM_WD, ADAM_STEP = 0.001, 0.9, 0.999, 1e-08, 0.01, 10

V7X_LANES = 128
V7X_VMEM_LIMIT = 56 * 1024 * 1024
ROW_TILE = 256
MM_TILE_M, MM_TILE_N, MM_TILE_K = 1024, 1024, 1024

ANY = pl.BlockSpec(memory_space=pl.ANY)
HBM = pl.BlockSpec(memory_space=pltpu.HBM)
SEM = pl.BlockSpec(memory_space=pltpu.SEMAPHORE)


def _tile(dim, pref):
    if dim <= pref:
        return dim
    t = (pref // V7X_LANES) * V7X_LANES
    while t >= V7X_LANES:
        if dim % t == 0:
            return t
        t -= V7X_LANES
    return dim


def _params(sem=None):
    return pltpu.CompilerParams(dimension_semantics=sem, vmem_limit_bytes=V7X_VMEM_LIMIT)


NN = (((1,), (0,)), ((), ()))
NT = (((1,), (1,)), ((), ()))
TN = (((0,), (0,)), ((), ()))


def _mm(name, a, b, *, dn, grid, a_spec, b_spec, o_specs, out_shapes, acc_shape, epi=None, extra=None,
        extra_spec=None):
    nk = grid[2]
    n_out = len(out_shapes)
    has_extra = extra is not None

    def body(*refs):
        a_ref, b_ref = refs[0], refs[1]
        pos = 2
        e_ref = None
        if has_extra:
            e_ref = refs[pos]
            pos += 1
        o_refs = refs[pos:pos + n_out]
        acc = refs[pos + n_out]
        k = pl.program_id(2)

        @pl.when(k == 0)
        def _():
            acc[...] = jnp.zeros_like(acc)

        acc[...] += lax.dot_general(a_ref[...].astype(BF16), b_ref[...].astype(BF16), dn,
                                    preferred_element_type=F32)

        @pl.when(k == nk - 1)
        def _():
            if epi is None:
                vals = (acc[...],)
            elif has_extra:
                vals = epi(acc[...], e_ref[...])
            else:
                vals = epi(acc[...])
            for o, v in zip(o_refs, vals):
                o[...] = v.astype(o.dtype)

    ins, in_specs = [a, b], [a_spec, b_spec]
    if has_extra:
        ins.append(extra)
        in_specs.append(extra_spec)
    outs = pl.pallas_call(
        body, name=name, grid=grid, in_specs=in_specs, out_specs=list(o_specs), out_shape=list(out_shapes),
        scratch_shapes=[pltpu.VMEM(acc_shape, F32)],
        compiler_params=_params(("parallel", "parallel", "arbitrary")))(*ins)
    return outs


def _mm_nn_row(name, a, w, out_dtypes, epi=None):
    m, k = a.shape
    n = w.shape[1]
    tm, tn, tk = _tile(m, MM_TILE_M), _tile(n, MM_TILE_N), _tile(k, MM_TILE_K)
    o_spec = pl.BlockSpec((tm, tn), lambda i, j, kk: (i, j))
    return _mm(name, a, w, dn=NN, grid=(m // tm, n // tn, k // tk),
               a_spec=pl.BlockSpec((tm, tk), lambda i, j, kk: (i, kk)),
               b_spec=pl.BlockSpec((tk, tn), lambda i, j, kk: (kk, j)),
               o_specs=[o_spec] * len(out_dtypes),
               out_shapes=[jax.ShapeDtypeStruct((m, n), d) for d in out_dtypes], acc_shape=(tm, tn), epi=epi)


def _mm_nn_col(name, a, w, out_dtypes, epi=None):
    m, k = a.shape
    c = w.shape[2]
    tm, tn, tk = _tile(m, MM_TILE_M), _tile(c, MM_TILE_N), _tile(k, MM_TILE_K)
    nb = c // tn
    o_spec = pl.BlockSpec((tm, tn), lambda i, j, kk: (i, j))
    return _mm(name, a, w, dn=NN, grid=(m // tm, N_CHIPS * nb, k // tk),
               a_spec=pl.BlockSpec((tm, tk), lambda i, j, kk: (i, kk)),
               b_spec=pl.BlockSpec((None, tk, tn), lambda i, j, kk: (j // nb, kk, j % nb)),
               o_specs=[o_spec] * len(out_dtypes),
               out_shapes=[jax.ShapeDtypeStruct((m, N_CHIPS * c), d) for d in out_dtypes], acc_shape=(tm, tn),
               epi=epi)


def _mm_nt_row(name, dy, w, out_dtype, epi=None, extra=None):
    m, n = dy.shape
    k = w.shape[0]
    tm, tn, tk = _tile(m, MM_TILE_M), _tile(k, MM_TILE_N), _tile(n, MM_TILE_K)
    o_spec = pl.BlockSpec((tm, tn), lambda i, j, kk: (i, j))
    return _mm(name, dy, w, dn=NT, grid=(m // tm, k // tn, n // tk),
               a_spec=pl.BlockSpec((tm, tk), lambda i, j, kk: (i, kk)),
               b_spec=pl.BlockSpec((tn, tk), lambda i, j, kk: (j, kk)),
               o_specs=[o_spec], out_shapes=[jax.ShapeDtypeStruct((m, k), out_dtype)], acc_shape=(tm, tn),
               epi=epi, extra=extra, extra_spec=o_spec)[0]


def _mm_nt_col(name, dy, w, out_dtype):
    m = dy.shape[0]
    k, c = w.shape[1], w.shape[2]
    tm, tn, tk = _tile(m, MM_TILE_M), _tile(k, MM_TILE_N), _tile(c, MM_TILE_K)
    kb = c // tk
    return _mm(name, dy, w, dn=NT, grid=(m // tm, k // tn, N_CHIPS * kb),
               a_spec=pl.BlockSpec((tm, tk), lambda i, j, kk: (i, kk)),
               b_spec=pl.BlockSpec((None, tn, tk), lambda i, j, kk: (kk // kb, j, kk % kb)),
               o_specs=[pl.BlockSpec((tm, tn), lambda i, j, kk: (i, j))],
               out_shapes=[jax.ShapeDtypeStruct((m, k), out_dtype)], acc_shape=(tm, tn))[0]


def _mm_tn_row(name, a, dy):
    t, m = a.shape
    n = dy.shape[1]
    tm, tn, tk = _tile(m, MM_TILE_M), _tile(n, MM_TILE_N), _tile(t, MM_TILE_K)
    return _mm(name, a, dy, dn=TN, grid=(m // tm, n // tn, t // tk),
               a_spec=pl.BlockSpec((tk, tm), lambda i, j, kk: (kk, i)),
               b_spec=pl.BlockSpec((tk, tn), lambda i, j, kk: (kk, j)),
               o_specs=[pl.BlockSpec((tm, tn), lambda i, j, kk: (i, j))],
               out_shapes=[jax.ShapeDtypeStruct((m, n), F32)], acc_shape=(tm, tn))[0]


def _mm_tn_col(name, a, dy):
    t, m = a.shape
    c = dy.shape[1] // N_CHIPS
    tm, tn, tk = _tile(m, MM_TILE_M), _tile(c, MM_TILE_N), _tile(t, MM_TILE_K)
    nb = c // tn
    return _mm(name, a, dy, dn=TN, grid=(m // tm, N_CHIPS * nb, t // tk),
               a_spec=pl.BlockSpec((tk, tm), lambda i, j, kk: (kk, i)),
               b_spec=pl.BlockSpec((tk, tn), lambda i, j, kk: (kk, j)),
               o_specs=[pl.BlockSpec((None, tm, tn), lambda i, j, kk: (j // nb, i, j % nb))],
               out_shapes=[jax.ShapeDtypeStruct((N_CHIPS, m, c), F32)], acc_shape=(tm, tn))[0]


def _rms(x, g):
    r = lax.rsqrt(jnp.mean(x * x, axis=-1, keepdims=True) + RMS_EPS)
    return x * r * g


def _rms_bwd(x, g, dy):
    r = lax.rsqrt(jnp.mean(x * x, axis=-1, keepdims=True) + RMS_EPS)
    xr = x * r
    dyg = dy * g
    dx = r * (dyg - xr * jnp.mean(dyg * xr, axis=-1, keepdims=True))
    return dx, jnp.sum(dy * xr, axis=0, keepdims=True)


def _norm_fwd(name, x, o, g_post, g_next):
    s, d = x.shape
    tr = _tile(s, ROW_TILE)
    has_prev, has_next = o is not None, g_next is not None
    row = pl.BlockSpec((tr, d), lambda i: (i, 0))
    vec = pl.BlockSpec((1, d), lambda i: (0, 0))

    def body(*refs):
        refs = list(refs)
        xn = refs.pop(0)[...]
        if has_prev:
            o_ref, gp_ref = refs.pop(0), refs.pop(0)
            xn = xn + _rms(o_ref[...], gp_ref[...])
        gn_ref = refs.pop(0) if has_next else None
        if has_prev:
            refs.pop(0)[...] = xn
        if has_next:
            refs.pop(0)[...] = _rms(xn, gn_ref[...]).astype(BF16)

    ins, in_specs = [x], [row]
    if has_prev:
        ins += [o, g_post]
        in_specs += [row, vec]
    if has_next:
        ins.append(g_next)
        in_specs.append(vec)
    out_shapes, out_specs = [], []
    if has_prev:
        out_shapes.append(jax.ShapeDtypeStruct((s, d), F32))
        out_specs.append(row)
    if has_next:
        out_shapes.append(jax.ShapeDtypeStruct((s, d), BF16))
        out_specs.append(row)
    outs = pl.pallas_call(body, name=name, grid=(s // tr,), in_specs=in_specs, out_specs=out_specs,
                          out_shape=out_shapes, compiler_params=_params(("parallel",)))(*ins)
    outs = list(outs)
    x_new = outs.pop(0) if has_prev else x
    h = outs.pop(0) if has_next else None
    return x_new, h


def _norm_bwd(name, dxn, dh, xn, o, g_post, g_next):
    s, d = xn.shape
    tr = _tile(s, ROW_TILE)
    has_prev, has_next, has_dxn = o is not None, dh is not None, dxn is not None
    row = pl.BlockSpec((tr, d), lambda i: (i, 0))
    vec = pl.BlockSpec((1, d), lambda i: (0, 0))

    def body(*refs):
        refs = list(refs)
        first = pl.program_id(0) == 0
        dxn_ref = refs.pop(0) if has_dxn else None
        dh_ref = refs.pop(0) if has_next else None
        xn_ref = refs.pop(0)
        if has_prev:
            o_ref, gp_ref = refs.pop(0), refs.pop(0)
        gn_ref = refs.pop(0) if has_next else None
        dx_ref = refs.pop(0)
        if has_prev:
            do_ref, dgp_ref = refs.pop(0), refs.pop(0)
        dgn_ref = refs.pop(0) if has_next else None

        def accumulate(ref, val):
            @pl.when(first)
            def _():
                ref[...] = val

            @pl.when(jnp.logical_not(first))
            def _():
                ref[...] += val

        dx = dxn_ref[...] if has_dxn else None
        if has_next:
            dxh, dgn = _rms_bwd(xn_ref[...], gn_ref[...], dh_ref[...].astype(F32))
            dx = dxh if dx is None else dx + dxh
            accumulate(dgn_ref, dgn)
        dx_ref[...] = dx
        if has_prev:
            do, dgp = _rms_bwd(o_ref[...], gp_ref[...], dx)
            do_ref[...] = do.astype(BF16)
            accumulate(dgp_ref, dgp)

    ins, in_specs = [], []
    if has_dxn:
        ins.append(dxn)
        in_specs.append(row)
    if has_next:
        ins.append(dh)
        in_specs.append(row)
    ins.append(xn)
    in_specs.append(row)
    if has_prev:
        ins += [o, g_post]
        in_specs += [row, vec]
    if has_next:
        ins.append(g_next)
        in_specs.append(vec)
    out_shapes, out_specs = [jax.ShapeDtypeStruct((s, d), F32)], [row]
    if has_prev:
        out_shapes += [jax.ShapeDtypeStruct((s, d), BF16), jax.ShapeDtypeStruct((1, d), F32)]
        out_specs += [row, vec]
    if has_next:
        out_shapes.append(jax.ShapeDtypeStruct((1, d), F32))
        out_specs.append(vec)
    outs = list(pl.pallas_call(body, name=name, grid=(s // tr,), in_specs=in_specs, out_specs=out_specs,
                               out_shape=out_shapes, compiler_params=_params(("arbitrary",)))(*ins))
    dx = outs.pop(0)
    do, dgp = (outs.pop(0), outs.pop(0)) if has_prev else (None, None)
    dgn = outs.pop(0) if has_next else None
    return dx, do, dgp, dgn


def _loss_head(y, target):
    s, d = y.shape
    tr = _tile(s, ROW_TILE)
    row = pl.BlockSpec((tr, d), lambda i: (i, 0))
    vec = pl.BlockSpec((1, d), lambda i: (0, 0))

    def body(y_ref, t_ref, dy_ref, l_ref):
        err = y_ref[...] - t_ref[...]
        dy_ref[...] = err * (1.0 / d)
        part = jnp.sum(err * err, axis=0, keepdims=True) * (0.5 / d)

        @pl.when(pl.program_id(0) == 0)
        def _():
            l_ref[...] = part

        @pl.when(pl.program_id(0) != 0)
        def _():
            l_ref[...] += part

    return pl.pallas_call(body, name="loss_head", grid=(s // tr,), in_specs=[row, row], out_specs=[row, vec],
                          out_shape=[jax.ShapeDtypeStruct((s, d), F32), jax.ShapeDtypeStruct((1, d), F32)],
                          compiler_params=_params(("arbitrary",)))(y, target)


@jax.custom_vjp
def _bdot(a, b):
    return jnp.dot(a.astype(BF16), b.astype(BF16), preferred_element_type=F32)


def _bdot_fwd(a, b):
    return _bdot(a, b), (a, b)


def _bdot_bwd(res, ct):
    a, b = res
    ctb = ct.astype(BF16)
    da = lax.dot_general(ctb, b.astype(BF16), NT, preferred_element_type=F32)
    db = lax.dot_general(a.astype(BF16), ctb, TN, preferred_element_type=F32)
    return da, db


_bdot.defvjp(_bdot_fwd, _bdot_bwd)


@functools.partial(jax.custom_vjp, nondiff_argnums=(1,))
def _shift(x, k):
    n = x.shape[0]
    if k == 0:
        return x
    rolled = pltpu.roll(x, k % n, 0)
    t = lax.broadcasted_iota(jnp.int32, x.shape, 0)
    keep = (t >= k) if k > 0 else (t < n + k)
    return jnp.where(keep, rolled, 0.0)


def _shift_fwd(x, k):
    return _shift(x, k), None


def _shift_bwd(k, _, ct):
    return (_shift(ct, -k),)


_shift.defvjp(_shift_fwd, _shift_bwd)


def _sigmoid(x):
    return 1.0 / (1.0 + jnp.exp(-x))


def _layer_norm(x, g, b=None):
    mu = jnp.mean(x, axis=-1, keepdims=True)
    xc = x - mu
    var = jnp.mean(xc * xc, axis=-1, keepdims=True)
    y = xc * lax.rsqrt(var + LN_EPS) * g
    return y if b is None else y + b


def _gmlp_chunk(zu, zv, gv, w, bcol):
    ch = w.shape[0]
    u = jax.nn.gelu(zu)
    vn = _layer_norm(jax.nn.gelu(zv), gv)
    t = lax.broadcasted_iota(jnp.int32, (ch, ch), 0)
    s = lax.broadcasted_iota(jnp.int32, (ch, ch), 1)
    wm = jnp.where(t >= s, w, 0.0)
    return u * (_bdot(wm, vn) + bcol)


def _gmlp_specs(seq, heads, hd, ch, u_off, v_off):
    col = lambda off: pl.BlockSpec((seq, hd), lambda h: (0, off + h))
    return (col(u_off), col(v_off), pl.BlockSpec((None, 1, hd), lambda h: (h, 0, 0)),
            pl.BlockSpec((None, ch, ch), lambda h: (h, 0, 0)), pl.BlockSpec((None, ch, 1), lambda h: (h, 0, 0)))


def _gmlp_fwd(z, gv, ws, bcol):
    seq = z.shape[0]
    heads, _, hd = gv.shape
    ch = ws.shape[-1]
    zu_s, zv_s, gv_s, w_s, b_s = _gmlp_specs(seq, heads, hd, ch, 0, heads)

    def body(zu_ref, zv_ref, gv_ref, w_ref, b_ref, y_ref):
        gvv, w, bc = gv_ref[...], w_ref[...], b_ref[...]

        def step(c, carry):
            rows = pl.ds(pl.multiple_of(c * ch, ch), ch)
            y_ref[rows, :] = _gmlp_chunk(zu_ref[rows, :], zv_ref[rows, :], gvv, w, bc).astype(BF16)
            return carry

        lax.fori_loop(0, seq // ch, step, 0)

    return pl.pallas_call(body, name="gmlp_fwd", grid=(heads,), in_specs=[zu_s, zv_s, gv_s, w_s, b_s],
                          out_specs=pl.BlockSpec((seq, hd), lambda h: (0, h)),
                          out_shape=jax.ShapeDtypeStruct((seq, heads * hd), BF16),
                          compiler_params=_params(("parallel",)))(z, z, gv, ws, bcol)


def _gmlp_bwd(z, dy, gv, ws, bcol):
    seq = z.shape[0]
    heads, _, hd = gv.shape
    ch = ws.shape[-1]
    zu_s, zv_s, gv_s, w_s, b_s = _gmlp_specs(seq, heads, hd, ch, 0, heads)
    col = pl.BlockSpec((seq, hd), lambda h: (0, h))

    def body(zu_ref, zv_ref, dy_ref, gv_ref, w_ref, b_ref, dzu_ref, dzv_ref, dgv_ref, dw_ref, db_ref):
        gvv, w, bc = gv_ref[...], w_ref[...], b_ref[...]

        def step(c, carry):
            dgv, dw, db = carry
            rows = pl.ds(pl.multiple_of(c * ch, ch), ch)
            _, vjp = jax.vjp(_gmlp_chunk, zu_ref[rows, :], zv_ref[rows, :], gvv, w, bc)
            dzu, dzv, dgv_c, dw_c, db_c = vjp(dy_ref[rows, :])
            dzu_ref[rows, :] = dzu.astype(BF16)
            dzv_ref[rows, :] = dzv.astype(BF16)
            return dgv + dgv_c, dw + dw_c, db + db_c

        zero = (jnp.zeros((1, hd), F32), jnp.zeros((ch, ch), F32), jnp.zeros((ch, 1), F32))
        dgv, dw, db = lax.fori_loop(0, seq // ch, step, zero)
        dgv_ref[...] = dgv
        dw_ref[...] = dw
        db_ref[...] = db

    return pl.pallas_call(
        body, name="gmlp_bwd", grid=(heads,), in_specs=[zu_s, zv_s, col, gv_s, w_s, b_s],
        out_specs=[col, col, gv_s, w_s, b_s],
        out_shape=[jax.ShapeDtypeStruct((seq, heads * hd), BF16), jax.ShapeDtypeStruct((seq, heads * hd), BF16),
                   jax.ShapeDtypeStruct(gv.shape, F32), jax.ShapeDtypeStruct(ws.shape, F32),
                   jax.ShapeDtypeStruct(bcol.shape, F32)],
        compiler_params=_params(("parallel",)))(z, z, dy, gv, ws, bcol)


def _pool_group(p, w, s, window):
    win, span = p, 1
    while span < window:
        win = win + _shift(win, span)
        span *= 2
    t = lax.broadcasted_iota(jnp.int32, (p.shape[0], 1), 0).astype(F32)
    cnt = jnp.minimum(t + 1.0, float(window))
    return _bdot(win / cnt - p, w) * s


def _pool_fwd(z, w_pool, s_pool, col_block):
    seq = z.shape[0]
    groups, gw, _ = w_pool.shape
    pw = groups * gw

    def body(p_ref, w_ref, s_ref, y_ref):
        for g in range(groups):
            cols = slice(g * gw, (g + 1) * gw)
            y_ref[:, cols] = _pool_group(p_ref[:, cols], w_ref[g], s_ref[:, cols], POOL_WINDOWS[g]).astype(BF16)

    return pl.pallas_call(
        body, name="pool_fwd", grid=(1,),
        in_specs=[pl.BlockSpec((seq, pw), lambda i: (0, col_block)),
                  pl.BlockSpec((groups, gw, gw), lambda i: (0, 0, 0)), pl.BlockSpec((1, pw), lambda i: (0, 0))],
        out_specs=pl.BlockSpec((seq, pw), lambda i: (0, 0)), out_shape=jax.ShapeDtypeStruct((seq, pw), BF16),
        compiler_params=_params(("arbitrary",)))(z, w_pool, s_pool)


def _pool_bwd(z, dy, w_pool, s_pool, col_block, dy_block):
    seq = z.shape[0]
    groups, gw, _ = w_pool.shape
    pw = groups * gw

    def body(p_ref, dy_ref, w_ref, s_ref, dp_ref, dw_ref, ds_ref):
        for g in range(groups):
            cols = slice(g * gw, (g + 1) * gw)
            _, vjp = jax.vjp(functools.partial(_pool_group, window=POOL_WINDOWS[g]), p_ref[:, cols], w_ref[g],
                             s_ref[:, cols])
            dp, dw, ds = vjp(dy_ref[:, cols])
            dp_ref[:, cols] = dp.astype(BF16)
            dw_ref[g] = dw
            ds_ref[:, cols] = ds

    return pl.pallas_call(
        body, name="pool_bwd", grid=(1,),
        in_specs=[pl.BlockSpec((seq, pw), lambda i: (0, col_block)),
                  pl.BlockSpec((seq, pw), lambda i: (0, dy_block)),
                  pl.BlockSpec((groups, gw, gw), lambda i: (0, 0, 0)), pl.BlockSpec((1, pw), lambda i: (0, 0))],
        out_specs=[pl.BlockSpec((seq, pw), lambda i: (0, 0)), pl.BlockSpec((groups, gw, gw), lambda i: (0, 0, 0)),
                   pl.BlockSpec((1, pw), lambda i: (0, 0))],
        out_shape=[jax.ShapeDtypeStruct((seq, pw), BF16), jax.ShapeDtypeStruct(w_pool.shape, F32),
                   jax.ShapeDtypeStruct((1, pw), F32)],
        compiler_params=_params(("arbitrary",)))(z, dy, w_pool, s_pool)


def _conv_fwd(z, w_dw, taps, b_dw, val_block, gate_block):
    seq = z.shape[0]
    rows, cb = w_dw.shape[1], w_dw.shape[2]

    def body(val_ref, gate_ref, w_ref, b_ref, out_ref):
        h = val_ref[...] * _sigmoid(gate_ref[...])
        acc = jnp.broadcast_to(b_ref[...], h.shape)
        for d in range(taps):
            acc = acc + w_ref[pl.ds(taps - 1 - d, 1), :] * _shift(h, d)
        out_ref[...] = acc

    return pl.pallas_call(
        body, name="conv_fwd", grid=(N_CHIPS,),
        in_specs=[pl.BlockSpec((seq, cb), lambda j: (0, val_block + j)),
                  pl.BlockSpec((seq, cb), lambda j: (0, gate_block + j)),
                  pl.BlockSpec((None, rows, cb), lambda j: (j, 0, 0)),
                  pl.BlockSpec((1, cb), lambda j: (0, j))],
        out_specs=pl.BlockSpec((seq, cb), lambda j: (0, j)),
        out_shape=jax.ShapeDtypeStruct((seq, N_CHIPS * cb), F32),
        compiler_params=_params(("parallel",)))(z, z, w_dw, b_dw)


def _conv_bwd(z, dout, w_dw, taps, val_block, gate_block):
    seq = z.shape[0]
    rows, cb = w_dw.shape[1], w_dw.shape[2]
    col = pl.BlockSpec((seq, cb), lambda j: (0, j))

    def body(val_ref, gate_ref, do_ref, w_ref, dval_ref, dgate_ref, dw_ref, db_ref):
        val, sg, do = val_ref[...], _sigmoid(gate_ref[...]), do_ref[...]
        h = val * sg
        db_ref[...] = jnp.sum(do, axis=0, keepdims=True)
        dh = jnp.zeros_like(h)
        for d in range(taps):
            k = taps - 1 - d
            dw_ref[pl.ds(k, 1), :] = jnp.sum(do * _shift(h, d), axis=0, keepdims=True)
            dh = dh + w_ref[pl.ds(k, 1), :] * _shift(do, -d)
        dval_ref[...] = (dh * sg).astype(BF16)
        dgate_ref[...] = (dh * val * sg * (1.0 - sg)).astype(BF16)

    return pl.pallas_call(
        body, name="conv_bwd", grid=(N_CHIPS,),
        in_specs=[pl.BlockSpec((seq, cb), lambda j: (0, val_block + j)),
                  pl.BlockSpec((seq, cb), lambda j: (0, gate_block + j)), col,
                  pl.BlockSpec((None, rows, cb), lambda j: (j, 0, 0))],
        out_specs=[col, col, pl.BlockSpec((taps, cb), lambda j: (0, j)), pl.BlockSpec((1, cb), lambda j: (0, j))],
        out_shape=[jax.ShapeDtypeStruct((seq, N_CHIPS * cb), BF16), jax.ShapeDtypeStruct((seq, N_CHIPS * cb), BF16),
                   jax.ShapeDtypeStruct((taps, N_CHIPS * cb), F32), jax.ShapeDtypeStruct((1, N_CHIPS * cb), F32)],
        compiler_params=_params(("parallel",)))(z, z, dout, w_dw)


def _ln_swish(hc, g, b):
    y = _layer_norm(hc, g, b)
    return y * _sigmoid(y)


def _ln_swish_fwd(hc, g, b):
    s, cw = hc.shape
    tr = _tile(s, ROW_TILE)
    row = pl.BlockSpec((tr, cw), lambda i: (i, 0))
    vec = pl.BlockSpec((1, cw), lambda i: (0, 0))

    def body(h_ref, g_ref, b_ref, y_ref):
        y_ref[...] = _ln_swish(h_ref[...], g_ref[...], b_ref[...]).astype(BF16)

    return pl.pallas_call(body, name="ln_swish_fwd", grid=(s // tr,), in_specs=[row, vec, vec], out_specs=row,
                          out_shape=jax.ShapeDtypeStruct((s, cw), BF16),
                          compiler_params=_params(("parallel",)))(hc, g, b)


def _ln_swish_bwd(hc, dy, g, b, dy_block):
    s, cw = hc.shape
    tr = _tile(s, ROW_TILE)
    row = pl.BlockSpec((tr, cw), lambda i: (i, 0))
    vec = pl.BlockSpec((1, cw), lambda i: (0, 0))

    def body(h_ref, dy_ref, g_ref, b_ref, dh_ref, dg_ref, db_ref):
        _, vjp = jax.vjp(_ln_swish, h_ref[...], g_ref[...], b_ref[...])
        dh, dg, db = vjp(dy_ref[...])
        dh_ref[...] = dh

        @pl.when(pl.program_id(0) == 0)
        def _():
            dg_ref[...] = dg
            db_ref[...] = db

        @pl.when(pl.program_id(0) != 0)
        def _():
            dg_ref[...] += dg
            db_ref[...] += db

    return pl.pallas_call(
        body, name="ln_swish_bwd", grid=(s // tr,),
        in_specs=[row, pl.BlockSpec((tr, cw), lambda i: (i, dy_block)), vec, vec], out_specs=[row, vec, vec],
        out_shape=[jax.ShapeDtypeStruct((s, cw), F32), jax.ShapeDtypeStruct((1, cw), F32),
                   jax.ShapeDtypeStruct((1, cw), F32)],
        compiler_params=_params(("arbitrary",)))(hc, dy, g, b)


def _attn_probs(q, k, scale):
    s = lax.dot_general(q, k, NT, preferred_element_type=F32) * scale
    e = jnp.exp(s - jnp.max(s, axis=-1, keepdims=True))
    return e / jnp.sum(e, axis=-1, keepdims=True)


def _attn_fwd(q, k, v):
    seq, d = q.shape
    mem = k.shape[0]
    hd = d // XATTN_HEADS
    scale = hd ** -0.5
    qs = pl.BlockSpec((seq, hd), lambda h: (0, h))
    ms = pl.BlockSpec((mem, hd), lambda h: (0, h))

    def body(q_ref, k_ref, v_ref, a_ref):
        p = _attn_probs(q_ref[...], k_ref[...], scale)
        a_ref[...] = jnp.dot(p.astype(BF16), v_ref[...], preferred_element_type=F32).astype(BF16)

    return pl.pallas_call(body, name="attn_fwd", grid=(XATTN_HEADS,), in_specs=[qs, ms, ms], out_specs=qs,
                          out_shape=jax.ShapeDtypeStruct((seq, d), BF16),
                          compiler_params=_params(("parallel",)))(q, k, v)


def _attn_bwd(q, k, v, da):
    seq, d = q.shape
    mem = k.shape[0]
    hd = d // XATTN_HEADS
    scale = hd ** -0.5
    qs = pl.BlockSpec((seq, hd), lambda h: (0, h))
    ms = pl.BlockSpec((mem, hd), lambda h: (0, h))

    def body(q_ref, k_ref, v_ref, da_ref, dq_ref, dk_ref, dv_ref):
        q_, k_, v_, da_ = q_ref[...], k_ref[...], v_ref[...], da_ref[...]
        p = _attn_probs(q_, k_, scale)
        dv_ref[...] = lax.dot_general(p.astype(BF16), da_, TN, preferred_element_type=F32).astype(BF16)
        dp = lax.dot_general(da_, v_, NT, preferred_element_type=F32)
        ds = (p * (dp - jnp.sum(dp * p, axis=-1, keepdims=True)) * scale).astype(BF16)
        dq_ref[...] = jnp.dot(ds, k_, preferred_element_type=F32).astype(BF16)
        dk_ref[...] = lax.dot_general(ds, q_, TN, preferred_element_type=F32).astype(BF16)

    return pl.pallas_call(
        body, name="attn_bwd", grid=(XATTN_HEADS,), in_specs=[qs, ms, ms, qs], out_specs=[qs, ms, ms],
        out_shape=[jax.ShapeDtypeStruct((seq, d), BF16), jax.ShapeDtypeStruct((mem, d), BF16),
                   jax.ShapeDtypeStruct((mem, d), BF16)],
        compiler_params=_params(("parallel",)))(q, k, v, da)


def _place_shard(name, place, w, l, dtype):
    _, r, c = w.shape
    tr = _tile(r, 2 * ROW_TILE) if r % 16 == 0 else r

    def body(place_ref, w_ref, o_ref):
        o_ref[...] = w_ref[...].astype(dtype)

    return pl.pallas_call(
        body, name=name,
        grid_spec=pltpu.PrefetchScalarGridSpec(
            num_scalar_prefetch=1, grid=(r // tr,),
            in_specs=[pl.BlockSpec((None, tr, c), lambda i, p: (l, i, 0))],
            out_specs=pl.BlockSpec((None, tr, c), lambda i, p: (p[1], i, 0))),
        out_shape=jax.ShapeDtypeStruct((N_CHIPS, r, c), dtype),
        compiler_params=_params(("parallel",)))(place, w)


def _pair_sum(name, place, dw, got):
    n, _, r2, c = dw.shape
    tr = _tile(r2, ROW_TILE)

    def body(place_ref, own_ref, got_ref, s_ref, t_ref):
        val = (own_ref[...] + got_ref[...]).astype(BF16)
        s_ref[...] = val

        @pl.when(pl.program_id(1) == place_ref[1])
        def _():
            t_ref[...] = val

    slab = pl.BlockSpec((None, tr, c), lambda i, j, p: (j, i, 0))
    sds = jax.ShapeDtypeStruct((n, r2, c), BF16)
    return pl.pallas_call(
        body, name=name,
        grid_spec=pltpu.PrefetchScalarGridSpec(
            num_scalar_prefetch=1, grid=(r2 // tr, n),
            in_specs=[pl.BlockSpec((None, None, tr, c), lambda i, j, p: (j, p[0], i, 0)), slab],
            out_specs=[slab, pl.BlockSpec((None, tr, c), lambda i, j, p: (p[1], i, 0))]),
        out_shape=[sds, sds], compiler_params=_params(("parallel", "arbitrary")))(place, dw, got)


def _chip_sum(name, place, parts):
    n, r2, c = parts.shape
    tr = _tile(r2, ROW_TILE)

    def body(place_ref, *refs):
        o_ref = refs[n]
        acc = refs[0][...].astype(F32)
        for j in range(1, n):
            acc = acc + refs[j][...].astype(F32)
        o_ref[...] = acc

    part = lambda j: pl.BlockSpec((None, tr, c), lambda i, p: (j, i, 0))
    return pl.pallas_call(
        body, name=name,
        grid_spec=pltpu.PrefetchScalarGridSpec(
            num_scalar_prefetch=1, grid=(r2 // tr,), in_specs=[part(j) for j in range(n)],
            out_specs=pl.BlockSpec((None, tr, c), lambda i, p: (p[0], i, 0))),
        out_shape=jax.ShapeDtypeStruct((2, r2, c), F32),
        compiler_params=_params(("parallel",)))(place, *([parts] * n))


def _device_sum(parts):
    n, rows, lanes = parts.shape
    tr = _tile(rows, 4 * ROW_TILE) if rows % 8 == 0 else rows

    def body(p_ref, o_ref):
        acc = p_ref[0]
        for j in range(1, n):
            acc = acc + p_ref[j]
        o_ref[...] = acc

    return pl.pallas_call(
        body, name="device_sum", grid=(rows // tr,), in_specs=[pl.BlockSpec((n, tr, lanes), lambda i: (0, i, 0))],
        out_specs=pl.BlockSpec((tr, lanes), lambda i: (i, 0)), out_shape=jax.ShapeDtypeStruct((rows, lanes), F32),
        compiler_params=_params(("parallel",)))(parts)


def _adam_update(w, g, m, v):
    nm = ADAM_B1 * m + (1.0 - ADAM_B1) * g
    nv = ADAM_B2 * v + (1.0 - ADAM_B2) * (g * g)
    c1 = 1.0 - ADAM_B1 ** ADAM_STEP
    c2 = 1.0 - ADAM_B2 ** ADAM_STEP
    return -ADAM_LR * ((nm / c1) / (jnp.sqrt(nv / c2) + ADAM_EPS) + ADAM_WD * w), nm, nv


def _adamw_layer(name, l, w, g, m, v, prev):
    n_l, r, c = w.shape
    tr = _tile(r, ROW_TILE)
    slab = pl.BlockSpec((None, tr, c), lambda i: (l, i, 0))

    def body(w_ref, g_ref, m_ref, v_ref, *refs):
        go_ref, d_ref, nm_ref, nv_ref = refs[-4:]
        g_ = g_ref[...]
        delta, nm, nv = _adam_update(w_ref[...], g_, m_ref[...], v_ref[...])
        go_ref[...] = g_
        d_ref[...] = delta
        nm_ref[...] = nm
        nv_ref[...] = nv

    ins = [w, g, m, v]
    in_specs = [slab, pl.BlockSpec((tr, c), lambda i: (i, 0)), slab, slab]
    aliases = {}
    if prev is not None:
        aliases = {len(ins) + i: i for i in range(4)}
        ins += list(prev)
        in_specs += [ANY] * 4
    sds = jax.ShapeDtypeStruct((n_l, r, c), F32)
    return pl.pallas_call(body, name=name, grid=(r // tr,), in_specs=in_specs, out_specs=[slab] * 4,
                          out_shape=[sds] * 4, input_output_aliases=aliases,
                          compiler_params=_params(("parallel",)))(*ins)


def _adamw_flat(name, w, g, m, v):
    rows, cols = w.shape
    tr = _tile(rows, ROW_TILE) if rows % 8 == 0 else rows
    spec = pl.BlockSpec((tr, cols), lambda i: (i, 0))

    def body(w_ref, g_ref, m_ref, v_ref, d_ref, nm_ref, nv_ref):
        d_ref[...], nm_ref[...], nv_ref[...] = _adam_update(w_ref[...], g_ref[...], m_ref[...], v_ref[...])

    sds = jax.ShapeDtypeStruct((rows, cols), F32)
    return pl.pallas_call(body, name=name, grid=(rows // tr,), in_specs=[spec] * 4, out_specs=[spec] * 3,
                          out_shape=[sds] * 3, compiler_params=_params(("parallel",)))(w, g, m, v)


def _me():
    return lax.axis_index("x"), lax.axis_index("y"), lax.axis_index("c")


def _other_chips(x, y):
    return [(1 - x, y, 2 * (1 - x) + y), (x, 1 - y, 2 * x + 1 - y), (1 - x, 1 - y, 2 * (1 - x) + 1 - y)]


def _remote(src, dst, send_sem, recv_sem, target):
    return pltpu.make_async_remote_copy(src_ref=src, dst_ref=dst, send_sem=send_sem, recv_sem=recv_sem,
                                        device_id=target, device_id_type=MESH)


def _exchange(name, bufs, plan, n_copies):
    n = len(bufs)

    def body(*refs):
        send_sems, recv_sems = refs[2 * n:]
        copies = []
        for i, (src, dst, target) in enumerate(plan(refs[n:2 * n], _me())):
            if target is None:
                cp = pltpu.make_async_copy(src, dst, send_sems.at[i])
            else:
                cp = _remote(src, dst, send_sems.at[i], recv_sems.at[i], target)
            cp.start()
            copies.append((cp, target))
        assert len(copies) == n_copies
        for cp, target in copies:
            if target is None:
                cp.wait()
            else:
                cp.wait_recv()
        for cp, target in copies:
            if target is not None:
                cp.wait_send()

    return pl.pallas_call(
        body, name=name, in_specs=[ANY] * n, out_specs=[ANY] * n,
        out_shape=[jax.ShapeDtypeStruct(b.shape, b.dtype) for b in bufs],
        scratch_shapes=[pltpu.SemaphoreType.DMA((n_copies,)), pltpu.SemaphoreType.DMA((n_copies,))],
        input_output_aliases={i: i for i in range(n)},
        compiler_params=pltpu.CompilerParams(has_side_effects=True))(*bufs)


def _start_copies(name, groups):
    all_bufs = [b for bufs, _, _ in groups for b in bufs]
    n = len(all_bufs)
    n_g = len(groups)

    def body(*refs):
        in_refs, sem_refs = refs[:n], refs[n:n + 2 * n_g]
        pos = 0
        for g, (bufs, plan, n_copies) in enumerate(groups):
            copies = plan(in_refs[pos:pos + len(bufs)], _me())
            assert len(copies) == n_copies
            for i, (src, dst, target) in enumerate(copies):
                _remote(src, dst, sem_refs[2 * g].at[i], sem_refs[2 * g + 1].at[i], target).start()
            pos += len(bufs)

    sems = []
    for _, _, n_copies in groups:
        sems += [pltpu.SemaphoreType.DMA((n_copies,))] * 2
    outs = pl.pallas_call(
        body, name=name, in_specs=[HBM] * n, out_specs=[SEM] * (2 * n_g) + [HBM] * n,
        out_shape=sems + [pltpu.HBM(b.shape, b.dtype) for b in all_bufs],
        input_output_aliases={i: 2 * n_g + i for i in range(n)},
        compiler_params=pltpu.CompilerParams(has_side_effects=pltpu.SideEffectType.DATAFLOW_SIDE_EFFECTING))(
            *[pltpu.with_memory_space_constraint(b, pltpu.HBM) for b in all_bufs])
    result, pos = [], 2 * n_g
    for g, (bufs, _, _) in enumerate(groups):
        result.append((outs[2 * g], outs[2 * g + 1], list(outs[pos:pos + len(bufs)])))
        pos += len(bufs)
    return result


def _wait_copies(name, started, plan, n_copies, after):
    send_sems, recv_sems, bufs = started
    n = len(bufs)

    def body(*refs):
        copies = plan(refs[:n], _me())
        assert len(copies) == n_copies
        for i, (src, dst, target) in enumerate(copies):
            cp = _remote(src, dst, refs[n].at[i], refs[n + 1].at[i], target)
            cp.wait_send()
            cp.wait_recv()

    return list(pl.pallas_call(
        body, name=name, in_specs=[HBM] * n + [SEM, SEM, ANY], out_specs=[HBM] * n,
        out_shape=[pltpu.HBM(b.shape, b.dtype) for b in bufs], input_output_aliases={i: i for i in range(n)},
        compiler_params=pltpu.CompilerParams(has_side_effects=pltpu.SideEffectType.DATAFLOW_SIDE_EFFECTING))(
            *bufs, send_sems, recv_sems, after))


def _halves(a):
    return a.reshape(a.shape[0], 2, a.shape[1] // 2, a.shape[2])


def _gather_plan(refs, me):
    x, y, c = me
    mine = 2 * x + y
    return [(g.at[mine, c], g.at[mine, c], (px, py, c)) for g in refs for px, py, _ in _other_chips(x, y)]


def _forward_plan(refs, me):
    x, y, c = me
    return [(g.at[chip, c], g.at[chip, c], (x, y, 1 - c)) for g in refs for _, _, chip in _other_chips(x, y)]


def _swap_plan(refs, me):
    x, y, c = me
    k = len(refs) // 2
    return [(refs[i].at[j, 1 - c], refs[k + i].at[j], (x, y, 1 - c)) for i in range(k) for j in range(N_CHIPS)]


def _scatter_plan(refs, me):
    x, y, c = me
    mine = 2 * x + y
    k = len(refs) // 2
    return [(refs[i].at[chip], refs[k + i].at[mine], (px, py, c))
            for i in range(k) for px, py, chip in _other_chips(x, y)]


def _share_plan(refs, me):
    x, y, c = me
    return [(g.at[c], g.at[c], (x, y, 1 - c)) for g in refs]


def _gather_devices(flat):
    def plan(refs, me):
        x, y, c = me
        mine = 4 * x + 2 * y + c
        src, dst = refs
        copies = [(src, dst.at[mine], None)]
        for fx, fy, fc in [(0, 0, 1), (0, 1, 0), (0, 1, 1), (1, 0, 0), (1, 0, 1), (1, 1, 0), (1, 1, 1)]:
            peer = (x + fx - 2 * fx * x, y + fy - 2 * fy * y, c + fc - 2 * fc * c)
            copies.append((src, dst.at[mine], peer))
        return copies

    landing = lax.empty((N_DEVICES,) + flat.shape, flat.dtype)
    return _exchange("gather_devices", [flat, landing], plan, N_DEVICES)[1]


BIG = ("w_in", "w_out", "w_q", "w_k", "w_v", "w_o", "w_up", "w_down")
COLUMN_SPLIT = ("w_in", "w_up")
WEIGHTS = ("norm_mix_pre", "norm_mix_post", "w_in", "w_out", "gmlp_v_gain", "w_spatial", "b_spatial", "w_pool",
           "s_pool", "w_dw", "b_dw", "conv_ln_g", "conv_ln_b", "norm_xattn_pre", "norm_mem", "norm_xattn_post",
           "w_q", "w_k", "w_v", "w_o", "norm_ffn_pre", "norm_ffn_post", "w_up", "w_down")
SMALL = tuple(n for n in WEIGHTS if n not in BIG)
GATHER_GROUPS = (("w_in", "w_dw"), ("w_out", "w_q", "w_k", "w_v", "w_o"), ("w_up", "w_down"))


def _relu2(acc):
    r = jnp.maximum(acc, 0.0)
    return acc, r * r


def _relu2_bwd(acc, up):
    return (acc * (2.0 * jnp.maximum(up, 0.0)),)


def _pack(arrays):
    flat = jnp.concatenate([a.reshape(-1) for a in arrays])
    tile = 8 * V7X_LANES
    pad = (-flat.shape[0]) % tile
    return jnp.pad(flat, (0, pad)).reshape(-1, V7X_LANES)


def _unpack(packed, like):
    flat = packed.reshape(-1)
    out, pos = [], 0
    for a in like:
        out.append(flat[pos:pos + a.size].reshape(a.shape))
        pos += a.size
    return out


class _GradientReducer:
    def __init__(self, place, w, m, v):
        self.place, self.w, self.m, self.v = place, w, m, v
        self.flying = []
        self.done = {n: None for n in BIG}

    def add(self, tag, l, grads):
        names = list(grads)
        views = [_halves(grads[n]) for n in names]
        zones = [lax.empty((v.shape[0],) + v.shape[2:], F32) for v in views]
        started = _start_copies("swap_start_" + tag, [(views + zones, _swap_plan, N_CHIPS * len(names))])[0]
        self.flying.append(dict(stage=0, tag=tag, l=l, names=names, started=started))

    def advance(self, after):
        for item in list(self.flying):
            tag, names, k = item["tag"], item["names"], len(item["names"])
            if item["stage"] == 0:
                bufs = _wait_copies("swap_wait_" + tag, item["started"], _swap_plan, N_CHIPS * k, after)
                sums, parts = zip(*[_pair_sum("pair_sum_" + n, self.place, dv, got)
                                    for n, dv, got in zip(names, bufs[:k], bufs[k:])])
                item["started"] = _start_copies("scatter_start_" + tag,
                                                [(list(sums) + list(parts), _scatter_plan, 3 * k)])[0]
                item["stage"] = 1
            else:
                bufs = _wait_copies("scatter_wait_" + tag, item["started"], _scatter_plan, 3 * k, after)
                halves = [_chip_sum("chip_sum_" + n, self.place, p) for n, p in zip(names, bufs[k:])]
                halves = _exchange("share_" + tag, halves, _share_plan, k)
                for n, h in zip(names, halves):
                    g = h.reshape(self.w[n].shape[1:])
                    self.done[n] = _adamw_layer("adamw_" + n, item["l"], self.w[n], g, self.m[n], self.v[n],
                                                self.done[n])
                self.flying.remove(item)

    def drain(self, after):
        while self.flying:
            self.advance(after)


def _step(x, mem, target, w, m, v):
    n_layers = w["w_in"].shape[0]
    seq, d = x.shape
    heads, hd = w["gmlp_v_gain"].shape[1:]
    gw = heads * hd
    groups, pgw = w["w_pool"].shape[1:3]
    pw = groups * pgw
    cw = w["b_dw"].shape[1]
    cb = cw // N_CHIPS
    taps = w["w_dw"].shape[1]
    cx, cy, cc = _me()
    chip = 2 * cx + cy
    place = jnp.stack([cc, chip]).astype(jnp.int32)
    vec = lambda name, l: w[name][l].reshape(1, -1)

    taps_padded = jnp.pad(w["w_dw"], ((0, 0), (0, (-taps) % 16), (0, 0)))
    plans = []
    for l in range(n_layers):
        for names in GATHER_GROUPS:
            bufs = [_halves(_place_shard("place_" + n, place, taps_padded if n == "w_dw" else w[n], l,
                                         F32 if n == "w_dw" else BF16)) for n in names]
            plans.append((bufs, _gather_plan, 3 * len(names)))
    gathering = _start_copies("gather_start", plans)

    def arrive(l, g, after):
        names = GATHER_GROUPS[g]
        tag = "%d%d" % (l, g)
        bufs = _wait_copies("gather_wait_" + tag, gathering[l * len(GATHER_GROUPS) + g], _gather_plan,
                            3 * len(names), after)
        bufs = _exchange("gather_forward_" + tag, bufs, _forward_plan, 3 * len(names))
        out = {}
        for n, b in zip(names, bufs):
            full = b.reshape(N_CHIPS, 2 * b.shape[2], b.shape[3])
            out[n] = full if n in COLUMN_SPLIT + ("w_dw",) else full.reshape(-1, full.shape[2])
        return out

    saved = []
    _, h1 = _norm_fwd("norm_first", x, None, None, vec("norm_mix_pre", 0))
    for l in range(n_layers):
        gv = w["gmlp_v_gain"][l].reshape(heads, 1, hd)
        ws = w["w_spatial"][l]
        bcol = w["b_spatial"][l].reshape(heads, -1, 1)
        wl = arrive(l, 0, h1)
        z = _mm_nn_col("mm_in", h1, wl["w_in"], [F32])[0]
        ya = _gmlp_fwd(z, gv, ws, bcol)
        yb = _pool_fwd(z, w["w_pool"][l], vec("s_pool", l), (2 * gw) // pw)
        hc = _conv_fwd(z, wl["w_dw"], taps, vec("b_dw", l), (2 * gw + pw) // cb, (2 * gw + pw + cw) // cb)
        yc = _ln_swish_fwd(hc, vec("conv_ln_g", l), vec("conv_ln_b", l))
        y = jnp.concatenate([ya, yb, yc], axis=1)
        wl.update(arrive(l, 1, y))
        o = _mm_nn_row("mm_out", y, wl["w_out"], [F32])[0]
        x1, h2 = _norm_fwd("norm_mix", x, o, vec("norm_mix_post", l), vec("norm_xattn_pre", l))
        _, mn = _norm_fwd("norm_mem", mem, None, None, vec("norm_mem", l))
        q = _mm_nn_row("mm_q", h2, wl["w_q"], [BF16])[0]
        k = _mm_nn_row("mm_k", mn, wl["w_k"], [BF16])[0]
        vv = _mm_nn_row("mm_v", mn, wl["w_v"], [BF16])[0]
        a = _attn_fwd(q, k, vv)
        o2 = _mm_nn_row("mm_o", a, wl["w_o"], [F32])[0]
        x2, h3 = _norm_fwd("norm_xattn", x1, o2, vec("norm_xattn_post", l), vec("norm_ffn_pre", l))
        wl.update(arrive(l, 2, h3))
        up, r = _mm_nn_col("mm_up", h3, wl["w_up"], [F32, BF16], epi=_relu2)
        o3 = _mm_nn_row("mm_down", r, wl["w_down"], [F32])[0]
        g_next = vec("norm_mix_pre", l + 1) if l + 1 < n_layers else None
        x3, h_next = _norm_fwd("norm_ffn", x2, o3, vec("norm_ffn_post", l), g_next)
        saved.append(dict(x=x, h1=h1, z=z, hc=hc, y=y, o=o, x1=x1, h2=h2, mn=mn, q=q, k=k, v=vv, a=a, o2=o2, x2=x2,
                          h3=h3, up=up, r=r, o3=o3, x3=x3, gv=gv, ws=ws, bcol=bcol, w=wl))
        x, h1 = x3, h_next

    dx, loss_parts = _loss_head(x, target)
    loss = lax.psum(jnp.sum(loss_parts), ("x", "y", "c"))

    reducer = _GradientReducer(place, w, m, v)
    small = {n: [None] * n_layers for n in SMALL}
    by_chip = lambda g: g.reshape(N_CHIPS, g.shape[0] // N_CHIPS, g.shape[1])
    dh = None
    for l in reversed(range(n_layers)):
        t = saved[l]
        wl = t["w"]
        g_next = vec("norm_mix_pre", l + 1) if l + 1 < n_layers else None
        dx, do3, dgp, dgn = _norm_bwd("norm_ffn_bwd", dx, dh, t["x3"], t["o3"], vec("norm_ffn_post", l), g_next)
        small["norm_ffn_post"][l] = dgp
        if dgn is not None:
            small["norm_mix_pre"][l + 1] = dgn
        reducer.advance(dx)
        d_down = _mm_tn_row("mm_down_dw", t["r"], do3)
        dup = _mm_nt_row("mm_down_dx", do3, wl["w_down"], BF16, epi=_relu2_bwd, extra=t["up"])
        d_up = _mm_tn_col("mm_up_dw", t["h3"], dup)
        dh3 = _mm_nt_col("mm_up_dx", dup, wl["w_up"], F32)
        reducer.add("%d0" % l, l, {"w_down": by_chip(d_down), "w_up": d_up})
        dx, do2, dgp, dgn = _norm_bwd("norm_xattn_bwd", dx, dh3, t["x2"], t["o2"], vec("norm_xattn_post", l),
                                      vec("norm_ffn_pre", l))
        small["norm_xattn_post"][l], small["norm_ffn_pre"][l] = dgp, dgn
        reducer.advance(dx)
        d_o = _mm_tn_row("mm_o_dw", t["a"], do2)
        da = _mm_nt_row("mm_o_dx", do2, wl["w_o"], BF16)
        dq, dk, dv = _attn_bwd(t["q"], t["k"], t["v"], da)
        d_q = _mm_tn_row("mm_q_dw", t["h2"], dq)
        d_k = _mm_tn_row("mm_k_dw", t["mn"], dk)
        d_v = _mm_tn_row("mm_v_dw", t["mn"], dv)
        reducer.add("%d1" % l, l, {"w_o": by_chip(d_o), "w_q": by_chip(d_q), "w_k": by_chip(d_k),
                                   "w_v": by_chip(d_v)})
        dh2 = _mm_nt_row("mm_q_dx", dq, wl["w_q"], F32)
        dmn = _mm_nt_row("mm_k_dx", dk, wl["w_k"], F32) + _mm_nt_row("mm_v_dx", dv, wl["w_v"], F32)
        _, _, _, small["norm_mem"][l] = _norm_bwd("norm_mem_bwd", None, dmn, mem, None, None, vec("norm_mem", l))
        dx, do, dgp, dgn = _norm_bwd("norm_mix_bwd", dx, dh2, t["x1"], t["o"], vec("norm_mix_post", l),
                                     vec("norm_xattn_pre", l))
        small["norm_mix_post"][l], small["norm_xattn_pre"][l] = dgp, dgn
        reducer.advance(dx)
        d_out = _mm_tn_row("mm_out_dw", t["y"], do)
        dy = _mm_nt_row("mm_out_dx", do, wl["w_out"], F32)
        dzu, dzv, dgv, dws, dbcol = _gmlp_bwd(t["z"], dy, t["gv"], t["ws"], t["bcol"])
        small["gmlp_v_gain"][l] = dgv.reshape(heads, hd)
        small["w_spatial"][l] = dws
        small["b_spatial"][l] = dbcol.reshape(heads, -1)
        dzp, dwp, dsp = _pool_bwd(t["z"], dy, w["w_pool"][l], vec("s_pool", l), (2 * gw) // pw, gw // pw)
        small["w_pool"][l], small["s_pool"][l] = dwp, dsp.reshape(-1)
        dhc, dlg, dlb = _ln_swish_bwd(t["hc"], dy, vec("conv_ln_g", l), vec("conv_ln_b", l), (gw + pw) // cw)
        small["conv_ln_g"][l], small["conv_ln_b"][l] = dlg.reshape(-1), dlb.reshape(-1)
        dval, dgate, dwd, dbd = _conv_bwd(t["z"], dhc, wl["w_dw"], taps, (2 * gw + pw) // cb,
                                          (2 * gw + pw + cw) // cb)
        small["w_dw"][l], small["b_dw"][l] = dwd, dbd.reshape(-1)
        dz = jnp.concatenate([dzu, dzv, dzp, dval, dgate], axis=1)
        reducer.advance(dz)
        d_in = _mm_tn_col("mm_in_dw", t["h1"], dz)
        dh = _mm_nt_col("mm_in_dx", dz, wl["w_in"], F32)
        reducer.add("%d2" % l, l, {"w_out": by_chip(d_out), "w_in": d_in})
    grad_x, _, _, dgn = _norm_bwd("norm_first_bwd", dx, dh, saved[0]["x"], None, None, vec("norm_mix_pre", 0))
    small["norm_mix_pre"][0] = dgn
    reducer.drain(grad_x)

    small_full = [jnp.stack([g.reshape(w[n].shape[1:]) if n != "w_dw" else g for g in small[n]]) for n in SMALL]
    total = _device_sum(_gather_devices(_pack(small_full)))
    grad = {}
    for n, g in zip(SMALL, _unpack(total, small_full)):
        grad[n] = lax.dynamic_slice_in_dim(g, chip * cb, cb, axis=2) if n == "w_dw" else g
    like = [w[n] for n in SMALL]
    dl, nm, nv = _adamw_flat("adamw_small", _pack(like), _pack([grad[n] for n in SMALL]),
                             _pack([m[n] for n in SMALL]), _pack([v[n] for n in SMALL]))
    delta, new_m, new_v = (dict(zip(SMALL, _unpack(p, like))) for p in (dl, nm, nv))
    for n in BIG:
        grad[n], delta[n], new_m[n], new_v[n] = reducer.done[n]

    return (loss, grad_x[None], *[grad[n] for n in WEIGHTS], *[delta[n] for n in WEIGHTS],
            *[new_m[n] for n in WEIGHTS], *[new_v[n] for n in WEIGHTS])


def kernel(x, mem, norm_mix_pre, norm_mix_post, w_in, w_out, gmlp_v_gain, w_spatial, b_spatial, w_pool, s_pool, w_dw, b_dw, conv_ln_g, conv_ln_b, norm_xattn_pre, norm_mem, norm_xattn_post, w_q, w_k, w_v, w_o, norm_ffn_pre, norm_ffn_post, w_up, w_down, loss_target, m_norm_mix_pre, m_norm_mix_post, m_w_in, m_w_out, m_gmlp_v_gain, m_w_spatial, m_b_spatial, m_w_pool, m_s_pool, m_w_dw, m_b_dw, m_conv_ln_g, m_conv_ln_b, m_norm_xattn_pre, m_norm_mem, m_norm_xattn_post, m_w_q, m_w_k, m_w_v, m_w_o, m_norm_ffn_pre, m_norm_ffn_post, m_w_up, m_w_down, v_norm_mix_pre, v_norm_mix_post, v_w_in, v_w_out, v_gmlp_v_gain, v_w_spatial, v_b_spatial, v_w_pool, v_s_pool, v_w_dw, v_b_dw, v_conv_ln_g, v_conv_ln_b, v_norm_xattn_pre, v_norm_mem, v_norm_xattn_post, v_w_q, v_w_k, v_w_v, v_w_o, v_norm_ffn_pre, v_norm_ffn_post, v_w_up, v_w_down):
    given = dict(locals())
    w = {n: given[n] for n in WEIGHTS}
    m = {n: given["m_" + n] for n in WEIGHTS}
    v = {n: given["v_" + n] for n in WEIGHTS}
    return _step(x[0], mem[0], loss_target[0], w, m, v)
```

```python
import functools

import jax
import jax.numpy as jnp
from jax import lax
from jax.experimental import pallas as pl
from jax.experimental.pallas import tpu as pltpu

F32 = jnp.float32
BF16 = jnp.bfloat16
MESH = pl.DeviceIdType.MESH

N_CHIPS = 4
N_DEVICES = 8
XATTN_HEADS = 4
POOL_WINDOWS = (2, 4, 8, 16)
RMS_EPS = 1e-6
LN_EPS = 1e-5
ADAM_LR, ADAM_B1, ADAM_B2, ADAM_EPS, ADAM_WD, ADAM_STEP = 0.001, 0.9, 0.999, 1e-08, 0.01, 10

V7X_LANES = 128
V7X_VMEM_LIMIT = 56 * 1024 * 1024
ROW_TILE = 256
MM_TILE_M, MM_TILE_N, MM_TILE_K = 1024, 1024, 1024

ANY = pl.BlockSpec(memory_space=pl.ANY)
HBM = pl.BlockSpec(memory_space=pltpu.HBM)
SEM = pl.BlockSpec(memory_space=pltpu.SEMAPHORE)


def _tile(dim, pref):
    if dim <= pref:
        return dim
    t = (pref // V7X_LANES) * V7X_LANES
    while t >= V7X_LANES:
        if dim % t == 0:
            return t
        t -= V7X_LANES
    return dim


def _params(sem=None):
    return pltpu.CompilerParams(dimension_semantics=sem, vmem_limit_bytes=V7X_VMEM_LIMIT)


NN = (((1,), (0,)), ((), ()))
NT = (((1,), (1,)), ((), ()))
TN = (((0,), (0,)), ((), ()))


def _mm(name, a, b, *, dn, grid, a_spec, b_spec, o_specs, out_shapes, acc_shape, epi=None, extra=None,
        extra_spec=None, after=()):
    nk = grid[2]
    n_out = len(out_shapes)
    has_extra = extra is not None
    after = list(after)

    def body(*refs):
        a_ref, b_ref = refs[0], refs[1]
        pos = 2
        e_ref = None
        if has_extra:
            e_ref = refs[pos]
            pos += 1
        pos += len(after)
        o_refs = refs[pos:pos + n_out]
        acc = refs[pos + n_out]
        k = pl.program_id(2)

        @pl.when(k == 0)
        def _():
            acc[...] = jnp.zeros_like(acc)

        acc[...] += lax.dot_general(a_ref[...].astype(BF16), b_ref[...].astype(BF16), dn,
                                    preferred_element_type=F32)

        @pl.when(k == nk - 1)
        def _():
            if epi is None:
                vals = (acc[...],)
            elif has_extra:
                vals = epi(acc[...], e_ref[...])
            else:
                vals = epi(acc[...])
            for o, v in zip(o_refs, vals):
                o[...] = v.astype(o.dtype)

    ins, in_specs = [a, b], [a_spec, b_spec]
    if has_extra:
        ins.append(extra)
        in_specs.append(extra_spec)
    ins += after
    in_specs += [ANY] * len(after)
    outs = pl.pallas_call(
        body, name=name, grid=grid, in_specs=in_specs, out_specs=list(o_specs), out_shape=list(out_shapes),
        scratch_shapes=[pltpu.VMEM(acc_shape, F32)],
        compiler_params=_params(("parallel", "parallel", "arbitrary")))(*ins)
    return outs


def _mm_nn_row(name, a, w, out_dtypes, epi=None):
    m, k = a.shape
    n = w.shape[1]
    tm, tn, tk = _tile(m, MM_TILE_M), _tile(n, MM_TILE_N), _tile(k, MM_TILE_K)
    o_spec = pl.BlockSpec((tm, tn), lambda i, j, kk: (i, j))
    return _mm(name, a, w, dn=NN, grid=(m // tm, n // tn, k // tk),
               a_spec=pl.BlockSpec((tm, tk), lambda i, j, kk: (i, kk)),
               b_spec=pl.BlockSpec((tk, tn), lambda i, j, kk: (kk, j)),
               o_specs=[o_spec] * len(out_dtypes),
               out_shapes=[jax.ShapeDtypeStruct((m, n), d) for d in out_dtypes], acc_shape=(tm, tn), epi=epi)


def _mm_nn_col(name, a, w, out_dtypes, epi=None):
    m, k = a.shape
    c = w.shape[2]
    tm, tn, tk = _tile(m, MM_TILE_M), _tile(c, MM_TILE_N), _tile(k, MM_TILE_K)
    nb = c // tn
    o_spec = pl.BlockSpec((tm, tn), lambda i, j, kk: (i, j))
    return _mm(name, a, w, dn=NN, grid=(m // tm, N_CHIPS * nb, k // tk),
               a_spec=pl.BlockSpec((tm, tk), lambda i, j, kk: (i, kk)),
               b_spec=pl.BlockSpec((None, tk, tn), lambda i, j, kk: (j // nb, kk, j % nb)),
               o_specs=[o_spec] * len(out_dtypes),
               out_shapes=[jax.ShapeDtypeStruct((m, N_CHIPS * c), d) for d in out_dtypes], acc_shape=(tm, tn),
               epi=epi)


def _mm_nt_row(name, dy, w, out_dtype, epi=None, extra=None, after=()):
    m, n = dy.shape
    k = w.shape[0]
    tm, tn, tk = _tile(m, MM_TILE_M), _tile(k, MM_TILE_N), _tile(n, MM_TILE_K)
    o_spec = pl.BlockSpec((tm, tn), lambda i, j, kk: (i, j))
    return _mm(name, dy, w, dn=NT, grid=(m // tm, k // tn, n // tk),
               a_spec=pl.BlockSpec((tm, tk), lambda i, j, kk: (i, kk)),
               b_spec=pl.BlockSpec((tn, tk), lambda i, j, kk: (j, kk)),
               o_specs=[o_spec], out_shapes=[jax.ShapeDtypeStruct((m, k), out_dtype)], acc_shape=(tm, tn),
               epi=epi, extra=extra, extra_spec=o_spec, after=after)[0]


def _mm_nt_col(name, dy, w, out_dtype, after=()):
    m = dy.shape[0]
    k, c = w.shape[1], w.shape[2]
    tm, tn, tk = _tile(m, MM_TILE_M), _tile(k, MM_TILE_N), _tile(c, MM_TILE_K)
    kb = c // tk
    return _mm(name, dy, w, dn=NT, grid=(m // tm, k // tn, N_CHIPS * kb),
               a_spec=pl.BlockSpec((tm, tk), lambda i, j, kk: (i, kk)),
               b_spec=pl.BlockSpec((None, tn, tk), lambda i, j, kk: (kk // kb, j, kk % kb)),
               o_specs=[pl.BlockSpec((tm, tn), lambda i, j, kk: (i, j))],
               out_shapes=[jax.ShapeDtypeStruct((m, k), out_dtype)], acc_shape=(tm, tn), after=after)[0]


def _mm_tn_row(name, a, dy):
    t, m = a.shape
    n = dy.shape[1]
    tm, tn, tk = _tile(m, MM_TILE_M), _tile(n, MM_TILE_N), _tile(t, MM_TILE_K)
    return _mm(name, a, dy, dn=TN, grid=(m // tm, n // tn, t // tk),
               a_spec=pl.BlockSpec((tk, tm), lambda i, j, kk: (kk, i)),
               b_spec=pl.BlockSpec((tk, tn), lambda i, j, kk: (kk, j)),
               o_specs=[pl.BlockSpec((tm, tn), lambda i, j, kk: (i, j))],
               out_shapes=[jax.ShapeDtypeStruct((m, n), F32)], acc_shape=(tm, tn))[0]


def _mm_tn_col(name, a, dy):
    t, m = a.shape
    c = dy.shape[1] // N_CHIPS
    tm, tn, tk = _tile(m, MM_TILE_M), _tile(c, MM_TILE_N), _tile(t, MM_TILE_K)
    nb = c // tn
    return _mm(name, a, dy, dn=TN, grid=(m // tm, N_CHIPS * nb, t // tk),
               a_spec=pl.BlockSpec((tk, tm), lambda i, j, kk: (kk, i)),
               b_spec=pl.BlockSpec((tk, tn), lambda i, j, kk: (kk, j)),
               o_specs=[pl.BlockSpec((None, tm, tn), lambda i, j, kk: (j // nb, i, j % nb))],
               out_shapes=[jax.ShapeDtypeStruct((N_CHIPS, m, c), F32)], acc_shape=(tm, tn))[0]


def _rms(x, g):
    r = lax.rsqrt(jnp.mean(x * x, axis=-1, keepdims=True) + RMS_EPS)
    return x * r * g


def _rms_bwd(x, g, dy):
    r = lax.rsqrt(jnp.mean(x * x, axis=-1, keepdims=True) + RMS_EPS)
    xr = x * r
    dyg = dy * g
    dx = r * (dyg - xr * jnp.mean(dyg * xr, axis=-1, keepdims=True))
    return dx, jnp.sum(dy * xr, axis=0, keepdims=True)


def _norm_fwd(name, x, o, g_post, g_next, after=()):
    after = list(after)
    s, d = x.shape
    tr = _tile(s, ROW_TILE)
    has_prev, has_next = o is not None, g_next is not None
    row = pl.BlockSpec((tr, d), lambda i: (i, 0))
    vec = pl.BlockSpec((1, d), lambda i: (0, 0))

    def body(*refs):
        refs = list(refs)
        xn = refs.pop(0)[...]
        if has_prev:
            o_ref, gp_ref = refs.pop(0), refs.pop(0)
            xn = xn + _rms(o_ref[...], gp_ref[...])
        gn_ref = refs.pop(0) if has_next else None
        del refs[:len(after)]
        if has_prev:
            refs.pop(0)[...] = xn
        if has_next:
            refs.pop(0)[...] = _rms(xn, gn_ref[...]).astype(BF16)

    ins, in_specs = [x], [row]
    if has_prev:
        ins += [o, g_post]
        in_specs += [row, vec]
    if has_next:
        ins.append(g_next)
        in_specs.append(vec)
    ins += after
    in_specs += [ANY] * len(after)
    out_shapes, out_specs = [], []
    if has_prev:
        out_shapes.append(jax.ShapeDtypeStruct((s, d), F32))
        out_specs.append(row)
    if has_next:
        out_shapes.append(jax.ShapeDtypeStruct((s, d), BF16))
        out_specs.append(row)
    outs = pl.pallas_call(body, name=name, grid=(s // tr,), in_specs=in_specs, out_specs=out_specs,
                          out_shape=out_shapes, compiler_params=_params(("parallel",)))(*ins)
    outs = list(outs)
    x_new = outs.pop(0) if has_prev else x
    h = outs.pop(0) if has_next else None
    return x_new, h


def _norm_bwd(name, dxn, dh, xn, o, g_post, g_next, after=()):
    after = list(after)
    s, d = xn.shape
    tr = _tile(s, ROW_TILE)
    has_prev, has_next, has_dxn = o is not None, dh is not None, dxn is not None
    row = pl.BlockSpec((tr, d), lambda i: (i, 0))
    vec = pl.BlockSpec((1, d), lambda i: (0, 0))

    def body(*refs):
        refs = list(refs)
        first = pl.program_id(0) == 0
        dxn_ref = refs.pop(0) if has_dxn else None
        dh_ref = refs.pop(0) if has_next else None
        xn_ref = refs.pop(0)
        if has_prev:
            o_ref, gp_ref = refs.pop(0), refs.pop(0)
        gn_ref = refs.pop(0) if has_next else None
        del refs[:len(after)]
        dx_ref = refs.pop(0)
        if has_prev:
            do_ref, dgp_ref = refs.pop(0), refs.pop(0)
        dgn_ref = refs.pop(0) if has_next else None

        def accumulate(ref, val):
            @pl.when(first)
            def _():
                ref[...] = val

            @pl.when(jnp.logical_not(first))
            def _():
                ref[...] += val

        dx = dxn_ref[...] if has_dxn else None
        if has_next:
            dxh, dgn = _rms_bwd(xn_ref[...], gn_ref[...], dh_ref[...].astype(F32))
            dx = dxh if dx is None else dx + dxh
            accumulate(dgn_ref, dgn)
        dx_ref[...] = dx
        if has_prev:
            do, dgp = _rms_bwd(o_ref[...], gp_ref[...], dx)
            do_ref[...] = do.astype(BF16)
            accumulate(dgp_ref, dgp)

    ins, in_specs = [], []
    if has_dxn:
        ins.append(dxn)
        in_specs.append(row)
    if has_next:
        ins.append(dh)
        in_specs.append(row)
    ins.append(xn)
    in_specs.append(row)
    if has_prev:
        ins += [o, g_post]
        in_specs += [row, vec]
    if has_next:
        ins.append(g_next)
        in_specs.append(vec)
    ins += after
    in_specs += [ANY] * len(after)
    out_shapes, out_specs = [jax.ShapeDtypeStruct((s, d), F32)], [row]
    if has_prev:
        out_shapes += [jax.ShapeDtypeStruct((s, d), BF16), jax.ShapeDtypeStruct((1, d), F32)]
        out_specs += [row, vec]
    if has_next:
        out_shapes.append(jax.ShapeDtypeStruct((1, d), F32))
        out_specs.append(vec)
    outs = list(pl.pallas_call(body, name=name, grid=(s // tr,), in_specs=in_specs, out_specs=out_specs,
                               out_shape=out_shapes, compiler_params=_params(("arbitrary",)))(*ins))
    dx = outs.pop(0)
    do, dgp = (outs.pop(0), outs.pop(0)) if has_prev else (None, None)
    dgn = outs.pop(0) if has_next else None
    return dx, do, dgp, dgn


def _loss_head(y, target):
    s, d = y.shape
    tr = _tile(s, ROW_TILE)
    row = pl.BlockSpec((tr, d), lambda i: (i, 0))
    vec = pl.BlockSpec((1, d), lambda i: (0, 0))

    def body(y_ref, t_ref, dy_ref, l_ref):
        err = y_ref[...] - t_ref[...]
        dy_ref[...] = err * (1.0 / d)
        part = jnp.sum(err * err, axis=0, keepdims=True) * (0.5 / d)

        @pl.when(pl.program_id(0) == 0)
        def _():
            l_ref[...] = part

        @pl.when(pl.program_id(0) != 0)
        def _():
            l_ref[...] += part

    return pl.pallas_call(body, name="loss_head", grid=(s // tr,), in_specs=[row, row], out_specs=[row, vec],
                          out_shape=[jax.ShapeDtypeStruct((s, d), F32), jax.ShapeDtypeStruct((1, d), F32)],
                          compiler_params=_params(("arbitrary",)))(y, target)


@jax.custom_vjp
def _bdot(a, b):
    return jnp.dot(a.astype(BF16), b.astype(BF16), preferred_element_type=F32)


def _bdot_fwd(a, b):
    return _bdot(a, b), (a, b)


def _bdot_bwd(res, ct):
    a, b = res
    ctb = ct.astype(BF16)
    da = lax.dot_general(ctb, b.astype(BF16), NT, preferred_element_type=F32)
    db = lax.dot_general(a.astype(BF16), ctb, TN, preferred_element_type=F32)
    return da, db


_bdot.defvjp(_bdot_fwd, _bdot_bwd)


@functools.partial(jax.custom_vjp, nondiff_argnums=(1,))
def _shift(x, k):
    n = x.shape[0]
    if k == 0:
        return x
    rolled = pltpu.roll(x, k % n, 0)
    t = lax.broadcasted_iota(jnp.int32, x.shape, 0)
    keep = (t >= k) if k > 0 else (t < n + k)
    return jnp.where(keep, rolled, 0.0)


def _shift_fwd(x, k):
    return _shift(x, k), None


def _shift_bwd(k, _, ct):
    return (_shift(ct, -k),)


_shift.defvjp(_shift_fwd, _shift_bwd)


def _sigmoid(x):
    return 1.0 / (1.0 + jnp.exp(-x))


def _layer_norm(x, g, b=None):
    mu = jnp.mean(x, axis=-1, keepdims=True)
    xc = x - mu
    var = jnp.mean(xc * xc, axis=-1, keepdims=True)
    y = xc * lax.rsqrt(var + LN_EPS) * g
    return y if b is None else y + b


def _gmlp_chunk(zu, zv, gv, w, bcol):
    ch = w.shape[0]
    u = jax.nn.gelu(zu)
    vn = _layer_norm(jax.nn.gelu(zv), gv)
    t = lax.broadcasted_iota(jnp.int32, (ch, ch), 0)
    s = lax.broadcasted_iota(jnp.int32, (ch, ch), 1)
    wm = jnp.where(t >= s, w, 0.0)
    return u * (_bdot(wm, vn) + bcol)


def _gmlp_specs(seq, heads, hd, ch, u_off, v_off):
    col = lambda off: pl.BlockSpec((seq, hd), lambda h: (0, off + h))
    return (col(u_off), col(v_off), pl.BlockSpec((None, 1, hd), lambda h: (h, 0, 0)),
            pl.BlockSpec((None, ch, ch), lambda h: (h, 0, 0)), pl.BlockSpec((None, ch, 1), lambda h: (h, 0, 0)))


def _gmlp_fwd(z, gv, ws, bcol):
    seq = z.shape[0]
    heads, _, hd = gv.shape
    ch = ws.shape[-1]
    zu_s, zv_s, gv_s, w_s, b_s = _gmlp_specs(seq, heads, hd, ch, 0, heads)

    def body(zu_ref, zv_ref, gv_ref, w_ref, b_ref, y_ref):
        gvv, w, bc = gv_ref[...], w_ref[...], b_ref[...]

        def step(c, carry):
            rows = pl.ds(pl.multiple_of(c * ch, ch), ch)
            y_ref[rows, :] = _gmlp_chunk(zu_ref[rows, :], zv_ref[rows, :], gvv, w, bc).astype(BF16)
            return carry

        lax.fori_loop(0, seq // ch, step, 0)

    return pl.pallas_call(body, name="gmlp_fwd", grid=(heads,), in_specs=[zu_s, zv_s, gv_s, w_s, b_s],
                          out_specs=pl.BlockSpec((seq, hd), lambda h: (0, h)),
                          out_shape=jax.ShapeDtypeStruct((seq, heads * hd), BF16),
                          compiler_params=_params(("parallel",)))(z, z, gv, ws, bcol)


def _gmlp_bwd(z, dy, gv, ws, bcol):
    seq = z.shape[0]
    heads, _, hd = gv.shape
    ch = ws.shape[-1]
    zu_s, zv_s, gv_s, w_s, b_s = _gmlp_specs(seq, heads, hd, ch, 0, heads)
    col = pl.BlockSpec((seq, hd), lambda h: (0, h))

    def body(zu_ref, zv_ref, dy_ref, gv_ref, w_ref, b_ref, dzu_ref, dzv_ref, dgv_ref, dw_ref, db_ref):
        gvv, w, bc = gv_ref[...], w_ref[...], b_ref[...]

        def step(c, carry):
            dgv, dw, db = carry
            rows = pl.ds(pl.multiple_of(c * ch, ch), ch)
            _, vjp = jax.vjp(_gmlp_chunk, zu_ref[rows, :], zv_ref[rows, :], gvv, w, bc)
            dzu, dzv, dgv_c, dw_c, db_c = vjp(dy_ref[rows, :])
            dzu_ref[rows, :] = dzu.astype(BF16)
            dzv_ref[rows, :] = dzv.astype(BF16)
            return dgv + dgv_c, dw + dw_c, db + db_c

        zero = (jnp.zeros((1, hd), F32), jnp.zeros((ch, ch), F32), jnp.zeros((ch, 1), F32))
        dgv, dw, db = lax.fori_loop(0, seq // ch, step, zero)
        dgv_ref[...] = dgv
        dw_ref[...] = dw
        db_ref[...] = db

    return pl.pallas_call(
        body, name="gmlp_bwd", grid=(heads,), in_specs=[zu_s, zv_s, col, gv_s, w_s, b_s],
        out_specs=[col, col, gv_s, w_s, b_s],
        out_shape=[jax.ShapeDtypeStruct((seq, heads * hd), BF16), jax.ShapeDtypeStruct((seq, heads * hd), BF16),
                   jax.ShapeDtypeStruct(gv.shape, F32), jax.ShapeDtypeStruct(ws.shape, F32),
                   jax.ShapeDtypeStruct(bcol.shape, F32)],
        compiler_params=_params(("parallel",)))(z, z, dy, gv, ws, bcol)


def _pool_group(p, w, s, window):
    win, span = p, 1
    while span < window:
        win = win + _shift(win, span)
        span *= 2
    t = lax.broadcasted_iota(jnp.int32, (p.shape[0], 1), 0).astype(F32)
    cnt = jnp.minimum(t + 1.0, float(window))
    return _bdot(win / cnt - p, w) * s


def _pool_fwd(z, w_pool, s_pool, col_block):
    seq = z.shape[0]
    groups, gw, _ = w_pool.shape
    pw = groups * gw

    def body(p_ref, w_ref, s_ref, y_ref):
        for g in range(groups):
            cols = slice(g * gw, (g + 1) * gw)
            y_ref[:, cols] = _pool_group(p_ref[:, cols], w_ref[g], s_ref[:, cols], POOL_WINDOWS[g]).astype(BF16)

    return pl.pallas_call(
        body, name="pool_fwd", grid=(1,),
        in_specs=[pl.BlockSpec((seq, pw), lambda i: (0, col_block)),
                  pl.BlockSpec((groups, gw, gw), lambda i: (0, 0, 0)), pl.BlockSpec((1, pw), lambda i: (0, 0))],
        out_specs=pl.BlockSpec((seq, pw), lambda i: (0, 0)), out_shape=jax.ShapeDtypeStruct((seq, pw), BF16),
        compiler_params=_params(("arbitrary",)))(z, w_pool, s_pool)


def _pool_bwd(z, dy, w_pool, s_pool, col_block, dy_block):
    seq = z.shape[0]
    groups, gw, _ = w_pool.shape
    pw = groups * gw

    def body(p_ref, dy_ref, w_ref, s_ref, dp_ref, dw_ref, ds_ref):
        for g in range(groups):
            cols = slice(g * gw, (g + 1) * gw)
            _, vjp = jax.vjp(functools.partial(_pool_group, window=POOL_WINDOWS[g]), p_ref[:, cols], w_ref[g],
                             s_ref[:, cols])
            dp, dw, ds = vjp(dy_ref[:, cols])
            dp_ref[:, cols] = dp.astype(BF16)
            dw_ref[g] = dw
            ds_ref[:, cols] = ds

    return pl.pallas_call(
        body, name="pool_bwd", grid=(1,),
        in_specs=[pl.BlockSpec((seq, pw), lambda i: (0, col_block)),
                  pl.BlockSpec((seq, pw), lambda i: (0, dy_block)),
                  pl.BlockSpec((groups, gw, gw), lambda i: (0, 0, 0)), pl.BlockSpec((1, pw), lambda i: (0, 0))],
        out_specs=[pl.BlockSpec((seq, pw), lambda i: (0, 0)), pl.BlockSpec((groups, gw, gw), lambda i: (0, 0, 0)),
                   pl.BlockSpec((1, pw), lambda i: (0, 0))],
        out_shape=[jax.ShapeDtypeStruct((seq, pw), BF16), jax.ShapeDtypeStruct(w_pool.shape, F32),
                   jax.ShapeDtypeStruct((1, pw), F32)],
        compiler_params=_params(("arbitrary",)))(z, dy, w_pool, s_pool)


def _conv_fwd(z, w_dw, taps, b_dw, val_block, gate_block):
    seq = z.shape[0]
    rows, cb = w_dw.shape[1], w_dw.shape[2]

    def body(val_ref, gate_ref, w_ref, b_ref, out_ref):
        h = val_ref[...] * _sigmoid(gate_ref[...])
        acc = jnp.broadcast_to(b_ref[...], h.shape)
        for d in range(taps):
            acc = acc + w_ref[pl.ds(taps - 1 - d, 1), :] * _shift(h, d)
        out_ref[...] = acc

    return pl.pallas_call(
        body, name="conv_fwd", grid=(N_CHIPS,),
        in_specs=[pl.BlockSpec((seq, cb), lambda j: (0, val_block + j)),
                  pl.BlockSpec((seq, cb), lambda j: (0, gate_block + j)),
                  pl.BlockSpec((None, rows, cb), lambda j: (j, 0, 0)),
                  pl.BlockSpec((1, cb), lambda j: (0, j))],
        out_specs=pl.BlockSpec((seq, cb), lambda j: (0, j)),
        out_shape=jax.ShapeDtypeStruct((seq, N_CHIPS * cb), F32),
        compiler_params=_params(("parallel",)))(z, z, w_dw, b_dw)


def _conv_bwd(z, dout, w_dw, taps, val_block, gate_block):
    seq = z.shape[0]
    rows, cb = w_dw.shape[1], w_dw.shape[2]
    col = pl.BlockSpec((seq, cb), lambda j: (0, j))

    def body(val_ref, gate_ref, do_ref, w_ref, dval_ref, dgate_ref, dw_ref, db_ref):
        val, sg, do = val_ref[...], _sigmoid(gate_ref[...]), do_ref[...]
        h = val * sg
        db_ref[...] = jnp.sum(do, axis=0, keepdims=True)
        dh = jnp.zeros_like(h)
        for d in range(taps):
            k = taps - 1 - d
            dw_ref[pl.ds(k, 1), :] = jnp.sum(do * _shift(h, d), axis=0, keepdims=True)
            dh = dh + w_ref[pl.ds(k, 1), :] * _shift(do, -d)
        dval_ref[...] = (dh * sg).astype(BF16)
        dgate_ref[...] = (dh * val * sg * (1.0 - sg)).astype(BF16)

    return pl.pallas_call(
        body, name="conv_bwd", grid=(N_CHIPS,),
        in_specs=[pl.BlockSpec((seq, cb), lambda j: (0, val_block + j)),
                  pl.BlockSpec((seq, cb), lambda j: (0, gate_block + j)), col,
                  pl.BlockSpec((None, rows, cb), lambda j: (j, 0, 0))],
        out_specs=[col, col, pl.BlockSpec((taps, cb), lambda j: (0, j)), pl.BlockSpec((1, cb), lambda j: (0, j))],
        out_shape=[jax.ShapeDtypeStruct((seq, N_CHIPS * cb), BF16), jax.ShapeDtypeStruct((seq, N_CHIPS * cb), BF16),
                   jax.ShapeDtypeStruct((taps, N_CHIPS * cb), F32), jax.ShapeDtypeStruct((1, N_CHIPS * cb), F32)],
        compiler_params=_params(("parallel",)))(z, z, dout, w_dw)


def _ln_swish(hc, g, b):
    y = _layer_norm(hc, g, b)
    return y * _sigmoid(y)


def _ln_swish_fwd(hc, g, b):
    s, cw = hc.shape
    tr = _tile(s, ROW_TILE)
    row = pl.BlockSpec((tr, cw), lambda i: (i, 0))
    vec = pl.BlockSpec((1, cw), lambda i: (0, 0))

    def body(h_ref, g_ref, b_ref, y_ref):
        y_ref[...] = _ln_swish(h_ref[...], g_ref[...], b_ref[...]).astype(BF16)

    return pl.pallas_call(body, name="ln_swish_fwd", grid=(s // tr,), in_specs=[row, vec, vec], out_specs=row,
                          out_shape=jax.ShapeDtypeStruct((s, cw), BF16),
                          compiler_params=_params(("parallel",)))(hc, g, b)


def _ln_swish_bwd(hc, dy, g, b, dy_block):
    s, cw = hc.shape
    tr = _tile(s, ROW_TILE)
    row = pl.BlockSpec((tr, cw), lambda i: (i, 0))
    vec = pl.BlockSpec((1, cw), lambda i: (0, 0))

    def body(h_ref, dy_ref, g_ref, b_ref, dh_ref, dg_ref, db_ref):
        _, vjp = jax.vjp(_ln_swish, h_ref[...], g_ref[...], b_ref[...])
        dh, dg, db = vjp(dy_ref[...])
        dh_ref[...] = dh

        @pl.when(pl.program_id(0) == 0)
        def _():
            dg_ref[...] = dg
            db_ref[...] = db

        @pl.when(pl.program_id(0) != 0)
        def _():
            dg_ref[...] += dg
            db_ref[...] += db

    return pl.pallas_call(
        body, name="ln_swish_bwd", grid=(s // tr,),
        in_specs=[row, pl.BlockSpec((tr, cw), lambda i: (i, dy_block)), vec, vec], out_specs=[row, vec, vec],
        out_shape=[jax.ShapeDtypeStruct((s, cw), F32), jax.ShapeDtypeStruct((1, cw), F32),
                   jax.ShapeDtypeStruct((1, cw), F32)],
        compiler_params=_params(("arbitrary",)))(hc, dy, g, b)


def _attn_probs(q, k, scale):
    s = lax.dot_general(q, k, NT, preferred_element_type=F32) * scale
    e = jnp.exp(s - jnp.max(s, axis=-1, keepdims=True))
    return e / jnp.sum(e, axis=-1, keepdims=True)


def _attn_fwd(q, k, v):
    seq, d = q.shape
    mem = k.shape[0]
    hd = d // XATTN_HEADS
    scale = hd ** -0.5
    qs = pl.BlockSpec((seq, hd), lambda h: (0, h))
    ms = pl.BlockSpec((mem, hd), lambda h: (0, h))

    def body(q_ref, k_ref, v_ref, a_ref):
        p = _attn_probs(q_ref[...], k_ref[...], scale)
        a_ref[...] = jnp.dot(p.astype(BF16), v_ref[...], preferred_element_type=F32).astype(BF16)

    return pl.pallas_call(body, name="attn_fwd", grid=(XATTN_HEADS,), in_specs=[qs, ms, ms], out_specs=qs,
                          out_shape=jax.ShapeDtypeStruct((seq, d), BF16),
                          compiler_params=_params(("parallel",)))(q, k, v)


def _attn_bwd(q, k, v, da):
    seq, d = q.shape
    mem = k.shape[0]
    hd = d // XATTN_HEADS
    scale = hd ** -0.5
    qs = pl.BlockSpec((seq, hd), lambda h: (0, h))
    ms = pl.BlockSpec((mem, hd), lambda h: (0, h))

    def body(q_ref, k_ref, v_ref, da_ref, dq_ref, dk_ref, dv_ref):
        q_, k_, v_, da_ = q_ref[...], k_ref[...], v_ref[...], da_ref[...]
        p = _attn_probs(q_, k_, scale)
        dv_ref[...] = lax.dot_general(p.astype(BF16), da_, TN, preferred_element_type=F32).astype(BF16)
        dp = lax.dot_general(da_, v_, NT, preferred_element_type=F32)
        ds = (p * (dp - jnp.sum(dp * p, axis=-1, keepdims=True)) * scale).astype(BF16)
        dq_ref[...] = jnp.dot(ds, k_, preferred_element_type=F32).astype(BF16)
        dk_ref[...] = lax.dot_general(ds, q_, TN, preferred_element_type=F32).astype(BF16)

    return pl.pallas_call(
        body, name="attn_bwd", grid=(XATTN_HEADS,), in_specs=[qs, ms, ms, qs], out_specs=[qs, ms, ms],
        out_shape=[jax.ShapeDtypeStruct((seq, d), BF16), jax.ShapeDtypeStruct((mem, d), BF16),
                   jax.ShapeDtypeStruct((mem, d), BF16)],
        compiler_params=_params(("parallel",)))(q, k, v, da)


def _place_shard(name, place, w, l, dtype):
    _, r, c = w.shape
    tr = _tile(r, 2 * ROW_TILE) if r % 16 == 0 else r

    def body(place_ref, w_ref, o_ref):
        o_ref[...] = w_ref[...].astype(dtype)

    return pl.pallas_call(
        body, name=name,
        grid_spec=pltpu.PrefetchScalarGridSpec(
            num_scalar_prefetch=1, grid=(r // tr,),
            in_specs=[pl.BlockSpec((None, tr, c), lambda i, p: (l, i, 0))],
            out_specs=pl.BlockSpec((None, tr, c), lambda i, p: (p[1], i, 0))),
        out_shape=jax.ShapeDtypeStruct((N_CHIPS, r, c), dtype),
        compiler_params=_params(("parallel",)))(place, w)


def _pair_sum(name, place, dw, got):
    n, _, r2, c = dw.shape
    tr = _tile(r2, ROW_TILE)

    def body(place_ref, own_ref, got_ref, s_ref, t_ref):
        val = (own_ref[...] + got_ref[...]).astype(BF16)
        s_ref[...] = val

        @pl.when(pl.program_id(1) == place_ref[1])
        def _():
            t_ref[...] = val

    slab = pl.BlockSpec((None, tr, c), lambda i, j, p: (j, i, 0))
    sds = jax.ShapeDtypeStruct((n, r2, c), BF16)
    return pl.pallas_call(
        body, name=name,
        grid_spec=pltpu.PrefetchScalarGridSpec(
            num_scalar_prefetch=1, grid=(r2 // tr, n),
            in_specs=[pl.BlockSpec((None, None, tr, c), lambda i, j, p: (j, p[0], i, 0)), slab],
            out_specs=[slab, pl.BlockSpec((None, tr, c), lambda i, j, p: (p[1], i, 0))]),
        out_shape=[sds, sds], compiler_params=_params(("parallel", "arbitrary")))(place, dw, got)


def _chip_sum(name, place, parts):
    n, r2, c = parts.shape
    tr = _tile(r2, ROW_TILE)

    def body(place_ref, *refs):
        o_ref = refs[n]
        acc = refs[0][...].astype(F32)
        for j in range(1, n):
            acc = acc + refs[j][...].astype(F32)
        o_ref[...] = acc

    part = lambda j: pl.BlockSpec((None, tr, c), lambda i, p: (j, i, 0))
    return pl.pallas_call(
        body, name=name,
        grid_spec=pltpu.PrefetchScalarGridSpec(
            num_scalar_prefetch=1, grid=(r2 // tr,), in_specs=[part(j) for j in range(n)],
            out_specs=pl.BlockSpec((None, tr, c), lambda i, p: (p[0], i, 0))),
        out_shape=jax.ShapeDtypeStruct((2, r2, c), F32),
        compiler_params=_params(("parallel",)))(place, *([parts] * n))


def _device_sum(parts):
    n, rows, lanes = parts.shape
    tr = _tile(rows, 4 * ROW_TILE) if rows % 8 == 0 else rows

    def body(p_ref, o_ref):
        acc = p_ref[0]
        for j in range(1, n):
            acc = acc + p_ref[j]
        o_ref[...] = acc

    return pl.pallas_call(
        body, name="device_sum", grid=(rows // tr,), in_specs=[pl.BlockSpec((n, tr, lanes), lambda i: (0, i, 0))],
        out_specs=pl.BlockSpec((tr, lanes), lambda i: (i, 0)), out_shape=jax.ShapeDtypeStruct((rows, lanes), F32),
        compiler_params=_params(("parallel",)))(parts)


def _adam_update(w, g, m, v):
    nm = ADAM_B1 * m + (1.0 - ADAM_B1) * g
    nv = ADAM_B2 * v + (1.0 - ADAM_B2) * (g * g)
    c1 = 1.0 - ADAM_B1 ** ADAM_STEP
    c2 = 1.0 - ADAM_B2 ** ADAM_STEP
    return -ADAM_LR * ((nm / c1) / (jnp.sqrt(nv / c2) + ADAM_EPS) + ADAM_WD * w), nm, nv


def _adamw_layer(name, l, w, g, m, v, prev):
    n_l, r, c = w.shape
    tr = _tile(r, ROW_TILE)
    slab = pl.BlockSpec((None, tr, c), lambda i: (l, i, 0))

    def body(w_ref, g_ref, m_ref, v_ref, *refs):
        go_ref, d_ref, nm_ref, nv_ref = refs[-4:]
        g_ = g_ref[...]
        delta, nm, nv = _adam_update(w_ref[...], g_, m_ref[...], v_ref[...])
        go_ref[...] = g_
        d_ref[...] = delta
        nm_ref[...] = nm
        nv_ref[...] = nv

    ins = [w, g, m, v]
    in_specs = [slab, pl.BlockSpec((tr, c), lambda i: (i, 0)), slab, slab]
    aliases = {}
    if prev is not None:
        aliases = {len(ins) + i: i for i in range(4)}
        ins += list(prev)
        in_specs += [ANY] * 4
    sds = jax.ShapeDtypeStruct((n_l, r, c), F32)
    return pl.pallas_call(body, name=name, grid=(r // tr,), in_specs=in_specs, out_specs=[slab] * 4,
                          out_shape=[sds] * 4, input_output_aliases=aliases,
                          compiler_params=_params(("parallel",)))(*ins)


def _adamw_flat(name, w, g, m, v):
    rows, cols = w.shape
    tr = _tile(rows, ROW_TILE) if rows % 8 == 0 else rows
    spec = pl.BlockSpec((tr, cols), lambda i: (i, 0))

    def body(w_ref, g_ref, m_ref, v_ref, d_ref, nm_ref, nv_ref):
        d_ref[...], nm_ref[...], nv_ref[...] = _adam_update(w_ref[...], g_ref[...], m_ref[...], v_ref[...])

    sds = jax.ShapeDtypeStruct((rows, cols), F32)
    return pl.pallas_call(body, name=name, grid=(rows // tr,), in_specs=[spec] * 4, out_specs=[spec] * 3,
                          out_shape=[sds] * 3, compiler_params=_params(("parallel",)))(w, g, m, v)


def _me():
    return lax.axis_index("x"), lax.axis_index("y"), lax.axis_index("c")


def _other_chips(x, y):
    return [(1 - x, y, 2 * (1 - x) + y), (x, 1 - y, 2 * x + 1 - y), (1 - x, 1 - y, 2 * (1 - x) + 1 - y)]


def _remote(src, dst, send_sem, recv_sem, target):
    return pltpu.make_async_remote_copy(src_ref=src, dst_ref=dst, send_sem=send_sem, recv_sem=recv_sem,
                                        device_id=target, device_id_type=MESH)


def _exchange(name, bufs, plan, n_copies):
    n = len(bufs)

    def body(*refs):
        send_sems, recv_sems = refs[2 * n:]
        copies = []
        for i, (src, dst, target) in enumerate(plan(refs[n:2 * n], _me())):
            if target is None:
                cp = pltpu.make_async_copy(src, dst, send_sems.at[i])
            else:
                cp = _remote(src, dst, send_sems.at[i], recv_sems.at[i], target)
            cp.start()
            copies.append((cp, target))
        assert len(copies) == n_copies
        for cp, target in copies:
            if target is None:
                cp.wait()
            else:
                cp.wait_recv()
        for cp, target in copies:
            if target is not None:
                cp.wait_send()

    return pl.pallas_call(
        body, name=name, in_specs=[ANY] * n, out_specs=[ANY] * n,
        out_shape=[jax.ShapeDtypeStruct(b.shape, b.dtype) for b in bufs],
        scratch_shapes=[pltpu.SemaphoreType.DMA((n_copies,)), pltpu.SemaphoreType.DMA((n_copies,))],
        input_output_aliases={i: i for i in range(n)},
        compiler_params=pltpu.CompilerParams(has_side_effects=True))(*bufs)


def _start_copies(name, groups, after=()):
    all_bufs = [b for bufs, _, _ in groups for b in bufs]
    after = list(after)
    n = len(all_bufs)
    n_g = len(groups)

    def body(*refs):
        in_refs, sem_refs = refs[:n], refs[n + len(after):n + len(after) + 2 * n_g]
        pos = 0
        for g, (bufs, plan, n_copies) in enumerate(groups):
            copies = plan(in_refs[pos:pos + len(bufs)], _me())
            assert len(copies) == n_copies
            for i, (src, dst, target) in enumerate(copies):
                _remote(src, dst, sem_refs[2 * g].at[i], sem_refs[2 * g + 1].at[i], target).start()
            pos += len(bufs)

    sems = []
    for _, _, n_copies in groups:
        sems += [pltpu.SemaphoreType.DMA((n_copies,))] * 2
    outs = pl.pallas_call(
        body, name=name, in_specs=[HBM] * n + [ANY] * len(after), out_specs=[SEM] * (2 * n_g) + [HBM] * n,
        out_shape=sems + [pltpu.HBM(b.shape, b.dtype) for b in all_bufs],
        input_output_aliases={i: 2 * n_g + i for i in range(n)},
        compiler_params=pltpu.CompilerParams(has_side_effects=pltpu.SideEffectType.DATAFLOW_SIDE_EFFECTING))(
            *[pltpu.with_memory_space_constraint(b, pltpu.HBM) for b in all_bufs], *after)
    result, pos = [], 2 * n_g
    for g, (bufs, _, _) in enumerate(groups):
        result.append((outs[2 * g], outs[2 * g + 1], list(outs[pos:pos + len(bufs)])))
        pos += len(bufs)
    return result


def _wait_copies(name, started, plan, n_copies, after):
    send_sems, recv_sems, bufs = started
    n = len(bufs)

    def body(*refs):
        copies = plan(refs[:n], _me())
        assert len(copies) == n_copies
        for i, (src, dst, target) in enumerate(copies):
            cp = _remote(src, dst, refs[n].at[i], refs[n + 1].at[i], target)
            cp.wait_send()
            cp.wait_recv()

    return list(pl.pallas_call(
        body, name=name, in_specs=[HBM] * n + [SEM, SEM, ANY], out_specs=[HBM] * n,
        out_shape=[pltpu.HBM(b.shape, b.dtype) for b in bufs], input_output_aliases={i: i for i in range(n)},
        compiler_params=pltpu.CompilerParams(has_side_effects=pltpu.SideEffectType.DATAFLOW_SIDE_EFFECTING))(
            *bufs, send_sems, recv_sems, after))


def _halves(a):
    return a.reshape(a.shape[0], 2, a.shape[1] // 2, a.shape[2])


def _gather_plan(refs, me):
    x, y, c = me
    mine = 2 * x + y
    return [(g.at[mine, c], g.at[mine, c], (px, py, c)) for g in refs for px, py, _ in _other_chips(x, y)]


def _forward_plan(refs, me):
    x, y, c = me
    return [(g.at[chip, c], g.at[chip, c], (x, y, 1 - c)) for g in refs for _, _, chip in _other_chips(x, y)]


def _swap_plan(refs, me):
    x, y, c = me
    k = len(refs) // 2
    return [(refs[i].at[j, 1 - c], refs[k + i].at[j], (x, y, 1 - c)) for i in range(k) for j in range(N_CHIPS)]


def _scatter_plan(refs, me):
    x, y, c = me
    mine = 2 * x + y
    k = len(refs) // 2
    return [(refs[i].at[chip], refs[k + i].at[mine], (px, py, c))
            for i in range(k) for px, py, chip in _other_chips(x, y)]


def _share_plan(refs, me):
    x, y, c = me
    return [(g.at[c], g.at[c], (x, y, 1 - c)) for g in refs]


def _gather_devices(flat):
    def plan(refs, me):
        x, y, c = me
        mine = 4 * x + 2 * y + c
        src, dst = refs
        copies = [(src, dst.at[mine], None)]
        for fx, fy, fc in [(0, 0, 1), (0, 1, 0), (0, 1, 1), (1, 0, 0), (1, 0, 1), (1, 1, 0), (1, 1, 1)]:
            peer = (x + fx - 2 * fx * x, y + fy - 2 * fy * y, c + fc - 2 * fc * c)
            copies.append((src, dst.at[mine], peer))
        return copies

    landing = lax.empty((N_DEVICES,) + flat.shape, flat.dtype)
    return _exchange("gather_devices", [flat, landing], plan, N_DEVICES)[1]


BIG = ("w_in", "w_out", "w_q", "w_k", "w_v", "w_o", "w_up", "w_down")
COLUMN_SPLIT = ("w_in", "w_up")
WEIGHTS = ("norm_mix_pre", "norm_mix_post", "w_in", "w_out", "gmlp_v_gain", "w_spatial", "b_spatial", "w_pool",
           "s_pool", "w_dw", "b_dw", "conv_ln_g", "conv_ln_b", "norm_xattn_pre", "norm_mem", "norm_xattn_post",
           "w_q", "w_k", "w_v", "w_o", "norm_ffn_pre", "norm_ffn_post", "w_up", "w_down")
SMALL = tuple(n for n in WEIGHTS if n not in BIG)
GATHER_GROUPS = (("w_in", "w_dw"), ("w_out", "w_q", "w_k", "w_v", "w_o"), ("w_up", "w_down"))


def _relu2(acc):
    r = jnp.maximum(acc, 0.0)
    return acc, r * r


def _relu2_bwd(acc, up):
    return (acc * (2.0 * jnp.maximum(up, 0.0)),)


def _pack(arrays):
    flat = jnp.concatenate([a.reshape(-1) for a in arrays])
    tile = 8 * V7X_LANES
    pad = (-flat.shape[0]) % tile
    return jnp.pad(flat, (0, pad)).reshape(-1, V7X_LANES)


def _unpack(packed, like):
    flat = packed.reshape(-1)
    out, pos = [], 0
    for a in like:
        out.append(flat[pos:pos + a.size].reshape(a.shape))
        pos += a.size
    return out


class _GradientReducer:
    def __init__(self, place, w, m, v):
        self.place, self.w, self.m, self.v = place, w, m, v
        self.flying = []
        self.done = {n: None for n in BIG}

    def add(self, tag, l, grads):
        names = list(grads)
        views = [_halves(grads[n]) for n in names]
        zones = [lax.empty((v.shape[0],) + v.shape[2:], F32) for v in views]
        started = _start_copies("swap_start_" + tag, [(views + zones, _swap_plan, N_CHIPS * len(names))])[0]
        self.flying.append(dict(stage=0, tag=tag, l=l, names=names, started=started))
        return [started[2][0]]

    def advance(self, after):
        made = []
        for item in list(self.flying):
            tag, names, k = item["tag"], item["names"], len(item["names"])
            item["stage"] += 1
            if item["stage"] == 1:
                bufs = _wait_copies("swap_wait_" + tag, item["started"], _swap_plan, N_CHIPS * k, after)
                sums, parts = zip(*[_pair_sum("pair_sum_" + n, self.place, dv, got)
                                    for n, dv, got in zip(names, bufs[:k], bufs[k:])])
                item["started"] = _start_copies("scatter_start_" + tag,
                                                [(list(sums) + list(parts), _scatter_plan, 3 * k)])[0]
                made.append(item["started"][2][0])
            elif item["stage"] == 3:
                bufs = _wait_copies("scatter_wait_" + tag, item["started"], _scatter_plan, 3 * k, after)
                halves = [_chip_sum("chip_sum_" + n, self.place, p) for n, p in zip(names, bufs[k:])]
                halves = _exchange("share_" + tag, halves, _share_plan, k)
                for n, h in zip(names, halves):
                    g = h.reshape(self.w[n].shape[1:])
                    self.done[n] = _adamw_layer("adamw_" + n, item["l"], self.w[n], g, self.m[n], self.v[n],
                                                self.done[n])
                    made.append(self.done[n][3])
                self.flying.remove(item)
        return made

    def drain(self, after):
        while self.flying:
            self.advance(after)


def _step(x, mem, target, w, m, v):
    n_layers = w["w_in"].shape[0]
    seq, d = x.shape
    heads, hd = w["gmlp_v_gain"].shape[1:]
    gw = heads * hd
    groups, pgw = w["w_pool"].shape[1:3]
    pw = groups * pgw
    cw = w["b_dw"].shape[1]
    cb = cw // N_CHIPS
    taps = w["w_dw"].shape[1]
    cx, cy, cc = _me()
    chip = 2 * cx + cy
    place = jnp.stack([cc, chip]).astype(jnp.int32)
    vec = lambda name, l: w[name][l].reshape(1, -1)

    taps_padded = jnp.pad(w["w_dw"], ((0, 0), (0, (-taps) % 16), (0, 0)))
    gathering = []
    for l in range(n_layers):
        for g, names in enumerate(GATHER_GROUPS):
            bufs = [_halves(_place_shard("place_" + n, place, taps_padded if n == "w_dw" else w[n], l,
                                         F32 if n == "w_dw" else BF16)) for n in names]
            gathering += _start_copies("gather_start_%d%d" % (l, g), [(bufs, _gather_plan, 3 * len(names))],
                                       after=[gathering[-1][2][0]] if gathering else [])

    def arrive(l, g, after):
        names = GATHER_GROUPS[g]
        tag = "%d%d" % (l, g)
        bufs = _wait_copies("gather_wait_" + tag, gathering[l * len(GATHER_GROUPS) + g], _gather_plan,
                            3 * len(names), after)
        bufs = _exchange("gather_forward_" + tag, bufs, _forward_plan, 3 * len(names))
        out = {}
        for n, b in zip(names, bufs):
            full = b.reshape(N_CHIPS, 2 * b.shape[2], b.shape[3])
            out[n] = full if n in COLUMN_SPLIT + ("w_dw",) else full.reshape(-1, full.shape[2])
        return out

    saved = []
    _, h1 = _norm_fwd("norm_first", x, None, None, vec("norm_mix_pre", 0), after=[gathering[-1][2][0]])
    for l in range(n_layers):
        gv = w["gmlp_v_gain"][l].reshape(heads, 1, hd)
        ws = w["w_spatial"][l]
        bcol = w["b_spatial"][l].reshape(heads, -1, 1)
        wl = arrive(l, 0, h1)
        z = _mm_nn_col("mm_in", h1, wl["w_in"], [F32])[0]
        ya = _gmlp_fwd(z, gv, ws, bcol)
        yb = _pool_fwd(z, w["w_pool"][l], vec("s_pool", l), (2 * gw) // pw)
        hc = _conv_fwd(z, wl["w_dw"], taps, vec("b_dw", l), (2 * gw + pw) // cb, (2 * gw + pw + cw) // cb)
        yc = _ln_swish_fwd(hc, vec("conv_ln_g", l), vec("conv_ln_b", l))
        y = jnp.concatenate([ya, yb, yc], axis=1)
        wl.update(arrive(l, 1, y))
        o = _mm_nn_row("mm_out", y, wl["w_out"], [F32])[0]
        x1, h2 = _norm_fwd("norm_mix", x, o, vec("norm_mix_post", l), vec("norm_xattn_pre", l))
        _, mn = _norm_fwd("norm_mem", mem, None, None, vec("norm_mem", l))
        q = _mm_nn_row("mm_q", h2, wl["w_q"], [BF16])[0]
        k = _mm_nn_row("mm_k", mn, wl["w_k"], [BF16])[0]
        vv = _mm_nn_row("mm_v", mn, wl["w_v"], [BF16])[0]
        a = _attn_fwd(q, k, vv)
        o2 = _mm_nn_row("mm_o", a, wl["w_o"], [F32])[0]
        x2, h3 = _norm_fwd("norm_xattn", x1, o2, vec("norm_xattn_post", l), vec("norm_ffn_pre", l))
        wl.update(arrive(l, 2, h3))
        up, r = _mm_nn_col("mm_up", h3, wl["w_up"], [F32, BF16], epi=_relu2)
        o3 = _mm_nn_row("mm_down", r, wl["w_down"], [F32])[0]
        g_next = vec("norm_mix_pre", l + 1) if l + 1 < n_layers else None
        x3, h_next = _norm_fwd("norm_ffn", x2, o3, vec("norm_ffn_post", l), g_next)
        saved.append(dict(x=x, h1=h1, z=z, hc=hc, y=y, o=o, x1=x1, h2=h2, mn=mn, q=q, k=k, v=vv, a=a, o2=o2, x2=x2,
                          h3=h3, up=up, r=r, o3=o3, x3=x3, gv=gv, ws=ws, bcol=bcol, w=wl))
        x, h1 = x3, h_next

    dx, loss_parts = _loss_head(x, target)
    loss = lax.psum(jnp.sum(loss_parts), ("x", "y", "c"))

    reducer = _GradientReducer(place, w, m, v)
    small = {n: [None] * n_layers for n in SMALL}
    by_chip = lambda g: g.reshape(N_CHIPS, g.shape[0] // N_CHIPS, g.shape[1])
    dh = None
    made = []
    for l in reversed(range(n_layers)):
        t = saved[l]
        wl = t["w"]
        g_next = vec("norm_mix_pre", l + 1) if l + 1 < n_layers else None
        dx, do3, dgp, dgn = _norm_bwd("norm_ffn_bwd", dx, dh, t["x3"], t["o3"], vec("norm_ffn_post", l), g_next,
                                      after=made)
        small["norm_ffn_post"][l] = dgp
        if dgn is not None:
            small["norm_mix_pre"][l + 1] = dgn
        made = reducer.advance(dx)
        d_down = _mm_tn_row("mm_down_dw", t["r"], do3)
        dup = _mm_nt_row("mm_down_dx", do3, wl["w_down"], BF16, epi=_relu2_bwd, extra=t["up"], after=made)
        d_up = _mm_tn_col("mm_up_dw", t["h3"], dup)
        dh3 = _mm_nt_col("mm_up_dx", dup, wl["w_up"], F32)
        made = reducer.add("%d0" % l, l, {"w_down": by_chip(d_down), "w_up": d_up})
        dx, do2, dgp, dgn = _norm_bwd("norm_xattn_bwd", dx, dh3, t["x2"], t["o2"], vec("norm_xattn_post", l),
                                      vec("norm_ffn_pre", l), after=made)
        small["norm_xattn_post"][l], small["norm_ffn_pre"][l] = dgp, dgn
        made = reducer.advance(dx)
        d_o = _mm_tn_row("mm_o_dw", t["a"], do2)
        da = _mm_nt_row("mm_o_dx", do2, wl["w_o"], BF16, after=made)
        dq, dk, dv = _attn_bwd(t["q"], t["k"], t["v"], da)
        d_q = _mm_tn_row("mm_q_dw", t["h2"], dq)
        d_k = _mm_tn_row("mm_k_dw", t["mn"], dk)
        d_v = _mm_tn_row("mm_v_dw", t["mn"], dv)
        made = reducer.add("%d1" % l, l, {"w_o": by_chip(d_o), "w_q": by_chip(d_q), "w_k": by_chip(d_k),
                                   "w_v": by_chip(d_v)})
        dh2 = _mm_nt_row("mm_q_dx", dq, wl["w_q"], F32, after=made)
        dmn = _mm_nt_row("mm_k_dx", dk, wl["w_k"], F32) + _mm_nt_row("mm_v_dx", dv, wl["w_v"], F32)
        _, _, _, small["norm_mem"][l] = _norm_bwd("norm_mem_bwd", None, dmn, mem, None, None, vec("norm_mem", l))
        dx, do, dgp, dgn = _norm_bwd("norm_mix_bwd", dx, dh2, t["x1"], t["o"], vec("norm_mix_post", l),
                                     vec("norm_xattn_pre", l))
        small["norm_mix_post"][l], small["norm_xattn_pre"][l] = dgp, dgn
        made = reducer.advance(dx)
        d_out = _mm_tn_row("mm_out_dw", t["y"], do)
        dy = _mm_nt_row("mm_out_dx", do, wl["w_out"], F32, after=made)
        dzu, dzv, dgv, dws, dbcol = _gmlp_bwd(t["z"], dy, t["gv"], t["ws"], t["bcol"])
        small["gmlp_v_gain"][l] = dgv.reshape(heads, hd)
        small["w_spatial"][l] = dws
        small["b_spatial"][l] = dbcol.reshape(heads, -1)
        dzp, dwp, dsp = _pool_bwd(t["z"], dy, w["w_pool"][l], vec("s_pool", l), (2 * gw) // pw, gw // pw)
        small["w_pool"][l], small["s_pool"][l] = dwp, dsp.reshape(-1)
        dhc, dlg, dlb = _ln_swish_bwd(t["hc"], dy, vec("conv_ln_g", l), vec("conv_ln_b", l), (gw + pw) // cw)
        small["conv_ln_g"][l], small["conv_ln_b"][l] = dlg.reshape(-1), dlb.reshape(-1)
        dval, dgate, dwd, dbd = _conv_bwd(t["z"], dhc, wl["w_dw"], taps, (2 * gw + pw) // cb,
                                          (2 * gw + pw + cw) // cb)
        small["w_dw"][l], small["b_dw"][l] = dwd, dbd.reshape(-1)
        dz = jnp.concatenate([dzu, dzv, dzp, dval, dgate], axis=1)
        made = reducer.advance(dz)
        d_in = _mm_tn_col("mm_in_dw", t["h1"], dz)
        dh = _mm_nt_col("mm_in_dx", dz, wl["w_in"], F32, after=made)
        made = reducer.add("%d2" % l, l, {"w_out": by_chip(d_out), "w_in": d_in})
    grad_x, _, _, dgn = _norm_bwd("norm_first_bwd", dx, dh, saved[0]["x"], None, None, vec("norm_mix_pre", 0),
                                   after=made)
    small["norm_mix_pre"][0] = dgn
    reducer.drain(grad_x)

    small_full = [jnp.stack([g.reshape(w[n].shape[1:]) if n != "w_dw" else g for g in small[n]]) for n in SMALL]
    total = _device_sum(_gather_devices(_pack(small_full)))
    grad = {}
    for n, g in zip(SMALL, _unpack(total, small_full)):
        grad[n] = lax.dynamic_slice_in_dim(g, chip * cb, cb, axis=2) if n == "w_dw" else g
    like = [w[n] for n in SMALL]
    dl, nm, nv = _adamw_flat("adamw_small", _pack(like), _pack([grad[n] for n in SMALL]),
                             _pack([m[n] for n in SMALL]), _pack([v[n] for n in SMALL]))
    delta, new_m, new_v = (dict(zip(SMALL, _unpack(p, like))) for p in (dl, nm, nv))
    for n in BIG:
        grad[n], delta[n], new_m[n], new_v[n] = reducer.done[n]

    return (loss, grad_x[None], *[grad[n] for n in WEIGHTS], *[delta[n] for n in WEIGHTS],
            *[new_m[n] for n in WEIGHTS], *[new_v[n] for n in WEIGHTS])


def kernel(x, mem, norm_mix_pre, norm_mix_post, w_in, w_out, gmlp_v_gain, w_spatial, b_spatial, w_pool, s_pool, w_dw, b_dw, conv_ln_g, conv_ln_b, norm_xattn_pre, norm_mem, norm_xattn_post, w_q, w_k, w_v, w_o, norm_ffn_pre, norm_ffn_post, w_up, w_down, loss_target, m_norm_mix_pre, m_norm_mix_post, m_w_in, m_w_out, m_gmlp_v_gain, m_w_spatial, m_b_spatial, m_w_pool, m_s_pool, m_w_dw, m_b_dw, m_conv_ln_g, m_conv_ln_b, m_norm_xattn_pre, m_norm_mem, m_norm_xattn_post, m_w_q, m_w_k, m_w_v, m_w_o, m_norm_ffn_pre, m_norm_ffn_post, m_w_up, m_w_down, v_norm_mix_pre, v_norm_mix_post, v_w_in, v_w_out, v_gmlp_v_gain, v_w_spatial, v_b_spatial, v_w_pool, v_s_pool, v_w_dw, v_b_dw, v_conv_ln_g, v_conv_ln_b, v_norm_xattn_pre, v_norm_mem, v_norm_xattn_post, v_w_q, v_w_k, v_w_v, v_w_o, v_norm_ffn_pre, v_norm_ffn_post, v_w_up, v_w_down):
    given = dict(locals())
    w = {n: given[n] for n in WEIGHTS}
    m = {n: given["m_" + n] for n in WEIGHTS}
    v = {n: given["v_" + n] for n in WEIGHTS}
    return _step(x[0], mem[0], loss_target[0], w, m, v)
```

```python
import functools

import jax
import jax.numpy as jnp
from jax import lax
from jax.experimental import pallas as pl
from jax.experimental.pallas import tpu as pltpu

F32 = jnp.float32
BF16 = jnp.bfloat16
MESH = pl.DeviceIdType.MESH

N_CHIPS = 4
N_DEVICES = 8
XATTN_HEADS = 4
POOL_WINDOWS = (2, 4, 8, 16)
RMS_EPS = 1e-6
LN_EPS = 1e-5
ADAM_LR, ADAM_B1, ADAM_B2, ADAM_EPS, ADAM_WD, ADAM_STEP = 0.001, 0.9, 0.999, 1e-08, 0.01, 10

V7X_LANES = 128
V7X_VMEM_LIMIT = 56 * 1024 * 1024
ROW_TILE = 256
MM_TILE_M, MM_TILE_N, MM_TILE_K = 1024, 1024, 1024

ANY = pl.BlockSpec(memory_space=pl.ANY)
HBM = pl.BlockSpec(memory_space=pltpu.HBM)
SEM = pl.BlockSpec(memory_space=pltpu.SEMAPHORE)


def _tile(dim, pref):
    if dim <= pref:
        return dim
    t = (pref // V7X_LANES) * V7X_LANES
    while t >= V7X_LANES:
        if dim % t == 0:
            return t
        t -= V7X_LANES
    return dim


def _params(sem=None):
    return pltpu.CompilerParams(dimension_semantics=sem, vmem_limit_bytes=V7X_VMEM_LIMIT)


NN = (((1,), (0,)), ((), ()))
NT = (((1,), (1,)), ((), ()))
TN = (((0,), (0,)), ((), ()))


def _mm(name, a, b, *, dn, grid, a_spec, b_spec, o_specs, out_shapes, acc_shape, epi=None, extra=None,
        extra_spec=None, after=()):
    nk = grid[2]
    n_out = len(out_shapes)
    has_extra = extra is not None
    after = list(after)

    def body(*refs):
        a_ref, b_ref = refs[0], refs[1]
        pos = 2
        e_ref = None
        if has_extra:
            e_ref = refs[pos]
            pos += 1
        pos += len(after)
        o_refs = refs[pos:pos + n_out]
        acc = refs[pos + n_out]
        k = pl.program_id(2)

        @pl.when(k == 0)
        def _():
            acc[...] = jnp.zeros_like(acc)

        acc[...] += lax.dot_general(a_ref[...].astype(BF16), b_ref[...].astype(BF16), dn,
                                    preferred_element_type=F32)

        @pl.when(k == nk - 1)
        def _():
            if epi is None:
                vals = (acc[...],)
            elif has_extra:
                vals = epi(acc[...], e_ref[...])
            else:
                vals = epi(acc[...])
            for o, v in zip(o_refs, vals):
                o[...] = v.astype(o.dtype)

    ins, in_specs = [a, b], [a_spec, b_spec]
    if has_extra:
        ins.append(extra)
        in_specs.append(extra_spec)
    ins += after
    in_specs += [ANY] * len(after)
    outs = pl.pallas_call(
        body, name=name, grid=grid, in_specs=in_specs, out_specs=list(o_specs), out_shape=list(out_shapes),
        scratch_shapes=[pltpu.VMEM(acc_shape, F32)],
        compiler_params=_params(("parallel", "parallel", "arbitrary")))(*ins)
    return outs


def _mm_nn_row(name, a, w, out_dtypes, epi=None):
    m, k = a.shape
    n = w.shape[1]
    tm, tn, tk = _tile(m, MM_TILE_M), _tile(n, MM_TILE_N), _tile(k, MM_TILE_K)
    o_spec = pl.BlockSpec((tm, tn), lambda i, j, kk: (i, j))
    return _mm(name, a, w, dn=NN, grid=(m // tm, n // tn, k // tk),
               a_spec=pl.BlockSpec((tm, tk), lambda i, j, kk: (i, kk)),
               b_spec=pl.BlockSpec((tk, tn), lambda i, j, kk: (kk, j)),
               o_specs=[o_spec] * len(out_dtypes),
               out_shapes=[jax.ShapeDtypeStruct((m, n), d) for d in out_dtypes], acc_shape=(tm, tn), epi=epi)


def _mm_nn_col(name, a, w, out_dtypes, epi=None):
    m, k = a.shape
    c = w.shape[2]
    tm, tn, tk = _tile(m, MM_TILE_M), _tile(c, MM_TILE_N), _tile(k, MM_TILE_K)
    nb = c // tn
    o_spec = pl.BlockSpec((tm, tn), lambda i, j, kk: (i, j))
    return _mm(name, a, w, dn=NN, grid=(m // tm, N_CHIPS * nb, k // tk),
               a_spec=pl.BlockSpec((tm, tk), lambda i, j, kk: (i, kk)),
               b_spec=pl.BlockSpec((None, tk, tn), lambda i, j, kk: (j // nb, kk, j % nb)),
               o_specs=[o_spec] * len(out_dtypes),
               out_shapes=[jax.ShapeDtypeStruct((m, N_CHIPS * c), d) for d in out_dtypes], acc_shape=(tm, tn),
               epi=epi)


def _mm_nt_row(name, dy, w, out_dtype, epi=None, extra=None, after=()):
    m, n = dy.shape
    k = w.shape[0]
    tm, tn, tk = _tile(m, MM_TILE_M), _tile(k, MM_TILE_N), _tile(n, MM_TILE_K)
    o_spec = pl.BlockSpec((tm, tn), lambda i, j, kk: (i, j))
    return _mm(name, dy, w, dn=NT, grid=(m // tm, k // tn, n // tk),
               a_spec=pl.BlockSpec((tm, tk), lambda i, j, kk: (i, kk)),
               b_spec=pl.BlockSpec((tn, tk), lambda i, j, kk: (j, kk)),
               o_specs=[o_spec], out_shapes=[jax.ShapeDtypeStruct((m, k), out_dtype)], acc_shape=(tm, tn),
               epi=epi, extra=extra, extra_spec=o_spec, after=after)[0]


def _mm_nt_col(name, dy, w, out_dtype, after=()):
    m = dy.shape[0]
    k, c = w.shape[1], w.shape[2]
    tm, tn, tk = _tile(m, MM_TILE_M), _tile(k, MM_TILE_N), _tile(c, MM_TILE_K)
    kb = c // tk
    return _mm(name, dy, w, dn=NT, grid=(m // tm, k // tn, N_CHIPS * kb),
               a_spec=pl.BlockSpec((tm, tk), lambda i, j, kk: (i, kk)),
               b_spec=pl.BlockSpec((None, tn, tk), lambda i, j, kk: (kk // kb, j, kk % kb)),
               o_specs=[pl.BlockSpec((tm, tn), lambda i, j, kk: (i, j))],
               out_shapes=[jax.ShapeDtypeStruct((m, k), out_dtype)], acc_shape=(tm, tn), after=after)[0]


def _mm_tn_row(name, a, dy):
    t, m = a.shape
    n = dy.shape[1]
    tm, tn, tk = _tile(m, MM_TILE_M), _tile(n, MM_TILE_N), _tile(t, MM_TILE_K)
    return _mm(name, a, dy, dn=TN, grid=(m // tm, n // tn, t // tk),
               a_spec=pl.BlockSpec((tk, tm), lambda i, j, kk: (kk, i)),
               b_spec=pl.BlockSpec((tk, tn), lambda i, j, kk: (kk, j)),
               o_specs=[pl.BlockSpec((tm, tn), lambda i, j, kk: (i, j))],
               out_shapes=[jax.ShapeDtypeStruct((m, n), F32)], acc_shape=(tm, tn))[0]


def _mm_tn_col(name, a, dy):
    t, m = a.shape
    c = dy.shape[1] // N_CHIPS
    tm, tn, tk = _tile(m, MM_TILE_M), _tile(c, MM_TILE_N), _tile(t, MM_TILE_K)
    nb = c // tn
    return _mm(name, a, dy, dn=TN, grid=(m // tm, N_CHIPS * nb, t // tk),
               a_spec=pl.BlockSpec((tk, tm), lambda i, j, kk: (kk, i)),
               b_spec=pl.BlockSpec((tk, tn), lambda i, j, kk: (kk, j)),
               o_specs=[pl.BlockSpec((None, tm, tn), lambda i, j, kk: (j // nb, i, j % nb))],
               out_shapes=[jax.ShapeDtypeStruct((N_CHIPS, m, c), F32)], acc_shape=(tm, tn))[0]


def _rms(x, g):
    r = lax.rsqrt(jnp.mean(x * x, axis=-1, keepdims=True) + RMS_EPS)
    return x * r * g


def _rms_bwd(x, g, dy):
    r = lax.rsqrt(jnp.mean(x * x, axis=-1, keepdims=True) + RMS_EPS)
    xr = x * r
    dyg = dy * g
    dx = r * (dyg - xr * jnp.mean(dyg * xr, axis=-1, keepdims=True))
    return dx, jnp.sum(dy * xr, axis=0, keepdims=True)


def _norm_fwd(name, x, o, g_post, g_next, after=()):
    after = list(after)
    s, d = x.shape
    tr = _tile(s, ROW_TILE)
    has_prev, has_next = o is not None, g_next is not None
    row = pl.BlockSpec((tr, d), lambda i: (i, 0))
    vec = pl.BlockSpec((1, d), lambda i: (0, 0))

    def body(*refs):
        refs = list(refs)
        xn = refs.pop(0)[...]
        if has_prev:
            o_ref, gp_ref = refs.pop(0), refs.pop(0)
            xn = xn + _rms(o_ref[...], gp_ref[...])
        gn_ref = refs.pop(0) if has_next else None
        del refs[:len(after)]
        if has_prev:
            refs.pop(0)[...] = xn
        if has_next:
            refs.pop(0)[...] = _rms(xn, gn_ref[...]).astype(BF16)

    ins, in_specs = [x], [row]
    if has_prev:
        ins += [o, g_post]
        in_specs += [row, vec]
    if has_next:
        ins.append(g_next)
        in_specs.append(vec)
    ins += after
    in_specs += [ANY] * len(after)
    out_shapes, out_specs = [], []
    if has_prev:
        out_shapes.append(jax.ShapeDtypeStruct((s, d), F32))
        out_specs.append(row)
    if has_next:
        out_shapes.append(jax.ShapeDtypeStruct((s, d), BF16))
        out_specs.append(row)
    outs = pl.pallas_call(body, name=name, grid=(s // tr,), in_specs=in_specs, out_specs=out_specs,
                          out_shape=out_shapes, compiler_params=_params(("parallel",)))(*ins)
    outs = list(outs)
    x_new = outs.pop(0) if has_prev else x
    h = outs.pop(0) if has_next else None
    return x_new, h


def _norm_bwd(name, dxn, dh, xn, o, g_post, g_next, after=()):
    after = list(after)
    s, d = xn.shape
    tr = _tile(s, ROW_TILE)
    has_prev, has_next, has_dxn = o is not None, dh is not None, dxn is not None
    row = pl.BlockSpec((tr, d), lambda i: (i, 0))
    vec = pl.BlockSpec((1, d), lambda i: (0, 0))

    def body(*refs):
        refs = list(refs)
        first = pl.program_id(0) == 0
        dxn_ref = refs.pop(0) if has_dxn else None
        dh_ref = refs.pop(0) if has_next else None
        xn_ref = refs.pop(0)
        if has_prev:
            o_ref, gp_ref = refs.pop(0), refs.pop(0)
        gn_ref = refs.pop(0) if has_next else None
        del refs[:len(after)]
        dx_ref = refs.pop(0)
        if has_prev:
            do_ref, dgp_ref = refs.pop(0), refs.pop(0)
        dgn_ref = refs.pop(0) if has_next else None

        def accumulate(ref, val):
            @pl.when(first)
            def _():
                ref[...] = val

            @pl.when(jnp.logical_not(first))
            def _():
                ref[...] += val

        dx = dxn_ref[...] if has_dxn else None
        if has_next:
            dxh, dgn = _rms_bwd(xn_ref[...], gn_ref[...], dh_ref[...].astype(F32))
            dx = dxh if dx is None else dx + dxh
            accumulate(dgn_ref, dgn)
        dx_ref[...] = dx
        if has_prev:
            do, dgp = _rms_bwd(o_ref[...], gp_ref[...], dx)
            do_ref[...] = do.astype(BF16)
            accumulate(dgp_ref, dgp)

    ins, in_specs = [], []
    if has_dxn:
        ins.append(dxn)
        in_specs.append(row)
    if has_next:
        ins.append(dh)
        in_specs.append(row)
    ins.append(xn)
    in_specs.append(row)
    if has_prev:
        ins += [o, g_post]
        in_specs += [row, vec]
    if has_next:
        ins.append(g_next)
        in_specs.append(vec)
    ins += after
    in_specs += [ANY] * len(after)
    out_shapes, out_specs = [jax.ShapeDtypeStruct((s, d), F32)], [row]
    if has_prev:
        out_shapes += [jax.ShapeDtypeStruct((s, d), BF16), jax.ShapeDtypeStruct((1, d), F32)]
        out_specs += [row, vec]
    if has_next:
        out_shapes.append(jax.ShapeDtypeStruct((1, d), F32))
        out_specs.append(vec)
    outs = list(pl.pallas_call(body, name=name, grid=(s // tr,), in_specs=in_specs, out_specs=out_specs,
                               out_shape=out_shapes, compiler_params=_params(("arbitrary",)))(*ins))
    dx = outs.pop(0)
    do, dgp = (outs.pop(0), outs.pop(0)) if has_prev else (None, None)
    dgn = outs.pop(0) if has_next else None
    return dx, do, dgp, dgn


def _loss_head(y, target):
    s, d = y.shape
    tr = _tile(s, ROW_TILE)
    row = pl.BlockSpec((tr, d), lambda i: (i, 0))
    vec = pl.BlockSpec((1, d), lambda i: (0, 0))

    def body(y_ref, t_ref, dy_ref, l_ref):
        err = y_ref[...] - t_ref[...]
        dy_ref[...] = err * (1.0 / d)
        part = jnp.sum(err * err, axis=0, keepdims=True) * (0.5 / d)

        @pl.when(pl.program_id(0) == 0)
        def _():
            l_ref[...] = part

        @pl.when(pl.program_id(0) != 0)
        def _():
            l_ref[...] += part

    return pl.pallas_call(body, name="loss_head", grid=(s // tr,), in_specs=[row, row], out_specs=[row, vec],
                          out_shape=[jax.ShapeDtypeStruct((s, d), F32), jax.ShapeDtypeStruct((1, d), F32)],
                          compiler_params=_params(("arbitrary",)))(y, target)


@jax.custom_vjp
def _bdot(a, b):
    return jnp.dot(a.astype(BF16), b.astype(BF16), preferred_element_type=F32)


def _bdot_fwd(a, b):
    return _bdot(a, b), (a, b)


def _bdot_bwd(res, ct):
    a, b = res
    ctb = ct.astype(BF16)
    da = lax.dot_general(ctb, b.astype(BF16), NT, preferred_element_type=F32)
    db = lax.dot_general(a.astype(BF16), ctb, TN, preferred_element_type=F32)
    return da, db


_bdot.defvjp(_bdot_fwd, _bdot_bwd)


@functools.partial(jax.custom_vjp, nondiff_argnums=(1,))
def _shift(x, k):
    n = x.shape[0]
    if k == 0:
        return x
    rolled = pltpu.roll(x, k % n, 0)
    t = lax.broadcasted_iota(jnp.int32, x.shape, 0)
    keep = (t >= k) if k > 0 else (t < n + k)
    return jnp.where(keep, rolled, 0.0)


def _shift_fwd(x, k):
    return _shift(x, k), None


def _shift_bwd(k, _, ct):
    return (_shift(ct, -k),)


_shift.defvjp(_shift_fwd, _shift_bwd)


def _sigmoid(x):
    return 1.0 / (1.0 + jnp.exp(-x))


def _layer_norm(x, g, b=None):
    mu = jnp.mean(x, axis=-1, keepdims=True)
    xc = x - mu
    var = jnp.mean(xc * xc, axis=-1, keepdims=True)
    y = xc * lax.rsqrt(var + LN_EPS) * g
    return y if b is None else y + b


def _gmlp_chunk(zu, zv, gv, w, bcol):
    ch = w.shape[0]
    u = jax.nn.gelu(zu)
    vn = _layer_norm(jax.nn.gelu(zv), gv)
    t = lax.broadcasted_iota(jnp.int32, (ch, ch), 0)
    s = lax.broadcasted_iota(jnp.int32, (ch, ch), 1)
    wm = jnp.where(t >= s, w, 0.0)
    return u * (_bdot(wm, vn) + bcol)


def _gmlp_specs(seq, heads, hd, ch, u_off, v_off):
    col = lambda off: pl.BlockSpec((seq, hd), lambda h: (0, off + h))
    return (col(u_off), col(v_off), pl.BlockSpec((None, 1, hd), lambda h: (h, 0, 0)),
            pl.BlockSpec((None, ch, ch), lambda h: (h, 0, 0)), pl.BlockSpec((None, ch, 1), lambda h: (h, 0, 0)))


def _gmlp_fwd(z, gv, ws, bcol):
    seq = z.shape[0]
    heads, _, hd = gv.shape
    ch = ws.shape[-1]
    zu_s, zv_s, gv_s, w_s, b_s = _gmlp_specs(seq, heads, hd, ch, 0, heads)

    def body(zu_ref, zv_ref, gv_ref, w_ref, b_ref, y_ref):
        gvv, w, bc = gv_ref[...], w_ref[...], b_ref[...]

        def step(c, carry):
            rows = pl.ds(pl.multiple_of(c * ch, ch), ch)
            y_ref[rows, :] = _gmlp_chunk(zu_ref[rows, :], zv_ref[rows, :], gvv, w, bc).astype(BF16)
            return carry

        lax.fori_loop(0, seq // ch, step, 0)

    return pl.pallas_call(body, name="gmlp_fwd", grid=(heads,), in_specs=[zu_s, zv_s, gv_s, w_s, b_s],
                          out_specs=pl.BlockSpec((seq, hd), lambda h: (0, h)),
                          out_shape=jax.ShapeDtypeStruct((seq, heads * hd), BF16),
                          compiler_params=_params(("parallel",)))(z, z, gv, ws, bcol)


def _gmlp_bwd(z, dy, gv, ws, bcol):
    seq = z.shape[0]
    heads, _, hd = gv.shape
    ch = ws.shape[-1]
    zu_s, zv_s, gv_s, w_s, b_s = _gmlp_specs(seq, heads, hd, ch, 0, heads)
    col = pl.BlockSpec((seq, hd), lambda h: (0, h))

    def body(zu_ref, zv_ref, dy_ref, gv_ref, w_ref, b_ref, dzu_ref, dzv_ref, dgv_ref, dw_ref, db_ref):
        gvv, w, bc = gv_ref[...], w_ref[...], b_ref[...]

        def step(c, carry):
            dgv, dw, db = carry
            rows = pl.ds(pl.multiple_of(c * ch, ch), ch)
            _, vjp = jax.vjp(_gmlp_chunk, zu_ref[rows, :], zv_ref[rows, :], gvv, w, bc)
            dzu, dzv, dgv_c, dw_c, db_c = vjp(dy_ref[rows, :])
            dzu_ref[rows, :] = dzu.astype(BF16)
            dzv_ref[rows, :] = dzv.astype(BF16)
            return dgv + dgv_c, dw + dw_c, db + db_c

        zero = (jnp.zeros((1, hd), F32), jnp.zeros((ch, ch), F32), jnp.zeros((ch, 1), F32))
        dgv, dw, db = lax.fori_loop(0, seq // ch, step, zero)
        dgv_ref[...] = dgv
        dw_ref[...] = dw
        db_ref[...] = db

    return pl.pallas_call(
        body, name="gmlp_bwd", grid=(heads,), in_specs=[zu_s, zv_s, col, gv_s, w_s, b_s],
        out_specs=[col, col, gv_s, w_s, b_s],
        out_shape=[jax.ShapeDtypeStruct((seq, heads * hd), BF16), jax.ShapeDtypeStruct((seq, heads * hd), BF16),
                   jax.ShapeDtypeStruct(gv.shape, F32), jax.ShapeDtypeStruct(ws.shape, F32),
                   jax.ShapeDtypeStruct(bcol.shape, F32)],
        compiler_params=_params(("parallel",)))(z, z, dy, gv, ws, bcol)


def _pool_group(p, w, s, window):
    win, span = p, 1
    while span < window:
        win = win + _shift(win, span)
        span *= 2
    t = lax.broadcasted_iota(jnp.int32, (p.shape[0], 1), 0).astype(F32)
    cnt = jnp.minimum(t + 1.0, float(window))
    return _bdot(win / cnt - p, w) * s


def _pool_fwd(z, w_pool, s_pool, col_block):
    seq = z.shape[0]
    groups, gw, _ = w_pool.shape
    pw = groups * gw

    def body(p_ref, w_ref, s_ref, y_ref):
        for g in range(groups):
            cols = slice(g * gw, (g + 1) * gw)
            y_ref[:, cols] = _pool_group(p_ref[:, cols], w_ref[g], s_ref[:, cols], POOL_WINDOWS[g]).astype(BF16)

    return pl.pallas_call(
        body, name="pool_fwd", grid=(1,),
        in_specs=[pl.BlockSpec((seq, pw), lambda i: (0, col_block)),
                  pl.BlockSpec((groups, gw, gw), lambda i: (0, 0, 0)), pl.BlockSpec((1, pw), lambda i: (0, 0))],
        out_specs=pl.BlockSpec((seq, pw), lambda i: (0, 0)), out_shape=jax.ShapeDtypeStruct((seq, pw), BF16),
        compiler_params=_params(("arbitrary",)))(z, w_pool, s_pool)


def _pool_bwd(z, dy, w_pool, s_pool, col_block, dy_block):
    seq = z.shape[0]
    groups, gw, _ = w_pool.shape
    pw = groups * gw

    def body(p_ref, dy_ref, w_ref, s_ref, dp_ref, dw_ref, ds_ref):
        for g in range(groups):
            cols = slice(g * gw, (g + 1) * gw)
            _, vjp = jax.vjp(functools.partial(_pool_group, window=POOL_WINDOWS[g]), p_ref[:, cols], w_ref[g],
                             s_ref[:, cols])
            dp, dw, ds = vjp(dy_ref[:, cols])
            dp_ref[:, cols] = dp.astype(BF16)
            dw_ref[g] = dw
            ds_ref[:, cols] = ds

    return pl.pallas_call(
        body, name="pool_bwd", grid=(1,),
        in_specs=[pl.BlockSpec((seq, pw), lambda i: (0, col_block)),
                  pl.BlockSpec((seq, pw), lambda i: (0, dy_block)),
                  pl.BlockSpec((groups, gw, gw), lambda i: (0, 0, 0)), pl.BlockSpec((1, pw), lambda i: (0, 0))],
        out_specs=[pl.BlockSpec((seq, pw), lambda i: (0, 0)), pl.BlockSpec((groups, gw, gw), lambda i: (0, 0, 0)),
                   pl.BlockSpec((1, pw), lambda i: (0, 0))],
        out_shape=[jax.ShapeDtypeStruct((seq, pw), BF16), jax.ShapeDtypeStruct(w_pool.shape, F32),
                   jax.ShapeDtypeStruct((1, pw), F32)],
        compiler_params=_params(("arbitrary",)))(z, dy, w_pool, s_pool)


def _conv_fwd(z, w_dw, taps, b_dw, val_block, gate_block):
    seq = z.shape[0]
    rows, cb = w_dw.shape[1], w_dw.shape[2]

    def body(val_ref, gate_ref, w_ref, b_ref, out_ref):
        h = val_ref[...] * _sigmoid(gate_ref[...])
        acc = jnp.broadcast_to(b_ref[...], h.shape)
        for d in range(taps):
            acc = acc + w_ref[pl.ds(taps - 1 - d, 1), :] * _shift(h, d)
        out_ref[...] = acc

    return pl.pallas_call(
        body, name="conv_fwd", grid=(N_CHIPS,),
        in_specs=[pl.BlockSpec((seq, cb), lambda j: (0, val_block + j)),
                  pl.BlockSpec((seq, cb), lambda j: (0, gate_block + j)),
                  pl.BlockSpec((None, rows, cb), lambda j: (j, 0, 0)),
                  pl.BlockSpec((1, cb), lambda j: (0, j))],
        out_specs=pl.BlockSpec((seq, cb), lambda j: (0, j)),
        out_shape=jax.ShapeDtypeStruct((seq, N_CHIPS * cb), F32),
        compiler_params=_params(("parallel",)))(z, z, w_dw, b_dw)


def _conv_bwd(z, dout, w_dw, taps, val_block, gate_block):
    seq = z.shape[0]
    rows, cb = w_dw.shape[1], w_dw.shape[2]
    col = pl.BlockSpec((seq, cb), lambda j: (0, j))

    def body(val_ref, gate_ref, do_ref, w_ref, dval_ref, dgate_ref, dw_ref, db_ref):
        val, sg, do = val_ref[...], _sigmoid(gate_ref[...]), do_ref[...]
        h = val * sg
        db_ref[...] = jnp.sum(do, axis=0, keepdims=True)
        dh = jnp.zeros_like(h)
        for d in range(taps):
            k = taps - 1 - d
            dw_ref[pl.ds(k, 1), :] = jnp.sum(do * _shift(h, d), axis=0, keepdims=True)
            dh = dh + w_ref[pl.ds(k, 1), :] * _shift(do, -d)
        dval_ref[...] = (dh * sg).astype(BF16)
        dgate_ref[...] = (dh * val * sg * (1.0 - sg)).astype(BF16)

    return pl.pallas_call(
        body, name="conv_bwd", grid=(N_CHIPS,),
        in_specs=[pl.BlockSpec((seq, cb), lambda j: (0, val_block + j)),
                  pl.BlockSpec((seq, cb), lambda j: (0, gate_block + j)), col,
                  pl.BlockSpec((None, rows, cb), lambda j: (j, 0, 0))],
        out_specs=[col, col, pl.BlockSpec((taps, cb), lambda j: (0, j)), pl.BlockSpec((1, cb), lambda j: (0, j))],
        out_shape=[jax.ShapeDtypeStruct((seq, N_CHIPS * cb), BF16), jax.ShapeDtypeStruct((seq, N_CHIPS * cb), BF16),
                   jax.ShapeDtypeStruct((taps, N_CHIPS * cb), F32), jax.ShapeDtypeStruct((1, N_CHIPS * cb), F32)],
        compiler_params=_params(("parallel",)))(z, z, dout, w_dw)


def _ln_swish(hc, g, b):
    y = _layer_norm(hc, g, b)
    return y * _sigmoid(y)


def _ln_swish_fwd(hc, g, b):
    s, cw = hc.shape
    tr = _tile(s, ROW_TILE)
    row = pl.BlockSpec((tr, cw), lambda i: (i, 0))
    vec = pl.BlockSpec((1, cw), lambda i: (0, 0))

    def body(h_ref, g_ref, b_ref, y_ref):
        y_ref[...] = _ln_swish(h_ref[...], g_ref[...], b_ref[...]).astype(BF16)

    return pl.pallas_call(body, name="ln_swish_fwd", grid=(s // tr,), in_specs=[row, vec, vec], out_specs=row,
                          out_shape=jax.ShapeDtypeStruct((s, cw), BF16),
                          compiler_params=_params(("parallel",)))(hc, g, b)


def _ln_swish_bwd(hc, dy, g, b, dy_block):
    s, cw = hc.shape
    tr = _tile(s, ROW_TILE)
    row = pl.BlockSpec((tr, cw), lambda i: (i, 0))
    vec = pl.BlockSpec((1, cw), lambda i: (0, 0))

    def body(h_ref, dy_ref, g_ref, b_ref, dh_ref, dg_ref, db_ref):
        _, vjp = jax.vjp(_ln_swish, h_ref[...], g_ref[...], b_ref[...])
        dh, dg, db = vjp(dy_ref[...])
        dh_ref[...] = dh

        @pl.when(pl.program_id(0) == 0)
        def _():
            dg_ref[...] = dg
            db_ref[...] = db

        @pl.when(pl.program_id(0) != 0)
        def _():
            dg_ref[...] += dg
            db_ref[...] += db

    return pl.pallas_call(
        body, name="ln_swish_bwd", grid=(s // tr,),
        in_specs=[row, pl.BlockSpec((tr, cw), lambda i: (i, dy_block)), vec, vec], out_specs=[row, vec, vec],
        out_shape=[jax.ShapeDtypeStruct((s, cw), F32), jax.ShapeDtypeStruct((1, cw), F32),
                   jax.ShapeDtypeStruct((1, cw), F32)],
        compiler_params=_params(("arbitrary",)))(hc, dy, g, b)


def _attn_probs(q, k, scale):
    s = lax.dot_general(q, k, NT, preferred_element_type=F32) * scale
    e = jnp.exp(s - jnp.max(s, axis=-1, keepdims=True))
    return e / jnp.sum(e, axis=-1, keepdims=True)


def _attn_fwd(q, k, v):
    seq, d = q.shape
    mem = k.shape[0]
    hd = d // XATTN_HEADS
    scale = hd ** -0.5
    qs = pl.BlockSpec((seq, hd), lambda h: (0, h))
    ms = pl.BlockSpec((mem, hd), lambda h: (0, h))

    def body(q_ref, k_ref, v_ref, a_ref):
        p = _attn_probs(q_ref[...], k_ref[...], scale)
        a_ref[...] = jnp.dot(p.astype(BF16), v_ref[...], preferred_element_type=F32).astype(BF16)

    return pl.pallas_call(body, name="attn_fwd", grid=(XATTN_HEADS,), in_specs=[qs, ms, ms], out_specs=qs,
                          out_shape=jax.ShapeDtypeStruct((seq, d), BF16),
                          compiler_params=_params(("parallel",)))(q, k, v)


def _attn_bwd(q, k, v, da):
    seq, d = q.shape
    mem = k.shape[0]
    hd = d // XATTN_HEADS
    scale = hd ** -0.5
    qs = pl.BlockSpec((seq, hd), lambda h: (0, h))
    ms = pl.BlockSpec((mem, hd), lambda h: (0, h))

    def body(q_ref, k_ref, v_ref, da_ref, dq_ref, dk_ref, dv_ref):
        q_, k_, v_, da_ = q_ref[...], k_ref[...], v_ref[...], da_ref[...]
        p = _attn_probs(q_, k_, scale)
        dv_ref[...] = lax.dot_general(p.astype(BF16), da_, TN, preferred_element_type=F32).astype(BF16)
        dp = lax.dot_general(da_, v_, NT, preferred_element_type=F32)
        ds = (p * (dp - jnp.sum(dp * p, axis=-1, keepdims=True)) * scale).astype(BF16)
        dq_ref[...] = jnp.dot(ds, k_, preferred_element_type=F32).astype(BF16)
        dk_ref[...] = lax.dot_general(ds, q_, TN, preferred_element_type=F32).astype(BF16)

    return pl.pallas_call(
        body, name="attn_bwd", grid=(XATTN_HEADS,), in_specs=[qs, ms, ms, qs], out_specs=[qs, ms, ms],
        out_shape=[jax.ShapeDtypeStruct((seq, d), BF16), jax.ShapeDtypeStruct((mem, d), BF16),
                   jax.ShapeDtypeStruct((mem, d), BF16)],
        compiler_params=_params(("parallel",)))(q, k, v, da)


def _place_shard(name, place, w, l, dtype, after=()):
    _, r, c = w.shape
    tr = _tile(r, 2 * ROW_TILE) if r % 16 == 0 else r
    after = list(after)

    def body(place_ref, w_ref, *refs):
        refs[-1][...] = w_ref[...].astype(dtype)

    return pl.pallas_call(
        body, name=name,
        grid_spec=pltpu.PrefetchScalarGridSpec(
            num_scalar_prefetch=1, grid=(r // tr,),
            in_specs=[pl.BlockSpec((None, tr, c), lambda i, p: (l, i, 0))] + [ANY] * len(after),
            out_specs=pl.BlockSpec((None, tr, c), lambda i, p: (p[1], i, 0))),
        out_shape=jax.ShapeDtypeStruct((N_CHIPS, r, c), dtype),
        compiler_params=_params(("parallel",)))(place, w, *after)


def _place_flat(place, flat):
    rows, lanes = flat.shape

    def body(place_ref, f_ref, o_ref):
        o_ref[...] = f_ref[...]

    return pl.pallas_call(
        body, name="place_flat",
        grid_spec=pltpu.PrefetchScalarGridSpec(
            num_scalar_prefetch=1, grid=(1,), in_specs=[pl.BlockSpec((rows, lanes), lambda i, p: (0, 0))],
            out_specs=pl.BlockSpec((None, rows, lanes), lambda i, p: (p[2], 0, 0))),
        out_shape=jax.ShapeDtypeStruct((N_DEVICES, rows, lanes), flat.dtype),
        compiler_params=_params(("arbitrary",)))(place, flat)


def _pair_sum(name, place, dw, got):
    n, _, r2, c = dw.shape
    tr = _tile(r2, ROW_TILE)

    def body(place_ref, own_ref, got_ref, s_ref, t_ref):
        val = (own_ref[...] + got_ref[...]).astype(BF16)
        s_ref[...] = val

        @pl.when(pl.program_id(1) == place_ref[1])
        def _():
            t_ref[...] = val

    slab = pl.BlockSpec((None, tr, c), lambda i, j, p: (j, i, 0))
    sds = jax.ShapeDtypeStruct((n, r2, c), BF16)
    return pl.pallas_call(
        body, name=name,
        grid_spec=pltpu.PrefetchScalarGridSpec(
            num_scalar_prefetch=1, grid=(r2 // tr, n),
            in_specs=[pl.BlockSpec((None, None, tr, c), lambda i, j, p: (j, p[0], i, 0)), slab],
            out_specs=[slab, pl.BlockSpec((None, tr, c), lambda i, j, p: (p[1], i, 0))]),
        out_shape=[sds, sds], compiler_params=_params(("parallel", "arbitrary")))(place, dw, got)


def _chip_sum(name, place, parts):
    n, r2, c = parts.shape
    tr = _tile(r2, ROW_TILE)

    def body(place_ref, *refs):
        o_ref = refs[n]
        acc = refs[0][...].astype(F32)
        for j in range(1, n):
            acc = acc + refs[j][...].astype(F32)
        o_ref[...] = acc

    part = lambda j: pl.BlockSpec((None, tr, c), lambda i, p: (j, i, 0))
    return pl.pallas_call(
        body, name=name,
        grid_spec=pltpu.PrefetchScalarGridSpec(
            num_scalar_prefetch=1, grid=(r2 // tr,), in_specs=[part(j) for j in range(n)],
            out_specs=pl.BlockSpec((None, tr, c), lambda i, p: (p[0], i, 0))),
        out_shape=jax.ShapeDtypeStruct((2, r2, c), F32),
        compiler_params=_params(("parallel",)))(place, *([parts] * n))


def _device_sum(parts):
    n, rows, lanes = parts.shape
    tr = _tile(rows, 4 * ROW_TILE) if rows % 8 == 0 else rows

    def body(p_ref, o_ref):
        acc = p_ref[0]
        for j in range(1, n):
            acc = acc + p_ref[j]
        o_ref[...] = acc

    return pl.pallas_call(
        body, name="device_sum", grid=(rows // tr,), in_specs=[pl.BlockSpec((n, tr, lanes), lambda i: (0, i, 0))],
        out_specs=pl.BlockSpec((tr, lanes), lambda i: (i, 0)), out_shape=jax.ShapeDtypeStruct((rows, lanes), F32),
        compiler_params=_params(("parallel",)))(parts)


def _adam_update(w, g, m, v):
    nm = ADAM_B1 * m + (1.0 - ADAM_B1) * g
    nv = ADAM_B2 * v + (1.0 - ADAM_B2) * (g * g)
    c1 = 1.0 - ADAM_B1 ** ADAM_STEP
    c2 = 1.0 - ADAM_B2 ** ADAM_STEP
    return -ADAM_LR * ((nm / c1) / (jnp.sqrt(nv / c2) + ADAM_EPS) + ADAM_WD * w), nm, nv


def _adamw_layer(name, l, w, g, m, v, prev):
    n_l, r, c = w.shape
    tr = _tile(r, ROW_TILE)
    slab = pl.BlockSpec((None, tr, c), lambda i: (l, i, 0))

    def body(w_ref, g_ref, m_ref, v_ref, *refs):
        go_ref, d_ref, nm_ref, nv_ref = refs[-4:]
        g_ = g_ref[...]
        delta, nm, nv = _adam_update(w_ref[...], g_, m_ref[...], v_ref[...])
        go_ref[...] = g_
        d_ref[...] = delta
        nm_ref[...] = nm
        nv_ref[...] = nv

    ins = [w, g, m, v]
    in_specs = [slab, pl.BlockSpec((tr, c), lambda i: (i, 0)), slab, slab]
    aliases = {}
    if prev is not None:
        aliases = {len(ins) + i: i for i in range(4)}
        ins += list(prev)
        in_specs += [ANY] * 4
    sds = jax.ShapeDtypeStruct((n_l, r, c), F32)
    return pl.pallas_call(body, name=name, grid=(r // tr,), in_specs=in_specs, out_specs=[slab] * 4,
                          out_shape=[sds] * 4, input_output_aliases=aliases,
                          compiler_params=_params(("parallel",)))(*ins)


def _adamw_flat(name, w, g, m, v):
    rows, cols = w.shape
    tr = _tile(rows, ROW_TILE) if rows % 8 == 0 else rows
    spec = pl.BlockSpec((tr, cols), lambda i: (i, 0))

    def body(w_ref, g_ref, m_ref, v_ref, d_ref, nm_ref, nv_ref):
        d_ref[...], nm_ref[...], nv_ref[...] = _adam_update(w_ref[...], g_ref[...], m_ref[...], v_ref[...])

    sds = jax.ShapeDtypeStruct((rows, cols), F32)
    return pl.pallas_call(body, name=name, grid=(rows // tr,), in_specs=[spec] * 4, out_specs=[spec] * 3,
                          out_shape=[sds] * 3, compiler_params=_params(("parallel",)))(w, g, m, v)


def _me():
    return lax.axis_index("x"), lax.axis_index("y"), lax.axis_index("c")


def _other_chips(x, y):
    return [(1 - x, y, 2 * (1 - x) + y), (x, 1 - y, 2 * x + 1 - y), (1 - x, 1 - y, 2 * (1 - x) + 1 - y)]


def _remote(src, dst, send_sem, recv_sem, target):
    return pltpu.make_async_remote_copy(src_ref=src, dst_ref=dst, send_sem=send_sem, recv_sem=recv_sem,
                                        device_id=target, device_id_type=MESH)


def _exchange(name, bufs, plan, n_copies):
    n = len(bufs)

    def body(*refs):
        send_sems, recv_sems = refs[2 * n:]
        copies = []
        for i, (src, dst, target) in enumerate(plan(refs[n:2 * n], _me())):
            if target is None:
                cp = pltpu.make_async_copy(src, dst, send_sems.at[i])
            else:
                cp = _remote(src, dst, send_sems.at[i], recv_sems.at[i], target)
            cp.start()
            copies.append((cp, target))
        assert len(copies) == n_copies
        for cp, target in copies:
            if target is None:
                cp.wait()
            else:
                cp.wait_recv()
        for cp, target in copies:
            if target is not None:
                cp.wait_send()

    return pl.pallas_call(
        body, name=name, in_specs=[ANY] * n, out_specs=[ANY] * n,
        out_shape=[jax.ShapeDtypeStruct(b.shape, b.dtype) for b in bufs],
        scratch_shapes=[pltpu.SemaphoreType.DMA((n_copies,)), pltpu.SemaphoreType.DMA((n_copies,))],
        input_output_aliases={i: i for i in range(n)},
        compiler_params=pltpu.CompilerParams(has_side_effects=True))(*bufs)


def _start_copies(name, groups, after=()):
    all_bufs = [b for bufs, _, _ in groups for b in bufs]
    after = list(after)
    n = len(all_bufs)
    n_g = len(groups)

    def body(*refs):
        in_refs, sem_refs = refs[:n], refs[n + len(after):n + len(after) + 2 * n_g]
        pos = 0
        for g, (bufs, plan, n_copies) in enumerate(groups):
            copies = plan(in_refs[pos:pos + len(bufs)], _me())
            assert len(copies) == n_copies
            for i, (src, dst, target) in enumerate(copies):
                _remote(src, dst, sem_refs[2 * g].at[i], sem_refs[2 * g + 1].at[i], target).start()
            pos += len(bufs)

    sems = []
    for _, _, n_copies in groups:
        sems += [pltpu.SemaphoreType.DMA((n_copies,))] * 2
    outs = pl.pallas_call(
        body, name=name, in_specs=[HBM] * n + [ANY] * len(after), out_specs=[SEM] * (2 * n_g) + [HBM] * n,
        out_shape=sems + [pltpu.HBM(b.shape, b.dtype) for b in all_bufs],
        input_output_aliases={i: 2 * n_g + i for i in range(n)},
        compiler_params=pltpu.CompilerParams(has_side_effects=pltpu.SideEffectType.DATAFLOW_SIDE_EFFECTING))(
            *[pltpu.with_memory_space_constraint(b, pltpu.HBM) for b in all_bufs], *after)
    result, pos = [], 2 * n_g
    for g, (bufs, _, _) in enumerate(groups):
        result.append((outs[2 * g], outs[2 * g + 1], list(outs[pos:pos + len(bufs)])))
        pos += len(bufs)
    return result


def _wait_copies(name, started, plan, n_copies, after):
    send_sems, recv_sems, bufs = started
    n = len(bufs)
    after = list(after) if isinstance(after, (list, tuple)) else [after]

    def body(*refs):
        copies = plan(refs[:n], _me())
        assert len(copies) == n_copies
        for i, (src, dst, target) in enumerate(copies):
            cp = _remote(src, dst, refs[n].at[i], refs[n + 1].at[i], target)
            cp.wait_send()
            cp.wait_recv()

    return list(pl.pallas_call(
        body, name=name, in_specs=[HBM] * n + [SEM, SEM] + [ANY] * len(after), out_specs=[HBM] * n,
        out_shape=[pltpu.HBM(b.shape, b.dtype) for b in bufs], input_output_aliases={i: i for i in range(n)},
        compiler_params=pltpu.CompilerParams(has_side_effects=pltpu.SideEffectType.DATAFLOW_SIDE_EFFECTING))(
            *bufs, send_sems, recv_sems, *after))


def _halves(a):
    return a.reshape(a.shape[0], 2, a.shape[1] // 2, a.shape[2])


def _gather_plan(refs, me):
    x, y, c = me
    mine = 2 * x + y
    return [(g.at[mine, c], g.at[mine, c], (px, py, c)) for g in refs for px, py, _ in _other_chips(x, y)]


def _forward_plan(refs, me):
    x, y, c = me
    return [(g.at[chip, c], g.at[chip, c], (x, y, 1 - c)) for g in refs for _, _, chip in _other_chips(x, y)]


def _swap_plan(refs, me):
    x, y, c = me
    k = len(refs) // 2
    return [(refs[i].at[j, 1 - c], refs[k + i].at[j], (x, y, 1 - c)) for i in range(k) for j in range(N_CHIPS)]


def _scatter_plan(refs, me):
    x, y, c = me
    mine = 2 * x + y
    k = len(refs) // 2
    return [(refs[i].at[chip], refs[k + i].at[mine], (px, py, c))
            for i in range(k) for px, py, chip in _other_chips(x, y)]


def _share_plan(refs, me):
    x, y, c = me
    return [(g.at[c], g.at[c], (x, y, 1 - c)) for g in refs]


def _broadcast_plan(refs, me):
    x, y, c = me
    mine = 4 * x + 2 * y + c
    copies = []
    for fx, fy, fc in [(0, 0, 1), (0, 1, 0), (0, 1, 1), (1, 0, 0), (1, 0, 1), (1, 1, 0), (1, 1, 1)]:
        peer = (x + fx - 2 * fx * x, y + fy - 2 * fy * y, c + fc - 2 * fc * c)
        copies.append((refs[0].at[mine], refs[0].at[mine], peer))
    return copies


BIG = ("w_in", "w_out", "w_q", "w_k", "w_v", "w_o", "w_up", "w_down")
COLUMN_SPLIT = ("w_in", "w_up")
WEIGHTS = ("norm_mix_pre", "norm_mix_post", "w_in", "w_out", "gmlp_v_gain", "w_spatial", "b_spatial", "w_pool",
           "s_pool", "w_dw", "b_dw", "conv_ln_g", "conv_ln_b", "norm_xattn_pre", "norm_mem", "norm_xattn_post",
           "w_q", "w_k", "w_v", "w_o", "norm_ffn_pre", "norm_ffn_post", "w_up", "w_down")
SMALL = tuple(n for n in WEIGHTS if n not in BIG)
GATHER_GROUPS = (("w_in", "w_dw"), ("w_out", "w_q", "w_k", "w_v", "w_o"), ("w_up",), ("w_down",))


def _relu2(acc):
    r = jnp.maximum(acc, 0.0)
    return acc, r * r


def _relu2_bwd(acc, up):
    return (acc * (2.0 * jnp.maximum(up, 0.0)),)


def _pack(arrays):
    flat = jnp.concatenate([a.reshape(-1) for a in arrays])
    tile = 8 * V7X_LANES
    pad = (-flat.shape[0]) % tile
    return jnp.pad(flat, (0, pad)).reshape(-1, V7X_LANES)


def _unpack(packed, like):
    flat = packed.reshape(-1)
    out, pos = [], 0
    for a in like:
        out.append(flat[pos:pos + a.size].reshape(a.shape))
        pos += a.size
    return out


class _GradientReducer:
    def __init__(self, place, w, m, v):
        self.place, self.w, self.m, self.v = place, w, m, v
        self.flying = []
        self.done = {n: None for n in BIG}

    def add(self, tag, l, grads):
        names = list(grads)
        views = [_halves(grads[n]) for n in names]
        zones = [lax.empty((v.shape[0],) + v.shape[2:], F32) for v in views]
        started = _start_copies("swap_start_" + tag, [(views + zones, _swap_plan, N_CHIPS * len(names))])[0]
        self.flying.append(dict(stage=0, tag=tag, l=l, names=names, started=started))
        return [started[2][0]]

    def advance(self, after):
        made = []
        after = list(after) if isinstance(after, (list, tuple)) else [after]
        for item in self.flying:
            item["stage"] += 1
        for item in self.flying:
            tag, names, k = item["tag"], item["names"], len(item["names"])
            if item["stage"] == 1:
                bufs = _wait_copies("swap_wait_" + tag, item["started"], _swap_plan, N_CHIPS * k, after)
                sums, parts = zip(*[_pair_sum("pair_sum_" + n, self.place, dv, got)
                                    for n, dv, got in zip(names, bufs[:k], bufs[k:])])
                item["started"] = _start_copies("scatter_start_" + tag,
                                                [(list(sums) + list(parts), _scatter_plan, 3 * k)])[0]
                made.append(item["started"][2][0])
        after = after + made
        for item in list(self.flying):
            tag, names, k = item["tag"], item["names"], len(item["names"])
            if item["stage"] == 3:
                bufs = _wait_copies("scatter_wait_" + tag, item["started"], _scatter_plan, 3 * k, after)
                halves = [_chip_sum("chip_sum_" + n, self.place, p) for n, p in zip(names, bufs[k:])]
                halves = _exchange("share_" + tag, halves, _share_plan, k)
                for n, h in zip(names, halves):
                    g = h.reshape(self.w[n].shape[1:])
                    self.done[n] = _adamw_layer("adamw_" + n, item["l"], self.w[n], g, self.m[n], self.v[n],
                                                self.done[n])
                    made.append(self.done[n][3])
                self.flying.remove(item)
        return made

    def drain(self, after):
        made = list(after)
        while self.flying:
            made = list(after) + self.advance(made)
        return made


def _step(x, mem, target, w, m, v):
    n_layers = w["w_in"].shape[0]
    seq, d = x.shape
    heads, hd = w["gmlp_v_gain"].shape[1:]
    gw = heads * hd
    groups, pgw = w["w_pool"].shape[1:3]
    pw = groups * pgw
    cw = w["b_dw"].shape[1]
    cb = cw // N_CHIPS
    taps = w["w_dw"].shape[1]
    cx, cy, cc = _me()
    chip = 2 * cx + cy
    place = jnp.stack([cc, chip, 2 * chip + cc]).astype(jnp.int32)
    vec = lambda name, l: w[name][l].reshape(1, -1)

    taps_padded = jnp.pad(w["w_dw"], ((0, 0), (0, (-taps) % 16), (0, 0)))
    gathering = []
    for l in range(n_layers):
        for g, names in enumerate(GATHER_GROUPS):
            last = [gathering[-1][2][0]] if gathering else []
            bufs = [_halves(_place_shard("place_" + n, place, taps_padded if n == "w_dw" else w[n], l,
                                         F32 if n == "w_dw" else BF16, after=last)) for n in names]
            gathering += _start_copies("gather_start_%d%d" % (l, g), [(bufs, _gather_plan, 3 * len(names))],
                                       after=last)
    all_started = [gathering[-1][2][0]]

    def arrive(l, g, after):
        names = GATHER_GROUPS[g]
        tag = "%d%d" % (l, g)
        bufs = _wait_copies("gather_wait_" + tag, gathering[l * len(GATHER_GROUPS) + g], _gather_plan,
                            3 * len(names), [after] + (all_started if (l, g) == (0, 0) else []))
        bufs = _exchange("gather_forward_" + tag, bufs, _forward_plan, 3 * len(names))
        out = {}
        for n, b in zip(names, bufs):
            full = b.reshape(N_CHIPS, 2 * b.shape[2], b.shape[3])
            out[n] = full if n in COLUMN_SPLIT + ("w_dw",) else full.reshape(-1, full.shape[2])
        return out

    saved = []
    _, h1 = _norm_fwd("norm_first", x, None, None, vec("norm_mix_pre", 0))
    for l in range(n_layers):
        gv = w["gmlp_v_gain"][l].reshape(heads, 1, hd)
        ws = w["w_spatial"][l]
        bcol = w["b_spatial"][l].reshape(heads, -1, 1)
        wl = arrive(l, 0, h1)
        z = _mm_nn_col("mm_in", h1, wl["w_in"], [F32])[0]
        ya = _gmlp_fwd(z, gv, ws, bcol)
        yb = _pool_fwd(z, w["w_pool"][l], vec("s_pool", l), (2 * gw) // pw)
        hc = _conv_fwd(z, wl["w_dw"], taps, vec("b_dw", l), (2 * gw + pw) // cb, (2 * gw + pw + cw) // cb)
        yc = _ln_swish_fwd(hc, vec("conv_ln_g", l), vec("conv_ln_b", l))
        y = jnp.concatenate([ya, yb, yc], axis=1)
        wl.update(arrive(l, 1, y))
        o = _mm_nn_row("mm_out", y, wl["w_out"], [F32])[0]
        x1, h2 = _norm_fwd("norm_mix", x, o, vec("norm_mix_post", l), vec("norm_xattn_pre", l))
        _, mn = _norm_fwd("norm_mem", mem, None, None, vec("norm_mem", l))
        q = _mm_nn_row("mm_q", h2, wl["w_q"], [BF16])[0]
        k = _mm_nn_row("mm_k", mn, wl["w_k"], [BF16])[0]
        vv = _mm_nn_row("mm_v", mn, wl["w_v"], [BF16])[0]
        a = _attn_fwd(q, k, vv)
        o2 = _mm_nn_row("mm_o", a, wl["w_o"], [F32])[0]
        x2, h3 = _norm_fwd("norm_xattn", x1, o2, vec("norm_xattn_post", l), vec("norm_ffn_pre", l))
        wl.update(arrive(l, 2, h3))
        up, r = _mm_nn_col("mm_up", h3, wl["w_up"], [F32, BF16], epi=_relu2)
        wl.update(arrive(l, 3, r))
        o3 = _mm_nn_row("mm_down", r, wl["w_down"], [F32])[0]
        g_next = vec("norm_mix_pre", l + 1) if l + 1 < n_layers else None
        x3, h_next = _norm_fwd("norm_ffn", x2, o3, vec("norm_ffn_post", l), g_next)
        saved.append(dict(x=x, h1=h1, z=z, hc=hc, y=y, o=o, x1=x1, h2=h2, mn=mn, q=q, k=k, v=vv, a=a, o2=o2, x2=x2,
                          h3=h3, up=up, r=r, o3=o3, x3=x3, gv=gv, ws=ws, bcol=bcol, w=wl))
        x, h1 = x3, h_next

    dx, loss_parts = _loss_head(x, target)
    loss = lax.psum(jnp.sum(loss_parts), ("x", "y", "c"))

    reducer = _GradientReducer(place, w, m, v)
    small = {n: [None] * n_layers for n in SMALL}
    by_chip = lambda g: g.reshape(N_CHIPS, g.shape[0] // N_CHIPS, g.shape[1])
    small_sent = [None] * n_layers

    def small_layer(l):
        return [small[n][l] if n == "w_dw" else small[n][l].reshape(w[n].shape[1:]) for n in SMALL]

    def send_small(l):
        landing = _place_flat(place, _pack(small_layer(l)))
        small_sent[l] = _start_copies("small_start_%d" % l, [([landing], _broadcast_plan, N_DEVICES - 1)])[0]
        return [small_sent[l][2][0]]

    dh = None
    made = []
    for l in reversed(range(n_layers)):
        t = saved[l]
        wl = t["w"]
        g_next = vec("norm_mix_pre", l + 1) if l + 1 < n_layers else None
        dx, do3, dgp, dgn = _norm_bwd("norm_ffn_bwd", dx, dh, t["x3"], t["o3"], vec("norm_ffn_post", l), g_next,
                                      after=made)
        small["norm_ffn_post"][l] = dgp
        if dgn is not None:
            small["norm_mix_pre"][l + 1] = dgn
        made = reducer.advance(dx)
        if l + 1 < n_layers:
            made += send_small(l + 1)
        d_down = _mm_tn_row("mm_down_dw", t["r"], do3)
        made += reducer.add("%d0" % l, l, {"w_down": by_chip(d_down)})
        dup = _mm_nt_row("mm_down_dx", do3, wl["w_down"], BF16, epi=_relu2_bwd, extra=t["up"], after=made)
        d_up = _mm_tn_col("mm_up_dw", t["h3"], dup)
        made = reducer.add("%d3" % l, l, {"w_up": d_up})
        dh3 = _mm_nt_col("mm_up_dx", dup, wl["w_up"], F32, after=made)
        made = []
        dx, do2, dgp, dgn = _norm_bwd("norm_xattn_bwd", dx, dh3, t["x2"], t["o2"], vec("norm_xattn_post", l),
                                      vec("norm_ffn_pre", l), after=made)
        small["norm_xattn_post"][l], small["norm_ffn_pre"][l] = dgp, dgn
        made = reducer.advance(dx)
        d_o = _mm_tn_row("mm_o_dw", t["a"], do2)
        da = _mm_nt_row("mm_o_dx", do2, wl["w_o"], BF16, after=made)
        dq, dk, dv = _attn_bwd(t["q"], t["k"], t["v"], da)
        d_q = _mm_tn_row("mm_q_dw", t["h2"], dq)
        d_k = _mm_tn_row("mm_k_dw", t["mn"], dk)
        d_v = _mm_tn_row("mm_v_dw", t["mn"], dv)
        dh2 = _mm_nt_row("mm_q_dx", dq, wl["w_q"], F32)
        dmn = _mm_nt_row("mm_k_dx", dk, wl["w_k"], F32) + _mm_nt_row("mm_v_dx", dv, wl["w_v"], F32)
        _, _, _, small["norm_mem"][l] = _norm_bwd("norm_mem_bwd", None, dmn, mem, None, None, vec("norm_mem", l))
        dx, do, dgp, dgn = _norm_bwd("norm_mix_bwd", dx, dh2, t["x1"], t["o"], vec("norm_mix_post", l),
                                     vec("norm_xattn_pre", l))
        small["norm_mix_post"][l], small["norm_xattn_pre"][l] = dgp, dgn
        made = reducer.advance(dx)
        d_out = _mm_tn_row("mm_out_dw", t["y"], do)
        made += reducer.add("%d1" % l, l, {"w_o": by_chip(d_o), "w_q": by_chip(d_q), "w_k": by_chip(d_k),
                                    "w_v": by_chip(d_v), "w_out": by_chip(d_out)})
        dy = _mm_nt_row("mm_out_dx", do, wl["w_out"], F32, after=made)
        dzu, dzv, dgv, dws, dbcol = _gmlp_bwd(t["z"], dy, t["gv"], t["ws"], t["bcol"])
        small["gmlp_v_gain"][l] = dgv.reshape(heads, hd)
        small["w_spatial"][l] = dws
        small["b_spatial"][l] = dbcol.reshape(heads, -1)
        dzp, dwp, dsp = _pool_bwd(t["z"], dy, w["w_pool"][l], vec("s_pool", l), (2 * gw) // pw, gw // pw)
        small["w_pool"][l], small["s_pool"][l] = dwp, dsp.reshape(-1)
        dhc, dlg, dlb = _ln_swish_bwd(t["hc"], dy, vec("conv_ln_g", l), vec("conv_ln_b", l), (gw + pw) // cw)
        small["conv_ln_g"][l], small["conv_ln_b"][l] = dlg.reshape(-1), dlb.reshape(-1)
        dval, dgate, dwd, dbd = _conv_bwd(t["z"], dhc, wl["w_dw"], taps, (2 * gw + pw) // cb,
                                          (2 * gw + pw + cw) // cb)
        small["w_dw"][l], small["b_dw"][l] = dwd, dbd.reshape(-1)
        dz = jnp.concatenate([dzu, dzv, dzp, dval, dgate], axis=1)
        made = reducer.advance(dz)
        d_in = _mm_tn_col("mm_in_dw", t["h1"], dz)
        made += reducer.add("%d2" % l, l, {"w_in": d_in})
        dh = _mm_nt_col("mm_in_dx", dz, wl["w_in"], F32, after=made)
        made = []
    grad_x, _, _, dgn = _norm_bwd("norm_first_bwd", dx, dh, saved[0]["x"], None, None, vec("norm_mix_pre", 0))
    small["norm_mix_pre"][0] = dgn
    drained = reducer.drain([grad_x] + send_small(0))
    drained = [a for a in drained if all(a is not sent[2][0] for sent in small_sent)]

    totals = []
    for l in range(n_layers):
        landed = _wait_copies("small_wait_%d" % l, small_sent[l], _broadcast_plan, N_DEVICES - 1, drained)[0]
        totals.append(_unpack(_device_sum(landed), small_layer(l)))
    grad = {}
    for i, n in enumerate(SMALL):
        g = jnp.stack([totals[l][i] for l in range(n_layers)])
        grad[n] = lax.dynamic_slice_in_dim(g, chip * cb, cb, axis=2) if n == "w_dw" else g
    like = [w[n] for n in SMALL]
    dl, nm, nv = _adamw_flat("adamw_small", _pack(like), _pack([grad[n] for n in SMALL]),
                             _pack([m[n] for n in SMALL]), _pack([v[n] for n in SMALL]))
    delta, new_m, new_v = (dict(zip(SMALL, _unpack(p, like))) for p in (dl, nm, nv))
    for n in BIG:
        grad[n], delta[n], new_m[n], new_v[n] = reducer.done[n]

    return (loss, grad_x[None], *[grad[n] for n in WEIGHTS], *[delta[n] for n in WEIGHTS],
            *[new_m[n] for n in WEIGHTS], *[new_v[n] for n in WEIGHTS])


def kernel(x, mem, norm_mix_pre, norm_mix_post, w_in, w_out, gmlp_v_gain, w_spatial, b_spatial, w_pool, s_pool, w_dw, b_dw, conv_ln_g, conv_ln_b, norm_xattn_pre, norm_mem, norm_xattn_post, w_q, w_k, w_v, w_o, norm_ffn_pre, norm_ffn_post, w_up, w_down, loss_target, m_norm_mix_pre, m_norm_mix_post, m_w_in, m_w_out, m_gmlp_v_gain, m_w_spatial, m_b_spatial, m_w_pool, m_s_pool, m_w_dw, m_b_dw, m_conv_ln_g, m_conv_ln_b, m_norm_xattn_pre, m_norm_mem, m_norm_xattn_post, m_w_q, m_w_k, m_w_v, m_w_o, m_norm_ffn_pre, m_norm_ffn_post, m_w_up, m_w_down, v_norm_mix_pre, v_norm_mix_post, v_w_in, v_w_out, v_gmlp_v_gain, v_w_spatial, v_b_spatial, v_w_pool, v_s_pool, v_w_dw, v_b_dw, v_conv_ln_g, v_conv_ln_b, v_norm_xattn_pre, v_norm_mem, v_norm_xattn_post, v_w_q, v_w_k, v_w_v, v_w_o, v_norm_ffn_pre, v_norm_ffn_post, v_w_up, v_w_down):
    given = dict(locals())
    w = {n: given[n] for n in WEIGHTS}
    m = {n: given["m_" + n] for n in WEIGHTS}
    v = {n: given["v_" + n] for n in WEIGHTS}
    return _step(x[0], mem[0], loss_target[0], w, m, v)
```

```python
import functools

import jax
import jax.numpy as jnp
from jax import lax
from jax.experimental import pallas as pl
from jax.experimental.pallas import tpu as pltpu

F32 = jnp.float32
BF16 = jnp.bfloat16
MESH = pl.DeviceIdType.MESH

N_CHIPS = 4
N_DEVICES = 8
XATTN_HEADS = 4
POOL_WINDOWS = (2, 4, 8, 16)
RMS_EPS = 1e-6
LN_EPS = 1e-5
ADAM_LR, ADAM_B1, ADAM_B2, ADAM_EPS, ADAM_WD, ADAM_STEP = 0.001, 0.9, 0.999, 1e-08, 0.01, 10

V7X_LANES = 128
V7X_VMEM_LIMIT = 56 * 1024 * 1024
ROW_TILE = 256
MM_TILE_M, MM_TILE_N, MM_TILE_K = 1024, 1024, 2048

ANY = pl.BlockSpec(memory_space=pl.ANY)
HBM = pl.BlockSpec(memory_space=pltpu.HBM)
SEM = pl.BlockSpec(memory_space=pltpu.SEMAPHORE)


def _tile(dim, pref):
    if dim <= pref:
        return dim
    t = (pref // V7X_LANES) * V7X_LANES
    while t >= V7X_LANES:
        if dim % t == 0:
            return t
        t -= V7X_LANES
    return dim


def _params(sem=None):
    return pltpu.CompilerParams(dimension_semantics=sem, vmem_limit_bytes=V7X_VMEM_LIMIT)


NN = (((1,), (0,)), ((), ()))
NT = (((1,), (1,)), ((), ()))
TN = (((0,), (0,)), ((), ()))


def _mm(name, a, b, *, dn, grid, a_spec, b_spec, o_specs, out_shapes, acc_shape, epi=None, extra=None,
        extra_spec=None, after=()):
    nk = grid[2]
    n_out = len(out_shapes)
    has_extra = extra is not None
    after = list(after)

    def body(*refs):
        a_ref, b_ref = refs[0], refs[1]
        pos = 2
        e_ref = None
        if has_extra:
            e_ref = refs[pos]
            pos += 1
        pos += len(after)
        o_refs = refs[pos:pos + n_out]
        part = lax.dot_general(a_ref[...].astype(BF16), b_ref[...].astype(BF16), dn, preferred_element_type=F32)

        def finish(total):
            if epi is None:
                vals = (total,)
            elif has_extra:
                vals = epi(total, e_ref[...])
            else:
                vals = epi(total)
            for o, v in zip(o_refs, vals):
                o[...] = v.astype(o.dtype)

        if nk == 1:
            finish(part)
            return
        acc = refs[pos + n_out]
        k = pl.program_id(2)

        @pl.when(k == 0)
        def _():
            acc[...] = part

        @pl.when(jnp.logical_and(k > 0, k < nk - 1))
        def _():
            acc[...] += part

        @pl.when(k == nk - 1)
        def _():
            finish(acc[...] + part)

    ins, in_specs = [a, b], [a_spec, b_spec]
    if has_extra:
        ins.append(extra)
        in_specs.append(extra_spec)
    ins += after
    in_specs += [ANY] * len(after)
    outs = pl.pallas_call(
        body, name=name, grid=grid, in_specs=in_specs, out_specs=list(o_specs), out_shape=list(out_shapes),
        scratch_shapes=[pltpu.VMEM(acc_shape, F32)] if nk > 1 else [],
        compiler_params=_params(("parallel", "parallel", "arbitrary")))(*ins)
    return outs


def _mm_nn_row(name, a, w, out_dtypes, epi=None):
    m, k = a.shape
    n = w.shape[1]
    tm, tn, tk = _tile(m, MM_TILE_M), _tile(n, MM_TILE_N), _tile(k, MM_TILE_K)
    o_spec = pl.BlockSpec((tm, tn), lambda i, j, kk: (i, j))
    return _mm(name, a, w, dn=NN, grid=(m // tm, n // tn, k // tk),
               a_spec=pl.BlockSpec((tm, tk), lambda i, j, kk: (i, kk)),
               b_spec=pl.BlockSpec((tk, tn), lambda i, j, kk: (kk, j)),
               o_specs=[o_spec] * len(out_dtypes),
               out_shapes=[jax.ShapeDtypeStruct((m, n), d) for d in out_dtypes], acc_shape=(tm, tn), epi=epi)


def _mm_nn_col(name, a, w, out_dtypes, epi=None):
    m, k = a.shape
    c = w.shape[2]
    tm, tn, tk = _tile(m, MM_TILE_M), _tile(c, MM_TILE_N), _tile(k, MM_TILE_K)
    nb = c // tn
    o_spec = pl.BlockSpec((tm, tn), lambda i, j, kk: (i, j))
    return _mm(name, a, w, dn=NN, grid=(m // tm, N_CHIPS * nb, k // tk),
               a_spec=pl.BlockSpec((tm, tk), lambda i, j, kk: (i, kk)),
               b_spec=pl.BlockSpec((None, tk, tn), lambda i, j, kk: (j // nb, kk, j % nb)),
               o_specs=[o_spec] * len(out_dtypes),
               out_shapes=[jax.ShapeDtypeStruct((m, N_CHIPS * c), d) for d in out_dtypes], acc_shape=(tm, tn),
               epi=epi)


def _mm_nt_row(name, dy, w, out_dtype, epi=None, extra=None, after=()):
    m, n = dy.shape
    k = w.shape[0]
    tm, tn, tk = _tile(m, MM_TILE_M), _tile(k, MM_TILE_N), _tile(n, MM_TILE_K)
    o_spec = pl.BlockSpec((tm, tn), lambda i, j, kk: (i, j))
    return _mm(name, dy, w, dn=NT, grid=(m // tm, k // tn, n // tk),
               a_spec=pl.BlockSpec((tm, tk), lambda i, j, kk: (i, kk)),
               b_spec=pl.BlockSpec((tn, tk), lambda i, j, kk: (j, kk)),
               o_specs=[o_spec], out_shapes=[jax.ShapeDtypeStruct((m, k), out_dtype)], acc_shape=(tm, tn),
               epi=epi, extra=extra, extra_spec=o_spec, after=after)[0]


def _mm_nt_col(name, dy, w, out_dtype, after=()):
    m = dy.shape[0]
    k, c = w.shape[1], w.shape[2]
    tm, tn, tk = _tile(m, MM_TILE_M), _tile(k, MM_TILE_N), _tile(c, MM_TILE_K)
    kb = c // tk
    return _mm(name, dy, w, dn=NT, grid=(m // tm, k // tn, N_CHIPS * kb),
               a_spec=pl.BlockSpec((tm, tk), lambda i, j, kk: (i, kk)),
               b_spec=pl.BlockSpec((None, tn, tk), lambda i, j, kk: (kk // kb, j, kk % kb)),
               o_specs=[pl.BlockSpec((tm, tn), lambda i, j, kk: (i, j))],
               out_shapes=[jax.ShapeDtypeStruct((m, k), out_dtype)], acc_shape=(tm, tn), after=after)[0]


def _mm_tn_row(name, a, dy):
    t, m = a.shape
    n = dy.shape[1]
    tm, tn, tk = _tile(m, MM_TILE_M), _tile(n, MM_TILE_N), _tile(t, MM_TILE_K)
    return _mm(name, a, dy, dn=TN, grid=(m // tm, n // tn, t // tk),
               a_spec=pl.BlockSpec((tk, tm), lambda i, j, kk: (kk, i)),
               b_spec=pl.BlockSpec((tk, tn), lambda i, j, kk: (kk, j)),
               o_specs=[pl.BlockSpec((tm, tn), lambda i, j, kk: (i, j))],
               out_shapes=[jax.ShapeDtypeStruct((m, n), F32)], acc_shape=(tm, tn))[0]


def _mm_tn_col(name, a, dy):
    t, m = a.shape
    c = dy.shape[1] // N_CHIPS
    tm, tn, tk = _tile(m, MM_TILE_M), _tile(c, MM_TILE_N), _tile(t, MM_TILE_K)
    nb = c // tn
    return _mm(name, a, dy, dn=TN, grid=(m // tm, N_CHIPS * nb, t // tk),
               a_spec=pl.BlockSpec((tk, tm), lambda i, j, kk: (kk, i)),
               b_spec=pl.BlockSpec((tk, tn), lambda i, j, kk: (kk, j)),
               o_specs=[pl.BlockSpec((None, tm, tn), lambda i, j, kk: (j // nb, i, j % nb))],
               out_shapes=[jax.ShapeDtypeStruct((N_CHIPS, m, c), F32)], acc_shape=(tm, tn))[0]


def _rms(x, g):
    r = lax.rsqrt(jnp.mean(x * x, axis=-1, keepdims=True) + RMS_EPS)
    return x * r * g


def _rms_bwd(x, g, dy):
    r = lax.rsqrt(jnp.mean(x * x, axis=-1, keepdims=True) + RMS_EPS)
    xr = x * r
    dyg = dy * g
    dx = r * (dyg - xr * jnp.mean(dyg * xr, axis=-1, keepdims=True))
    return dx, jnp.sum(dy * xr, axis=0, keepdims=True)


def _norm_fwd(name, x, o, g_post, g_next, after=()):
    after = list(after)
    s, d = x.shape
    tr = _tile(s, ROW_TILE)
    has_prev, has_next = o is not None, g_next is not None
    row = pl.BlockSpec((tr, d), lambda i: (i, 0))
    vec = pl.BlockSpec((1, d), lambda i: (0, 0))

    def body(*refs):
        refs = list(refs)
        xn = refs.pop(0)[...]
        if has_prev:
            o_ref, gp_ref = refs.pop(0), refs.pop(0)
            xn = xn + _rms(o_ref[...], gp_ref[...])
        gn_ref = refs.pop(0) if has_next else None
        del refs[:len(after)]
        if has_prev:
            refs.pop(0)[...] = xn
        if has_next:
            refs.pop(0)[...] = _rms(xn, gn_ref[...]).astype(BF16)

    ins, in_specs = [x], [row]
    if has_prev:
        ins += [o, g_post]
        in_specs += [row, vec]
    if has_next:
        ins.append(g_next)
        in_specs.append(vec)
    ins += after
    in_specs += [ANY] * len(after)
    out_shapes, out_specs = [], []
    if has_prev:
        out_shapes.append(jax.ShapeDtypeStruct((s, d), F32))
        out_specs.append(row)
    if has_next:
        out_shapes.append(jax.ShapeDtypeStruct((s, d), BF16))
        out_specs.append(row)
    outs = pl.pallas_call(body, name=name, grid=(s // tr,), in_specs=in_specs, out_specs=out_specs,
                          out_shape=out_shapes, compiler_params=_params(("parallel",)))(*ins)
    outs = list(outs)
    x_new = outs.pop(0) if has_prev else x
    h = outs.pop(0) if has_next else None
    return x_new, h


def _norm_bwd(name, dxn, dh, xn, o, g_post, g_next, after=()):
    after = list(after)
    s, d = xn.shape
    tr = _tile(s, ROW_TILE)
    has_prev, has_next, has_dxn = o is not None, dh is not None, dxn is not None
    row = pl.BlockSpec((tr, d), lambda i: (i, 0))
    vec = pl.BlockSpec((1, d), lambda i: (0, 0))

    def body(*refs):
        refs = list(refs)
        first = pl.program_id(0) == 0
        dxn_ref = refs.pop(0) if has_dxn else None
        dh_ref = refs.pop(0) if has_next else None
        xn_ref = refs.pop(0)
        if has_prev:
            o_ref, gp_ref = refs.pop(0), refs.pop(0)
        gn_ref = refs.pop(0) if has_next else None
        del refs[:len(after)]
        dx_ref = refs.pop(0)
        if has_prev:
            do_ref, dgp_ref = refs.pop(0), refs.pop(0)
        dgn_ref = refs.pop(0) if has_next else None

        def accumulate(ref, val):
            @pl.when(first)
            def _():
                ref[...] = val

            @pl.when(jnp.logical_not(first))
            def _():
                ref[...] += val

        dx = dxn_ref[...] if has_dxn else None
        if has_next:
            dxh, dgn = _rms_bwd(xn_ref[...], gn_ref[...], dh_ref[...].astype(F32))
            dx = dxh if dx is None else dx + dxh
            accumulate(dgn_ref, dgn)
        dx_ref[...] = dx
        if has_prev:
            do, dgp = _rms_bwd(o_ref[...], gp_ref[...], dx)
            do_ref[...] = do.astype(BF16)
            accumulate(dgp_ref, dgp)

    ins, in_specs = [], []
    if has_dxn:
        ins.append(dxn)
        in_specs.append(row)
    if has_next:
        ins.append(dh)
        in_specs.append(row)
    ins.append(xn)
    in_specs.append(row)
    if has_prev:
        ins += [o, g_post]
        in_specs += [row, vec]
    if has_next:
        ins.append(g_next)
        in_specs.append(vec)
    ins += after
    in_specs += [ANY] * len(after)
    out_shapes, out_specs = [jax.ShapeDtypeStruct((s, d), F32)], [row]
    if has_prev:
        out_shapes += [jax.ShapeDtypeStruct((s, d), BF16), jax.ShapeDtypeStruct((1, d), F32)]
        out_specs += [row, vec]
    if has_next:
        out_shapes.append(jax.ShapeDtypeStruct((1, d), F32))
        out_specs.append(vec)
    outs = list(pl.pallas_call(body, name=name, grid=(s // tr,), in_specs=in_specs, out_specs=out_specs,
                               out_shape=out_shapes, compiler_params=_params(("arbitrary",)))(*ins))
    dx = outs.pop(0)
    do, dgp = (outs.pop(0), outs.pop(0)) if has_prev else (None, None)
    dgn = outs.pop(0) if has_next else None
    return dx, do, dgp, dgn


def _loss_head(y, target):
    s, d = y.shape
    tr = _tile(s, ROW_TILE)
    row = pl.BlockSpec((tr, d), lambda i: (i, 0))
    vec = pl.BlockSpec((1, d), lambda i: (0, 0))

    def body(y_ref, t_ref, dy_ref, l_ref):
        err = y_ref[...] - t_ref[...]
        dy_ref[...] = err * (1.0 / d)
        part = jnp.sum(err * err, axis=0, keepdims=True) * (0.5 / d)

        @pl.when(pl.program_id(0) == 0)
        def _():
            l_ref[...] = part

        @pl.when(pl.program_id(0) != 0)
        def _():
            l_ref[...] += part

    return pl.pallas_call(body, name="loss_head", grid=(s // tr,), in_specs=[row, row], out_specs=[row, vec],
                          out_shape=[jax.ShapeDtypeStruct((s, d), F32), jax.ShapeDtypeStruct((1, d), F32)],
                          compiler_params=_params(("arbitrary",)))(y, target)


@jax.custom_vjp
def _bdot(a, b):
    return jnp.dot(a.astype(BF16), b.astype(BF16), preferred_element_type=F32)


def _bdot_fwd(a, b):
    return _bdot(a, b), (a, b)


def _bdot_bwd(res, ct):
    a, b = res
    ctb = ct.astype(BF16)
    da = lax.dot_general(ctb, b.astype(BF16), NT, preferred_element_type=F32)
    db = lax.dot_general(a.astype(BF16), ctb, TN, preferred_element_type=F32)
    return da, db


_bdot.defvjp(_bdot_fwd, _bdot_bwd)


@functools.partial(jax.custom_vjp, nondiff_argnums=(1,))
def _shift(x, k):
    n = x.shape[0]
    if k == 0:
        return x
    rolled = pltpu.roll(x, k % n, 0)
    t = lax.broadcasted_iota(jnp.int32, x.shape, 0)
    keep = (t >= k) if k > 0 else (t < n + k)
    return jnp.where(keep, rolled, 0.0)


def _shift_fwd(x, k):
    return _shift(x, k), None


def _shift_bwd(k, _, ct):
    return (_shift(ct, -k),)


_shift.defvjp(_shift_fwd, _shift_bwd)


def _sigmoid(x):
    return 1.0 / (1.0 + jnp.exp(-x))


def _layer_norm(x, g, b=None):
    mu = jnp.mean(x, axis=-1, keepdims=True)
    xc = x - mu
    var = jnp.mean(xc * xc, axis=-1, keepdims=True)
    y = xc * lax.rsqrt(var + LN_EPS) * g
    return y if b is None else y + b


def _gmlp_chunk(zu, zv, gv, w, bcol):
    ch = w.shape[0]
    u = jax.nn.gelu(zu)
    vn = _layer_norm(jax.nn.gelu(zv), gv)
    t = lax.broadcasted_iota(jnp.int32, (ch, ch), 0)
    s = lax.broadcasted_iota(jnp.int32, (ch, ch), 1)
    wm = jnp.where(t >= s, w, 0.0)
    return u * (_bdot(wm, vn) + bcol)


def _gmlp_specs(seq, heads, hd, ch, u_off, v_off):
    col = lambda off: pl.BlockSpec((seq, hd), lambda h: (0, off + h))
    return (col(u_off), col(v_off), pl.BlockSpec((None, 1, hd), lambda h: (h, 0, 0)),
            pl.BlockSpec((None, ch, ch), lambda h: (h, 0, 0)), pl.BlockSpec((None, ch, 1), lambda h: (h, 0, 0)))


def _gmlp_fwd(z, gv, ws, bcol):
    seq = z.shape[0]
    heads, _, hd = gv.shape
    ch = ws.shape[-1]
    zu_s, zv_s, gv_s, w_s, b_s = _gmlp_specs(seq, heads, hd, ch, 0, heads)

    def body(zu_ref, zv_ref, gv_ref, w_ref, b_ref, y_ref):
        gvv, w, bc = gv_ref[...], w_ref[...], b_ref[...]

        def step(c, carry):
            rows = pl.ds(pl.multiple_of(c * ch, ch), ch)
            y_ref[rows, :] = _gmlp_chunk(zu_ref[rows, :], zv_ref[rows, :], gvv, w, bc).astype(BF16)
            return carry

        lax.fori_loop(0, seq // ch, step, 0)

    return pl.pallas_call(body, name="gmlp_fwd", grid=(heads,), in_specs=[zu_s, zv_s, gv_s, w_s, b_s],
                          out_specs=pl.BlockSpec((seq, hd), lambda h: (0, h)),
                          out_shape=jax.ShapeDtypeStruct((seq, heads * hd), BF16),
                          compiler_params=_params(("parallel",)))(z, z, gv, ws, bcol)


def _gmlp_bwd(z, dy, gv, ws, bcol):
    seq = z.shape[0]
    heads, _, hd = gv.shape
    ch = ws.shape[-1]
    zu_s, zv_s, gv_s, w_s, b_s = _gmlp_specs(seq, heads, hd, ch, 0, heads)
    col = pl.BlockSpec((seq, hd), lambda h: (0, h))

    def body(zu_ref, zv_ref, dy_ref, gv_ref, w_ref, b_ref, dzu_ref, dzv_ref, dgv_ref, dw_ref, db_ref):
        gvv, w, bc = gv_ref[...], w_ref[...], b_ref[...]

        def step(c, carry):
            dgv, dw, db = carry
            rows = pl.ds(pl.multiple_of(c * ch, ch), ch)
            _, vjp = jax.vjp(_gmlp_chunk, zu_ref[rows, :], zv_ref[rows, :], gvv, w, bc)
            dzu, dzv, dgv_c, dw_c, db_c = vjp(dy_ref[rows, :])
            dzu_ref[rows, :] = dzu.astype(BF16)
            dzv_ref[rows, :] = dzv.astype(BF16)
            return dgv + dgv_c, dw + dw_c, db + db_c

        zero = (jnp.zeros((1, hd), F32), jnp.zeros((ch, ch), F32), jnp.zeros((ch, 1), F32))
        dgv, dw, db = lax.fori_loop(0, seq // ch, step, zero)
        dgv_ref[...] = dgv
        dw_ref[...] = dw
        db_ref[...] = db

    return pl.pallas_call(
        body, name="gmlp_bwd", grid=(heads,), in_specs=[zu_s, zv_s, col, gv_s, w_s, b_s],
        out_specs=[col, col, gv_s, w_s, b_s],
        out_shape=[jax.ShapeDtypeStruct((seq, heads * hd), BF16), jax.ShapeDtypeStruct((seq, heads * hd), BF16),
                   jax.ShapeDtypeStruct(gv.shape, F32), jax.ShapeDtypeStruct(ws.shape, F32),
                   jax.ShapeDtypeStruct(bcol.shape, F32)],
        compiler_params=_params(("parallel",)))(z, z, dy, gv, ws, bcol)


def _pool_group(p, w, s, window):
    win, span = p, 1
    while span < window:
        win = win + _shift(win, span)
        span *= 2
    t = lax.broadcasted_iota(jnp.int32, (p.shape[0], 1), 0).astype(F32)
    cnt = jnp.minimum(t + 1.0, float(window))
    return _bdot(win / cnt - p, w) * s


def _pool_fwd(z, w_pool, s_pool, col_block):
    seq = z.shape[0]
    groups, gw, _ = w_pool.shape
    pw = groups * gw

    def body(p_ref, w_ref, s_ref, y_ref):
        for g in range(groups):
            cols = slice(g * gw, (g + 1) * gw)
            y_ref[:, cols] = _pool_group(p_ref[:, cols], w_ref[g], s_ref[:, cols], POOL_WINDOWS[g]).astype(BF16)

    return pl.pallas_call(
        body, name="pool_fwd", grid=(1,),
        in_specs=[pl.BlockSpec((seq, pw), lambda i: (0, col_block)),
                  pl.BlockSpec((groups, gw, gw), lambda i: (0, 0, 0)), pl.BlockSpec((1, pw), lambda i: (0, 0))],
        out_specs=pl.BlockSpec((seq, pw), lambda i: (0, 0)), out_shape=jax.ShapeDtypeStruct((seq, pw), BF16),
        compiler_params=_params(("arbitrary",)))(z, w_pool, s_pool)


def _pool_bwd(z, dy, w_pool, s_pool, col_block, dy_block):
    seq = z.shape[0]
    groups, gw, _ = w_pool.shape
    pw = groups * gw

    def body(p_ref, dy_ref, w_ref, s_ref, dp_ref, dw_ref, ds_ref):
        for g in range(groups):
            cols = slice(g * gw, (g + 1) * gw)
            _, vjp = jax.vjp(functools.partial(_pool_group, window=POOL_WINDOWS[g]), p_ref[:, cols], w_ref[g],
                             s_ref[:, cols])
            dp, dw, ds = vjp(dy_ref[:, cols])
            dp_ref[:, cols] = dp.astype(BF16)
            dw_ref[g] = dw
            ds_ref[:, cols] = ds

    return pl.pallas_call(
        body, name="pool_bwd", grid=(1,),
        in_specs=[pl.BlockSpec((seq, pw), lambda i: (0, col_block)),
                  pl.BlockSpec((seq, pw), lambda i: (0, dy_block)),
                  pl.BlockSpec((groups, gw, gw), lambda i: (0, 0, 0)), pl.BlockSpec((1, pw), lambda i: (0, 0))],
        out_specs=[pl.BlockSpec((seq, pw), lambda i: (0, 0)), pl.BlockSpec((groups, gw, gw), lambda i: (0, 0, 0)),
                   pl.BlockSpec((1, pw), lambda i: (0, 0))],
        out_shape=[jax.ShapeDtypeStruct((seq, pw), BF16), jax.ShapeDtypeStruct(w_pool.shape, F32),
                   jax.ShapeDtypeStruct((1, pw), F32)],
        compiler_params=_params(("arbitrary",)))(z, dy, w_pool, s_pool)


def _conv_fwd(z, w_dw, taps, b_dw, val_block, gate_block):
    seq = z.shape[0]
    rows, cb = w_dw.shape[1], w_dw.shape[2]

    def body(val_ref, gate_ref, w_ref, b_ref, out_ref):
        h = val_ref[...] * _sigmoid(gate_ref[...])
        acc = jnp.broadcast_to(b_ref[...], h.shape)
        for d in range(taps):
            acc = acc + w_ref[pl.ds(taps - 1 - d, 1), :] * _shift(h, d)
        out_ref[...] = acc

    return pl.pallas_call(
        body, name="conv_fwd", grid=(N_CHIPS,),
        in_specs=[pl.BlockSpec((seq, cb), lambda j: (0, val_block + j)),
                  pl.BlockSpec((seq, cb), lambda j: (0, gate_block + j)),
                  pl.BlockSpec((None, rows, cb), lambda j: (j, 0, 0)),
                  pl.BlockSpec((1, cb), lambda j: (0, j))],
        out_specs=pl.BlockSpec((seq, cb), lambda j: (0, j)),
        out_shape=jax.ShapeDtypeStruct((seq, N_CHIPS * cb), F32),
        compiler_params=_params(("parallel",)))(z, z, w_dw, b_dw)


def _conv_bwd(z, dout, w_dw, taps, val_block, gate_block):
    seq = z.shape[0]
    rows, cb = w_dw.shape[1], w_dw.shape[2]
    col = pl.BlockSpec((seq, cb), lambda j: (0, j))

    def body(val_ref, gate_ref, do_ref, w_ref, dval_ref, dgate_ref, dw_ref, db_ref):
        val, sg, do = val_ref[...], _sigmoid(gate_ref[...]), do_ref[...]
        h = val * sg
        db_ref[...] = jnp.sum(do, axis=0, keepdims=True)
        dh = jnp.zeros_like(h)
        for d in range(taps):
            k = taps - 1 - d
            dw_ref[pl.ds(k, 1), :] = jnp.sum(do * _shift(h, d), axis=0, keepdims=True)
            dh = dh + w_ref[pl.ds(k, 1), :] * _shift(do, -d)
        dval_ref[...] = (dh * sg).astype(BF16)
        dgate_ref[...] = (dh * val * sg * (1.0 - sg)).astype(BF16)

    return pl.pallas_call(
        body, name="conv_bwd", grid=(N_CHIPS,),
        in_specs=[pl.BlockSpec((seq, cb), lambda j: (0, val_block + j)),
                  pl.BlockSpec((seq, cb), lambda j: (0, gate_block + j)), col,
                  pl.BlockSpec((None, rows, cb), lambda j: (j, 0, 0))],
        out_specs=[col, col, pl.BlockSpec((taps, cb), lambda j: (0, j)), pl.BlockSpec((1, cb), lambda j: (0, j))],
        out_shape=[jax.ShapeDtypeStruct((seq, N_CHIPS * cb), BF16), jax.ShapeDtypeStruct((seq, N_CHIPS * cb), BF16),
                   jax.ShapeDtypeStruct((taps, N_CHIPS * cb), F32), jax.ShapeDtypeStruct((1, N_CHIPS * cb), F32)],
        compiler_params=_params(("parallel",)))(z, z, dout, w_dw)


def _ln_swish(hc, g, b):
    y = _layer_norm(hc, g, b)
    return y * _sigmoid(y)


def _ln_swish_fwd(hc, g, b):
    s, cw = hc.shape
    tr = _tile(s, ROW_TILE)
    row = pl.BlockSpec((tr, cw), lambda i: (i, 0))
    vec = pl.BlockSpec((1, cw), lambda i: (0, 0))

    def body(h_ref, g_ref, b_ref, y_ref):
        y_ref[...] = _ln_swish(h_ref[...], g_ref[...], b_ref[...]).astype(BF16)

    return pl.pallas_call(body, name="ln_swish_fwd", grid=(s // tr,), in_specs=[row, vec, vec], out_specs=row,
                          out_shape=jax.ShapeDtypeStruct((s, cw), BF16),
                          compiler_params=_params(("parallel",)))(hc, g, b)


def _ln_swish_bwd(hc, dy, g, b, dy_block):
    s, cw = hc.shape
    tr = _tile(s, ROW_TILE)
    row = pl.BlockSpec((tr, cw), lambda i: (i, 0))
    vec = pl.BlockSpec((1, cw), lambda i: (0, 0))

    def body(h_ref, dy_ref, g_ref, b_ref, dh_ref, dg_ref, db_ref):
        _, vjp = jax.vjp(_ln_swish, h_ref[...], g_ref[...], b_ref[...])
        dh, dg, db = vjp(dy_ref[...])
        dh_ref[...] = dh

        @pl.when(pl.program_id(0) == 0)
        def _():
            dg_ref[...] = dg
            db_ref[...] = db

        @pl.when(pl.program_id(0) != 0)
        def _():
            dg_ref[...] += dg
            db_ref[...] += db

    return pl.pallas_call(
        body, name="ln_swish_bwd", grid=(s // tr,),
        in_specs=[row, pl.BlockSpec((tr, cw), lambda i: (i, dy_block)), vec, vec], out_specs=[row, vec, vec],
        out_shape=[jax.ShapeDtypeStruct((s, cw), F32), jax.ShapeDtypeStruct((1, cw), F32),
                   jax.ShapeDtypeStruct((1, cw), F32)],
        compiler_params=_params(("arbitrary",)))(hc, dy, g, b)


def _attn_probs(q, k, scale):
    s = lax.dot_general(q, k, NT, preferred_element_type=F32) * scale
    e = jnp.exp(s - jnp.max(s, axis=-1, keepdims=True))
    return e / jnp.sum(e, axis=-1, keepdims=True)


def _attn_fwd(q, k, v):
    seq, d = q.shape
    mem = k.shape[0]
    hd = d // XATTN_HEADS
    scale = hd ** -0.5
    qs = pl.BlockSpec((seq, hd), lambda h: (0, h))
    ms = pl.BlockSpec((mem, hd), lambda h: (0, h))

    def body(q_ref, k_ref, v_ref, a_ref):
        p = _attn_probs(q_ref[...], k_ref[...], scale)
        a_ref[...] = jnp.dot(p.astype(BF16), v_ref[...], preferred_element_type=F32).astype(BF16)

    return pl.pallas_call(body, name="attn_fwd", grid=(XATTN_HEADS,), in_specs=[qs, ms, ms], out_specs=qs,
                          out_shape=jax.ShapeDtypeStruct((seq, d), BF16),
                          compiler_params=_params(("parallel",)))(q, k, v)


def _attn_bwd(q, k, v, da):
    seq, d = q.shape
    mem = k.shape[0]
    hd = d // XATTN_HEADS
    scale = hd ** -0.5
    qs = pl.BlockSpec((seq, hd), lambda h: (0, h))
    ms = pl.BlockSpec((mem, hd), lambda h: (0, h))

    def body(q_ref, k_ref, v_ref, da_ref, dq_ref, dk_ref, dv_ref):
        q_, k_, v_, da_ = q_ref[...], k_ref[...], v_ref[...], da_ref[...]
        p = _attn_probs(q_, k_, scale)
        dv_ref[...] = lax.dot_general(p.astype(BF16), da_, TN, preferred_element_type=F32).astype(BF16)
        dp = lax.dot_general(da_, v_, NT, preferred_element_type=F32)
        ds = (p * (dp - jnp.sum(dp * p, axis=-1, keepdims=True)) * scale).astype(BF16)
        dq_ref[...] = jnp.dot(ds, k_, preferred_element_type=F32).astype(BF16)
        dk_ref[...] = lax.dot_general(ds, q_, TN, preferred_element_type=F32).astype(BF16)

    return pl.pallas_call(
        body, name="attn_bwd", grid=(XATTN_HEADS,), in_specs=[qs, ms, ms, qs], out_specs=[qs, ms, ms],
        out_shape=[jax.ShapeDtypeStruct((seq, d), BF16), jax.ShapeDtypeStruct((mem, d), BF16),
                   jax.ShapeDtypeStruct((mem, d), BF16)],
        compiler_params=_params(("parallel",)))(q, k, v, da)


def _place_shard(name, place, w, l, dtype, after=()):
    _, r, c = w.shape
    tr = _tile(r, 2 * ROW_TILE) if r % 16 == 0 else r
    after = list(after)

    def body(place_ref, w_ref, *refs):
        refs[-1][...] = w_ref[...].astype(dtype)

    return pl.pallas_call(
        body, name=name,
        grid_spec=pltpu.PrefetchScalarGridSpec(
            num_scalar_prefetch=1, grid=(r // tr,),
            in_specs=[pl.BlockSpec((None, tr, c), lambda i, p: (l, i, 0))] + [ANY] * len(after),
            out_specs=pl.BlockSpec((None, tr, c), lambda i, p: (p[1], i, 0))),
        out_shape=jax.ShapeDtypeStruct((N_CHIPS, r, c), dtype),
        compiler_params=_params(("parallel",)))(place, w, *after)


def _place_flat(place, flat):
    rows, lanes = flat.shape

    def body(place_ref, f_ref, o_ref):
        o_ref[...] = f_ref[...]

    return pl.pallas_call(
        body, name="place_flat",
        grid_spec=pltpu.PrefetchScalarGridSpec(
            num_scalar_prefetch=1, grid=(1,), in_specs=[pl.BlockSpec((rows, lanes), lambda i, p: (0, 0))],
            out_specs=pl.BlockSpec((None, rows, lanes), lambda i, p: (p[2], 0, 0))),
        out_shape=jax.ShapeDtypeStruct((N_DEVICES, rows, lanes), flat.dtype),
        compiler_params=_params(("arbitrary",)))(place, flat)


def _pair_sum(name, place, dw, got):
    n, _, r2, c = dw.shape
    tr = _tile(r2, ROW_TILE)

    def body(place_ref, own_ref, got_ref, s_ref, t_ref):
        val = (own_ref[...] + got_ref[...]).astype(BF16)
        s_ref[...] = val

        @pl.when(pl.program_id(1) == place_ref[1])
        def _():
            t_ref[...] = val

    slab = pl.BlockSpec((None, tr, c), lambda i, j, p: (j, i, 0))
    sds = jax.ShapeDtypeStruct((n, r2, c), BF16)
    return pl.pallas_call(
        body, name=name,
        grid_spec=pltpu.PrefetchScalarGridSpec(
            num_scalar_prefetch=1, grid=(r2 // tr, n),
            in_specs=[pl.BlockSpec((None, None, tr, c), lambda i, j, p: (j, p[0], i, 0)), slab],
            out_specs=[slab, pl.BlockSpec((None, tr, c), lambda i, j, p: (p[1], i, 0))]),
        out_shape=[sds, sds], compiler_params=_params(("parallel", "arbitrary")))(place, dw, got)


def _chip_sum(name, place, parts):
    n, r2, c = parts.shape
    tr = _tile(r2, ROW_TILE)

    def body(place_ref, *refs):
        o_ref = refs[n]
        acc = refs[0][...].astype(F32)
        for j in range(1, n):
            acc = acc + refs[j][...].astype(F32)
        o_ref[...] = acc

    part = lambda j: pl.BlockSpec((None, tr, c), lambda i, p: (j, i, 0))
    return pl.pallas_call(
        body, name=name,
        grid_spec=pltpu.PrefetchScalarGridSpec(
            num_scalar_prefetch=1, grid=(r2 // tr,), in_specs=[part(j) for j in range(n)],
            out_specs=pl.BlockSpec((None, tr, c), lambda i, p: (p[0], i, 0))),
        out_shape=jax.ShapeDtypeStruct((2, r2, c), F32),
        compiler_params=_params(("parallel",)))(place, *([parts] * n))


def _device_sum(parts):
    n, rows, lanes = parts.shape
    tr = _tile(rows, 4 * ROW_TILE) if rows % 8 == 0 else rows

    def body(p_ref, o_ref):
        acc = p_ref[0]
        for j in range(1, n):
            acc = acc + p_ref[j]
        o_ref[...] = acc

    return pl.pallas_call(
        body, name="device_sum", grid=(rows // tr,), in_specs=[pl.BlockSpec((n, tr, lanes), lambda i: (0, i, 0))],
        out_specs=pl.BlockSpec((tr, lanes), lambda i: (i, 0)), out_shape=jax.ShapeDtypeStruct((rows, lanes), F32),
        compiler_params=_params(("parallel",)))(parts)


def _adam_update(w, g, m, v):
    nm = ADAM_B1 * m + (1.0 - ADAM_B1) * g
    nv = ADAM_B2 * v + (1.0 - ADAM_B2) * (g * g)
    c1 = 1.0 - ADAM_B1 ** ADAM_STEP
    c2 = 1.0 - ADAM_B2 ** ADAM_STEP
    return -ADAM_LR * ((nm / c1) / (jnp.sqrt(nv / c2) + ADAM_EPS) + ADAM_WD * w), nm, nv


def _adamw_layer(name, l, w, g, m, v, prev):
    n_l, r, c = w.shape
    tr = _tile(r, ROW_TILE)
    slab = pl.BlockSpec((None, tr, c), lambda i: (l, i, 0))

    def body(w_ref, g_ref, m_ref, v_ref, *refs):
        go_ref, d_ref, nm_ref, nv_ref = refs[-4:]
        g_ = g_ref[...]
        delta, nm, nv = _adam_update(w_ref[...], g_, m_ref[...], v_ref[...])
        go_ref[...] = g_
        d_ref[...] = delta
        nm_ref[...] = nm
        nv_ref[...] = nv

    ins = [w, g, m, v]
    in_specs = [slab, pl.BlockSpec((tr, c), lambda i: (i, 0)), slab, slab]
    aliases = {}
    if prev is not None:
        aliases = {len(ins) + i: i for i in range(4)}
        ins += list(prev)
        in_specs += [ANY] * 4
    sds = jax.ShapeDtypeStruct((n_l, r, c), F32)
    return pl.pallas_call(body, name=name, grid=(r // tr,), in_specs=in_specs, out_specs=[slab] * 4,
                          out_shape=[sds] * 4, input_output_aliases=aliases,
                          compiler_params=_params(("parallel",)))(*ins)


def _adamw_flat(name, w, g, m, v):
    rows, cols = w.shape
    tr = _tile(rows, ROW_TILE) if rows % 8 == 0 else rows
    spec = pl.BlockSpec((tr, cols), lambda i: (i, 0))

    def body(w_ref, g_ref, m_ref, v_ref, d_ref, nm_ref, nv_ref):
        d_ref[...], nm_ref[...], nv_ref[...] = _adam_update(w_ref[...], g_ref[...], m_ref[...], v_ref[...])

    sds = jax.ShapeDtypeStruct((rows, cols), F32)
    return pl.pallas_call(body, name=name, grid=(rows // tr,), in_specs=[spec] * 4, out_specs=[spec] * 3,
                          out_shape=[sds] * 3, compiler_params=_params(("parallel",)))(w, g, m, v)


def _me():
    return lax.axis_index("x"), lax.axis_index("y"), lax.axis_index("c")


def _other_chips(x, y):
    return [(1 - x, y, 2 * (1 - x) + y), (x, 1 - y, 2 * x + 1 - y), (1 - x, 1 - y, 2 * (1 - x) + 1 - y)]


def _remote(src, dst, send_sem, recv_sem, target):
    return pltpu.make_async_remote_copy(src_ref=src, dst_ref=dst, send_sem=send_sem, recv_sem=recv_sem,
                                        device_id=target, device_id_type=MESH)


def _exchange(name, bufs, plan, n_copies):
    n = len(bufs)

    def body(*refs):
        send_sems, recv_sems = refs[2 * n:]
        copies = []
        for i, (src, dst, target) in enumerate(plan(refs[n:2 * n], _me())):
            if target is None:
                cp = pltpu.make_async_copy(src, dst, send_sems.at[i])
            else:
                cp = _remote(src, dst, send_sems.at[i], recv_sems.at[i], target)
            cp.start()
            copies.append((cp, target))
        assert len(copies) == n_copies
        for cp, target in copies:
            if target is None:
                cp.wait()
            else:
                cp.wait_recv()
        for cp, target in copies:
            if target is not None:
                cp.wait_send()

    return pl.pallas_call(
        body, name=name, in_specs=[ANY] * n, out_specs=[ANY] * n,
        out_shape=[jax.ShapeDtypeStruct(b.shape, b.dtype) for b in bufs],
        scratch_shapes=[pltpu.SemaphoreType.DMA((n_copies,)), pltpu.SemaphoreType.DMA((n_copies,))],
        input_output_aliases={i: i for i in range(n)},
        compiler_params=pltpu.CompilerParams(has_side_effects=True))(*bufs)


def _start_copies(name, groups, after=()):
    all_bufs = [b for bufs, _, _ in groups for b in bufs]
    after = list(after)
    n = len(all_bufs)
    n_g = len(groups)

    def body(*refs):
        in_refs, sem_refs = refs[:n], refs[n + len(after):n + len(after) + 2 * n_g]
        pos = 0
        for g, (bufs, plan, n_copies) in enumerate(groups):
            copies = plan(in_refs[pos:pos + len(bufs)], _me())
            assert len(copies) == n_copies
            for i, (src, dst, target) in enumerate(copies):
                _remote(src, dst, sem_refs[2 * g].at[i], sem_refs[2 * g + 1].at[i], target).start()
            pos += len(bufs)

    sems = []
    for _, _, n_copies in groups:
        sems += [pltpu.SemaphoreType.DMA((n_copies,))] * 2
    outs = pl.pallas_call(
        body, name=name, in_specs=[HBM] * n + [ANY] * len(after), out_specs=[SEM] * (2 * n_g) + [HBM] * n,
        out_shape=sems + [pltpu.HBM(b.shape, b.dtype) for b in all_bufs],
        input_output_aliases={i: 2 * n_g + i for i in range(n)},
        compiler_params=pltpu.CompilerParams(has_side_effects=pltpu.SideEffectType.DATAFLOW_SIDE_EFFECTING))(
            *[pltpu.with_memory_space_constraint(b, pltpu.HBM) for b in all_bufs], *after)
    result, pos = [], 2 * n_g
    for g, (bufs, _, _) in enumerate(groups):
        result.append((outs[2 * g], outs[2 * g + 1], list(outs[pos:pos + len(bufs)])))
        pos += len(bufs)
    return result


def _wait_copies(name, started, plan, n_copies, after):
    send_sems, recv_sems, bufs = started
    n = len(bufs)
    after = list(after) if isinstance(after, (list, tuple)) else [after]

    def body(*refs):
        copies = plan(refs[:n], _me())
        assert len(copies) == n_copies
        for i, (src, dst, target) in enumerate(copies):
            cp = _remote(src, dst, refs[n].at[i], refs[n + 1].at[i], target)
            cp.wait_send()
            cp.wait_recv()

    return list(pl.pallas_call(
        body, name=name, in_specs=[HBM] * n + [SEM, SEM] + [ANY] * len(after), out_specs=[HBM] * n,
        out_shape=[pltpu.HBM(b.shape, b.dtype) for b in bufs], input_output_aliases={i: i for i in range(n)},
        compiler_params=pltpu.CompilerParams(has_side_effects=pltpu.SideEffectType.DATAFLOW_SIDE_EFFECTING))(
            *bufs, send_sems, recv_sems, *after))


def _halves(a):
    return a.reshape(a.shape[0], 2, a.shape[1] // 2, a.shape[2])


def _gather_plan(refs, me):
    x, y, c = me
    mine = 2 * x + y
    return [(g.at[mine, c], g.at[mine, c], (px, py, c)) for g in refs for px, py, _ in _other_chips(x, y)]


def _forward_plan(refs, me):
    x, y, c = me
    return [(g.at[chip, c], g.at[chip, c], (x, y, 1 - c)) for g in refs for _, _, chip in _other_chips(x, y)]


def _swap_plan(refs, me):
    x, y, c = me
    k = len(refs) // 2
    return [(refs[i].at[j, 1 - c], refs[k + i].at[j], (x, y, 1 - c)) for i in range(k) for j in range(N_CHIPS)]


def _scatter_plan(refs, me):
    x, y, c = me
    mine = 2 * x + y
    k = len(refs) // 2
    return [(refs[i].at[chip], refs[k + i].at[mine], (px, py, c))
            for i in range(k) for px, py, chip in _other_chips(x, y)]


def _share_plan(refs, me):
    x, y, c = me
    return [(g.at[c], g.at[c], (x, y, 1 - c)) for g in refs]


def _broadcast_plan(refs, me):
    x, y, c = me
    mine = 4 * x + 2 * y + c
    copies = []
    for fx, fy, fc in [(0, 0, 1), (0, 1, 0), (0, 1, 1), (1, 0, 0), (1, 0, 1), (1, 1, 0), (1, 1, 1)]:
        peer = (x + fx - 2 * fx * x, y + fy - 2 * fy * y, c + fc - 2 * fc * c)
        copies.append((refs[0].at[mine], refs[0].at[mine], peer))
    return copies


BIG = ("w_in", "w_out", "w_q", "w_k", "w_v", "w_o", "w_up", "w_down")
COLUMN_SPLIT = ("w_in", "w_up")
WEIGHTS = ("norm_mix_pre", "norm_mix_post", "w_in", "w_out", "gmlp_v_gain", "w_spatial", "b_spatial", "w_pool",
           "s_pool", "w_dw", "b_dw", "conv_ln_g", "conv_ln_b", "norm_xattn_pre", "norm_mem", "norm_xattn_post",
           "w_q", "w_k", "w_v", "w_o", "norm_ffn_pre", "norm_ffn_post", "w_up", "w_down")
SMALL = tuple(n for n in WEIGHTS if n not in BIG)
GATHER_GROUPS = (("w_in", "w_dw"), ("w_out", "w_q", "w_k", "w_v", "w_o"), ("w_up",), ("w_down",))


def _relu2(acc):
    r = jnp.maximum(acc, 0.0)
    return acc, r * r


def _relu2_bwd(acc, up):
    return (acc * (2.0 * jnp.maximum(up, 0.0)),)


def _pack(arrays):
    flat = jnp.concatenate([a.reshape(-1) for a in arrays])
    tile = 8 * V7X_LANES
    pad = (-flat.shape[0]) % tile
    return jnp.pad(flat, (0, pad)).reshape(-1, V7X_LANES)


def _unpack(packed, like):
    flat = packed.reshape(-1)
    out, pos = [], 0
    for a in like:
        out.append(flat[pos:pos + a.size].reshape(a.shape))
        pos += a.size
    return out


class _GradientReducer:
    def __init__(self, place, w, m, v):
        self.place, self.w, self.m, self.v = place, w, m, v
        self.flying = []
        self.done = {n: None for n in BIG}

    def add(self, tag, l, grads):
        names = list(grads)
        views = [_halves(grads[n]) for n in names]
        zones = [lax.empty((v.shape[0],) + v.shape[2:], F32) for v in views]
        started = _start_copies("swap_start_" + tag, [(views + zones, _swap_plan, N_CHIPS * len(names))])[0]
        self.flying.append(dict(stage=0, tag=tag, l=l, names=names, started=started))
        return [started[2][0]]

    def advance(self, after):
        made = []
        after = list(after) if isinstance(after, (list, tuple)) else [after]
        for item in self.flying:
            item["stage"] += 1
        for item in self.flying:
            tag, names, k = item["tag"], item["names"], len(item["names"])
            if item["stage"] == 1:
                bufs = _wait_copies("swap_wait_" + tag, item["started"], _swap_plan, N_CHIPS * k, after)
                sums, parts = zip(*[_pair_sum("pair_sum_" + n, self.place, dv, got)
                                    for n, dv, got in zip(names, bufs[:k], bufs[k:])])
                item["started"] = _start_copies("scatter_start_" + tag,
                                                [(list(sums) + list(parts), _scatter_plan, 3 * k)])[0]
                made.append(item["started"][2][0])
        after = after + made
        for item in list(self.flying):
            tag, names, k = item["tag"], item["names"], len(item["names"])
            if item["stage"] == 3:
                bufs = _wait_copies("scatter_wait_" + tag, item["started"], _scatter_plan, 3 * k, after)
                halves = [_chip_sum("chip_sum_" + n, self.place, p) for n, p in zip(names, bufs[k:])]
                halves = _exchange("share_" + tag, halves, _share_plan, k)
                for n, h in zip(names, halves):
                    g = h.reshape(self.w[n].shape[1:])
                    self.done[n] = _adamw_layer("adamw_" + n, item["l"], self.w[n], g, self.m[n], self.v[n],
                                                self.done[n])
                    made.append(self.done[n][3])
                self.flying.remove(item)
        return made

    def drain(self, after):
        made = list(after)
        while self.flying:
            made = list(after) + self.advance(made)
        return made


def _step(x, mem, target, w, m, v):
    n_layers = w["w_in"].shape[0]
    seq, d = x.shape
    heads, hd = w["gmlp_v_gain"].shape[1:]
    gw = heads * hd
    groups, pgw = w["w_pool"].shape[1:3]
    pw = groups * pgw
    cw = w["b_dw"].shape[1]
    cb = cw // N_CHIPS
    taps = w["w_dw"].shape[1]
    cx, cy, cc = _me()
    chip = 2 * cx + cy
    place = jnp.stack([cc, chip, 2 * chip + cc]).astype(jnp.int32)
    vec = lambda name, l: w[name][l].reshape(1, -1)

    taps_padded = jnp.pad(w["w_dw"], ((0, 0), (0, (-taps) % 16), (0, 0)))
    gathering = []

    def send_layer(l, last):
        for g, names in enumerate(GATHER_GROUPS):
            bufs = [_halves(_place_shard("place_" + n, place, taps_padded if n == "w_dw" else w[n], l,
                                         F32 if n == "w_dw" else BF16, after=last)) for n in names]
            gathering.extend(_start_copies("gather_start_%d%d" % (l, g), [(bufs, _gather_plan, 3 * len(names))],
                                           after=last))
            last = [gathering[-1][2][0]]
        return last

    def arrive(l, g, after):
        names = GATHER_GROUPS[g]
        tag = "%d%d" % (l, g)
        bufs = _wait_copies("gather_wait_" + tag, gathering[l * len(GATHER_GROUPS) + g], _gather_plan,
                            3 * len(names), after)
        bufs = _exchange("gather_forward_" + tag, bufs, _forward_plan, 3 * len(names))
        out = {}
        for n, b in zip(names, bufs):
            full = b.reshape(N_CHIPS, 2 * b.shape[2], b.shape[3])
            out[n] = full if n in COLUMN_SPLIT + ("w_dw",) else full.reshape(-1, full.shape[2])
        return out

    saved = []
    sent = send_layer(0, [])
    _, h1 = _norm_fwd("norm_first", x, None, None, vec("norm_mix_pre", 0))
    for l in range(n_layers):
        gv = w["gmlp_v_gain"][l].reshape(heads, 1, hd)
        ws = w["w_spatial"][l]
        bcol = w["b_spatial"][l].reshape(heads, -1, 1)
        wl = arrive(l, 0, [h1] + sent)
        z = _mm_nn_col("mm_in", h1, wl["w_in"], [F32])[0]
        sent = send_layer(l + 1, [z]) if l + 1 < n_layers else []
        ya = _gmlp_fwd(z, gv, ws, bcol)
        yb = _pool_fwd(z, w["w_pool"][l], vec("s_pool", l), (2 * gw) // pw)
        hc = _conv_fwd(z, wl["w_dw"], taps, vec("b_dw", l), (2 * gw + pw) // cb, (2 * gw + pw + cw) // cb)
        yc = _ln_swish_fwd(hc, vec("conv_ln_g", l), vec("conv_ln_b", l))
        y = jnp.concatenate([ya, yb, yc], axis=1)
        wl.update(arrive(l, 1, [y] + sent))
        sent = []
        o = _mm_nn_row("mm_out", y, wl["w_out"], [F32])[0]
        x1, h2 = _norm_fwd("norm_mix", x, o, vec("norm_mix_post", l), vec("norm_xattn_pre", l))
        _, mn = _norm_fwd("norm_mem", mem, None, None, vec("norm_mem", l))
        q = _mm_nn_row("mm_q", h2, wl["w_q"], [BF16])[0]
        k = _mm_nn_row("mm_k", mn, wl["w_k"], [BF16])[0]
        vv = _mm_nn_row("mm_v", mn, wl["w_v"], [BF16])[0]
        a = _attn_fwd(q, k, vv)
        o2 = _mm_nn_row("mm_o", a, wl["w_o"], [F32])[0]
        x2, h3 = _norm_fwd("norm_xattn", x1, o2, vec("norm_xattn_post", l), vec("norm_ffn_pre", l))
        wl.update(arrive(l, 2, h3))
        up, r = _mm_nn_col("mm_up", h3, wl["w_up"], [F32, BF16], epi=_relu2)
        wl.update(arrive(l, 3, r))
        o3 = _mm_nn_row("mm_down", r, wl["w_down"], [F32])[0]
        g_next = vec("norm_mix_pre", l + 1) if l + 1 < n_layers else None
        x3, h_next = _norm_fwd("norm_ffn", x2, o3, vec("norm_ffn_post", l), g_next)
        saved.append(dict(x=x, h1=h1, z=z, hc=hc, y=y, o=o, x1=x1, h2=h2, mn=mn, q=q, k=k, v=vv, a=a, o2=o2, x2=x2,
                          h3=h3, up=up, r=r, o3=o3, x3=x3, gv=gv, ws=ws, bcol=bcol, w=wl))
        x, h1 = x3, h_next

    dx, loss_parts = _loss_head(x, target)
    loss = lax.psum(jnp.sum(loss_parts), ("x", "y", "c"))

    reducer = _GradientReducer(place, w, m, v)
    small = {n: [None] * n_layers for n in SMALL}
    by_chip = lambda g: g.reshape(N_CHIPS, g.shape[0] // N_CHIPS, g.shape[1])
    small_sent = [None] * n_layers

    def small_layer(l):
        return [small[n][l] if n == "w_dw" else small[n][l].reshape(w[n].shape[1:]) for n in SMALL]

    def send_small(l):
        landing = _place_flat(place, _pack(small_layer(l)))
        small_sent[l] = _start_copies("small_start_%d" % l, [([landing], _broadcast_plan, N_DEVICES - 1)])[0]
        return [small_sent[l][2][0]]

    dh = None
    made = []
    for l in reversed(range(n_layers)):
        t = saved[l]
        wl = t["w"]
        g_next = vec("norm_mix_pre", l + 1) if l + 1 < n_layers else None
        dx, do3, dgp, dgn = _norm_bwd("norm_ffn_bwd", dx, dh, t["x3"], t["o3"], vec("norm_ffn_post", l), g_next,
                                      after=made)
        small["norm_ffn_post"][l] = dgp
        if dgn is not None:
            small["norm_mix_pre"][l + 1] = dgn
        made = reducer.advance(dx)
        if l + 1 < n_layers:
            made += send_small(l + 1)
        d_down = _mm_tn_row("mm_down_dw", t["r"], do3)
        made += reducer.add("%d0" % l, l, {"w_down": by_chip(d_down)})
        dup = _mm_nt_row("mm_down_dx", do3, wl["w_down"], BF16, epi=_relu2_bwd, extra=t["up"], after=made)
        d_up = _mm_tn_col("mm_up_dw", t["h3"], dup)
        made = reducer.add("%d3" % l, l, {"w_up": d_up})
        dh3 = _mm_nt_col("mm_up_dx", dup, wl["w_up"], F32, after=made)
        made = []
        dx, do2, dgp, dgn = _norm_bwd("norm_xattn_bwd", dx, dh3, t["x2"], t["o2"], vec("norm_xattn_post", l),
                                      vec("norm_ffn_pre", l), after=made)
        small["norm_xattn_post"][l], small["norm_ffn_pre"][l] = dgp, dgn
        made = reducer.advance(dx)
        d_o = _mm_tn_row("mm_o_dw", t["a"], do2)
        da = _mm_nt_row("mm_o_dx", do2, wl["w_o"], BF16, after=made)
        dq, dk, dv = _attn_bwd(t["q"], t["k"], t["v"], da)
        d_q = _mm_tn_row("mm_q_dw", t["h2"], dq)
        d_k = _mm_tn_row("mm_k_dw", t["mn"], dk)
        d_v = _mm_tn_row("mm_v_dw", t["mn"], dv)
        dh2 = _mm_nt_row("mm_q_dx", dq, wl["w_q"], F32)
        dmn = _mm_nt_row("mm_k_dx", dk, wl["w_k"], F32) + _mm_nt_row("mm_v_dx", dv, wl["w_v"], F32)
        _, _, _, small["norm_mem"][l] = _norm_bwd("norm_mem_bwd", None, dmn, mem, None, None, vec("norm_mem", l))
        dx, do, dgp, dgn = _norm_bwd("norm_mix_bwd", dx, dh2, t["x1"], t["o"], vec("norm_mix_post", l),
                                     vec("norm_xattn_pre", l))
        small["norm_mix_post"][l], small["norm_xattn_pre"][l] = dgp, dgn
        made = reducer.advance(dx)
        d_out = _mm_tn_row("mm_out_dw", t["y"], do)
        made += reducer.add("%d1" % l, l, {"w_o": by_chip(d_o), "w_q": by_chip(d_q), "w_k": by_chip(d_k),
                                    "w_v": by_chip(d_v), "w_out": by_chip(d_out)})
        dy = _mm_nt_row("mm_out_dx", do, wl["w_out"], F32, after=made)
        dzu, dzv, dgv, dws, dbcol = _gmlp_bwd(t["z"], dy, t["gv"], t["ws"], t["bcol"])
        small["gmlp_v_gain"][l] = dgv.reshape(heads, hd)
        small["w_spatial"][l] = dws
        small["b_spatial"][l] = dbcol.reshape(heads, -1)
        dzp, dwp, dsp = _pool_bwd(t["z"], dy, w["w_pool"][l], vec("s_pool", l), (2 * gw) // pw, gw // pw)
        small["w_pool"][l], small["s_pool"][l] = dwp, dsp.reshape(-1)
        dhc, dlg, dlb = _ln_swish_bwd(t["hc"], dy, vec("conv_ln_g", l), vec("conv_ln_b", l), (gw + pw) // cw)
        small["conv_ln_g"][l], small["conv_ln_b"][l] = dlg.reshape(-1), dlb.reshape(-1)
        dval, dgate, dwd, dbd = _conv_bwd(t["z"], dhc, wl["w_dw"], taps, (2 * gw + pw) // cb,
                                          (2 * gw + pw + cw) // cb)
        small["w_dw"][l], small["b_dw"][l] = dwd, dbd.reshape(-1)
        dz = jnp.concatenate([dzu, dzv, dzp, dval, dgate], axis=1)
        made = reducer.advance(dz)
        d_in = _mm_tn_col("mm_in_dw", t["h1"], dz)
        made += reducer.add("%d2" % l, l, {"w_in": d_in})
        dh = _mm_nt_col("mm_in_dx", dz, wl["w_in"], F32, after=made)
        made = []
    grad_x, _, _, dgn = _norm_bwd("norm_first_bwd", dx, dh, saved[0]["x"], None, None, vec("norm_mix_pre", 0))
    small["norm_mix_pre"][0] = dgn
    drained = reducer.drain([grad_x] + send_small(0))
    drained = [a for a in drained if all(a is not sent[2][0] for sent in small_sent)]

    totals = []
    for l in range(n_layers):
        landed = _wait_copies("small_wait_%d" % l, small_sent[l], _broadcast_plan, N_DEVICES - 1, drained)[0]
        totals.append(_unpack(_device_sum(landed), small_layer(l)))
    grad = {}
    for i, n in enumerate(SMALL):
        g = jnp.stack([totals[l][i] for l in range(n_layers)])
        grad[n] = lax.dynamic_slice_in_dim(g, chip * cb, cb, axis=2) if n == "w_dw" else g
    like = [w[n] for n in SMALL]
    dl, nm, nv = _adamw_flat("adamw_small", _pack(like), _pack([grad[n] for n in SMALL]),
                             _pack([m[n] for n in SMALL]), _pack([v[n] for n in SMALL]))
    delta, new_m, new_v = (dict(zip(SMALL, _unpack(p, like))) for p in (dl, nm, nv))
    for n in BIG:
        grad[n], delta[n], new_m[n], new_v[n] = reducer.done[n]

    return (loss, grad_x[None], *[grad[n] for n in WEIGHTS], *[delta[n] for n in WEIGHTS],
            *[new_m[n] for n in WEIGHTS], *[new_v[n] for n in WEIGHTS])


def kernel(x, mem, norm_mix_pre, norm_mix_post, w_in, w_out, gmlp_v_gain, w_spatial, b_spatial, w_pool, s_pool, w_dw, b_dw, conv_ln_g, conv_ln_b, norm_xattn_pre, norm_mem, norm_xattn_post, w_q, w_k, w_v, w_o, norm_ffn_pre, norm_ffn_post, w_up, w_down, loss_target, m_norm_mix_pre, m_norm_mix_post, m_w_in, m_w_out, m_gmlp_v_gain, m_w_spatial, m_b_spatial, m_w_pool, m_s_pool, m_w_dw, m_b_dw, m_conv_ln_g, m_conv_ln_b, m_norm_xattn_pre, m_norm_mem, m_norm_xattn_post, m_w_q, m_w_k, m_w_v, m_w_o, m_norm_ffn_pre, m_norm_ffn_post, m_w_up, m_w_down, v_norm_mix_pre, v_norm_mix_post, v_w_in, v_w_out, v_gmlp_v_gain, v_w_spatial, v_b_spatial, v_w_pool, v_s_pool, v_w_dw, v_b_dw, v_conv_ln_g, v_conv_ln_b, v_norm_xattn_pre, v_norm_mem, v_norm_xattn_post, v_w_q, v_w_k, v_w_v, v_w_o, v_norm_ffn_pre, v_norm_ffn_post, v_w_up, v_w_down):
    given = dict(locals())
    w = {n: given[n] for n in WEIGHTS}
    m = {n: given["m_" + n] for n in WEIGHTS}
    v = {n: given["v_" + n] for n in WEIGHTS}
    return _step(x[0], mem[0], loss_target[0], w, m, v)
```

```python
import functools

import jax
import jax.numpy as jnp
from jax import lax
from jax.experimental import pallas as pl
from jax.experimental.pallas import tpu as pltpu

F32 = jnp.float32
BF16 = jnp.bfloat16
MESH = pl.DeviceIdType.MESH

N_CHIPS = 4
N_DEVICES = 8
XATTN_HEADS = 4
POOL_WINDOWS = (2, 4, 8, 16)
RMS_EPS = 1e-6
LN_EPS = 1e-5
ADAM_LR, ADAM_B1, ADAM_B2, ADAM_EPS, ADAM_WD, ADAM_STEP = 0.001, 0.9, 0.999, 1e-08, 0.01, 10

V7X_LANES = 128
V7X_VMEM_LIMIT = 56 * 1024 * 1024
ROW_TILE = 256
MM_TILE_M, MM_TILE_N, MM_TILE_K = 1024, 1024, 2048

ANY = pl.BlockSpec(memory_space=pl.ANY)
HBM = pl.BlockSpec(memory_space=pltpu.HBM)
SEM = pl.BlockSpec(memory_space=pltpu.SEMAPHORE)


def _tile(dim, pref):
    if dim <= pref:
        return dim
    t = (pref // V7X_LANES) * V7X_LANES
    while t >= V7X_LANES:
        if dim % t == 0:
            return t
        t -= V7X_LANES
    return dim


def _params(sem=None):
    return pltpu.CompilerParams(dimension_semantics=sem, vmem_limit_bytes=V7X_VMEM_LIMIT)


NN = (((1,), (0,)), ((), ()))
NT = (((1,), (1,)), ((), ()))
TN = (((0,), (0,)), ((), ()))


def _mm(name, a, b, *, dn, grid, a_spec, b_spec, o_specs, out_shapes, acc_shape, epi=None, extra=None,
        extra_spec=None, after=()):
    nk = grid[2]
    n_out = len(out_shapes)
    has_extra = extra is not None
    after = list(after)

    def body(*refs):
        a_ref, b_ref = refs[0], refs[1]
        pos = 2
        e_ref = None
        if has_extra:
            e_ref = refs[pos]
            pos += 1
        pos += len(after)
        o_refs = refs[pos:pos + n_out]
        part = lax.dot_general(a_ref[...].astype(BF16), b_ref[...].astype(BF16), dn, preferred_element_type=F32)

        def finish(total):
            if epi is None:
                vals = (total,)
            elif has_extra:
                vals = epi(total, e_ref[...])
            else:
                vals = epi(total)
            for o, v in zip(o_refs, vals):
                o[...] = v.astype(o.dtype)

        if nk == 1:
            finish(part)
            return
        acc = refs[pos + n_out]
        k = pl.program_id(2)

        @pl.when(k == 0)
        def _():
            acc[...] = part

        @pl.when(jnp.logical_and(k > 0, k < nk - 1))
        def _():
            acc[...] += part

        @pl.when(k == nk - 1)
        def _():
            finish(acc[...] + part)

    ins, in_specs = [a, b], [a_spec, b_spec]
    if has_extra:
        ins.append(extra)
        in_specs.append(extra_spec)
    ins += after
    in_specs += [ANY] * len(after)
    outs = pl.pallas_call(
        body, name=name, grid=grid, in_specs=in_specs, out_specs=list(o_specs), out_shape=list(out_shapes),
        scratch_shapes=[pltpu.VMEM(acc_shape, F32)] if nk > 1 else [],
        compiler_params=_params(("parallel", "parallel", "arbitrary")))(*ins)
    return outs


def _mm_nn_row(name, a, w, out_dtypes, epi=None):
    m, k = a.shape
    n = w.shape[1]
    tm, tn, tk = _tile(m, MM_TILE_M), _tile(n, MM_TILE_N), _tile(k, MM_TILE_K)
    o_spec = pl.BlockSpec((tm, tn), lambda i, j, kk: (i, j))
    return _mm(name, a, w, dn=NN, grid=(m // tm, n // tn, k // tk),
               a_spec=pl.BlockSpec((tm, tk), lambda i, j, kk: (i, kk)),
               b_spec=pl.BlockSpec((tk, tn), lambda i, j, kk: (kk, j)),
               o_specs=[o_spec] * len(out_dtypes),
               out_shapes=[jax.ShapeDtypeStruct((m, n), d) for d in out_dtypes], acc_shape=(tm, tn), epi=epi)


def _mm_nn_col(name, a, w, out_dtypes, epi=None):
    m, k = a.shape
    c = w.shape[2]
    tm, tn, tk = _tile(m, MM_TILE_M), _tile(c, MM_TILE_N), _tile(k, MM_TILE_K)
    nb = c // tn
    o_spec = pl.BlockSpec((tm, tn), lambda i, j, kk: (i, j))
    return _mm(name, a, w, dn=NN, grid=(m // tm, N_CHIPS * nb, k // tk),
               a_spec=pl.BlockSpec((tm, tk), lambda i, j, kk: (i, kk)),
               b_spec=pl.BlockSpec((None, tk, tn), lambda i, j, kk: (j // nb, kk, j % nb)),
               o_specs=[o_spec] * len(out_dtypes),
               out_shapes=[jax.ShapeDtypeStruct((m, N_CHIPS * c), d) for d in out_dtypes], acc_shape=(tm, tn),
               epi=epi)


def _mm_nt_row(name, dy, w, out_dtype, epi=None, extra=None, after=()):
    m, n = dy.shape
    k = w.shape[0]
    tm, tn, tk = _tile(m, MM_TILE_M), _tile(k, MM_TILE_N), _tile(n, MM_TILE_K)
    o_spec = pl.BlockSpec((tm, tn), lambda i, j, kk: (i, j))
    return _mm(name, dy, w, dn=NT, grid=(m // tm, k // tn, n // tk),
               a_spec=pl.BlockSpec((tm, tk), lambda i, j, kk: (i, kk)),
               b_spec=pl.BlockSpec((tn, tk), lambda i, j, kk: (j, kk)),
               o_specs=[o_spec], out_shapes=[jax.ShapeDtypeStruct((m, k), out_dtype)], acc_shape=(tm, tn),
               epi=epi, extra=extra, extra_spec=o_spec, after=after)[0]


def _mm_nt_col(name, dy, w, out_dtype, after=()):
    m = dy.shape[0]
    k, c = w.shape[1], w.shape[2]
    tm, tn, tk = _tile(m, MM_TILE_M), _tile(k, MM_TILE_N), _tile(c, MM_TILE_K)
    kb = c // tk
    return _mm(name, dy, w, dn=NT, grid=(m // tm, k // tn, N_CHIPS * kb),
               a_spec=pl.BlockSpec((tm, tk), lambda i, j, kk: (i, kk)),
               b_spec=pl.BlockSpec((None, tn, tk), lambda i, j, kk: (kk // kb, j, kk % kb)),
               o_specs=[pl.BlockSpec((tm, tn), lambda i, j, kk: (i, j))],
               out_shapes=[jax.ShapeDtypeStruct((m, k), out_dtype)], acc_shape=(tm, tn), after=after)[0]


def _mm_tn_row(name, a, dy):
    t, m = a.shape
    n = dy.shape[1]
    tm, tn, tk = _tile(m, MM_TILE_M), _tile(n, MM_TILE_N), _tile(t, MM_TILE_K)
    return _mm(name, a, dy, dn=TN, grid=(m // tm, n // tn, t // tk),
               a_spec=pl.BlockSpec((tk, tm), lambda i, j, kk: (kk, i)),
               b_spec=pl.BlockSpec((tk, tn), lambda i, j, kk: (kk, j)),
               o_specs=[pl.BlockSpec((tm, tn), lambda i, j, kk: (i, j))],
               out_shapes=[jax.ShapeDtypeStruct((m, n), BF16)], acc_shape=(tm, tn))[0]


def _mm_tn_col(name, a, dy):
    t, m = a.shape
    c = dy.shape[1] // N_CHIPS
    tm, tn, tk = _tile(m, MM_TILE_M), _tile(c, MM_TILE_N), _tile(t, MM_TILE_K)
    nb = c // tn
    return _mm(name, a, dy, dn=TN, grid=(m // tm, N_CHIPS * nb, t // tk),
               a_spec=pl.BlockSpec((tk, tm), lambda i, j, kk: (kk, i)),
               b_spec=pl.BlockSpec((tk, tn), lambda i, j, kk: (kk, j)),
               o_specs=[pl.BlockSpec((None, tm, tn), lambda i, j, kk: (j // nb, i, j % nb))],
               out_shapes=[jax.ShapeDtypeStruct((N_CHIPS, m, c), BF16)], acc_shape=(tm, tn))[0]


def _rms(x, g):
    r = lax.rsqrt(jnp.mean(x * x, axis=-1, keepdims=True) + RMS_EPS)
    return x * r * g


def _rms_bwd(x, g, dy):
    r = lax.rsqrt(jnp.mean(x * x, axis=-1, keepdims=True) + RMS_EPS)
    xr = x * r
    dyg = dy * g
    dx = r * (dyg - xr * jnp.mean(dyg * xr, axis=-1, keepdims=True))
    return dx, jnp.sum(dy * xr, axis=0, keepdims=True)


def _norm_fwd(name, x, o, g_post, g_next, after=()):
    after = list(after)
    s, d = x.shape
    tr = _tile(s, ROW_TILE)
    has_prev, has_next = o is not None, g_next is not None
    row = pl.BlockSpec((tr, d), lambda i: (i, 0))
    vec = pl.BlockSpec((1, d), lambda i: (0, 0))

    def body(*refs):
        refs = list(refs)
        xn = refs.pop(0)[...]
        if has_prev:
            o_ref, gp_ref = refs.pop(0), refs.pop(0)
            xn = xn + _rms(o_ref[...], gp_ref[...])
        gn_ref = refs.pop(0) if has_next else None
        del refs[:len(after)]
        if has_prev:
            refs.pop(0)[...] = xn
        if has_next:
            refs.pop(0)[...] = _rms(xn, gn_ref[...]).astype(BF16)

    ins, in_specs = [x], [row]
    if has_prev:
        ins += [o, g_post]
        in_specs += [row, vec]
    if has_next:
        ins.append(g_next)
        in_specs.append(vec)
    ins += after
    in_specs += [ANY] * len(after)
    out_shapes, out_specs = [], []
    if has_prev:
        out_shapes.append(jax.ShapeDtypeStruct((s, d), F32))
        out_specs.append(row)
    if has_next:
        out_shapes.append(jax.ShapeDtypeStruct((s, d), BF16))
        out_specs.append(row)
    outs = pl.pallas_call(body, name=name, grid=(s // tr,), in_specs=in_specs, out_specs=out_specs,
                          out_shape=out_shapes, compiler_params=_params(("parallel",)))(*ins)
    outs = list(outs)
    x_new = outs.pop(0) if has_prev else x
    h = outs.pop(0) if has_next else None
    return x_new, h


def _norm_bwd(name, dxn, dh, xn, o, g_post, g_next, after=()):
    after = list(after)
    s, d = xn.shape
    tr = _tile(s, ROW_TILE)
    has_prev, has_next, has_dxn = o is not None, dh is not None, dxn is not None
    row = pl.BlockSpec((tr, d), lambda i: (i, 0))
    vec = pl.BlockSpec((1, d), lambda i: (0, 0))

    def body(*refs):
        refs = list(refs)
        first = pl.program_id(0) == 0
        dxn_ref = refs.pop(0) if has_dxn else None
        dh_ref = refs.pop(0) if has_next else None
        xn_ref = refs.pop(0)
        if has_prev:
            o_ref, gp_ref = refs.pop(0), refs.pop(0)
        gn_ref = refs.pop(0) if has_next else None
        del refs[:len(after)]
        dx_ref = refs.pop(0)
        if has_prev:
            do_ref, dgp_ref = refs.pop(0), refs.pop(0)
        dgn_ref = refs.pop(0) if has_next else None

        def accumulate(ref, val):
            @pl.when(first)
            def _():
                ref[...] = val

            @pl.when(jnp.logical_not(first))
            def _():
                ref[...] += val

        dx = dxn_ref[...] if has_dxn else None
        if has_next:
            dxh, dgn = _rms_bwd(xn_ref[...], gn_ref[...], dh_ref[...].astype(F32))
            dx = dxh if dx is None else dx + dxh
            accumulate(dgn_ref, dgn)
        dx_ref[...] = dx
        if has_prev:
            do, dgp = _rms_bwd(o_ref[...], gp_ref[...], dx)
            do_ref[...] = do.astype(BF16)
            accumulate(dgp_ref, dgp)

    ins, in_specs = [], []
    if has_dxn:
        ins.append(dxn)
        in_specs.append(row)
    if has_next:
        ins.append(dh)
        in_specs.append(row)
    ins.append(xn)
    in_specs.append(row)
    if has_prev:
        ins += [o, g_post]
        in_specs += [row, vec]
    if has_next:
        ins.append(g_next)
        in_specs.append(vec)
    ins += after
    in_specs += [ANY] * len(after)
    out_shapes, out_specs = [jax.ShapeDtypeStruct((s, d), F32)], [row]
    if has_prev:
        out_shapes += [jax.ShapeDtypeStruct((s, d), BF16), jax.ShapeDtypeStruct((1, d), F32)]
        out_specs += [row, vec]
    if has_next:
        out_shapes.append(jax.ShapeDtypeStruct((1, d), F32))
        out_specs.append(vec)
    outs = list(pl.pallas_call(body, name=name, grid=(s // tr,), in_specs=in_specs, out_specs=out_specs,
                               out_shape=out_shapes, compiler_params=_params(("arbitrary",)))(*ins))
    dx = outs.pop(0)
    do, dgp = (outs.pop(0), outs.pop(0)) if has_prev else (None, None)
    dgn = outs.pop(0) if has_next else None
    return dx, do, dgp, dgn


def _loss_head(y, target):
    s, d = y.shape
    tr = _tile(s, ROW_TILE)
    row = pl.BlockSpec((tr, d), lambda i: (i, 0))
    vec = pl.BlockSpec((1, d), lambda i: (0, 0))

    def body(y_ref, t_ref, dy_ref, l_ref):
        err = y_ref[...] - t_ref[...]
        dy_ref[...] = err * (1.0 / d)
        part = jnp.sum(err * err, axis=0, keepdims=True) * (0.5 / d)

        @pl.when(pl.program_id(0) == 0)
        def _():
            l_ref[...] = part

        @pl.when(pl.program_id(0) != 0)
        def _():
            l_ref[...] += part

    return pl.pallas_call(body, name="loss_head", grid=(s // tr,), in_specs=[row, row], out_specs=[row, vec],
                          out_shape=[jax.ShapeDtypeStruct((s, d), F32), jax.ShapeDtypeStruct((1, d), F32)],
                          compiler_params=_params(("arbitrary",)))(y, target)


@jax.custom_vjp
def _bdot(a, b):
    return jnp.dot(a.astype(BF16), b.astype(BF16), preferred_element_type=F32)


def _bdot_fwd(a, b):
    return _bdot(a, b), (a, b)


def _bdot_bwd(res, ct):
    a, b = res
    ctb = ct.astype(BF16)
    da = lax.dot_general(ctb, b.astype(BF16), NT, preferred_element_type=F32)
    db = lax.dot_general(a.astype(BF16), ctb, TN, preferred_element_type=F32)
    return da, db


_bdot.defvjp(_bdot_fwd, _bdot_bwd)


@functools.partial(jax.custom_vjp, nondiff_argnums=(1,))
def _shift(x, k):
    n = x.shape[0]
    if k == 0:
        return x
    rolled = pltpu.roll(x, k % n, 0)
    t = lax.broadcasted_iota(jnp.int32, x.shape, 0)
    keep = (t >= k) if k > 0 else (t < n + k)
    return jnp.where(keep, rolled, 0.0)


def _shift_fwd(x, k):
    return _shift(x, k), None


def _shift_bwd(k, _, ct):
    return (_shift(ct, -k),)


_shift.defvjp(_shift_fwd, _shift_bwd)


def _sigmoid(x):
    return 1.0 / (1.0 + jnp.exp(-x))


def _layer_norm(x, g, b=None):
    mu = jnp.mean(x, axis=-1, keepdims=True)
    xc = x - mu
    var = jnp.mean(xc * xc, axis=-1, keepdims=True)
    y = xc * lax.rsqrt(var + LN_EPS) * g
    return y if b is None else y + b


def _gmlp_chunk(zu, zv, gv, w, bcol):
    ch = w.shape[0]
    u = jax.nn.gelu(zu)
    vn = _layer_norm(jax.nn.gelu(zv), gv)
    t = lax.broadcasted_iota(jnp.int32, (ch, ch), 0)
    s = lax.broadcasted_iota(jnp.int32, (ch, ch), 1)
    wm = jnp.where(t >= s, w, 0.0)
    return u * (_bdot(wm, vn) + bcol)


def _gmlp_specs(seq, heads, hd, ch, u_off, v_off):
    col = lambda off: pl.BlockSpec((seq, hd), lambda h: (0, off + h))
    return (col(u_off), col(v_off), pl.BlockSpec((None, 1, hd), lambda h: (h, 0, 0)),
            pl.BlockSpec((None, ch, ch), lambda h: (h, 0, 0)), pl.BlockSpec((None, ch, 1), lambda h: (h, 0, 0)))


def _gmlp_fwd(z, gv, ws, bcol):
    seq = z.shape[0]
    heads, _, hd = gv.shape
    ch = ws.shape[-1]
    zu_s, zv_s, gv_s, w_s, b_s = _gmlp_specs(seq, heads, hd, ch, 0, heads)

    def body(zu_ref, zv_ref, gv_ref, w_ref, b_ref, y_ref):
        gvv, w, bc = gv_ref[...], w_ref[...], b_ref[...]

        def step(c, carry):
            rows = pl.ds(pl.multiple_of(c * ch, ch), ch)
            y_ref[rows, :] = _gmlp_chunk(zu_ref[rows, :], zv_ref[rows, :], gvv, w, bc).astype(BF16)
            return carry

        lax.fori_loop(0, seq // ch, step, 0)

    return pl.pallas_call(body, name="gmlp_fwd", grid=(heads,), in_specs=[zu_s, zv_s, gv_s, w_s, b_s],
                          out_specs=pl.BlockSpec((seq, hd), lambda h: (0, h)),
                          out_shape=jax.ShapeDtypeStruct((seq, heads * hd), BF16),
                          compiler_params=_params(("parallel",)))(z, z, gv, ws, bcol)


def _gmlp_bwd(z, dy, gv, ws, bcol):
    seq = z.shape[0]
    heads, _, hd = gv.shape
    ch = ws.shape[-1]
    zu_s, zv_s, gv_s, w_s, b_s = _gmlp_specs(seq, heads, hd, ch, 0, heads)
    col = pl.BlockSpec((seq, hd), lambda h: (0, h))

    def body(zu_ref, zv_ref, dy_ref, gv_ref, w_ref, b_ref, dzu_ref, dzv_ref, dgv_ref, dw_ref, db_ref):
        gvv, w, bc = gv_ref[...], w_ref[...], b_ref[...]

        def step(c, carry):
            dgv, dw, db = carry
            rows = pl.ds(pl.multiple_of(c * ch, ch), ch)
            _, vjp = jax.vjp(_gmlp_chunk, zu_ref[rows, :], zv_ref[rows, :], gvv, w, bc)
            dzu, dzv, dgv_c, dw_c, db_c = vjp(dy_ref[rows, :])
            dzu_ref[rows, :] = dzu.astype(BF16)
            dzv_ref[rows, :] = dzv.astype(BF16)
            return dgv + dgv_c, dw + dw_c, db + db_c

        zero = (jnp.zeros((1, hd), F32), jnp.zeros((ch, ch), F32), jnp.zeros((ch, 1), F32))
        dgv, dw, db = lax.fori_loop(0, seq // ch, step, zero)
        dgv_ref[...] = dgv
        dw_ref[...] = dw
        db_ref[...] = db

    return pl.pallas_call(
        body, name="gmlp_bwd", grid=(heads,), in_specs=[zu_s, zv_s, col, gv_s, w_s, b_s],
        out_specs=[col, col, gv_s, w_s, b_s],
        out_shape=[jax.ShapeDtypeStruct((seq, heads * hd), BF16), jax.ShapeDtypeStruct((seq, heads * hd), BF16),
                   jax.ShapeDtypeStruct(gv.shape, F32), jax.ShapeDtypeStruct(ws.shape, F32),
                   jax.ShapeDtypeStruct(bcol.shape, F32)],
        compiler_params=_params(("parallel",)))(z, z, dy, gv, ws, bcol)


def _pool_group(p, w, s, window):
    win, span = p, 1
    while span < window:
        win = win + _shift(win, span)
        span *= 2
    t = lax.broadcasted_iota(jnp.int32, (p.shape[0], 1), 0).astype(F32)
    cnt = jnp.minimum(t + 1.0, float(window))
    return _bdot(win / cnt - p, w) * s


def _pool_fwd(z, w_pool, s_pool, col_block):
    seq = z.shape[0]
    groups, gw, _ = w_pool.shape
    pw = groups * gw

    def body(p_ref, w_ref, s_ref, y_ref):
        for g in range(groups):
            cols = slice(g * gw, (g + 1) * gw)
            y_ref[:, cols] = _pool_group(p_ref[:, cols], w_ref[g], s_ref[:, cols], POOL_WINDOWS[g]).astype(BF16)

    return pl.pallas_call(
        body, name="pool_fwd", grid=(1,),
        in_specs=[pl.BlockSpec((seq, pw), lambda i: (0, col_block)),
                  pl.BlockSpec((groups, gw, gw), lambda i: (0, 0, 0)), pl.BlockSpec((1, pw), lambda i: (0, 0))],
        out_specs=pl.BlockSpec((seq, pw), lambda i: (0, 0)), out_shape=jax.ShapeDtypeStruct((seq, pw), BF16),
        compiler_params=_params(("arbitrary",)))(z, w_pool, s_pool)


def _pool_bwd(z, dy, w_pool, s_pool, col_block, dy_block):
    seq = z.shape[0]
    groups, gw, _ = w_pool.shape
    pw = groups * gw

    def body(p_ref, dy_ref, w_ref, s_ref, dp_ref, dw_ref, ds_ref):
        for g in range(groups):
            cols = slice(g * gw, (g + 1) * gw)
            _, vjp = jax.vjp(functools.partial(_pool_group, window=POOL_WINDOWS[g]), p_ref[:, cols], w_ref[g],
                             s_ref[:, cols])
            dp, dw, ds = vjp(dy_ref[:, cols])
            dp_ref[:, cols] = dp.astype(BF16)
            dw_ref[g] = dw
            ds_ref[:, cols] = ds

    return pl.pallas_call(
        body, name="pool_bwd", grid=(1,),
        in_specs=[pl.BlockSpec((seq, pw), lambda i: (0, col_block)),
                  pl.BlockSpec((seq, pw), lambda i: (0, dy_block)),
                  pl.BlockSpec((groups, gw, gw), lambda i: (0, 0, 0)), pl.BlockSpec((1, pw), lambda i: (0, 0))],
        out_specs=[pl.BlockSpec((seq, pw), lambda i: (0, 0)), pl.BlockSpec((groups, gw, gw), lambda i: (0, 0, 0)),
                   pl.BlockSpec((1, pw), lambda i: (0, 0))],
        out_shape=[jax.ShapeDtypeStruct((seq, pw), BF16), jax.ShapeDtypeStruct(w_pool.shape, F32),
                   jax.ShapeDtypeStruct((1, pw), F32)],
        compiler_params=_params(("arbitrary",)))(z, dy, w_pool, s_pool)


def _conv_fwd(z, w_dw, taps, b_dw, val_block, gate_block):
    seq = z.shape[0]
    rows, cb = w_dw.shape[1], w_dw.shape[2]

    def body(val_ref, gate_ref, w_ref, b_ref, out_ref):
        h = val_ref[...] * _sigmoid(gate_ref[...])
        acc = jnp.broadcast_to(b_ref[...], h.shape)
        for d in range(taps):
            acc = acc + w_ref[pl.ds(taps - 1 - d, 1), :] * _shift(h, d)
        out_ref[...] = acc

    return pl.pallas_call(
        body, name="conv_fwd", grid=(N_CHIPS,),
        in_specs=[pl.BlockSpec((seq, cb), lambda j: (0, val_block + j)),
                  pl.BlockSpec((seq, cb), lambda j: (0, gate_block + j)),
                  pl.BlockSpec((None, rows, cb), lambda j: (j, 0, 0)),
                  pl.BlockSpec((1, cb), lambda j: (0, j))],
        out_specs=pl.BlockSpec((seq, cb), lambda j: (0, j)),
        out_shape=jax.ShapeDtypeStruct((seq, N_CHIPS * cb), F32),
        compiler_params=_params(("parallel",)))(z, z, w_dw, b_dw)


def _conv_bwd(z, dout, w_dw, taps, val_block, gate_block):
    seq = z.shape[0]
    rows, cb = w_dw.shape[1], w_dw.shape[2]
    col = pl.BlockSpec((seq, cb), lambda j: (0, j))

    def body(val_ref, gate_ref, do_ref, w_ref, dval_ref, dgate_ref, dw_ref, db_ref):
        val, sg, do = val_ref[...], _sigmoid(gate_ref[...]), do_ref[...]
        h = val * sg
        db_ref[...] = jnp.sum(do, axis=0, keepdims=True)
        dh = jnp.zeros_like(h)
        for d in range(taps):
            k = taps - 1 - d
            dw_ref[pl.ds(k, 1), :] = jnp.sum(do * _shift(h, d), axis=0, keepdims=True)
            dh = dh + w_ref[pl.ds(k, 1), :] * _shift(do, -d)
        dval_ref[...] = (dh * sg).astype(BF16)
        dgate_ref[...] = (dh * val * sg * (1.0 - sg)).astype(BF16)

    return pl.pallas_call(
        body, name="conv_bwd", grid=(N_CHIPS,),
        in_specs=[pl.BlockSpec((seq, cb), lambda j: (0, val_block + j)),
                  pl.BlockSpec((seq, cb), lambda j: (0, gate_block + j)), col,
                  pl.BlockSpec((None, rows, cb), lambda j: (j, 0, 0))],
        out_specs=[col, col, pl.BlockSpec((taps, cb), lambda j: (0, j)), pl.BlockSpec((1, cb), lambda j: (0, j))],
        out_shape=[jax.ShapeDtypeStruct((seq, N_CHIPS * cb), BF16), jax.ShapeDtypeStruct((seq, N_CHIPS * cb), BF16),
                   jax.ShapeDtypeStruct((taps, N_CHIPS * cb), F32), jax.ShapeDtypeStruct((1, N_CHIPS * cb), F32)],
        compiler_params=_params(("parallel",)))(z, z, dout, w_dw)


def _ln_swish(hc, g, b):
    y = _layer_norm(hc, g, b)
    return y * _sigmoid(y)


def _ln_swish_fwd(hc, g, b):
    s, cw = hc.shape
    tr = _tile(s, ROW_TILE)
    row = pl.BlockSpec((tr, cw), lambda i: (i, 0))
    vec = pl.BlockSpec((1, cw), lambda i: (0, 0))

    def body(h_ref, g_ref, b_ref, y_ref):
        y_ref[...] = _ln_swish(h_ref[...], g_ref[...], b_ref[...]).astype(BF16)

    return pl.pallas_call(body, name="ln_swish_fwd", grid=(s // tr,), in_specs=[row, vec, vec], out_specs=row,
                          out_shape=jax.ShapeDtypeStruct((s, cw), BF16),
                          compiler_params=_params(("parallel",)))(hc, g, b)


def _ln_swish_bwd(hc, dy, g, b, dy_block):
    s, cw = hc.shape
    tr = _tile(s, ROW_TILE)
    row = pl.BlockSpec((tr, cw), lambda i: (i, 0))
    vec = pl.BlockSpec((1, cw), lambda i: (0, 0))

    def body(h_ref, dy_ref, g_ref, b_ref, dh_ref, dg_ref, db_ref):
        _, vjp = jax.vjp(_ln_swish, h_ref[...], g_ref[...], b_ref[...])
        dh, dg, db = vjp(dy_ref[...])
        dh_ref[...] = dh

        @pl.when(pl.program_id(0) == 0)
        def _():
            dg_ref[...] = dg
            db_ref[...] = db

        @pl.when(pl.program_id(0) != 0)
        def _():
            dg_ref[...] += dg
            db_ref[...] += db

    return pl.pallas_call(
        body, name="ln_swish_bwd", grid=(s // tr,),
        in_specs=[row, pl.BlockSpec((tr, cw), lambda i: (i, dy_block)), vec, vec], out_specs=[row, vec, vec],
        out_shape=[jax.ShapeDtypeStruct((s, cw), F32), jax.ShapeDtypeStruct((1, cw), F32),
                   jax.ShapeDtypeStruct((1, cw), F32)],
        compiler_params=_params(("arbitrary",)))(hc, dy, g, b)


def _attn_probs(q, k, scale):
    s = lax.dot_general(q, k, NT, preferred_element_type=F32) * scale
    e = jnp.exp(s - jnp.max(s, axis=-1, keepdims=True))
    return e / jnp.sum(e, axis=-1, keepdims=True)


def _attn_fwd(q, k, v):
    seq, d = q.shape
    mem = k.shape[0]
    hd = d // XATTN_HEADS
    scale = hd ** -0.5
    qs = pl.BlockSpec((seq, hd), lambda h: (0, h))
    ms = pl.BlockSpec((mem, hd), lambda h: (0, h))

    def body(q_ref, k_ref, v_ref, a_ref):
        p = _attn_probs(q_ref[...], k_ref[...], scale)
        a_ref[...] = jnp.dot(p.astype(BF16), v_ref[...], preferred_element_type=F32).astype(BF16)

    return pl.pallas_call(body, name="attn_fwd", grid=(XATTN_HEADS,), in_specs=[qs, ms, ms], out_specs=qs,
                          out_shape=jax.ShapeDtypeStruct((seq, d), BF16),
                          compiler_params=_params(("parallel",)))(q, k, v)


def _attn_bwd(q, k, v, da):
    seq, d = q.shape
    mem = k.shape[0]
    hd = d // XATTN_HEADS
    scale = hd ** -0.5
    qs = pl.BlockSpec((seq, hd), lambda h: (0, h))
    ms = pl.BlockSpec((mem, hd), lambda h: (0, h))

    def body(q_ref, k_ref, v_ref, da_ref, dq_ref, dk_ref, dv_ref):
        q_, k_, v_, da_ = q_ref[...], k_ref[...], v_ref[...], da_ref[...]
        p = _attn_probs(q_, k_, scale)
        dv_ref[...] = lax.dot_general(p.astype(BF16), da_, TN, preferred_element_type=F32).astype(BF16)
        dp = lax.dot_general(da_, v_, NT, preferred_element_type=F32)
        ds = (p * (dp - jnp.sum(dp * p, axis=-1, keepdims=True)) * scale).astype(BF16)
        dq_ref[...] = jnp.dot(ds, k_, preferred_element_type=F32).astype(BF16)
        dk_ref[...] = lax.dot_general(ds, q_, TN, preferred_element_type=F32).astype(BF16)

    return pl.pallas_call(
        body, name="attn_bwd", grid=(XATTN_HEADS,), in_specs=[qs, ms, ms, qs], out_specs=[qs, ms, ms],
        out_shape=[jax.ShapeDtypeStruct((seq, d), BF16), jax.ShapeDtypeStruct((mem, d), BF16),
                   jax.ShapeDtypeStruct((mem, d), BF16)],
        compiler_params=_params(("parallel",)))(q, k, v, da)


def _place_shard(name, place, w, l, dtype, after=()):
    _, r, c = w.shape
    tr = _tile(r, 2 * ROW_TILE) if r % 16 == 0 else r
    after = list(after)

    def body(place_ref, w_ref, *refs):
        refs[-1][...] = w_ref[...].astype(dtype)

    return pl.pallas_call(
        body, name=name,
        grid_spec=pltpu.PrefetchScalarGridSpec(
            num_scalar_prefetch=1, grid=(r // tr,),
            in_specs=[pl.BlockSpec((None, tr, c), lambda i, p: (l, i, 0))] + [ANY] * len(after),
            out_specs=pl.BlockSpec((None, tr, c), lambda i, p: (p[1], i, 0))),
        out_shape=jax.ShapeDtypeStruct((N_CHIPS, r, c), dtype),
        compiler_params=_params(("parallel",)))(place, w, *after)


def _place_flat(place, flat):
    rows, lanes = flat.shape

    def body(place_ref, f_ref, o_ref):
        o_ref[...] = f_ref[...]

    return pl.pallas_call(
        body, name="place_flat",
        grid_spec=pltpu.PrefetchScalarGridSpec(
            num_scalar_prefetch=1, grid=(1,), in_specs=[pl.BlockSpec((rows, lanes), lambda i, p: (0, 0))],
            out_specs=pl.BlockSpec((None, rows, lanes), lambda i, p: (p[2], 0, 0))),
        out_shape=jax.ShapeDtypeStruct((N_DEVICES, rows, lanes), flat.dtype),
        compiler_params=_params(("arbitrary",)))(place, flat)


def _pair_sum(name, place, dw, got):
    n, _, r2, c = dw.shape
    tr = _tile(r2, 2 * ROW_TILE)

    def body(place_ref, own_ref, got_ref, s_ref, t_ref):
        val = (own_ref[...].astype(F32) + got_ref[...].astype(F32)).astype(BF16)
        s_ref[...] = val

        @pl.when(pl.program_id(1) == place_ref[1])
        def _():
            t_ref[...] = val

    slab = pl.BlockSpec((None, tr, c), lambda i, j, p: (j, i, 0))
    sds = jax.ShapeDtypeStruct((n, r2, c), BF16)
    return pl.pallas_call(
        body, name=name,
        grid_spec=pltpu.PrefetchScalarGridSpec(
            num_scalar_prefetch=1, grid=(r2 // tr, n),
            in_specs=[pl.BlockSpec((None, None, tr, c), lambda i, j, p: (j, p[0], i, 0)), slab],
            out_specs=[slab, pl.BlockSpec((None, tr, c), lambda i, j, p: (p[1], i, 0))]),
        out_shape=[sds, sds], compiler_params=_params(("parallel", "arbitrary")))(place, dw, got)


def _chip_sum(name, place, parts):
    n, r2, c = parts.shape
    tr = _tile(r2, ROW_TILE)

    def body(place_ref, *refs):
        o_ref = refs[n]
        acc = refs[0][...].astype(F32)
        for j in range(1, n):
            acc = acc + refs[j][...].astype(F32)
        o_ref[...] = acc

    part = lambda j: pl.BlockSpec((None, tr, c), lambda i, p: (j, i, 0))
    return pl.pallas_call(
        body, name=name,
        grid_spec=pltpu.PrefetchScalarGridSpec(
            num_scalar_prefetch=1, grid=(r2 // tr,), in_specs=[part(j) for j in range(n)],
            out_specs=pl.BlockSpec((None, tr, c), lambda i, p: (p[0], i, 0))),
        out_shape=jax.ShapeDtypeStruct((2, r2, c), F32),
        compiler_params=_params(("parallel",)))(place, *([parts] * n))


def _device_sum(parts):
    n, rows, lanes = parts.shape
    tr = _tile(rows, 4 * ROW_TILE) if rows % 8 == 0 else rows

    def body(p_ref, o_ref):
        acc = p_ref[0]
        for j in range(1, n):
            acc = acc + p_ref[j]
        o_ref[...] = acc

    return pl.pallas_call(
        body, name="device_sum", grid=(rows // tr,), in_specs=[pl.BlockSpec((n, tr, lanes), lambda i: (0, i, 0))],
        out_specs=pl.BlockSpec((tr, lanes), lambda i: (i, 0)), out_shape=jax.ShapeDtypeStruct((rows, lanes), F32),
        compiler_params=_params(("parallel",)))(parts)


def _adam_update(w, g, m, v):
    nm = ADAM_B1 * m + (1.0 - ADAM_B1) * g
    nv = ADAM_B2 * v + (1.0 - ADAM_B2) * (g * g)
    c1 = 1.0 - ADAM_B1 ** ADAM_STEP
    c2 = 1.0 - ADAM_B2 ** ADAM_STEP
    return -ADAM_LR * ((nm / c1) / (jnp.sqrt(nv / c2) + ADAM_EPS) + ADAM_WD * w), nm, nv


def _adamw_layer(name, l, w, g, m, v, prev):
    n_l, r, c = w.shape
    tr = _tile(r, ROW_TILE)
    slab = pl.BlockSpec((None, tr, c), lambda i: (l, i, 0))

    def body(w_ref, g_ref, m_ref, v_ref, *refs):
        go_ref, d_ref, nm_ref, nv_ref = refs[-4:]
        g_ = g_ref[...]
        delta, nm, nv = _adam_update(w_ref[...], g_, m_ref[...], v_ref[...])
        go_ref[...] = g_
        d_ref[...] = delta
        nm_ref[...] = nm
        nv_ref[...] = nv

    ins = [w, g, m, v]
    in_specs = [slab, pl.BlockSpec((tr, c), lambda i: (i, 0)), slab, slab]
    aliases = {}
    if prev is not None:
        aliases = {len(ins) + i: i for i in range(4)}
        ins += list(prev)
        in_specs += [ANY] * 4
    sds = jax.ShapeDtypeStruct((n_l, r, c), F32)
    return pl.pallas_call(body, name=name, grid=(r // tr,), in_specs=in_specs, out_specs=[slab] * 4,
                          out_shape=[sds] * 4, input_output_aliases=aliases,
                          compiler_params=_params(("parallel",)))(*ins)


def _adamw_flat(name, w, g, m, v):
    rows, cols = w.shape
    tr = _tile(rows, ROW_TILE) if rows % 8 == 0 else rows
    spec = pl.BlockSpec((tr, cols), lambda i: (i, 0))

    def body(w_ref, g_ref, m_ref, v_ref, d_ref, nm_ref, nv_ref):
        d_ref[...], nm_ref[...], nv_ref[...] = _adam_update(w_ref[...], g_ref[...], m_ref[...], v_ref[...])

    sds = jax.ShapeDtypeStruct((rows, cols), F32)
    return pl.pallas_call(body, name=name, grid=(rows // tr,), in_specs=[spec] * 4, out_specs=[spec] * 3,
                          out_shape=[sds] * 3, compiler_params=_params(("parallel",)))(w, g, m, v)


def _me():
    return lax.axis_index("x"), lax.axis_index("y"), lax.axis_index("c")


def _other_chips(x, y):
    return [(1 - x, y, 2 * (1 - x) + y), (x, 1 - y, 2 * x + 1 - y), (1 - x, 1 - y, 2 * (1 - x) + 1 - y)]


def _remote(src, dst, send_sem, recv_sem, target):
    return pltpu.make_async_remote_copy(src_ref=src, dst_ref=dst, send_sem=send_sem, recv_sem=recv_sem,
                                        device_id=target, device_id_type=MESH)


def _exchange(name, bufs, plan, n_copies):
    n = len(bufs)

    def body(*refs):
        send_sems, recv_sems = refs[2 * n:]
        copies = []
        for i, (src, dst, target) in enumerate(plan(refs[n:2 * n], _me())):
            if target is None:
                cp = pltpu.make_async_copy(src, dst, send_sems.at[i])
            else:
                cp = _remote(src, dst, send_sems.at[i], recv_sems.at[i], target)
            cp.start()
            copies.append((cp, target))
        assert len(copies) == n_copies
        for cp, target in copies:
            if target is None:
                cp.wait()
            else:
                cp.wait_recv()
        for cp, target in copies:
            if target is not None:
                cp.wait_send()

    return pl.pallas_call(
        body, name=name, in_specs=[ANY] * n, out_specs=[ANY] * n,
        out_shape=[jax.ShapeDtypeStruct(b.shape, b.dtype) for b in bufs],
        scratch_shapes=[pltpu.SemaphoreType.DMA((n_copies,)), pltpu.SemaphoreType.DMA((n_copies,))],
        input_output_aliases={i: i for i in range(n)},
        compiler_params=pltpu.CompilerParams(has_side_effects=True))(*bufs)


def _start_copies(name, groups, after=()):
    all_bufs = [b for bufs, _, _ in groups for b in bufs]
    after = list(after)
    n = len(all_bufs)
    n_g = len(groups)

    def body(*refs):
        in_refs, sem_refs = refs[:n], refs[n + len(after):n + len(after) + 2 * n_g]
        pos = 0
        for g, (bufs, plan, n_copies) in enumerate(groups):
            copies = plan(in_refs[pos:pos + len(bufs)], _me())
            assert len(copies) == n_copies
            for i, (src, dst, target) in enumerate(copies):
                _remote(src, dst, sem_refs[2 * g].at[i], sem_refs[2 * g + 1].at[i], target).start()
            pos += len(bufs)

    sems = []
    for _, _, n_copies in groups:
        sems += [pltpu.SemaphoreType.DMA((n_copies,))] * 2
    outs = pl.pallas_call(
        body, name=name, in_specs=[HBM] * n + [ANY] * len(after), out_specs=[SEM] * (2 * n_g) + [HBM] * n,
        out_shape=sems + [pltpu.HBM(b.shape, b.dtype) for b in all_bufs],
        input_output_aliases={i: 2 * n_g + i for i in range(n)},
        compiler_params=pltpu.CompilerParams(has_side_effects=pltpu.SideEffectType.DATAFLOW_SIDE_EFFECTING))(
            *[pltpu.with_memory_space_constraint(b, pltpu.HBM) for b in all_bufs], *after)
    result, pos = [], 2 * n_g
    for g, (bufs, _, _) in enumerate(groups):
        result.append((outs[2 * g], outs[2 * g + 1], list(outs[pos:pos + len(bufs)])))
        pos += len(bufs)
    return result


def _wait_copies(name, started, plan, n_copies, after):
    send_sems, recv_sems, bufs = started
    n = len(bufs)
    after = list(after) if isinstance(after, (list, tuple)) else [after]
    after = [a for a in after if all(a is not b for b in bufs)]

    def body(*refs):
        copies = plan(refs[:n], _me())
        assert len(copies) == n_copies
        for i, (src, dst, target) in enumerate(copies):
            cp = _remote(src, dst, refs[n].at[i], refs[n + 1].at[i], target)
            cp.wait_send()
            cp.wait_recv()

    return list(pl.pallas_call(
        body, name=name, in_specs=[HBM] * n + [SEM, SEM] + [ANY] * len(after), out_specs=[HBM] * n,
        out_shape=[pltpu.HBM(b.shape, b.dtype) for b in bufs], input_output_aliases={i: i for i in range(n)},
        compiler_params=pltpu.CompilerParams(has_side_effects=pltpu.SideEffectType.DATAFLOW_SIDE_EFFECTING))(
            *bufs, send_sems, recv_sems, *after))


def _halves(a):
    return a.reshape(a.shape[0], 2, a.shape[1] // 2, a.shape[2])


def _gather_plan(refs, me):
    x, y, c = me
    mine = 2 * x + y
    return [(g.at[mine, c], g.at[mine, c], (px, py, c)) for g in refs for px, py, _ in _other_chips(x, y)]


def _forward_plan(refs, me):
    x, y, c = me
    return [(g.at[chip, c], g.at[chip, c], (x, y, 1 - c)) for g in refs for _, _, chip in _other_chips(x, y)]


def _swap_plan(refs, me):
    x, y, c = me
    k = len(refs) // 2
    return [(refs[i].at[j, 1 - c], refs[k + i].at[j], (x, y, 1 - c)) for i in range(k) for j in range(N_CHIPS)]


def _scatter_plan(refs, me):
    x, y, c = me
    mine = 2 * x + y
    k = len(refs) // 2
    return [(refs[i].at[chip], refs[k + i].at[mine], (px, py, c))
            for i in range(k) for px, py, chip in _other_chips(x, y)]


def _share_plan(refs, me):
    x, y, c = me
    return [(g.at[c], g.at[c], (x, y, 1 - c)) for g in refs]


def _broadcast_plan(refs, me):
    x, y, c = me
    mine = 4 * x + 2 * y + c
    copies = []
    for fx, fy, fc in [(0, 0, 1), (0, 1, 0), (0, 1, 1), (1, 0, 0), (1, 0, 1), (1, 1, 0), (1, 1, 1)]:
        peer = (x + fx - 2 * fx * x, y + fy - 2 * fy * y, c + fc - 2 * fc * c)
        copies.append((refs[0].at[mine], refs[0].at[mine], peer))
    return copies


BIG = ("w_in", "w_out", "w_q", "w_k", "w_v", "w_o", "w_up", "w_down")
COLUMN_SPLIT = ("w_in", "w_up")
WEIGHTS = ("norm_mix_pre", "norm_mix_post", "w_in", "w_out", "gmlp_v_gain", "w_spatial", "b_spatial", "w_pool",
           "s_pool", "w_dw", "b_dw", "conv_ln_g", "conv_ln_b", "norm_xattn_pre", "norm_mem", "norm_xattn_post",
           "w_q", "w_k", "w_v", "w_o", "norm_ffn_pre", "norm_ffn_post", "w_up", "w_down")
SMALL = tuple(n for n in WEIGHTS if n not in BIG)
GATHER_GROUPS = (("w_in", "w_dw"), ("w_out", "w_q", "w_k", "w_v", "w_o"), ("w_up",), ("w_down",))


def _relu2(acc):
    r = jnp.maximum(acc, 0.0)
    return acc, r * r


def _relu2_bwd(acc, up):
    return (acc * (2.0 * jnp.maximum(up, 0.0)),)


def _pack(arrays):
    flat = jnp.concatenate([a.reshape(-1) for a in arrays])
    tile = 8 * V7X_LANES
    pad = (-flat.shape[0]) % tile
    return jnp.pad(flat, (0, pad)).reshape(-1, V7X_LANES)


def _unpack(packed, like):
    flat = packed.reshape(-1)
    out, pos = [], 0
    for a in like:
        out.append(flat[pos:pos + a.size].reshape(a.shape))
        pos += a.size
    return out


class _GradientReducer:
    def __init__(self, place, w, m, v):
        self.place, self.w, self.m, self.v = place, w, m, v
        self.flying = []
        self.done = {n: None for n in BIG}

    def add(self, tag, l, grads):
        names = list(grads)
        views = [_halves(grads[n]) for n in names]
        zones = [lax.empty((v.shape[0],) + v.shape[2:], v.dtype) for v in views]
        started = _start_copies("swap_start_" + tag, [(views + zones, _swap_plan, N_CHIPS * len(names))])[0]
        self.flying.append(dict(stage=0, tag=tag, l=l, names=names, started=started))
        return [started[2][0]]

    def advance(self, after):
        made = []
        after = list(after) if isinstance(after, (list, tuple)) else [after]
        for item in self.flying:
            item["stage"] += 1
        for item in self.flying:
            tag, names, k = item["tag"], item["names"], len(item["names"])
            if item["stage"] == 1:
                bufs = _wait_copies("swap_wait_" + tag, item["started"], _swap_plan, N_CHIPS * k, after)
                sums, parts = zip(*[_pair_sum("pair_sum_" + n, self.place, dv, got)
                                    for n, dv, got in zip(names, bufs[:k], bufs[k:])])
                item["started"] = _start_copies("scatter_start_" + tag,
                                                [(list(sums) + list(parts), _scatter_plan, 3 * k)])[0]
                made.append(item["started"][2][0])
        after = after + made
        for item in list(self.flying):
            tag, names, k = item["tag"], item["names"], len(item["names"])
            if item["stage"] == 3:
                bufs = _wait_copies("scatter_wait_" + tag, item["started"], _scatter_plan, 3 * k, after)
                halves = [_chip_sum("chip_sum_" + n, self.place, p) for n, p in zip(names, bufs[k:])]
                item["started"] = _start_copies("share_start_" + tag, [(halves, _share_plan, k)])[0]
                made.append(item["started"][2][0])
            elif item["stage"] == 4:
                halves = _wait_copies("share_wait_" + tag, item["started"], _share_plan, k, after)
                for n, h in zip(names, halves):
                    g = h.reshape(self.w[n].shape[1:])
                    self.done[n] = _adamw_layer("adamw_" + n, item["l"], self.w[n], g, self.m[n], self.v[n],
                                                self.done[n])
                    made.append(self.done[n][3])
                self.flying.remove(item)
        return made

    def drain(self, after):
        made = list(after)
        while self.flying:
            made = list(after) + self.advance(made)
        return made


def _step(x, mem, target, w, m, v):
    n_layers = w["w_in"].shape[0]
    seq, d = x.shape
    heads, hd = w["gmlp_v_gain"].shape[1:]
    gw = heads * hd
    groups, pgw = w["w_pool"].shape[1:3]
    pw = groups * pgw
    cw = w["b_dw"].shape[1]
    cb = cw // N_CHIPS
    taps = w["w_dw"].shape[1]
    cx, cy, cc = _me()
    chip = 2 * cx + cy
    place = jnp.stack([cc, chip, 2 * chip + cc]).astype(jnp.int32)
    vec = lambda name, l: w[name][l].reshape(1, -1)

    taps_padded = jnp.pad(w["w_dw"], ((0, 0), (0, (-taps) % 16), (0, 0)))
    gathering = []

    def send_layer(l, last):
        for g, names in enumerate(GATHER_GROUPS):
            bufs = [_halves(_place_shard("place_" + n, place, taps_padded if n == "w_dw" else w[n], l,
                                         F32 if n == "w_dw" else BF16, after=last)) for n in names]
            gathering.extend(_start_copies("gather_start_%d%d" % (l, g), [(bufs, _gather_plan, 3 * len(names))],
                                           after=last))
            last = [gathering[-1][2][0]]
        return last

    def arrive(l, g, after):
        names = GATHER_GROUPS[g]
        tag = "%d%d" % (l, g)
        bufs = _wait_copies("gather_wait_" + tag, gathering[l * len(GATHER_GROUPS) + g], _gather_plan,
                            3 * len(names), after)
        bufs = _exchange("gather_forward_" + tag, bufs, _forward_plan, 3 * len(names))
        out = {}
        for n, b in zip(names, bufs):
            full = b.reshape(N_CHIPS, 2 * b.shape[2], b.shape[3])
            out[n] = full if n in COLUMN_SPLIT + ("w_dw",) else full.reshape(-1, full.shape[2])
        return out

    saved = []
    sent = send_layer(0, [])
    _, h1 = _norm_fwd("norm_first", x, None, None, vec("norm_mix_pre", 0))
    for l in range(n_layers):
        gv = w["gmlp_v_gain"][l].reshape(heads, 1, hd)
        ws = w["w_spatial"][l]
        bcol = w["b_spatial"][l].reshape(heads, -1, 1)
        wl = arrive(l, 0, [h1] + sent)
        z = _mm_nn_col("mm_in", h1, wl["w_in"], [F32])[0]
        sent = send_layer(l + 1, [z]) if l + 1 < n_layers else []
        ya = _gmlp_fwd(z, gv, ws, bcol)
        yb = _pool_fwd(z, w["w_pool"][l], vec("s_pool", l), (2 * gw) // pw)
        hc = _conv_fwd(z, wl["w_dw"], taps, vec("b_dw", l), (2 * gw + pw) // cb, (2 * gw + pw + cw) // cb)
        yc = _ln_swish_fwd(hc, vec("conv_ln_g", l), vec("conv_ln_b", l))
        y = jnp.concatenate([ya, yb, yc], axis=1)
        wl.update(arrive(l, 1, [y] + sent))
        sent = []
        o = _mm_nn_row("mm_out", y, wl["w_out"], [F32])[0]
        x1, h2 = _norm_fwd("norm_mix", x, o, vec("norm_mix_post", l), vec("norm_xattn_pre", l))
        _, mn = _norm_fwd("norm_mem", mem, None, None, vec("norm_mem", l))
        q = _mm_nn_row("mm_q", h2, wl["w_q"], [BF16])[0]
        k = _mm_nn_row("mm_k", mn, wl["w_k"], [BF16])[0]
        vv = _mm_nn_row("mm_v", mn, wl["w_v"], [BF16])[0]
        a = _attn_fwd(q, k, vv)
        o2 = _mm_nn_row("mm_o", a, wl["w_o"], [F32])[0]
        x2, h3 = _norm_fwd("norm_xattn", x1, o2, vec("norm_xattn_post", l), vec("norm_ffn_pre", l))
        wl.update(arrive(l, 2, h3))
        up, r = _mm_nn_col("mm_up", h3, wl["w_up"], [F32, BF16], epi=_relu2)
        wl.update(arrive(l, 3, r))
        o3 = _mm_nn_row("mm_down", r, wl["w_down"], [F32])[0]
        g_next = vec("norm_mix_pre", l + 1) if l + 1 < n_layers else None
        x3, h_next = _norm_fwd("norm_ffn", x2, o3, vec("norm_ffn_post", l), g_next)
        saved.append(dict(x=x, h1=h1, z=z, hc=hc, y=y, o=o, x1=x1, h2=h2, mn=mn, q=q, k=k, v=vv, a=a, o2=o2, x2=x2,
                          h3=h3, up=up, r=r, o3=o3, x3=x3, gv=gv, ws=ws, bcol=bcol, w=wl))
        x, h1 = x3, h_next

    dx, loss_parts = _loss_head(x, target)
    loss = lax.psum(jnp.sum(loss_parts), ("x", "y", "c"))

    reducer = _GradientReducer(place, w, m, v)
    small = {n: [None] * n_layers for n in SMALL}
    by_chip = lambda g: g.reshape(N_CHIPS, g.shape[0] // N_CHIPS, g.shape[1])
    small_sent = [None] * n_layers

    def small_layer(l):
        return [small[n][l] if n == "w_dw" else small[n][l].reshape(w[n].shape[1:]) for n in SMALL]

    def send_small(l):
        landing = _place_flat(place, _pack(small_layer(l)))
        small_sent[l] = _start_copies("small_start_%d" % l, [([landing], _broadcast_plan, N_DEVICES - 1)])[0]
        return [small_sent[l][2][0]]

    dh = None
    made = []
    for l in reversed(range(n_layers)):
        t = saved[l]
        wl = t["w"]
        g_next = vec("norm_mix_pre", l + 1) if l + 1 < n_layers else None
        dx, do3, dgp, dgn = _norm_bwd("norm_ffn_bwd", dx, dh, t["x3"], t["o3"], vec("norm_ffn_post", l), g_next,
                                      after=made)
        small["norm_ffn_post"][l] = dgp
        if dgn is not None:
            small["norm_mix_pre"][l + 1] = dgn
        made = reducer.advance(dx)
        if l + 1 < n_layers:
            made += send_small(l + 1)
        d_down = _mm_tn_row("mm_down_dw", t["r"], do3)
        made += reducer.add("%d0" % l, l, {"w_down": by_chip(d_down)})
        dup = _mm_nt_row("mm_down_dx", do3, wl["w_down"], BF16, epi=_relu2_bwd, extra=t["up"], after=made)
        d_up = _mm_tn_col("mm_up_dw", t["h3"], dup)
        made = reducer.add("%d3" % l, l, {"w_up": d_up})
        dh3 = _mm_nt_col("mm_up_dx", dup, wl["w_up"], F32, after=made)
        made = []
        dx, do2, dgp, dgn = _norm_bwd("norm_xattn_bwd", dx, dh3, t["x2"], t["o2"], vec("norm_xattn_post", l),
                                      vec("norm_ffn_pre", l), after=made)
        small["norm_xattn_post"][l], small["norm_ffn_pre"][l] = dgp, dgn
        made = reducer.advance(dx)
        d_o = _mm_tn_row("mm_o_dw", t["a"], do2)
        da = _mm_nt_row("mm_o_dx", do2, wl["w_o"], BF16, after=made)
        dq, dk, dv = _attn_bwd(t["q"], t["k"], t["v"], da)
        d_q = _mm_tn_row("mm_q_dw", t["h2"], dq)
        d_k = _mm_tn_row("mm_k_dw", t["mn"], dk)
        d_v = _mm_tn_row("mm_v_dw", t["mn"], dv)
        dh2 = _mm_nt_row("mm_q_dx", dq, wl["w_q"], F32)
        dmn = _mm_nt_row("mm_k_dx", dk, wl["w_k"], F32) + _mm_nt_row("mm_v_dx", dv, wl["w_v"], F32)
        _, _, _, small["norm_mem"][l] = _norm_bwd("norm_mem_bwd", None, dmn, mem, None, None, vec("norm_mem", l))
        dx, do, dgp, dgn = _norm_bwd("norm_mix_bwd", dx, dh2, t["x1"], t["o"], vec("norm_mix_post", l),
                                     vec("norm_xattn_pre", l))
        small["norm_mix_post"][l], small["norm_xattn_pre"][l] = dgp, dgn
        made = reducer.advance(dx)
        d_out = _mm_tn_row("mm_out_dw", t["y"], do)
        made += reducer.add("%d1" % l, l, {"w_o": by_chip(d_o), "w_q": by_chip(d_q), "w_k": by_chip(d_k),
                                    "w_v": by_chip(d_v), "w_out": by_chip(d_out)})
        dy = _mm_nt_row("mm_out_dx", do, wl["w_out"], F32, after=made)
        dzu, dzv, dgv, dws, dbcol = _gmlp_bwd(t["z"], dy, t["gv"], t["ws"], t["bcol"])
        small["gmlp_v_gain"][l] = dgv.reshape(heads, hd)
        small["w_spatial"][l] = dws
        small["b_spatial"][l] = dbcol.reshape(heads, -1)
        dzp, dwp, dsp = _pool_bwd(t["z"], dy, w["w_pool"][l], vec("s_pool", l), (2 * gw) // pw, gw // pw)
        small["w_pool"][l], small["s_pool"][l] = dwp, dsp.reshape(-1)
        dhc, dlg, dlb = _ln_swish_bwd(t["hc"], dy, vec("conv_ln_g", l), vec("conv_ln_b", l), (gw + pw) // cw)
        small["conv_ln_g"][l], small["conv_ln_b"][l] = dlg.reshape(-1), dlb.reshape(-1)
        dval, dgate, dwd, dbd = _conv_bwd(t["z"], dhc, wl["w_dw"], taps, (2 * gw + pw) // cb,
                                          (2 * gw + pw + cw) // cb)
        small["w_dw"][l], small["b_dw"][l] = dwd, dbd.reshape(-1)
        dz = jnp.concatenate([dzu, dzv, dzp, dval, dgate], axis=1)
        made = reducer.advance(dz)
        d_in = _mm_tn_col("mm_in_dw", t["h1"], dz)
        made += reducer.add("%d2" % l, l, {"w_in": d_in})
        dh = _mm_nt_col("mm_in_dx", dz, wl["w_in"], F32, after=made)
        made = []
    grad_x, _, _, dgn = _norm_bwd("norm_first_bwd", dx, dh, saved[0]["x"], None, None, vec("norm_mix_pre", 0))
    small["norm_mix_pre"][0] = dgn
    drained = reducer.drain([grad_x] + send_small(0))
    drained = [a for a in drained if all(a is not sent[2][0] for sent in small_sent)]

    totals = []
    for l in range(n_layers):
        landed = _wait_copies("small_wait_%d" % l, small_sent[l], _broadcast_plan, N_DEVICES - 1, drained)[0]
        totals.append(_unpack(_device_sum(landed), small_layer(l)))
    grad = {}
    for i, n in enumerate(SMALL):
        g = jnp.stack([totals[l][i] for l in range(n_layers)])
        grad[n] = lax.dynamic_slice_in_dim(g, chip * cb, cb, axis=2) if n == "w_dw" else g
    like = [w[n] for n in SMALL]
    dl, nm, nv = _adamw_flat("adamw_small", _pack(like), _pack([grad[n] for n in SMALL]),
                             _pack([m[n] for n in SMALL]), _pack([v[n] for n in SMALL]))
    delta, new_m, new_v = (dict(zip(SMALL, _unpack(p, like))) for p in (dl, nm, nv))
    for n in BIG:
        grad[n], delta[n], new_m[n], new_v[n] = reducer.done[n]

    return (loss, grad_x[None], *[grad[n] for n in WEIGHTS], *[delta[n] for n in WEIGHTS],
            *[new_m[n] for n in WEIGHTS], *[new_v[n] for n in WEIGHTS])


def kernel(x, mem, norm_mix_pre, norm_mix_post, w_in, w_out, gmlp_v_gain, w_spatial, b_spatial, w_pool, s_pool, w_dw, b_dw, conv_ln_g, conv_ln_b, norm_xattn_pre, norm_mem, norm_xattn_post, w_q, w_k, w_v, w_o, norm_ffn_pre, norm_ffn_post, w_up, w_down, loss_target, m_norm_mix_pre, m_norm_mix_post, m_w_in, m_w_out, m_gmlp_v_gain, m_w_spatial, m_b_spatial, m_w_pool, m_s_pool, m_w_dw, m_b_dw, m_conv_ln_g, m_conv_ln_b, m_norm_xattn_pre, m_norm_mem, m_norm_xattn_post, m_w_q, m_w_k, m_w_v, m_w_o, m_norm_ffn_pre, m_norm_ffn_post, m_w_up, m_w_down, v_norm_mix_pre, v_norm_mix_post, v_w_in, v_w_out, v_gmlp_v_gain, v_w_spatial, v_b_spatial, v_w_pool, v_s_pool, v_w_dw, v_b_dw, v_conv_ln_g, v_conv_ln_b, v_norm_xattn_pre, v_norm_mem, v_norm_xattn_post, v_w_q, v_w_k, v_w_v, v_w_o, v_norm_ffn_pre, v_norm_ffn_post, v_w_up, v_w_down):
    given = dict(locals())
    w = {n: given[n] for n in WEIGHTS}
    m = {n: given["m_" + n] for n in WEIGHTS}
    v = {n: given["v_" + n] for n in WEIGHTS}
    return _step(x[0], mem[0], loss_target[0], w, m, v)
```

```python
import functools

import jax
import jax.numpy as jnp
from jax import lax
from jax.experimental import pallas as pl
from jax.experimental.pallas import tpu as pltpu

F32 = jnp.float32
BF16 = jnp.bfloat16
MESH = pl.DeviceIdType.MESH

N_CHIPS = 4
N_DEVICES = 8
XATTN_HEADS = 4
POOL_WINDOWS = (2, 4, 8, 16)
RMS_EPS = 1e-6
LN_EPS = 1e-5
ADAM_LR, ADAM_B1, ADAM_B2, ADAM_EPS, ADAM_WD, ADAM_STEP = 0.001, 0.9, 0.999, 1e-08, 0.01, 10

V7X_LANES = 128
V7X_VMEM_LIMIT = 56 * 1024 * 1024
ROW_TILE = 256
GMLP_UNROLL = 2
MM_TILE_M, MM_TILE_N, MM_TILE_K = 1024, 1024, 2048

ANY = pl.BlockSpec(memory_space=pl.ANY)
HBM = pl.BlockSpec(memory_space=pltpu.HBM)
SEM = pl.BlockSpec(memory_space=pltpu.SEMAPHORE)


def _tile(dim, pref):
    if dim <= pref:
        return dim
    t = (pref // V7X_LANES) * V7X_LANES
    while t >= V7X_LANES:
        if dim % t == 0:
            return t
        t -= V7X_LANES
    return dim


def _params(sem=None):
    return pltpu.CompilerParams(dimension_semantics=sem, vmem_limit_bytes=V7X_VMEM_LIMIT)


NN = (((1,), (0,)), ((), ()))
NT = (((1,), (1,)), ((), ()))
TN = (((0,), (0,)), ((), ()))


def _mm(name, a, b, *, dn, grid, a_spec, b_spec, o_specs, out_shapes, acc_shape, epi=None, extra=None,
        extra_spec=None, after=()):
    nk = grid[2]
    n_out = len(out_shapes)
    has_extra = extra is not None
    after = list(after)

    def body(*refs):
        a_ref, b_ref = refs[0], refs[1]
        pos = 2
        e_ref = None
        if has_extra:
            e_ref = refs[pos]
            pos += 1
        pos += len(after)
        o_refs = refs[pos:pos + n_out]
        part = lax.dot_general(a_ref[...].astype(BF16), b_ref[...].astype(BF16), dn, preferred_element_type=F32)

        def finish(total):
            if epi is None:
                vals = (total,)
            elif has_extra:
                vals = epi(total, e_ref[...])
            else:
                vals = epi(total)
            for o, v in zip(o_refs, vals):
                o[...] = v.astype(o.dtype)

        if nk == 1:
            finish(part)
            return
        acc = refs[pos + n_out]
        k = pl.program_id(2)

        @pl.when(k == 0)
        def _():
            acc[...] = part

        @pl.when(jnp.logical_and(k > 0, k < nk - 1))
        def _():
            acc[...] += part

        @pl.when(k == nk - 1)
        def _():
            finish(acc[...] + part)

    ins, in_specs = [a, b], [a_spec, b_spec]
    if has_extra:
        ins.append(extra)
        in_specs.append(extra_spec)
    ins += after
    in_specs += [ANY] * len(after)
    outs = pl.pallas_call(
        body, name=name, grid=grid, in_specs=in_specs, out_specs=list(o_specs), out_shape=list(out_shapes),
        scratch_shapes=[pltpu.VMEM(acc_shape, F32)] if nk > 1 else [],
        compiler_params=_params(("parallel", "parallel", "arbitrary")))(*ins)
    return outs


def _mm_nn_row(name, a, w, out_dtypes, epi=None):
    m, k = a.shape
    n = w.shape[1]
    tm, tn, tk = _tile(m, MM_TILE_M), _tile(n, MM_TILE_N), _tile(k, MM_TILE_K)
    o_spec = pl.BlockSpec((tm, tn), lambda i, j, kk: (i, j))
    return _mm(name, a, w, dn=NN, grid=(m // tm, n // tn, k // tk),
               a_spec=pl.BlockSpec((tm, tk), lambda i, j, kk: (i, kk)),
               b_spec=pl.BlockSpec((tk, tn), lambda i, j, kk: (kk, j)),
               o_specs=[o_spec] * len(out_dtypes),
               out_shapes=[jax.ShapeDtypeStruct((m, n), d) for d in out_dtypes], acc_shape=(tm, tn), epi=epi)


def _mm_nn_col(name, a, w, out_dtypes, epi=None):
    m, k = a.shape
    c = w.shape[2]
    tm, tn, tk = _tile(m, MM_TILE_M), _tile(c, MM_TILE_N), _tile(k, MM_TILE_K)
    nb = c // tn
    o_spec = pl.BlockSpec((tm, tn), lambda i, j, kk: (i, j))
    return _mm(name, a, w, dn=NN, grid=(m // tm, N_CHIPS * nb, k // tk),
               a_spec=pl.BlockSpec((tm, tk), lambda i, j, kk: (i, kk)),
               b_spec=pl.BlockSpec((None, tk, tn), lambda i, j, kk: (j // nb, kk, j % nb)),
               o_specs=[o_spec] * len(out_dtypes),
               out_shapes=[jax.ShapeDtypeStruct((m, N_CHIPS * c), d) for d in out_dtypes], acc_shape=(tm, tn),
               epi=epi)


def _mm_nt_row(name, dy, w, out_dtype, epi=None, extra=None, after=()):
    m, n = dy.shape
    k = w.shape[0]
    tm, tn, tk = _tile(m, MM_TILE_M), _tile(k, MM_TILE_N), _tile(n, MM_TILE_K)
    o_spec = pl.BlockSpec((tm, tn), lambda i, j, kk: (i, j))
    return _mm(name, dy, w, dn=NT, grid=(m // tm, k // tn, n // tk),
               a_spec=pl.BlockSpec((tm, tk), lambda i, j, kk: (i, kk)),
               b_spec=pl.BlockSpec((tn, tk), lambda i, j, kk: (j, kk)),
               o_specs=[o_spec], out_shapes=[jax.ShapeDtypeStruct((m, k), out_dtype)], acc_shape=(tm, tn),
               epi=epi, extra=extra, extra_spec=o_spec, after=after)[0]


def _mm_nt_col(name, dy, w, out_dtype, after=()):
    m = dy.shape[0]
    k, c = w.shape[1], w.shape[2]
    tm, tn, tk = _tile(m, MM_TILE_M), _tile(k, MM_TILE_N), _tile(c, MM_TILE_K)
    kb = c // tk
    return _mm(name, dy, w, dn=NT, grid=(m // tm, k // tn, N_CHIPS * kb),
               a_spec=pl.BlockSpec((tm, tk), lambda i, j, kk: (i, kk)),
               b_spec=pl.BlockSpec((None, tn, tk), lambda i, j, kk: (kk // kb, j, kk % kb)),
               o_specs=[pl.BlockSpec((tm, tn), lambda i, j, kk: (i, j))],
               out_shapes=[jax.ShapeDtypeStruct((m, k), out_dtype)], acc_shape=(tm, tn), after=after)[0]


def _mm_tn_row(name, a, dy):
    t, m = a.shape
    n = dy.shape[1]
    tm, tn, tk = _tile(m, MM_TILE_M), _tile(n, MM_TILE_N), _tile(t, MM_TILE_K)
    return _mm(name, a, dy, dn=TN, grid=(m // tm, n // tn, t // tk),
               a_spec=pl.BlockSpec((tk, tm), lambda i, j, kk: (kk, i)),
               b_spec=pl.BlockSpec((tk, tn), lambda i, j, kk: (kk, j)),
               o_specs=[pl.BlockSpec((tm, tn), lambda i, j, kk: (i, j))],
               out_shapes=[jax.ShapeDtypeStruct((m, n), BF16)], acc_shape=(tm, tn))[0]


def _mm_tn_col(name, a, dy):
    t, m = a.shape
    c = dy.shape[1] // N_CHIPS
    tm, tn, tk = _tile(m, MM_TILE_M), _tile(c, MM_TILE_N), _tile(t, MM_TILE_K)
    nb = c // tn
    return _mm(name, a, dy, dn=TN, grid=(m // tm, N_CHIPS * nb, t // tk),
               a_spec=pl.BlockSpec((tk, tm), lambda i, j, kk: (kk, i)),
               b_spec=pl.BlockSpec((tk, tn), lambda i, j, kk: (kk, j)),
               o_specs=[pl.BlockSpec((None, tm, tn), lambda i, j, kk: (j // nb, i, j % nb))],
               out_shapes=[jax.ShapeDtypeStruct((N_CHIPS, m, c), BF16)], acc_shape=(tm, tn))[0]


def _rms(x, g):
    r = lax.rsqrt(jnp.mean(x * x, axis=-1, keepdims=True) + RMS_EPS)
    return x * r * g


def _rms_bwd(x, g, dy):
    r = lax.rsqrt(jnp.mean(x * x, axis=-1, keepdims=True) + RMS_EPS)
    xr = x * r
    dyg = dy * g
    dx = r * (dyg - xr * jnp.mean(dyg * xr, axis=-1, keepdims=True))
    return dx, jnp.sum(dy * xr, axis=0, keepdims=True)


def _norm_fwd(name, x, o, g_post, g_next, after=()):
    after = list(after)
    s, d = x.shape
    tr = _tile(s, ROW_TILE)
    has_prev, has_next = o is not None, g_next is not None
    row = pl.BlockSpec((tr, d), lambda i: (i, 0))
    vec = pl.BlockSpec((1, d), lambda i: (0, 0))

    def body(*refs):
        refs = list(refs)
        xn = refs.pop(0)[...]
        if has_prev:
            o_ref, gp_ref = refs.pop(0), refs.pop(0)
            xn = xn + _rms(o_ref[...], gp_ref[...])
        gn_ref = refs.pop(0) if has_next else None
        del refs[:len(after)]
        if has_prev:
            refs.pop(0)[...] = xn
        if has_next:
            refs.pop(0)[...] = _rms(xn, gn_ref[...]).astype(BF16)

    ins, in_specs = [x], [row]
    if has_prev:
        ins += [o, g_post]
        in_specs += [row, vec]
    if has_next:
        ins.append(g_next)
        in_specs.append(vec)
    ins += after
    in_specs += [ANY] * len(after)
    out_shapes, out_specs = [], []
    if has_prev:
        out_shapes.append(jax.ShapeDtypeStruct((s, d), F32))
        out_specs.append(row)
    if has_next:
        out_shapes.append(jax.ShapeDtypeStruct((s, d), BF16))
        out_specs.append(row)
    outs = pl.pallas_call(body, name=name, grid=(s // tr,), in_specs=in_specs, out_specs=out_specs,
                          out_shape=out_shapes, compiler_params=_params(("parallel",)))(*ins)
    outs = list(outs)
    x_new = outs.pop(0) if has_prev else x
    h = outs.pop(0) if has_next else None
    return x_new, h


def _norm_bwd(name, dxn, dh, xn, o, g_post, g_next, after=()):
    after = list(after)
    s, d = xn.shape
    tr = _tile(s, ROW_TILE)
    has_prev, has_next, has_dxn = o is not None, dh is not None, dxn is not None
    row = pl.BlockSpec((tr, d), lambda i: (i, 0))
    vec = pl.BlockSpec((1, d), lambda i: (0, 0))

    def body(*refs):
        refs = list(refs)
        first = pl.program_id(0) == 0
        dxn_ref = refs.pop(0) if has_dxn else None
        dh_ref = refs.pop(0) if has_next else None
        xn_ref = refs.pop(0)
        if has_prev:
            o_ref, gp_ref = refs.pop(0), refs.pop(0)
        gn_ref = refs.pop(0) if has_next else None
        del refs[:len(after)]
        dx_ref = refs.pop(0)
        if has_prev:
            do_ref, dgp_ref = refs.pop(0), refs.pop(0)
        dgn_ref = refs.pop(0) if has_next else None

        def accumulate(ref, val):
            @pl.when(first)
            def _():
                ref[...] = val

            @pl.when(jnp.logical_not(first))
            def _():
                ref[...] += val

        dx = dxn_ref[...] if has_dxn else None
        if has_next:
            dxh, dgn = _rms_bwd(xn_ref[...], gn_ref[...], dh_ref[...].astype(F32))
            dx = dxh if dx is None else dx + dxh
            accumulate(dgn_ref, dgn)
        dx_ref[...] = dx
        if has_prev:
            do, dgp = _rms_bwd(o_ref[...], gp_ref[...], dx)
            do_ref[...] = do.astype(BF16)
            accumulate(dgp_ref, dgp)

    ins, in_specs = [], []
    if has_dxn:
        ins.append(dxn)
        in_specs.append(row)
    if has_next:
        ins.append(dh)
        in_specs.append(row)
    ins.append(xn)
    in_specs.append(row)
    if has_prev:
        ins += [o, g_post]
        in_specs += [row, vec]
    if has_next:
        ins.append(g_next)
        in_specs.append(vec)
    ins += after
    in_specs += [ANY] * len(after)
    out_shapes, out_specs = [jax.ShapeDtypeStruct((s, d), F32)], [row]
    if has_prev:
        out_shapes += [jax.ShapeDtypeStruct((s, d), BF16), jax.ShapeDtypeStruct((1, d), F32)]
        out_specs += [row, vec]
    if has_next:
        out_shapes.append(jax.ShapeDtypeStruct((1, d), F32))
        out_specs.append(vec)
    outs = list(pl.pallas_call(body, name=name, grid=(s // tr,), in_specs=in_specs, out_specs=out_specs,
                               out_shape=out_shapes, compiler_params=_params(("arbitrary",)))(*ins))
    dx = outs.pop(0)
    do, dgp = (outs.pop(0), outs.pop(0)) if has_prev else (None, None)
    dgn = outs.pop(0) if has_next else None
    return dx, do, dgp, dgn


def _loss_head(y, target):
    s, d = y.shape
    tr = _tile(s, ROW_TILE)
    row = pl.BlockSpec((tr, d), lambda i: (i, 0))
    vec = pl.BlockSpec((1, d), lambda i: (0, 0))

    def body(y_ref, t_ref, dy_ref, l_ref):
        err = y_ref[...] - t_ref[...]
        dy_ref[...] = err * (1.0 / d)
        part = jnp.sum(err * err, axis=0, keepdims=True) * (0.5 / d)

        @pl.when(pl.program_id(0) == 0)
        def _():
            l_ref[...] = part

        @pl.when(pl.program_id(0) != 0)
        def _():
            l_ref[...] += part

    return pl.pallas_call(body, name="loss_head", grid=(s // tr,), in_specs=[row, row], out_specs=[row, vec],
                          out_shape=[jax.ShapeDtypeStruct((s, d), F32), jax.ShapeDtypeStruct((1, d), F32)],
                          compiler_params=_params(("arbitrary",)))(y, target)


@jax.custom_vjp
def _bdot(a, b):
    return jnp.dot(a.astype(BF16), b.astype(BF16), preferred_element_type=F32)


def _bdot_fwd(a, b):
    return _bdot(a, b), (a, b)


def _bdot_bwd(res, ct):
    a, b = res
    ctb = ct.astype(BF16)
    da = lax.dot_general(ctb, b.astype(BF16), NT, preferred_element_type=F32)
    db = lax.dot_general(a.astype(BF16), ctb, TN, preferred_element_type=F32)
    return da, db


_bdot.defvjp(_bdot_fwd, _bdot_bwd)


@functools.partial(jax.custom_vjp, nondiff_argnums=(1,))
def _shift(x, k):
    n = x.shape[0]
    if k == 0:
        return x
    rolled = pltpu.roll(x, k % n, 0)
    t = lax.broadcasted_iota(jnp.int32, x.shape, 0)
    keep = (t >= k) if k > 0 else (t < n + k)
    return jnp.where(keep, rolled, 0.0)


def _shift_fwd(x, k):
    return _shift(x, k), None


def _shift_bwd(k, _, ct):
    return (_shift(ct, -k),)


_shift.defvjp(_shift_fwd, _shift_bwd)


def _sigmoid(x):
    return 1.0 / (1.0 + jnp.exp(-x))


def _layer_norm(x, g, b=None):
    mu = jnp.mean(x, axis=-1, keepdims=True)
    xc = x - mu
    var = jnp.mean(xc * xc, axis=-1, keepdims=True)
    y = xc * lax.rsqrt(var + LN_EPS) * g
    return y if b is None else y + b


def _gmlp_chunk(zu, zv, gv, w, bcol):
    ch = w.shape[0]
    u = jax.nn.gelu(zu)
    vn = _layer_norm(jax.nn.gelu(zv), gv)
    t = lax.broadcasted_iota(jnp.int32, (ch, ch), 0)
    s = lax.broadcasted_iota(jnp.int32, (ch, ch), 1)
    wm = jnp.where(t >= s, w, 0.0)
    return u * (_bdot(wm, vn) + bcol)


def _gmlp_specs(seq, heads, hd, ch, u_off, v_off):
    col = lambda off: pl.BlockSpec((seq, hd), lambda h: (0, off + h))
    return (col(u_off), col(v_off), pl.BlockSpec((None, 1, hd), lambda h: (h, 0, 0)),
            pl.BlockSpec((None, ch, ch), lambda h: (h, 0, 0)), pl.BlockSpec((None, ch, 1), lambda h: (h, 0, 0)))


def _gmlp_fwd(z, gv, ws, bcol):
    seq = z.shape[0]
    heads, _, hd = gv.shape
    ch = ws.shape[-1]
    zu_s, zv_s, gv_s, w_s, b_s = _gmlp_specs(seq, heads, hd, ch, 0, heads)

    def body(zu_ref, zv_ref, gv_ref, w_ref, b_ref, y_ref):
        gvv, w, bc = gv_ref[...], w_ref[...], b_ref[...]

        def step(c, carry):
            rows = pl.ds(pl.multiple_of(c * ch, ch), ch)
            y_ref[rows, :] = _gmlp_chunk(zu_ref[rows, :], zv_ref[rows, :], gvv, w, bc).astype(BF16)
            return carry

        lax.fori_loop(0, seq // ch, step, 0, unroll=GMLP_UNROLL)

    return pl.pallas_call(body, name="gmlp_fwd", grid=(heads,), in_specs=[zu_s, zv_s, gv_s, w_s, b_s],
                          out_specs=pl.BlockSpec((seq, hd), lambda h: (0, h)),
                          out_shape=jax.ShapeDtypeStruct((seq, heads * hd), BF16),
                          compiler_params=_params(("parallel",)))(z, z, gv, ws, bcol)


def _gmlp_bwd(z, dy, gv, ws, bcol):
    seq = z.shape[0]
    heads, _, hd = gv.shape
    ch = ws.shape[-1]
    zu_s, zv_s, gv_s, w_s, b_s = _gmlp_specs(seq, heads, hd, ch, 0, heads)
    col = pl.BlockSpec((seq, hd), lambda h: (0, h))

    def body(zu_ref, zv_ref, dy_ref, gv_ref, w_ref, b_ref, dzu_ref, dzv_ref, dgv_ref, dw_ref, db_ref):
        gvv, w, bc = gv_ref[...], w_ref[...], b_ref[...]

        together = GMLP_UNROLL if (seq // ch) % GMLP_UNROLL == 0 else 1

        def step(c, carry):
            dgv, dw, db = carry
            for u in range(together):
                rows = pl.ds(pl.multiple_of((c * together + u) * ch, ch), ch)
                _, vjp = jax.vjp(_gmlp_chunk, zu_ref[rows, :], zv_ref[rows, :], gvv, w, bc)
                dzu, dzv, dgv_c, dw_c, db_c = vjp(dy_ref[rows, :])
                dzu_ref[rows, :] = dzu.astype(BF16)
                dzv_ref[rows, :] = dzv.astype(BF16)
                dgv, dw, db = dgv + dgv_c, dw + dw_c, db + db_c
            return dgv, dw, db

        zero = (jnp.zeros((1, hd), F32), jnp.zeros((ch, ch), F32), jnp.zeros((ch, 1), F32))
        dgv, dw, db = lax.fori_loop(0, seq // ch // together, step, zero)
        dgv_ref[...] = dgv
        dw_ref[...] = dw
        db_ref[...] = db

    return pl.pallas_call(
        body, name="gmlp_bwd", grid=(heads,), in_specs=[zu_s, zv_s, col, gv_s, w_s, b_s],
        out_specs=[col, col, gv_s, w_s, b_s],
        out_shape=[jax.ShapeDtypeStruct((seq, heads * hd), BF16), jax.ShapeDtypeStruct((seq, heads * hd), BF16),
                   jax.ShapeDtypeStruct(gv.shape, F32), jax.ShapeDtypeStruct(ws.shape, F32),
                   jax.ShapeDtypeStruct(bcol.shape, F32)],
        compiler_params=_params(("parallel",)))(z, z, dy, gv, ws, bcol)


def _pool_group(p, w, s, window):
    win, span = p, 1
    while span < window:
        win = win + _shift(win, span)
        span *= 2
    t = lax.broadcasted_iota(jnp.int32, (p.shape[0], 1), 0).astype(F32)
    cnt = jnp.minimum(t + 1.0, float(window))
    return _bdot(win / cnt - p, w) * s


def _pool_fwd(z, w_pool, s_pool, col_block):
    seq = z.shape[0]
    groups, gw, _ = w_pool.shape
    pw = groups * gw

    def body(p_ref, w_ref, s_ref, y_ref):
        for g in range(groups):
            cols = slice(g * gw, (g + 1) * gw)
            y_ref[:, cols] = _pool_group(p_ref[:, cols], w_ref[g], s_ref[:, cols], POOL_WINDOWS[g]).astype(BF16)

    return pl.pallas_call(
        body, name="pool_fwd", grid=(1,),
        in_specs=[pl.BlockSpec((seq, pw), lambda i: (0, col_block)),
                  pl.BlockSpec((groups, gw, gw), lambda i: (0, 0, 0)), pl.BlockSpec((1, pw), lambda i: (0, 0))],
        out_specs=pl.BlockSpec((seq, pw), lambda i: (0, 0)), out_shape=jax.ShapeDtypeStruct((seq, pw), BF16),
        compiler_params=_params(("arbitrary",)))(z, w_pool, s_pool)


def _pool_bwd(z, dy, w_pool, s_pool, col_block, dy_block):
    seq = z.shape[0]
    groups, gw, _ = w_pool.shape
    pw = groups * gw

    def body(p_ref, dy_ref, w_ref, s_ref, dp_ref, dw_ref, ds_ref):
        for g in range(groups):
            cols = slice(g * gw, (g + 1) * gw)
            _, vjp = jax.vjp(functools.partial(_pool_group, window=POOL_WINDOWS[g]), p_ref[:, cols], w_ref[g],
                             s_ref[:, cols])
            dp, dw, ds = vjp(dy_ref[:, cols])
            dp_ref[:, cols] = dp.astype(BF16)
            dw_ref[g] = dw
            ds_ref[:, cols] = ds

    return pl.pallas_call(
        body, name="pool_bwd", grid=(1,),
        in_specs=[pl.BlockSpec((seq, pw), lambda i: (0, col_block)),
                  pl.BlockSpec((seq, pw), lambda i: (0, dy_block)),
                  pl.BlockSpec((groups, gw, gw), lambda i: (0, 0, 0)), pl.BlockSpec((1, pw), lambda i: (0, 0))],
        out_specs=[pl.BlockSpec((seq, pw), lambda i: (0, 0)), pl.BlockSpec((groups, gw, gw), lambda i: (0, 0, 0)),
                   pl.BlockSpec((1, pw), lambda i: (0, 0))],
        out_shape=[jax.ShapeDtypeStruct((seq, pw), BF16), jax.ShapeDtypeStruct(w_pool.shape, F32),
                   jax.ShapeDtypeStruct((1, pw), F32)],
        compiler_params=_params(("arbitrary",)))(z, dy, w_pool, s_pool)


def _conv_fwd(z, w_dw, taps, b_dw, val_block, gate_block):
    seq = z.shape[0]
    rows, cb = w_dw.shape[1], w_dw.shape[2]

    def body(val_ref, gate_ref, w_ref, b_ref, out_ref):
        h = val_ref[...] * _sigmoid(gate_ref[...])
        acc = jnp.broadcast_to(b_ref[...], h.shape)
        for d in range(taps):
            acc = acc + w_ref[pl.ds(taps - 1 - d, 1), :] * _shift(h, d)
        out_ref[...] = acc

    return pl.pallas_call(
        body, name="conv_fwd", grid=(N_CHIPS,),
        in_specs=[pl.BlockSpec((seq, cb), lambda j: (0, val_block + j)),
                  pl.BlockSpec((seq, cb), lambda j: (0, gate_block + j)),
                  pl.BlockSpec((None, rows, cb), lambda j: (j, 0, 0)),
                  pl.BlockSpec((1, cb), lambda j: (0, j))],
        out_specs=pl.BlockSpec((seq, cb), lambda j: (0, j)),
        out_shape=jax.ShapeDtypeStruct((seq, N_CHIPS * cb), F32),
        compiler_params=_params(("parallel",)))(z, z, w_dw, b_dw)


def _conv_bwd(z, dout, w_dw, taps, val_block, gate_block):
    seq = z.shape[0]
    rows, cb = w_dw.shape[1], w_dw.shape[2]
    col = pl.BlockSpec((seq, cb), lambda j: (0, j))

    def body(val_ref, gate_ref, do_ref, w_ref, dval_ref, dgate_ref, dw_ref, db_ref):
        val, sg, do = val_ref[...], _sigmoid(gate_ref[...]), do_ref[...]
        h = val * sg
        db_ref[...] = jnp.sum(do, axis=0, keepdims=True)
        dh = jnp.zeros_like(h)
        for d in range(taps):
            k = taps - 1 - d
            dw_ref[pl.ds(k, 1), :] = jnp.sum(do * _shift(h, d), axis=0, keepdims=True)
            dh = dh + w_ref[pl.ds(k, 1), :] * _shift(do, -d)
        dval_ref[...] = (dh * sg).astype(BF16)
        dgate_ref[...] = (dh * val * sg * (1.0 - sg)).astype(BF16)

    return pl.pallas_call(
        body, name="conv_bwd", grid=(N_CHIPS,),
        in_specs=[pl.BlockSpec((seq, cb), lambda j: (0, val_block + j)),
                  pl.BlockSpec((seq, cb), lambda j: (0, gate_block + j)), col,
                  pl.BlockSpec((None, rows, cb), lambda j: (j, 0, 0))],
        out_specs=[col, col, pl.BlockSpec((taps, cb), lambda j: (0, j)), pl.BlockSpec((1, cb), lambda j: (0, j))],
        out_shape=[jax.ShapeDtypeStruct((seq, N_CHIPS * cb), BF16), jax.ShapeDtypeStruct((seq, N_CHIPS * cb), BF16),
                   jax.ShapeDtypeStruct((taps, N_CHIPS * cb), F32), jax.ShapeDtypeStruct((1, N_CHIPS * cb), F32)],
        compiler_params=_params(("parallel",)))(z, z, dout, w_dw)


def _ln_swish(hc, g, b):
    y = _layer_norm(hc, g, b)
    return y * _sigmoid(y)


def _ln_swish_fwd(hc, g, b):
    s, cw = hc.shape
    tr = _tile(s, ROW_TILE)
    row = pl.BlockSpec((tr, cw), lambda i: (i, 0))
    vec = pl.BlockSpec((1, cw), lambda i: (0, 0))

    def body(h_ref, g_ref, b_ref, y_ref):
        y_ref[...] = _ln_swish(h_ref[...], g_ref[...], b_ref[...]).astype(BF16)

    return pl.pallas_call(body, name="ln_swish_fwd", grid=(s // tr,), in_specs=[row, vec, vec], out_specs=row,
                          out_shape=jax.ShapeDtypeStruct((s, cw), BF16),
                          compiler_params=_params(("parallel",)))(hc, g, b)


def _ln_swish_bwd(hc, dy, g, b, dy_block):
    s, cw = hc.shape
    tr = _tile(s, ROW_TILE)
    row = pl.BlockSpec((tr, cw), lambda i: (i, 0))
    vec = pl.BlockSpec((1, cw), lambda i: (0, 0))

    def body(h_ref, dy_ref, g_ref, b_ref, dh_ref, dg_ref, db_ref):
        _, vjp = jax.vjp(_ln_swish, h_ref[...], g_ref[...], b_ref[...])
        dh, dg, db = vjp(dy_ref[...])
        dh_ref[...] = dh

        @pl.when(pl.program_id(0) == 0)
        def _():
            dg_ref[...] = dg
            db_ref[...] = db

        @pl.when(pl.program_id(0) != 0)
        def _():
            dg_ref[...] += dg
            db_ref[...] += db

    return pl.pallas_call(
        body, name="ln_swish_bwd", grid=(s // tr,),
        in_specs=[row, pl.BlockSpec((tr, cw), lambda i: (i, dy_block)), vec, vec], out_specs=[row, vec, vec],
        out_shape=[jax.ShapeDtypeStruct((s, cw), F32), jax.ShapeDtypeStruct((1, cw), F32),
                   jax.ShapeDtypeStruct((1, cw), F32)],
        compiler_params=_params(("arbitrary",)))(hc, dy, g, b)


def _attn_probs(q, k, scale):
    s = lax.dot_general(q, k, NT, preferred_element_type=F32) * scale
    e = jnp.exp(s - jnp.max(s, axis=-1, keepdims=True))
    return e / jnp.sum(e, axis=-1, keepdims=True)


def _attn_fwd(q, k, v):
    seq, d = q.shape
    mem = k.shape[0]
    hd = d // XATTN_HEADS
    scale = hd ** -0.5
    qs = pl.BlockSpec((seq, hd), lambda h: (0, h))
    ms = pl.BlockSpec((mem, hd), lambda h: (0, h))

    def body(q_ref, k_ref, v_ref, a_ref):
        p = _attn_probs(q_ref[...], k_ref[...], scale)
        a_ref[...] = jnp.dot(p.astype(BF16), v_ref[...], preferred_element_type=F32).astype(BF16)

    return pl.pallas_call(body, name="attn_fwd", grid=(XATTN_HEADS,), in_specs=[qs, ms, ms], out_specs=qs,
                          out_shape=jax.ShapeDtypeStruct((seq, d), BF16),
                          compiler_params=_params(("parallel",)))(q, k, v)


def _attn_bwd(q, k, v, da):
    seq, d = q.shape
    mem = k.shape[0]
    hd = d // XATTN_HEADS
    scale = hd ** -0.5
    qs = pl.BlockSpec((seq, hd), lambda h: (0, h))
    ms = pl.BlockSpec((mem, hd), lambda h: (0, h))

    def body(q_ref, k_ref, v_ref, da_ref, dq_ref, dk_ref, dv_ref):
        q_, k_, v_, da_ = q_ref[...], k_ref[...], v_ref[...], da_ref[...]
        p = _attn_probs(q_, k_, scale)
        dv_ref[...] = lax.dot_general(p.astype(BF16), da_, TN, preferred_element_type=F32).astype(BF16)
        dp = lax.dot_general(da_, v_, NT, preferred_element_type=F32)
        ds = (p * (dp - jnp.sum(dp * p, axis=-1, keepdims=True)) * scale).astype(BF16)
        dq_ref[...] = jnp.dot(ds, k_, preferred_element_type=F32).astype(BF16)
        dk_ref[...] = lax.dot_general(ds, q_, TN, preferred_element_type=F32).astype(BF16)

    return pl.pallas_call(
        body, name="attn_bwd", grid=(XATTN_HEADS,), in_specs=[qs, ms, ms, qs], out_specs=[qs, ms, ms],
        out_shape=[jax.ShapeDtypeStruct((seq, d), BF16), jax.ShapeDtypeStruct((mem, d), BF16),
                   jax.ShapeDtypeStruct((mem, d), BF16)],
        compiler_params=_params(("parallel",)))(q, k, v, da)


def _place_shard(name, place, w, l, dtype, after=()):
    _, r, c = w.shape
    tr = _tile(r, 2 * ROW_TILE) if r % 16 == 0 else r
    after = list(after)

    def body(place_ref, w_ref, *refs):
        refs[-1][...] = w_ref[...].astype(dtype)

    return pl.pallas_call(
        body, name=name,
        grid_spec=pltpu.PrefetchScalarGridSpec(
            num_scalar_prefetch=1, grid=(r // tr,),
            in_specs=[pl.BlockSpec((None, tr, c), lambda i, p: (l, i, 0))] + [ANY] * len(after),
            out_specs=pl.BlockSpec((None, tr, c), lambda i, p: (p[1], i, 0))),
        out_shape=jax.ShapeDtypeStruct((N_CHIPS, r, c), dtype),
        compiler_params=_params(("parallel",)))(place, w, *after)


def _place_flat(place, flat):
    rows, lanes = flat.shape

    def body(place_ref, f_ref, o_ref):
        o_ref[...] = f_ref[...]

    return pl.pallas_call(
        body, name="place_flat",
        grid_spec=pltpu.PrefetchScalarGridSpec(
            num_scalar_prefetch=1, grid=(1,), in_specs=[pl.BlockSpec((rows, lanes), lambda i, p: (0, 0))],
            out_specs=pl.BlockSpec((None, rows, lanes), lambda i, p: (p[2], 0, 0))),
        out_shape=jax.ShapeDtypeStruct((N_DEVICES, rows, lanes), flat.dtype),
        compiler_params=_params(("arbitrary",)))(place, flat)


def _pair_sum(name, place, dw, got):
    n, _, r2, c = dw.shape
    tr = _tile(r2, 2 * ROW_TILE)

    def body(place_ref, own_ref, got_ref, s_ref, t_ref):
        val = (own_ref[...].astype(F32) + got_ref[...].astype(F32)).astype(BF16)
        s_ref[...] = val

        @pl.when(pl.program_id(1) == place_ref[1])
        def _():
            t_ref[...] = val

    slab = pl.BlockSpec((None, tr, c), lambda i, j, p: (j, i, 0))
    sds = jax.ShapeDtypeStruct((n, r2, c), BF16)
    return pl.pallas_call(
        body, name=name,
        grid_spec=pltpu.PrefetchScalarGridSpec(
            num_scalar_prefetch=1, grid=(r2 // tr, n),
            in_specs=[pl.BlockSpec((None, None, tr, c), lambda i, j, p: (j, p[0], i, 0)), slab],
            out_specs=[slab, pl.BlockSpec((None, tr, c), lambda i, j, p: (p[1], i, 0))]),
        out_shape=[sds, sds], compiler_params=_params(("parallel", "arbitrary")))(place, dw, got)


def _chip_sum(name, place, parts):
    n, r2, c = parts.shape
    tr = _tile(r2, ROW_TILE)

    def body(place_ref, *refs):
        o_ref = refs[n]
        acc = refs[0][...].astype(F32)
        for j in range(1, n):
            acc = acc + refs[j][...].astype(F32)
        o_ref[...] = acc

    part = lambda j: pl.BlockSpec((None, tr, c), lambda i, p: (j, i, 0))
    return pl.pallas_call(
        body, name=name,
        grid_spec=pltpu.PrefetchScalarGridSpec(
            num_scalar_prefetch=1, grid=(r2 // tr,), in_specs=[part(j) for j in range(n)],
            out_specs=pl.BlockSpec((None, tr, c), lambda i, p: (p[0], i, 0))),
        out_shape=jax.ShapeDtypeStruct((2, r2, c), F32),
        compiler_params=_params(("parallel",)))(place, *([parts] * n))


def _device_sum(parts):
    n, rows, lanes = parts.shape
    tr = _tile(rows, 4 * ROW_TILE) if rows % 8 == 0 else rows

    def body(p_ref, o_ref):
        acc = p_ref[0]
        for j in range(1, n):
            acc = acc + p_ref[j]
        o_ref[...] = acc

    return pl.pallas_call(
        body, name="device_sum", grid=(rows // tr,), in_specs=[pl.BlockSpec((n, tr, lanes), lambda i: (0, i, 0))],
        out_specs=pl.BlockSpec((tr, lanes), lambda i: (i, 0)), out_shape=jax.ShapeDtypeStruct((rows, lanes), F32),
        compiler_params=_params(("parallel",)))(parts)


def _adam_update(w, g, m, v):
    nm = ADAM_B1 * m + (1.0 - ADAM_B1) * g
    nv = ADAM_B2 * v + (1.0 - ADAM_B2) * (g * g)
    c1 = 1.0 - ADAM_B1 ** ADAM_STEP
    c2 = 1.0 - ADAM_B2 ** ADAM_STEP
    return -ADAM_LR * ((nm / c1) / (jnp.sqrt(nv / c2) + ADAM_EPS) + ADAM_WD * w), nm, nv


def _adamw_layer(name, l, w, g, m, v, prev):
    n_l, r, c = w.shape
    tr = _tile(r, ROW_TILE)
    slab = pl.BlockSpec((None, tr, c), lambda i: (l, i, 0))

    def body(w_ref, g_ref, m_ref, v_ref, *refs):
        go_ref, d_ref, nm_ref, nv_ref = refs[-4:]
        g_ = g_ref[...]
        delta, nm, nv = _adam_update(w_ref[...], g_, m_ref[...], v_ref[...])
        go_ref[...] = g_
        d_ref[...] = delta
        nm_ref[...] = nm
        nv_ref[...] = nv

    ins = [w, g, m, v]
    in_specs = [slab, pl.BlockSpec((tr, c), lambda i: (i, 0)), slab, slab]
    aliases = {}
    if prev is not None:
        aliases = {len(ins) + i: i for i in range(4)}
        ins += list(prev)
        in_specs += [ANY] * 4
    sds = jax.ShapeDtypeStruct((n_l, r, c), F32)
    return pl.pallas_call(body, name=name, grid=(r // tr,), in_specs=in_specs, out_specs=[slab] * 4,
                          out_shape=[sds] * 4, input_output_aliases=aliases,
                          compiler_params=_params(("parallel",)))(*ins)


def _adamw_flat(name, w, g, m, v):
    rows, cols = w.shape
    tr = _tile(rows, ROW_TILE) if rows % 8 == 0 else rows
    spec = pl.BlockSpec((tr, cols), lambda i: (i, 0))

    def body(w_ref, g_ref, m_ref, v_ref, d_ref, nm_ref, nv_ref):
        d_ref[...], nm_ref[...], nv_ref[...] = _adam_update(w_ref[...], g_ref[...], m_ref[...], v_ref[...])

    sds = jax.ShapeDtypeStruct((rows, cols), F32)
    return pl.pallas_call(body, name=name, grid=(rows // tr,), in_specs=[spec] * 4, out_specs=[spec] * 3,
                          out_shape=[sds] * 3, compiler_params=_params(("parallel",)))(w, g, m, v)


def _me():
    return lax.axis_index("x"), lax.axis_index("y"), lax.axis_index("c")


def _other_chips(x, y):
    return [(1 - x, y, 2 * (1 - x) + y), (x, 1 - y, 2 * x + 1 - y), (1 - x, 1 - y, 2 * (1 - x) + 1 - y)]


def _remote(src, dst, send_sem, recv_sem, target):
    return pltpu.make_async_remote_copy(src_ref=src, dst_ref=dst, send_sem=send_sem, recv_sem=recv_sem,
                                        device_id=target, device_id_type=MESH)


def _exchange(name, bufs, plan, n_copies):
    n = len(bufs)

    def body(*refs):
        send_sems, recv_sems = refs[2 * n:]
        copies = []
        for i, (src, dst, target) in enumerate(plan(refs[n:2 * n], _me())):
            if target is None:
                cp = pltpu.make_async_copy(src, dst, send_sems.at[i])
            else:
                cp = _remote(src, dst, send_sems.at[i], recv_sems.at[i], target)
            cp.start()
            copies.append((cp, target))
        assert len(copies) == n_copies
        for cp, target in copies:
            if target is None:
                cp.wait()
            else:
                cp.wait_recv()
        for cp, target in copies:
            if target is not None:
                cp.wait_send()

    return pl.pallas_call(
        body, name=name, in_specs=[ANY] * n, out_specs=[ANY] * n,
        out_shape=[jax.ShapeDtypeStruct(b.shape, b.dtype) for b in bufs],
        scratch_shapes=[pltpu.SemaphoreType.DMA((n_copies,)), pltpu.SemaphoreType.DMA((n_copies,))],
        input_output_aliases={i: i for i in range(n)},
        compiler_params=pltpu.CompilerParams(has_side_effects=True))(*bufs)


def _start_copies(name, groups, after=()):
    all_bufs = [b for bufs, _, _ in groups for b in bufs]
    after = list(after)
    n = len(all_bufs)
    n_g = len(groups)

    def body(*refs):
        in_refs, sem_refs = refs[:n], refs[n + len(after):n + len(after) + 2 * n_g]
        pos = 0
        for g, (bufs, plan, n_copies) in enumerate(groups):
            copies = plan(in_refs[pos:pos + len(bufs)], _me())
            assert len(copies) == n_copies
            for i, (src, dst, target) in enumerate(copies):
                _remote(src, dst, sem_refs[2 * g].at[i], sem_refs[2 * g + 1].at[i], target).start()
            pos += len(bufs)

    sems = []
    for _, _, n_copies in groups:
        sems += [pltpu.SemaphoreType.DMA((n_copies,))] * 2
    outs = pl.pallas_call(
        body, name=name, in_specs=[HBM] * n + [ANY] * len(after), out_specs=[SEM] * (2 * n_g) + [HBM] * n,
        out_shape=sems + [pltpu.HBM(b.shape, b.dtype) for b in all_bufs],
        input_output_aliases={i: 2 * n_g + i for i in range(n)},
        compiler_params=pltpu.CompilerParams(has_side_effects=pltpu.SideEffectType.DATAFLOW_SIDE_EFFECTING))(
            *[pltpu.with_memory_space_constraint(b, pltpu.HBM) for b in all_bufs], *after)
    result, pos = [], 2 * n_g
    for g, (bufs, _, _) in enumerate(groups):
        result.append((outs[2 * g], outs[2 * g + 1], list(outs[pos:pos + len(bufs)])))
        pos += len(bufs)
    return result


def _wait_copies(name, started, plan, n_copies, after):
    send_sems, recv_sems, bufs = started
    n = len(bufs)
    after = list(after) if isinstance(after, (list, tuple)) else [after]
    after = [a for a in after if all(a is not b for b in bufs)]

    def body(*refs):
        copies = plan(refs[:n], _me())
        assert len(copies) == n_copies
        for i, (src, dst, target) in enumerate(copies):
            cp = _remote(src, dst, refs[n].at[i], refs[n + 1].at[i], target)
            cp.wait_send()
            cp.wait_recv()

    return list(pl.pallas_call(
        body, name=name, in_specs=[HBM] * n + [SEM, SEM] + [ANY] * len(after), out_specs=[HBM] * n,
        out_shape=[pltpu.HBM(b.shape, b.dtype) for b in bufs], input_output_aliases={i: i for i in range(n)},
        compiler_params=pltpu.CompilerParams(has_side_effects=pltpu.SideEffectType.DATAFLOW_SIDE_EFFECTING))(
            *bufs, send_sems, recv_sems, *after))


def _halves(a):
    return a.reshape(a.shape[0], 2, a.shape[1] // 2, a.shape[2])


def _gather_plan(refs, me):
    x, y, c = me
    mine = 2 * x + y
    return [(g.at[mine, c], g.at[mine, c], (px, py, c)) for g in refs for px, py, _ in _other_chips(x, y)]


def _forward_plan(refs, me):
    x, y, c = me
    return [(g.at[chip, c], g.at[chip, c], (x, y, 1 - c)) for g in refs for _, _, chip in _other_chips(x, y)]


def _swap_plan(refs, me):
    x, y, c = me
    k = len(refs) // 2
    return [(refs[i].at[j, 1 - c], refs[k + i].at[j], (x, y, 1 - c)) for i in range(k) for j in range(N_CHIPS)]


def _scatter_plan(refs, me):
    x, y, c = me
    mine = 2 * x + y
    k = len(refs) // 2
    return [(refs[i].at[chip], refs[k + i].at[mine], (px, py, c))
            for i in range(k) for px, py, chip in _other_chips(x, y)]


def _share_plan(refs, me):
    x, y, c = me
    return [(g.at[c], g.at[c], (x, y, 1 - c)) for g in refs]


def _broadcast_plan(refs, me):
    x, y, c = me
    mine = 4 * x + 2 * y + c
    copies = []
    for fx, fy, fc in [(0, 0, 1), (0, 1, 0), (0, 1, 1), (1, 0, 0), (1, 0, 1), (1, 1, 0), (1, 1, 1)]:
        peer = (x + fx - 2 * fx * x, y + fy - 2 * fy * y, c + fc - 2 * fc * c)
        copies.append((refs[0].at[mine], refs[0].at[mine], peer))
    return copies


BIG = ("w_in", "w_out", "w_q", "w_k", "w_v", "w_o", "w_up", "w_down")
COLUMN_SPLIT = ("w_in", "w_up")
WEIGHTS = ("norm_mix_pre", "norm_mix_post", "w_in", "w_out", "gmlp_v_gain", "w_spatial", "b_spatial", "w_pool",
           "s_pool", "w_dw", "b_dw", "conv_ln_g", "conv_ln_b", "norm_xattn_pre", "norm_mem", "norm_xattn_post",
           "w_q", "w_k", "w_v", "w_o", "norm_ffn_pre", "norm_ffn_post", "w_up", "w_down")
SMALL = tuple(n for n in WEIGHTS if n not in BIG)
REPLICATED = tuple(n for n in SMALL if n != "w_dw")
GATHER_GROUPS = (("w_in", "w_dw"), ("w_out", "w_q", "w_k", "w_v", "w_o"), ("w_up",), ("w_down",))


def _relu2(acc):
    r = jnp.maximum(acc, 0.0)
    return acc, r * r


def _relu2_bwd(acc, up):
    return (acc * (2.0 * jnp.maximum(up, 0.0)),)


def _pack(arrays):
    flat = jnp.concatenate([a.reshape(-1) for a in arrays])
    tile = 8 * V7X_LANES
    pad = (-flat.shape[0]) % tile
    return jnp.pad(flat, (0, pad)).reshape(-1, V7X_LANES)


def _pack_layers(arrays):
    n_l = arrays[0].shape[0]
    flat = jnp.concatenate([a.reshape(n_l, -1) for a in arrays], axis=1)
    pad = (-flat.shape[1]) % (8 * V7X_LANES)
    return jnp.pad(flat, ((0, 0), (0, pad))).reshape(n_l, -1, V7X_LANES)


def _unpack_layers(packed, like):
    flat = packed.reshape(packed.shape[0], -1)
    out, pos = [], 0
    for a in like:
        out.append(flat[:, pos:pos + a[0].size].reshape(a.shape))
        pos += a[0].size
    return out


class _GradientReducer:
    def __init__(self, place, w, m, v):
        self.place, self.w, self.m, self.v = place, w, m, v
        self.flying = []
        self.done = {n: None for n in BIG}

    def add(self, tag, l, grads):
        names = list(grads)
        views = [_halves(grads[n]) for n in names]
        zones = [lax.empty((v.shape[0],) + v.shape[2:], v.dtype) for v in views]
        started = _start_copies("swap_start_" + tag, [(views + zones, _swap_plan, N_CHIPS * len(names))])[0]
        self.flying.append(dict(stage=0, tag=tag, l=l, names=names, started=started))
        return [started[2][0]]

    def advance(self, after):
        made = []
        after = list(after) if isinstance(after, (list, tuple)) else [after]
        for item in self.flying:
            item["stage"] += 1
        for item in self.flying:
            tag, names, k = item["tag"], item["names"], len(item["names"])
            if item["stage"] == 1:
                bufs = _wait_copies("swap_wait_" + tag, item["started"], _swap_plan, N_CHIPS * k, after)
                sums, parts = zip(*[_pair_sum("pair_sum_" + n, self.place, dv, got)
                                    for n, dv, got in zip(names, bufs[:k], bufs[k:])])
                item["started"] = _start_copies("scatter_start_" + tag,
                                                [(list(sums) + list(parts), _scatter_plan, 3 * k)])[0]
                made.append(item["started"][2][0])
        after = after + made
        for item in list(self.flying):
            tag, names, k = item["tag"], item["names"], len(item["names"])
            if item["stage"] == 3:
                bufs = _wait_copies("scatter_wait_" + tag, item["started"], _scatter_plan, 3 * k, after)
                halves = [_chip_sum("chip_sum_" + n, self.place, p) for n, p in zip(names, bufs[k:])]
                item["started"] = _start_copies("share_start_" + tag, [(halves, _share_plan, k)])[0]
                made.append(item["started"][2][0])
            elif item["stage"] == 4:
                halves = _wait_copies("share_wait_" + tag, item["started"], _share_plan, k, after)
                for n, h in zip(names, halves):
                    g = h.reshape(self.w[n].shape[1:])
                    self.done[n] = _adamw_layer("adamw_" + n, item["l"], self.w[n], g, self.m[n], self.v[n],
                                                self.done[n])
                    made.append(self.done[n][3])
                self.flying.remove(item)
        return made

    def drain(self, after):
        made = list(after)
        while self.flying:
            made = list(after) + self.advance(made)
        return made


def _step(x, mem, target, w, m, v):
    n_layers = w["w_in"].shape[0]
    seq, d = x.shape
    heads, hd = w["gmlp_v_gain"].shape[1:]
    gw = heads * hd
    groups, pgw = w["w_pool"].shape[1:3]
    pw = groups * pgw
    cw = w["b_dw"].shape[1]
    cb = cw // N_CHIPS
    taps = w["w_dw"].shape[1]
    cx, cy, cc = _me()
    chip = 2 * cx + cy
    place = jnp.stack([cc, chip, 2 * chip + cc]).astype(jnp.int32)
    vec = lambda name, l: w[name][l].reshape(1, -1)

    taps_padded = jnp.pad(w["w_dw"], ((0, 0), (0, (-taps) % 16), (0, 0)))
    gathering = []

    def send_layer(l, last):
        for g, names in enumerate(GATHER_GROUPS):
            bufs = [_halves(_place_shard("place_" + n, place, taps_padded if n == "w_dw" else w[n], l,
                                         F32 if n == "w_dw" else BF16, after=last)) for n in names]
            gathering.extend(_start_copies("gather_start_%d%d" % (l, g), [(bufs, _gather_plan, 3 * len(names))],
                                           after=last))
            last = [gathering[-1][2][0]]
        return last

    def arrive(l, g, after):
        names = GATHER_GROUPS[g]
        tag = "%d%d" % (l, g)
        bufs = _wait_copies("gather_wait_" + tag, gathering[l * len(GATHER_GROUPS) + g], _gather_plan,
                            3 * len(names), after)
        bufs = _exchange("gather_forward_" + tag, bufs, _forward_plan, 3 * len(names))
        out = {}
        for n, b in zip(names, bufs):
            full = b.reshape(N_CHIPS, 2 * b.shape[2], b.shape[3])
            out[n] = full if n in COLUMN_SPLIT + ("w_dw",) else full.reshape(-1, full.shape[2])
        return out

    saved = []
    sent = send_layer(0, [])
    _, h1 = _norm_fwd("norm_first", x, None, None, vec("norm_mix_pre", 0))
    for l in range(n_layers):
        gv = w["gmlp_v_gain"][l].reshape(heads, 1, hd)
        ws = w["w_spatial"][l]
        bcol = w["b_spatial"][l].reshape(heads, -1, 1)
        wl = arrive(l, 0, [h1] + sent)
        z = _mm_nn_col("mm_in", h1, wl["w_in"], [F32])[0]
        sent = send_layer(l + 1, [z]) if l + 1 < n_layers else []
        ya = _gmlp_fwd(z, gv, ws, bcol)
        yb = _pool_fwd(z, w["w_pool"][l], vec("s_pool", l), (2 * gw) // pw)
        hc = _conv_fwd(z, wl["w_dw"], taps, vec("b_dw", l), (2 * gw + pw) // cb, (2 * gw + pw + cw) // cb)
        yc = _ln_swish_fwd(hc, vec("conv_ln_g", l), vec("conv_ln_b", l))
        y = jnp.concatenate([ya, yb, yc], axis=1)
        wl.update(arrive(l, 1, [y] + sent))
        sent = []
        o = _mm_nn_row("mm_out", y, wl["w_out"], [F32])[0]
        x1, h2 = _norm_fwd("norm_mix", x, o, vec("norm_mix_post", l), vec("norm_xattn_pre", l))
        _, mn = _norm_fwd("norm_mem", mem, None, None, vec("norm_mem", l))
        q = _mm_nn_row("mm_q", h2, wl["w_q"], [BF16])[0]
        k = _mm_nn_row("mm_k", mn, wl["w_k"], [BF16])[0]
        vv = _mm_nn_row("mm_v", mn, wl["w_v"], [BF16])[0]
        a = _attn_fwd(q, k, vv)
        o2 = _mm_nn_row("mm_o", a, wl["w_o"], [F32])[0]
        x2, h3 = _norm_fwd("norm_xattn", x1, o2, vec("norm_xattn_post", l), vec("norm_ffn_pre", l))
        wl.update(arrive(l, 2, h3))
        up, r = _mm_nn_col("mm_up", h3, wl["w_up"], [BF16, BF16], epi=_relu2)
        wl.update(arrive(l, 3, r))
        o3 = _mm_nn_row("mm_down", r, wl["w_down"], [F32])[0]
        g_next = vec("norm_mix_pre", l + 1) if l + 1 < n_layers else None
        x3, h_next = _norm_fwd("norm_ffn", x2, o3, vec("norm_ffn_post", l), g_next)
        saved.append(dict(x=x, h1=h1, z=z, hc=hc, y=y, o=o, x1=x1, h2=h2, mn=mn, q=q, k=k, v=vv, a=a, o2=o2, x2=x2,
                          h3=h3, up=up, r=r, o3=o3, x3=x3, gv=gv, ws=ws, bcol=bcol, w=wl))
        x, h1 = x3, h_next

    dx, loss_parts = _loss_head(x, target)
    loss = lax.psum(jnp.sum(loss_parts), ("x", "y", "c"))

    reducer = _GradientReducer(place, w, m, v)
    small = {n: [None] * n_layers for n in SMALL}
    by_chip = lambda g: g.reshape(N_CHIPS, g.shape[0] // N_CHIPS, g.shape[1])
    small_sent = [None] * n_layers

    def small_layer(l):
        return [small[n][l].reshape(w[n].shape[1:]) for n in REPLICATED] + [small["w_dw"][l]]

    def send_small(l):
        landing = _place_flat(place, _pack(small_layer(l)))
        small_sent[l] = _start_copies("small_start_%d" % l, [([landing], _broadcast_plan, N_DEVICES - 1)])[0]
        return [small_sent[l][2][0]]

    dh = None
    made = []
    for l in reversed(range(n_layers)):
        t = saved[l]
        wl = t["w"]
        g_next = vec("norm_mix_pre", l + 1) if l + 1 < n_layers else None
        dx, do3, dgp, dgn = _norm_bwd("norm_ffn_bwd", dx, dh, t["x3"], t["o3"], vec("norm_ffn_post", l), g_next,
                                      after=made)
        small["norm_ffn_post"][l] = dgp
        if dgn is not None:
            small["norm_mix_pre"][l + 1] = dgn
        made = reducer.advance(dx)
        if l + 1 < n_layers:
            made += send_small(l + 1)
        d_down = _mm_tn_row("mm_down_dw", t["r"], do3)
        made += reducer.add("%d0" % l, l, {"w_down": by_chip(d_down)})
        dup = _mm_nt_row("mm_down_dx", do3, wl["w_down"], BF16, epi=_relu2_bwd, extra=t["up"], after=made)
        d_up = _mm_tn_col("mm_up_dw", t["h3"], dup)
        made = reducer.add("%d3" % l, l, {"w_up": d_up})
        dh3 = _mm_nt_col("mm_up_dx", dup, wl["w_up"], F32, after=made)
        made = []
        dx, do2, dgp, dgn = _norm_bwd("norm_xattn_bwd", dx, dh3, t["x2"], t["o2"], vec("norm_xattn_post", l),
                                      vec("norm_ffn_pre", l), after=made)
        small["norm_xattn_post"][l], small["norm_ffn_pre"][l] = dgp, dgn
        made = reducer.advance(dx)
        d_o = _mm_tn_row("mm_o_dw", t["a"], do2)
        da = _mm_nt_row("mm_o_dx", do2, wl["w_o"], BF16, after=made)
        dq, dk, dv = _attn_bwd(t["q"], t["k"], t["v"], da)
        d_q = _mm_tn_row("mm_q_dw", t["h2"], dq)
        d_k = _mm_tn_row("mm_k_dw", t["mn"], dk)
        d_v = _mm_tn_row("mm_v_dw", t["mn"], dv)
        dh2 = _mm_nt_row("mm_q_dx", dq, wl["w_q"], F32)
        dmn = _mm_nt_row("mm_k_dx", dk, wl["w_k"], F32) + _mm_nt_row("mm_v_dx", dv, wl["w_v"], F32)
        _, _, _, small["norm_mem"][l] = _norm_bwd("norm_mem_bwd", None, dmn, mem, None, None, vec("norm_mem", l))
        dx, do, dgp, dgn = _norm_bwd("norm_mix_bwd", dx, dh2, t["x1"], t["o"], vec("norm_mix_post", l),
                                     vec("norm_xattn_pre", l))
        small["norm_mix_post"][l], small["norm_xattn_pre"][l] = dgp, dgn
        made = reducer.advance(dx)
        d_out = _mm_tn_row("mm_out_dw", t["y"], do)
        made += reducer.add("%d1" % l, l, {"w_o": by_chip(d_o), "w_q": by_chip(d_q), "w_k": by_chip(d_k),
                                    "w_v": by_chip(d_v), "w_out": by_chip(d_out)})
        dy = _mm_nt_row("mm_out_dx", do, wl["w_out"], F32, after=made)
        dzu, dzv, dgv, dws, dbcol = _gmlp_bwd(t["z"], dy, t["gv"], t["ws"], t["bcol"])
        small["gmlp_v_gain"][l] = dgv.reshape(heads, hd)
        small["w_spatial"][l] = dws
        small["b_spatial"][l] = dbcol.reshape(heads, -1)
        dzp, dwp, dsp = _pool_bwd(t["z"], dy, w["w_pool"][l], vec("s_pool", l), (2 * gw) // pw, gw // pw)
        small["w_pool"][l], small["s_pool"][l] = dwp, dsp.reshape(-1)
        dhc, dlg, dlb = _ln_swish_bwd(t["hc"], dy, vec("conv_ln_g", l), vec("conv_ln_b", l), (gw + pw) // cw)
        small["conv_ln_g"][l], small["conv_ln_b"][l] = dlg.reshape(-1), dlb.reshape(-1)
        dval, dgate, dwd, dbd = _conv_bwd(t["z"], dhc, wl["w_dw"], taps, (2 * gw + pw) // cb,
                                          (2 * gw + pw + cw) // cb)
        small["w_dw"][l], small["b_dw"][l] = dwd, dbd.reshape(-1)
        dz = jnp.concatenate([dzu, dzv, dzp, dval, dgate], axis=1)
        made = reducer.advance(dz)
        d_in = _mm_tn_col("mm_in_dw", t["h1"], dz)
        made += reducer.add("%d2" % l, l, {"w_in": d_in})
        dh = _mm_nt_col("mm_in_dx", dz, wl["w_in"], F32, after=made)
        made = []
    grad_x, _, _, dgn = _norm_bwd("norm_first_bwd", dx, dh, saved[0]["x"], None, None, vec("norm_mix_pre", 0))
    small["norm_mix_pre"][0] = dgn
    drained = reducer.drain([grad_x] + send_small(0))
    drained = [a for a in drained if all(a is not sent[2][0] for sent in small_sent)]

    like = [w[n] for n in REPLICATED]
    packed = [_pack_layers([p[n] for n in REPLICATED]) for p in (w, m, v)]
    updated, taps_grad = None, []
    for l in range(n_layers):
        landed = _wait_copies("small_wait_%d" % l, small_sent[l], _broadcast_plan, N_DEVICES - 1, drained)[0]
        total = _device_sum(landed)
        updated = _adamw_layer("adamw_small", l, packed[0], total, packed[1], packed[2], updated)
        start = sum(a[0].size for a in like)
        taps_grad.append(total.reshape(-1)[start:start + small["w_dw"][l].size].reshape(small["w_dw"][l].shape))
    grad, delta, new_m, new_v = (dict(zip(REPLICATED, _unpack_layers(u, like))) for u in updated)
    view = lambda a: a.reshape(-1, a.shape[-1])
    taps_grad = lax.dynamic_slice_in_dim(jnp.stack(taps_grad), chip * cb, cb, axis=2)
    updated = _adamw_flat("adamw_taps", view(w["w_dw"]), view(taps_grad), view(m["w_dw"]), view(v["w_dw"]))
    grad["w_dw"] = taps_grad
    delta["w_dw"], new_m["w_dw"], new_v["w_dw"] = (u.reshape(w["w_dw"].shape) for u in updated)
    for n in BIG:
        grad[n], delta[n], new_m[n], new_v[n] = reducer.done[n]

    return (loss, grad_x[None], *[grad[n] for n in WEIGHTS], *[delta[n] for n in WEIGHTS],
            *[new_m[n] for n in WEIGHTS], *[new_v[n] for n in WEIGHTS])


def kernel(x, mem, norm_mix_pre, norm_mix_post, w_in, w_out, gmlp_v_gain, w_spatial, b_spatial, w_pool, s_pool, w_dw, b_dw, conv_ln_g, conv_ln_b, norm_xattn_pre, norm_mem, norm_xattn_post, w_q, w_k, w_v, w_o, norm_ffn_pre, norm_ffn_post, w_up, w_down, loss_target, m_norm_mix_pre, m_norm_mix_post, m_w_in, m_w_out, m_gmlp_v_gain, m_w_spatial, m_b_spatial, m_w_pool, m_s_pool, m_w_dw, m_b_dw, m_conv_ln_g, m_conv_ln_b, m_norm_xattn_pre, m_norm_mem, m_norm_xattn_post, m_w_q, m_w_k, m_w_v, m_w_o, m_norm_ffn_pre, m_norm_ffn_post, m_w_up, m_w_down, v_norm_mix_pre, v_norm_mix_post, v_w_in, v_w_out, v_gmlp_v_gain, v_w_spatial, v_b_spatial, v_w_pool, v_s_pool, v_w_dw, v_b_dw, v_conv_ln_g, v_conv_ln_b, v_norm_xattn_pre, v_norm_mem, v_norm_xattn_post, v_w_q, v_w_k, v_w_v, v_w_o, v_norm_ffn_pre, v_norm_ffn_post, v_w_up, v_w_down):
    given = dict(locals())
    w = {n: given[n] for n in WEIGHTS}
    m = {n: given["m_" + n] for n in WEIGHTS}
    v = {n: given["v_" + n] for n in WEIGHTS}
    return _step(x[0], mem[0], loss_target[0], w, m, v)
```

```python
import functools

import jax
import jax.numpy as jnp
from jax import lax
from jax.experimental import pallas as pl
from jax.experimental.pallas import tpu as pltpu

F32 = jnp.float32
BF16 = jnp.bfloat16
MESH = pl.DeviceIdType.MESH

N_CHIPS = 4
N_DEVICES = 8
XATTN_HEADS = 4
POOL_WINDOWS = (2, 4, 8, 16)
RMS_EPS = 1e-6
LN_EPS = 1e-5
ADAM_LR, ADAM_B1, ADAM_B2, ADAM_EPS, ADAM_WD, ADAM_STEP = 0.001, 0.9, 0.999, 1e-08, 0.01, 10

V7X_LANES = 128
V7X_VMEM_LIMIT = 56 * 1024 * 1024
ROW_TILE = 256
GMLP_UNROLL = 2
MM_TILE_M, MM_TILE_N, MM_TILE_K = 1024, 1024, 2048

ANY = pl.BlockSpec(memory_space=pl.ANY)
HBM = pl.BlockSpec(memory_space=pltpu.HBM)
SEM = pl.BlockSpec(memory_space=pltpu.SEMAPHORE)


def _tile(dim, pref):
    if dim <= pref:
        return dim
    t = (pref // V7X_LANES) * V7X_LANES
    while t >= V7X_LANES:
        if dim % t == 0:
            return t
        t -= V7X_LANES
    return dim


def _params(sem=None):
    return pltpu.CompilerParams(dimension_semantics=sem, vmem_limit_bytes=V7X_VMEM_LIMIT)


NN = (((1,), (0,)), ((), ()))
NT = (((1,), (1,)), ((), ()))
TN = (((0,), (0,)), ((), ()))


def _mm(name, a, b, *, dn, grid, a_spec, b_spec, o_specs, out_shapes, acc_shape, epi=None, extra=None,
        extra_spec=None, after=()):
    nk = grid[2]
    n_out = len(out_shapes)
    has_extra = extra is not None
    after = list(after)

    def body(*refs):
        a_ref, b_ref = refs[0], refs[1]
        pos = 2
        e_ref = None
        if has_extra:
            e_ref = refs[pos]
            pos += 1
        pos += len(after)
        o_refs = refs[pos:pos + n_out]
        part = lax.dot_general(a_ref[...].astype(BF16), b_ref[...].astype(BF16), dn, preferred_element_type=F32)

        def finish(total):
            if epi is None:
                vals = (total,)
            elif has_extra:
                vals = epi(total, e_ref[...])
            else:
                vals = epi(total)
            for o, v in zip(o_refs, vals):
                o[...] = v.astype(o.dtype)

        if nk == 1:
            finish(part)
            return
        acc = refs[pos + n_out]
        k = pl.program_id(2)

        @pl.when(k == 0)
        def _():
            acc[...] = part

        @pl.when(jnp.logical_and(k > 0, k < nk - 1))
        def _():
            acc[...] += part

        @pl.when(k == nk - 1)
        def _():
            finish(acc[...] + part)

    ins, in_specs = [a, b], [a_spec, b_spec]
    if has_extra:
        ins.append(extra)
        in_specs.append(extra_spec)
    ins += after
    in_specs += [ANY] * len(after)
    outs = pl.pallas_call(
        body, name=name, grid=grid, in_specs=in_specs, out_specs=list(o_specs), out_shape=list(out_shapes),
        scratch_shapes=[pltpu.VMEM(acc_shape, F32)] if nk > 1 else [],
        compiler_params=_params(("parallel", "parallel", "arbitrary")))(*ins)
    return outs


def _mm_nn_row(name, a, w, out_dtypes, epi=None, after=()):
    m, k = a.shape
    n = w.shape[1]
    tm, tn, tk = _tile(m, MM_TILE_M), _tile(n, MM_TILE_N), _tile(k, MM_TILE_K)
    o_spec = pl.BlockSpec((tm, tn), lambda i, j, kk: (i, j))
    return _mm(name, a, w, dn=NN, grid=(m // tm, n // tn, k // tk),
               a_spec=pl.BlockSpec((tm, tk), lambda i, j, kk: (i, kk)),
               b_spec=pl.BlockSpec((tk, tn), lambda i, j, kk: (kk, j)),
               o_specs=[o_spec] * len(out_dtypes),
               out_shapes=[jax.ShapeDtypeStruct((m, n), d) for d in out_dtypes], acc_shape=(tm, tn), epi=epi,
               after=after)


def _mm_nn_col(name, a, w, out_dtypes, epi=None):
    m, k = a.shape
    c = w.shape[2]
    tm, tn, tk = _tile(m, MM_TILE_M), _tile(c, MM_TILE_N), _tile(k, MM_TILE_K)
    nb = c // tn
    o_spec = pl.BlockSpec((tm, tn), lambda i, j, kk: (i, j))
    return _mm(name, a, w, dn=NN, grid=(m // tm, N_CHIPS * nb, k // tk),
               a_spec=pl.BlockSpec((tm, tk), lambda i, j, kk: (i, kk)),
               b_spec=pl.BlockSpec((None, tk, tn), lambda i, j, kk: (j // nb, kk, j % nb)),
               o_specs=[o_spec] * len(out_dtypes),
               out_shapes=[jax.ShapeDtypeStruct((m, N_CHIPS * c), d) for d in out_dtypes], acc_shape=(tm, tn),
               epi=epi)


def _mm_nt_row(name, dy, w, out_dtype, epi=None, extra=None, after=()):
    m, n = dy.shape
    k = w.shape[0]
    tm, tn, tk = _tile(m, MM_TILE_M), _tile(k, MM_TILE_N), _tile(n, MM_TILE_K)
    o_spec = pl.BlockSpec((tm, tn), lambda i, j, kk: (i, j))
    return _mm(name, dy, w, dn=NT, grid=(m // tm, k // tn, n // tk),
               a_spec=pl.BlockSpec((tm, tk), lambda i, j, kk: (i, kk)),
               b_spec=pl.BlockSpec((tn, tk), lambda i, j, kk: (j, kk)),
               o_specs=[o_spec], out_shapes=[jax.ShapeDtypeStruct((m, k), out_dtype)], acc_shape=(tm, tn),
               epi=epi, extra=extra, extra_spec=o_spec, after=after)[0]


def _mm_nt_col(name, dy, w, out_dtype, after=()):
    m = dy.shape[0]
    k, c = w.shape[1], w.shape[2]
    tm, tn, tk = _tile(m, MM_TILE_M), _tile(k, MM_TILE_N), _tile(c, MM_TILE_K)
    kb = c // tk
    return _mm(name, dy, w, dn=NT, grid=(m // tm, k // tn, N_CHIPS * kb),
               a_spec=pl.BlockSpec((tm, tk), lambda i, j, kk: (i, kk)),
               b_spec=pl.BlockSpec((None, tn, tk), lambda i, j, kk: (kk // kb, j, kk % kb)),
               o_specs=[pl.BlockSpec((tm, tn), lambda i, j, kk: (i, j))],
               out_shapes=[jax.ShapeDtypeStruct((m, k), out_dtype)], acc_shape=(tm, tn), after=after)[0]


def _mm_tn_row(name, a, dy):
    t, m = a.shape
    n = dy.shape[1]
    tm, tn, tk = _tile(m, MM_TILE_M), _tile(n, MM_TILE_N), _tile(t, MM_TILE_K)
    return _mm(name, a, dy, dn=TN, grid=(m // tm, n // tn, t // tk),
               a_spec=pl.BlockSpec((tk, tm), lambda i, j, kk: (kk, i)),
               b_spec=pl.BlockSpec((tk, tn), lambda i, j, kk: (kk, j)),
               o_specs=[pl.BlockSpec((tm, tn), lambda i, j, kk: (i, j))],
               out_shapes=[jax.ShapeDtypeStruct((m, n), BF16)], acc_shape=(tm, tn))[0]


def _mm_tn_col(name, a, dy):
    t, m = a.shape
    c = dy.shape[1] // N_CHIPS
    tm, tn, tk = _tile(m, MM_TILE_M), _tile(c, MM_TILE_N), _tile(t, MM_TILE_K)
    nb = c // tn
    return _mm(name, a, dy, dn=TN, grid=(m // tm, N_CHIPS * nb, t // tk),
               a_spec=pl.BlockSpec((tk, tm), lambda i, j, kk: (kk, i)),
               b_spec=pl.BlockSpec((tk, tn), lambda i, j, kk: (kk, j)),
               o_specs=[pl.BlockSpec((None, tm, tn), lambda i, j, kk: (j // nb, i, j % nb))],
               out_shapes=[jax.ShapeDtypeStruct((N_CHIPS, m, c), BF16)], acc_shape=(tm, tn))[0]


def _rms(x, g):
    r = lax.rsqrt(jnp.mean(x * x, axis=-1, keepdims=True) + RMS_EPS)
    return x * r * g


def _rms_bwd(x, g, dy):
    r = lax.rsqrt(jnp.mean(x * x, axis=-1, keepdims=True) + RMS_EPS)
    xr = x * r
    dyg = dy * g
    dx = r * (dyg - xr * jnp.mean(dyg * xr, axis=-1, keepdims=True))
    return dx, jnp.sum(dy * xr, axis=0, keepdims=True)


def _norm_fwd(name, x, o, g_post, g_next, after=()):
    after = list(after)
    s, d = x.shape
    tr = _tile(s, ROW_TILE)
    has_prev, has_next = o is not None, g_next is not None
    row = pl.BlockSpec((tr, d), lambda i: (i, 0))
    vec = pl.BlockSpec((1, d), lambda i: (0, 0))

    def body(*refs):
        refs = list(refs)
        xn = refs.pop(0)[...]
        if has_prev:
            o_ref, gp_ref = refs.pop(0), refs.pop(0)
            xn = xn + _rms(o_ref[...], gp_ref[...])
        gn_ref = refs.pop(0) if has_next else None
        del refs[:len(after)]
        if has_prev:
            refs.pop(0)[...] = xn
        if has_next:
            refs.pop(0)[...] = _rms(xn, gn_ref[...]).astype(BF16)

    ins, in_specs = [x], [row]
    if has_prev:
        ins += [o, g_post]
        in_specs += [row, vec]
    if has_next:
        ins.append(g_next)
        in_specs.append(vec)
    ins += after
    in_specs += [ANY] * len(after)
    out_shapes, out_specs = [], []
    if has_prev:
        out_shapes.append(jax.ShapeDtypeStruct((s, d), F32))
        out_specs.append(row)
    if has_next:
        out_shapes.append(jax.ShapeDtypeStruct((s, d), BF16))
        out_specs.append(row)
    outs = pl.pallas_call(body, name=name, grid=(s // tr,), in_specs=in_specs, out_specs=out_specs,
                          out_shape=out_shapes, compiler_params=_params(("parallel",)))(*ins)
    outs = list(outs)
    x_new = outs.pop(0) if has_prev else x
    h = outs.pop(0) if has_next else None
    return x_new, h


def _norm_bwd(name, dxn, dh, xn, o, g_post, g_next, after=()):
    after = list(after)
    s, d = xn.shape
    tr = _tile(s, ROW_TILE)
    has_prev, has_next, has_dxn = o is not None, dh is not None, dxn is not None
    row = pl.BlockSpec((tr, d), lambda i: (i, 0))
    vec = pl.BlockSpec((1, d), lambda i: (0, 0))

    def body(*refs):
        refs = list(refs)
        first = pl.program_id(0) == 0
        dxn_ref = refs.pop(0) if has_dxn else None
        dh_ref = refs.pop(0) if has_next else None
        xn_ref = refs.pop(0)
        if has_prev:
            o_ref, gp_ref = refs.pop(0), refs.pop(0)
        gn_ref = refs.pop(0) if has_next else None
        del refs[:len(after)]
        dx_ref = refs.pop(0)
        if has_prev:
            do_ref, dgp_ref = refs.pop(0), refs.pop(0)
        dgn_ref = refs.pop(0) if has_next else None

        def accumulate(ref, val):
            @pl.when(first)
            def _():
                ref[...] = val

            @pl.when(jnp.logical_not(first))
            def _():
                ref[...] += val

        dx = dxn_ref[...] if has_dxn else None
        if has_next:
            dxh, dgn = _rms_bwd(xn_ref[...], gn_ref[...], dh_ref[...].astype(F32))
            dx = dxh if dx is None else dx + dxh
            accumulate(dgn_ref, dgn)
        dx_ref[...] = dx
        if has_prev:
            do, dgp = _rms_bwd(o_ref[...], gp_ref[...], dx)
            do_ref[...] = do.astype(BF16)
            accumulate(dgp_ref, dgp)

    ins, in_specs = [], []
    if has_dxn:
        ins.append(dxn)
        in_specs.append(row)
    if has_next:
        ins.append(dh)
        in_specs.append(row)
    ins.append(xn)
    in_specs.append(row)
    if has_prev:
        ins += [o, g_post]
        in_specs += [row, vec]
    if has_next:
        ins.append(g_next)
        in_specs.append(vec)
    ins += after
    in_specs += [ANY] * len(after)
    out_shapes, out_specs = [jax.ShapeDtypeStruct((s, d), F32)], [row]
    if has_prev:
        out_shapes += [jax.ShapeDtypeStruct((s, d), BF16), jax.ShapeDtypeStruct((1, d), F32)]
        out_specs += [row, vec]
    if has_next:
        out_shapes.append(jax.ShapeDtypeStruct((1, d), F32))
        out_specs.append(vec)
    outs = list(pl.pallas_call(body, name=name, grid=(s // tr,), in_specs=in_specs, out_specs=out_specs,
                               out_shape=out_shapes, compiler_params=_params(("arbitrary",)))(*ins))
    dx = outs.pop(0)
    do, dgp = (outs.pop(0), outs.pop(0)) if has_prev else (None, None)
    dgn = outs.pop(0) if has_next else None
    return dx, do, dgp, dgn


def _loss_head(y, target):
    s, d = y.shape
    tr = _tile(s, ROW_TILE)
    row = pl.BlockSpec((tr, d), lambda i: (i, 0))
    vec = pl.BlockSpec((1, d), lambda i: (0, 0))

    def body(y_ref, t_ref, dy_ref, l_ref):
        err = y_ref[...] - t_ref[...]
        dy_ref[...] = err * (1.0 / d)
        part = jnp.sum(err * err, axis=0, keepdims=True) * (0.5 / d)

        @pl.when(pl.program_id(0) == 0)
        def _():
            l_ref[...] = part

        @pl.when(pl.program_id(0) != 0)
        def _():
            l_ref[...] += part

    return pl.pallas_call(body, name="loss_head", grid=(s // tr,), in_specs=[row, row], out_specs=[row, vec],
                          out_shape=[jax.ShapeDtypeStruct((s, d), F32), jax.ShapeDtypeStruct((1, d), F32)],
                          compiler_params=_params(("arbitrary",)))(y, target)


@jax.custom_vjp
def _bdot(a, b):
    return jnp.dot(a.astype(BF16), b.astype(BF16), preferred_element_type=F32)


def _bdot_fwd(a, b):
    return _bdot(a, b), (a, b)


def _bdot_bwd(res, ct):
    a, b = res
    ctb = ct.astype(BF16)
    da = lax.dot_general(ctb, b.astype(BF16), NT, preferred_element_type=F32)
    db = lax.dot_general(a.astype(BF16), ctb, TN, preferred_element_type=F32)
    return da, db


_bdot.defvjp(_bdot_fwd, _bdot_bwd)


@functools.partial(jax.custom_vjp, nondiff_argnums=(1,))
def _shift(x, k):
    n = x.shape[0]
    if k == 0:
        return x
    rolled = pltpu.roll(x, k % n, 0)
    t = lax.broadcasted_iota(jnp.int32, x.shape, 0)
    keep = (t >= k) if k > 0 else (t < n + k)
    return jnp.where(keep, rolled, 0.0)


def _shift_fwd(x, k):
    return _shift(x, k), None


def _shift_bwd(k, _, ct):
    return (_shift(ct, -k),)


_shift.defvjp(_shift_fwd, _shift_bwd)


def _sigmoid(x):
    return 1.0 / (1.0 + jnp.exp(-x))


def _layer_norm(x, g, b=None):
    mu = jnp.mean(x, axis=-1, keepdims=True)
    xc = x - mu
    var = jnp.mean(xc * xc, axis=-1, keepdims=True)
    y = xc * lax.rsqrt(var + LN_EPS) * g
    return y if b is None else y + b


def _gmlp_chunk(zu, zv, gv, w, bcol):
    ch = w.shape[0]
    u = jax.nn.gelu(zu)
    vn = _layer_norm(jax.nn.gelu(zv), gv)
    t = lax.broadcasted_iota(jnp.int32, (ch, ch), 0)
    s = lax.broadcasted_iota(jnp.int32, (ch, ch), 1)
    wm = jnp.where(t >= s, w, 0.0)
    return u * (_bdot(wm, vn) + bcol)


def _gmlp_specs(seq, heads, hd, ch, u_off, v_off):
    col = lambda off: pl.BlockSpec((seq, hd), lambda h: (0, off + h))
    return (col(u_off), col(v_off), pl.BlockSpec((None, 1, hd), lambda h: (h, 0, 0)),
            pl.BlockSpec((None, ch, ch), lambda h: (h, 0, 0)), pl.BlockSpec((None, ch, 1), lambda h: (h, 0, 0)))


def _gmlp_fwd(z, gv, ws, bcol):
    seq = z.shape[0]
    heads, _, hd = gv.shape
    ch = ws.shape[-1]
    zu_s, zv_s, gv_s, w_s, b_s = _gmlp_specs(seq, heads, hd, ch, 0, heads)

    def body(zu_ref, zv_ref, gv_ref, w_ref, b_ref, y_ref):
        gvv, w, bc = gv_ref[...], w_ref[...], b_ref[...]

        def step(c, carry):
            rows = pl.ds(pl.multiple_of(c * ch, ch), ch)
            y_ref[rows, :] = _gmlp_chunk(zu_ref[rows, :], zv_ref[rows, :], gvv, w, bc).astype(BF16)
            return carry

        lax.fori_loop(0, seq // ch, step, 0, unroll=GMLP_UNROLL)

    return pl.pallas_call(body, name="gmlp_fwd", grid=(heads,), in_specs=[zu_s, zv_s, gv_s, w_s, b_s],
                          out_specs=pl.BlockSpec((seq, hd), lambda h: (0, h)),
                          out_shape=jax.ShapeDtypeStruct((seq, heads * hd), BF16),
                          compiler_params=_params(("parallel",)))(z, z, gv, ws, bcol)


def _gmlp_bwd(z, dy, gv, ws, bcol):
    seq = z.shape[0]
    heads, _, hd = gv.shape
    ch = ws.shape[-1]
    zu_s, zv_s, gv_s, w_s, b_s = _gmlp_specs(seq, heads, hd, ch, 0, heads)
    col = pl.BlockSpec((seq, hd), lambda h: (0, h))

    def body(zu_ref, zv_ref, dy_ref, gv_ref, w_ref, b_ref, dzu_ref, dzv_ref, dgv_ref, dw_ref, db_ref):
        gvv, w, bc = gv_ref[...], w_ref[...], b_ref[...]

        together = GMLP_UNROLL if (seq // ch) % GMLP_UNROLL == 0 else 1

        def step(c, carry):
            dgv, dw, db = carry
            for u in range(together):
                rows = pl.ds(pl.multiple_of((c * together + u) * ch, ch), ch)
                _, vjp = jax.vjp(_gmlp_chunk, zu_ref[rows, :], zv_ref[rows, :], gvv, w, bc)
                dzu, dzv, dgv_c, dw_c, db_c = vjp(dy_ref[rows, :])
                dzu_ref[rows, :] = dzu.astype(BF16)
                dzv_ref[rows, :] = dzv.astype(BF16)
                dgv, dw, db = dgv + dgv_c, dw + dw_c, db + db_c
            return dgv, dw, db

        zero = (jnp.zeros((1, hd), F32), jnp.zeros((ch, ch), F32), jnp.zeros((ch, 1), F32))
        dgv, dw, db = lax.fori_loop(0, seq // ch // together, step, zero)
        dgv_ref[...] = dgv
        dw_ref[...] = dw
        db_ref[...] = db

    return pl.pallas_call(
        body, name="gmlp_bwd", grid=(heads,), in_specs=[zu_s, zv_s, col, gv_s, w_s, b_s],
        out_specs=[col, col, gv_s, w_s, b_s],
        out_shape=[jax.ShapeDtypeStruct((seq, heads * hd), BF16), jax.ShapeDtypeStruct((seq, heads * hd), BF16),
                   jax.ShapeDtypeStruct(gv.shape, F32), jax.ShapeDtypeStruct(ws.shape, F32),
                   jax.ShapeDtypeStruct(bcol.shape, F32)],
        compiler_params=_params(("parallel",)))(z, z, dy, gv, ws, bcol)


def _pool_group(p, w, s, window):
    win, span = p, 1
    while span < window:
        win = win + _shift(win, span)
        span *= 2
    t = lax.broadcasted_iota(jnp.int32, (p.shape[0], 1), 0).astype(F32)
    cnt = jnp.minimum(t + 1.0, float(window))
    return _bdot(win / cnt - p, w) * s


def _pool_fwd(z, w_pool, s_pool, col_block):
    seq = z.shape[0]
    groups, gw, _ = w_pool.shape
    pw = groups * gw

    def body(p_ref, w_ref, s_ref, y_ref):
        for g in range(groups):
            cols = slice(g * gw, (g + 1) * gw)
            y_ref[:, cols] = _pool_group(p_ref[:, cols], w_ref[g], s_ref[:, cols], POOL_WINDOWS[g]).astype(BF16)

    return pl.pallas_call(
        body, name="pool_fwd", grid=(1,),
        in_specs=[pl.BlockSpec((seq, pw), lambda i: (0, col_block)),
                  pl.BlockSpec((groups, gw, gw), lambda i: (0, 0, 0)), pl.BlockSpec((1, pw), lambda i: (0, 0))],
        out_specs=pl.BlockSpec((seq, pw), lambda i: (0, 0)), out_shape=jax.ShapeDtypeStruct((seq, pw), BF16),
        compiler_params=_params(("arbitrary",)))(z, w_pool, s_pool)


def _pool_bwd(z, dy, w_pool, s_pool, col_block, dy_block):
    seq = z.shape[0]
    groups, gw, _ = w_pool.shape
    pw = groups * gw

    def body(p_ref, dy_ref, w_ref, s_ref, dp_ref, dw_ref, ds_ref):
        for g in range(groups):
            cols = slice(g * gw, (g + 1) * gw)
            _, vjp = jax.vjp(functools.partial(_pool_group, window=POOL_WINDOWS[g]), p_ref[:, cols], w_ref[g],
                             s_ref[:, cols])
            dp, dw, ds = vjp(dy_ref[:, cols])
            dp_ref[:, cols] = dp.astype(BF16)
            dw_ref[g] = dw
            ds_ref[:, cols] = ds

    return pl.pallas_call(
        body, name="pool_bwd", grid=(1,),
        in_specs=[pl.BlockSpec((seq, pw), lambda i: (0, col_block)),
                  pl.BlockSpec((seq, pw), lambda i: (0, dy_block)),
                  pl.BlockSpec((groups, gw, gw), lambda i: (0, 0, 0)), pl.BlockSpec((1, pw), lambda i: (0, 0))],
        out_specs=[pl.BlockSpec((seq, pw), lambda i: (0, 0)), pl.BlockSpec((groups, gw, gw), lambda i: (0, 0, 0)),
                   pl.BlockSpec((1, pw), lambda i: (0, 0))],
        out_shape=[jax.ShapeDtypeStruct((seq, pw), BF16), jax.ShapeDtypeStruct(w_pool.shape, F32),
                   jax.ShapeDtypeStruct((1, pw), F32)],
        compiler_params=_params(("arbitrary",)))(z, dy, w_pool, s_pool)


def _conv_fwd(z, w_dw, taps, b_dw, val_block, gate_block, after=()):
    seq = z.shape[0]
    rows, cb = w_dw.shape[1], w_dw.shape[2]
    after = list(after)

    def body(val_ref, gate_ref, w_ref, b_ref, *refs):
        out_ref = refs[-1]
        h = val_ref[...] * _sigmoid(gate_ref[...])
        acc = jnp.broadcast_to(b_ref[...], h.shape)
        for d in range(taps):
            acc = acc + w_ref[pl.ds(taps - 1 - d, 1), :] * _shift(h, d)
        out_ref[...] = acc

    return pl.pallas_call(
        body, name="conv_fwd", grid=(N_CHIPS,),
        in_specs=[pl.BlockSpec((seq, cb), lambda j: (0, val_block + j)),
                  pl.BlockSpec((seq, cb), lambda j: (0, gate_block + j)),
                  pl.BlockSpec((None, rows, cb), lambda j: (j, 0, 0)),
                  pl.BlockSpec((1, cb), lambda j: (0, j))] + [ANY] * len(after),
        out_specs=pl.BlockSpec((seq, cb), lambda j: (0, j)),
        out_shape=jax.ShapeDtypeStruct((seq, N_CHIPS * cb), F32),
        compiler_params=_params(("parallel",)))(z, z, w_dw, b_dw, *after)


def _conv_bwd(z, dout, w_dw, taps, val_block, gate_block):
    seq = z.shape[0]
    rows, cb = w_dw.shape[1], w_dw.shape[2]
    col = pl.BlockSpec((seq, cb), lambda j: (0, j))

    def body(val_ref, gate_ref, do_ref, w_ref, dval_ref, dgate_ref, dw_ref, db_ref):
        val, sg, do = val_ref[...], _sigmoid(gate_ref[...]), do_ref[...]
        h = val * sg
        db_ref[...] = jnp.sum(do, axis=0, keepdims=True)
        dh = jnp.zeros_like(h)
        for d in range(taps):
            k = taps - 1 - d
            dw_ref[pl.ds(k, 1), :] = jnp.sum(do * _shift(h, d), axis=0, keepdims=True)
            dh = dh + w_ref[pl.ds(k, 1), :] * _shift(do, -d)
        dval_ref[...] = (dh * sg).astype(BF16)
        dgate_ref[...] = (dh * val * sg * (1.0 - sg)).astype(BF16)

    return pl.pallas_call(
        body, name="conv_bwd", grid=(N_CHIPS,),
        in_specs=[pl.BlockSpec((seq, cb), lambda j: (0, val_block + j)),
                  pl.BlockSpec((seq, cb), lambda j: (0, gate_block + j)), col,
                  pl.BlockSpec((None, rows, cb), lambda j: (j, 0, 0))],
        out_specs=[col, col, pl.BlockSpec((taps, cb), lambda j: (0, j)), pl.BlockSpec((1, cb), lambda j: (0, j))],
        out_shape=[jax.ShapeDtypeStruct((seq, N_CHIPS * cb), BF16), jax.ShapeDtypeStruct((seq, N_CHIPS * cb), BF16),
                   jax.ShapeDtypeStruct((taps, N_CHIPS * cb), F32), jax.ShapeDtypeStruct((1, N_CHIPS * cb), F32)],
        compiler_params=_params(("parallel",)))(z, z, dout, w_dw)


def _ln_swish(hc, g, b):
    y = _layer_norm(hc, g, b)
    return y * _sigmoid(y)


def _ln_swish_fwd(hc, g, b):
    s, cw = hc.shape
    tr = _tile(s, ROW_TILE)
    row = pl.BlockSpec((tr, cw), lambda i: (i, 0))
    vec = pl.BlockSpec((1, cw), lambda i: (0, 0))

    def body(h_ref, g_ref, b_ref, y_ref):
        y_ref[...] = _ln_swish(h_ref[...], g_ref[...], b_ref[...]).astype(BF16)

    return pl.pallas_call(body, name="ln_swish_fwd", grid=(s // tr,), in_specs=[row, vec, vec], out_specs=row,
                          out_shape=jax.ShapeDtypeStruct((s, cw), BF16),
                          compiler_params=_params(("parallel",)))(hc, g, b)


def _ln_swish_bwd(hc, dy, g, b, dy_block):
    s, cw = hc.shape
    tr = _tile(s, ROW_TILE)
    row = pl.BlockSpec((tr, cw), lambda i: (i, 0))
    vec = pl.BlockSpec((1, cw), lambda i: (0, 0))

    def body(h_ref, dy_ref, g_ref, b_ref, dh_ref, dg_ref, db_ref):
        _, vjp = jax.vjp(_ln_swish, h_ref[...], g_ref[...], b_ref[...])
        dh, dg, db = vjp(dy_ref[...])
        dh_ref[...] = dh

        @pl.when(pl.program_id(0) == 0)
        def _():
            dg_ref[...] = dg
            db_ref[...] = db

        @pl.when(pl.program_id(0) != 0)
        def _():
            dg_ref[...] += dg
            db_ref[...] += db

    return pl.pallas_call(
        body, name="ln_swish_bwd", grid=(s // tr,),
        in_specs=[row, pl.BlockSpec((tr, cw), lambda i: (i, dy_block)), vec, vec], out_specs=[row, vec, vec],
        out_shape=[jax.ShapeDtypeStruct((s, cw), F32), jax.ShapeDtypeStruct((1, cw), F32),
                   jax.ShapeDtypeStruct((1, cw), F32)],
        compiler_params=_params(("arbitrary",)))(hc, dy, g, b)


def _attn_probs(q, k, scale):
    s = lax.dot_general(q, k, NT, preferred_element_type=F32) * scale
    e = jnp.exp(s - jnp.max(s, axis=-1, keepdims=True))
    return e / jnp.sum(e, axis=-1, keepdims=True)


def _attn_fwd(q, k, v):
    seq, d = q.shape
    mem = k.shape[0]
    hd = d // XATTN_HEADS
    scale = hd ** -0.5
    qs = pl.BlockSpec((seq, hd), lambda h: (0, h))
    ms = pl.BlockSpec((mem, hd), lambda h: (0, h))

    def body(q_ref, k_ref, v_ref, a_ref):
        p = _attn_probs(q_ref[...], k_ref[...], scale)
        a_ref[...] = jnp.dot(p.astype(BF16), v_ref[...], preferred_element_type=F32).astype(BF16)

    return pl.pallas_call(body, name="attn_fwd", grid=(XATTN_HEADS,), in_specs=[qs, ms, ms], out_specs=qs,
                          out_shape=jax.ShapeDtypeStruct((seq, d), BF16),
                          compiler_params=_params(("parallel",)))(q, k, v)


def _attn_bwd(q, k, v, da):
    seq, d = q.shape
    mem = k.shape[0]
    hd = d // XATTN_HEADS
    scale = hd ** -0.5
    qs = pl.BlockSpec((seq, hd), lambda h: (0, h))
    ms = pl.BlockSpec((mem, hd), lambda h: (0, h))

    def body(q_ref, k_ref, v_ref, da_ref, dq_ref, dk_ref, dv_ref):
        q_, k_, v_, da_ = q_ref[...], k_ref[...], v_ref[...], da_ref[...]
        p = _attn_probs(q_, k_, scale)
        dv_ref[...] = lax.dot_general(p.astype(BF16), da_, TN, preferred_element_type=F32).astype(BF16)
        dp = lax.dot_general(da_, v_, NT, preferred_element_type=F32)
        ds = (p * (dp - jnp.sum(dp * p, axis=-1, keepdims=True)) * scale).astype(BF16)
        dq_ref[...] = jnp.dot(ds, k_, preferred_element_type=F32).astype(BF16)
        dk_ref[...] = lax.dot_general(ds, q_, TN, preferred_element_type=F32).astype(BF16)

    return pl.pallas_call(
        body, name="attn_bwd", grid=(XATTN_HEADS,), in_specs=[qs, ms, ms, qs], out_specs=[qs, ms, ms],
        out_shape=[jax.ShapeDtypeStruct((seq, d), BF16), jax.ShapeDtypeStruct((mem, d), BF16),
                   jax.ShapeDtypeStruct((mem, d), BF16)],
        compiler_params=_params(("parallel",)))(q, k, v, da)


def _place_shard(name, place, w, l, dtype, after=()):
    _, r, c = w.shape
    tr = _tile(r, 2 * ROW_TILE) if r % 16 == 0 else r
    after = list(after)

    def body(place_ref, w_ref, *refs):
        refs[-1][...] = w_ref[...].astype(dtype)

    return pl.pallas_call(
        body, name=name,
        grid_spec=pltpu.PrefetchScalarGridSpec(
            num_scalar_prefetch=1, grid=(r // tr,),
            in_specs=[pl.BlockSpec((None, tr, c), lambda i, p: (l, i, 0))] + [ANY] * len(after),
            out_specs=pl.BlockSpec((None, tr, c), lambda i, p: (p[1], i, 0))),
        out_shape=jax.ShapeDtypeStruct((N_CHIPS, r, c), dtype),
        compiler_params=_params(("parallel",)))(place, w, *after)


def _place_flat(place, flat):
    rows, lanes = flat.shape

    def body(place_ref, f_ref, o_ref):
        o_ref[...] = f_ref[...]

    return pl.pallas_call(
        body, name="place_flat",
        grid_spec=pltpu.PrefetchScalarGridSpec(
            num_scalar_prefetch=1, grid=(1,), in_specs=[pl.BlockSpec((rows, lanes), lambda i, p: (0, 0))],
            out_specs=pl.BlockSpec((None, rows, lanes), lambda i, p: (p[2], 0, 0))),
        out_shape=jax.ShapeDtypeStruct((N_DEVICES, rows, lanes), flat.dtype),
        compiler_params=_params(("arbitrary",)))(place, flat)


def _pair_sum(name, place, dw, got):
    n, _, r2, c = dw.shape
    tr = _tile(r2, 2 * ROW_TILE)

    def body(place_ref, own_ref, got_ref, s_ref, t_ref):
        val = (own_ref[...].astype(F32) + got_ref[...].astype(F32)).astype(BF16)
        s_ref[...] = val

        @pl.when(pl.program_id(1) == place_ref[1])
        def _():
            t_ref[...] = val

    slab = pl.BlockSpec((None, tr, c), lambda i, j, p: (j, i, 0))
    sds = jax.ShapeDtypeStruct((n, r2, c), BF16)
    return pl.pallas_call(
        body, name=name,
        grid_spec=pltpu.PrefetchScalarGridSpec(
            num_scalar_prefetch=1, grid=(r2 // tr, n),
            in_specs=[pl.BlockSpec((None, None, tr, c), lambda i, j, p: (j, p[0], i, 0)), slab],
            out_specs=[slab, pl.BlockSpec((None, tr, c), lambda i, j, p: (p[1], i, 0))]),
        out_shape=[sds, sds], compiler_params=_params(("parallel", "arbitrary")))(place, dw, got)


def _chip_sum(name, place, parts):
    n, r2, c = parts.shape
    tr = _tile(r2, ROW_TILE)

    def body(place_ref, *refs):
        o_ref = refs[n]
        acc = refs[0][...].astype(F32)
        for j in range(1, n):
            acc = acc + refs[j][...].astype(F32)
        o_ref[...] = acc

    part = lambda j: pl.BlockSpec((None, tr, c), lambda i, p: (j, i, 0))
    return pl.pallas_call(
        body, name=name,
        grid_spec=pltpu.PrefetchScalarGridSpec(
            num_scalar_prefetch=1, grid=(r2 // tr,), in_specs=[part(j) for j in range(n)],
            out_specs=pl.BlockSpec((None, tr, c), lambda i, p: (p[0], i, 0))),
        out_shape=jax.ShapeDtypeStruct((2, r2, c), F32),
        compiler_params=_params(("parallel",)))(place, *([parts] * n))


def _device_sum(parts):
    n, rows, lanes = parts.shape
    tr = _tile(rows, 4 * ROW_TILE) if rows % 8 == 0 else rows

    def body(p_ref, o_ref):
        acc = p_ref[0]
        for j in range(1, n):
            acc = acc + p_ref[j]
        o_ref[...] = acc

    return pl.pallas_call(
        body, name="device_sum", grid=(rows // tr,), in_specs=[pl.BlockSpec((n, tr, lanes), lambda i: (0, i, 0))],
        out_specs=pl.BlockSpec((tr, lanes), lambda i: (i, 0)), out_shape=jax.ShapeDtypeStruct((rows, lanes), F32),
        compiler_params=_params(("parallel",)))(parts)


def _adam_update(w, g, m, v):
    nm = ADAM_B1 * m + (1.0 - ADAM_B1) * g
    nv = ADAM_B2 * v + (1.0 - ADAM_B2) * (g * g)
    c1 = 1.0 - ADAM_B1 ** ADAM_STEP
    c2 = 1.0 - ADAM_B2 ** ADAM_STEP
    return -ADAM_LR * ((nm / c1) / (jnp.sqrt(nv / c2) + ADAM_EPS) + ADAM_WD * w), nm, nv


def _adamw_layer(name, l, w, g, m, v, prev):
    n_l, r, c = w.shape
    tr = _tile(r, ROW_TILE)
    slab = pl.BlockSpec((None, tr, c), lambda i: (l, i, 0))

    def body(w_ref, g_ref, m_ref, v_ref, *refs):
        go_ref, d_ref, nm_ref, nv_ref = refs[-4:]
        g_ = g_ref[...]
        delta, nm, nv = _adam_update(w_ref[...], g_, m_ref[...], v_ref[...])
        go_ref[...] = g_
        d_ref[...] = delta
        nm_ref[...] = nm
        nv_ref[...] = nv

    ins = [w, g, m, v]
    in_specs = [slab, pl.BlockSpec((tr, c), lambda i: (i, 0)), slab, slab]
    aliases = {}
    if prev is not None:
        aliases = {len(ins) + i: i for i in range(4)}
        ins += list(prev)
        in_specs += [ANY] * 4
    sds = jax.ShapeDtypeStruct((n_l, r, c), F32)
    return pl.pallas_call(body, name=name, grid=(r // tr,), in_specs=in_specs, out_specs=[slab] * 4,
                          out_shape=[sds] * 4, input_output_aliases=aliases,
                          compiler_params=_params(("parallel",)))(*ins)


def _adamw_flat(name, w, g, m, v):
    rows, cols = w.shape
    tr = _tile(rows, ROW_TILE) if rows % 8 == 0 else rows
    spec = pl.BlockSpec((tr, cols), lambda i: (i, 0))

    def body(w_ref, g_ref, m_ref, v_ref, d_ref, nm_ref, nv_ref):
        d_ref[...], nm_ref[...], nv_ref[...] = _adam_update(w_ref[...], g_ref[...], m_ref[...], v_ref[...])

    sds = jax.ShapeDtypeStruct((rows, cols), F32)
    return pl.pallas_call(body, name=name, grid=(rows // tr,), in_specs=[spec] * 4, out_specs=[spec] * 3,
                          out_shape=[sds] * 3, compiler_params=_params(("parallel",)))(w, g, m, v)


def _me():
    return lax.axis_index("x"), lax.axis_index("y"), lax.axis_index("c")


def _other_chips(x, y):
    return [(1 - x, y, 2 * (1 - x) + y), (x, 1 - y, 2 * x + 1 - y), (1 - x, 1 - y, 2 * (1 - x) + 1 - y)]


def _remote(src, dst, send_sem, recv_sem, target):
    return pltpu.make_async_remote_copy(src_ref=src, dst_ref=dst, send_sem=send_sem, recv_sem=recv_sem,
                                        device_id=target, device_id_type=MESH)


def _exchange(name, bufs, plan, n_copies):
    n = len(bufs)

    def body(*refs):
        send_sems, recv_sems = refs[2 * n:]
        copies = []
        for i, (src, dst, target) in enumerate(plan(refs[n:2 * n], _me())):
            if target is None:
                cp = pltpu.make_async_copy(src, dst, send_sems.at[i])
            else:
                cp = _remote(src, dst, send_sems.at[i], recv_sems.at[i], target)
            cp.start()
            copies.append((cp, target))
        assert len(copies) == n_copies
        for cp, target in copies:
            if target is None:
                cp.wait()
            else:
                cp.wait_recv()
        for cp, target in copies:
            if target is not None:
                cp.wait_send()

    return pl.pallas_call(
        body, name=name, in_specs=[ANY] * n, out_specs=[ANY] * n,
        out_shape=[jax.ShapeDtypeStruct(b.shape, b.dtype) for b in bufs],
        scratch_shapes=[pltpu.SemaphoreType.DMA((n_copies,)), pltpu.SemaphoreType.DMA((n_copies,))],
        input_output_aliases={i: i for i in range(n)},
        compiler_params=pltpu.CompilerParams(has_side_effects=True))(*bufs)


def _start_copies(name, groups, after=()):
    all_bufs = [b for bufs, _, _ in groups for b in bufs]
    after = list(after)
    n = len(all_bufs)
    n_g = len(groups)

    def body(*refs):
        in_refs, sem_refs = refs[:n], refs[n + len(after):n + len(after) + 2 * n_g]
        pos = 0
        for g, (bufs, plan, n_copies) in enumerate(groups):
            copies = plan(in_refs[pos:pos + len(bufs)], _me())
            assert len(copies) == n_copies
            for i, (src, dst, target) in enumerate(copies):
                _remote(src, dst, sem_refs[2 * g].at[i], sem_refs[2 * g + 1].at[i], target).start()
            pos += len(bufs)

    sems = []
    for _, _, n_copies in groups:
        sems += [pltpu.SemaphoreType.DMA((n_copies,))] * 2
    outs = pl.pallas_call(
        body, name=name, in_specs=[HBM] * n + [ANY] * len(after), out_specs=[SEM] * (2 * n_g) + [HBM] * n,
        out_shape=sems + [pltpu.HBM(b.shape, b.dtype) for b in all_bufs],
        input_output_aliases={i: 2 * n_g + i for i in range(n)},
        compiler_params=pltpu.CompilerParams(has_side_effects=pltpu.SideEffectType.DATAFLOW_SIDE_EFFECTING))(
            *[pltpu.with_memory_space_constraint(b, pltpu.HBM) for b in all_bufs], *after)
    result, pos = [], 2 * n_g
    for g, (bufs, _, _) in enumerate(groups):
        result.append((outs[2 * g], outs[2 * g + 1], list(outs[pos:pos + len(bufs)])))
        pos += len(bufs)
    return result


def _wait_copies(name, started, plan, n_copies, after):
    send_sems, recv_sems, bufs = started
    n = len(bufs)
    after = list(after) if isinstance(after, (list, tuple)) else [after]
    after = [a for a in after if all(a is not b for b in bufs)]

    def body(*refs):
        copies = plan(refs[:n], _me())
        assert len(copies) == n_copies
        for i, (src, dst, target) in enumerate(copies):
            cp = _remote(src, dst, refs[n].at[i], refs[n + 1].at[i], target)
            cp.wait_send()
            cp.wait_recv()

    return list(pl.pallas_call(
        body, name=name, in_specs=[HBM] * n + [SEM, SEM] + [ANY] * len(after), out_specs=[HBM] * n,
        out_shape=[pltpu.HBM(b.shape, b.dtype) for b in bufs], input_output_aliases={i: i for i in range(n)},
        compiler_params=pltpu.CompilerParams(has_side_effects=pltpu.SideEffectType.DATAFLOW_SIDE_EFFECTING))(
            *bufs, send_sems, recv_sems, *after))


def _halves(a):
    return a.reshape(a.shape[0], 2, a.shape[1] // 2, a.shape[2])


def _neighbour_slab(x, y, c):
    across_x, across_y = 2 * (1 - x) + y, 2 * x + 1 - y
    return across_x + c * (across_y - across_x)


def _direct_plan(refs, me):
    x, y, c = me
    mine = 2 * x + y
    target = (x + (1 - c) * (1 - 2 * x), y + c * (1 - 2 * y), c)
    return [(g.at[mine], g.at[mine], target) for g in refs]


def _relay_plan(refs, me):
    x, y, c = me
    src = _neighbour_slab(x, y, c)
    onward = (x + c * (1 - 2 * x), y + (1 - c) * (1 - 2 * y), c)
    copies = []
    for g in refs:
        copies += [(g.at[src, c], g.at[src, c], onward), (g.at[src], g.at[src], (x, y, 1 - c))]
    return copies


def _last_plan(refs, me):
    x, y, c = me
    far = 2 * (1 - x) + 1 - y
    return [(g.at[far, c], g.at[far, c], (x, y, 1 - c)) for g in refs]


def _swap_plan(refs, me):
    x, y, c = me
    k = len(refs) // 2
    return [(refs[i].at[j, 1 - c], refs[k + i].at[j], (x, y, 1 - c)) for i in range(k) for j in range(N_CHIPS)]


def _scatter_plan(refs, me):
    x, y, c = me
    mine = 2 * x + y
    k = len(refs) // 2
    return [(refs[i].at[chip], refs[k + i].at[mine], (px, py, c))
            for i in range(k) for px, py, chip in _other_chips(x, y)]


def _share_plan(refs, me):
    x, y, c = me
    return [(g.at[c], g.at[c], (x, y, 1 - c)) for g in refs]


def _broadcast_plan(refs, me):
    x, y, c = me
    mine = 4 * x + 2 * y + c
    copies = []
    for fx, fy, fc in [(0, 0, 1), (0, 1, 0), (0, 1, 1), (1, 0, 0), (1, 0, 1), (1, 1, 0), (1, 1, 1)]:
        peer = (x + fx - 2 * fx * x, y + fy - 2 * fy * y, c + fc - 2 * fc * c)
        copies.append((refs[0].at[mine], refs[0].at[mine], peer))
    return copies


BIG = ("w_in", "w_out", "w_q", "w_k", "w_v", "w_o", "w_up", "w_down")
COLUMN_SPLIT = ("w_in", "w_up")
WEIGHTS = ("norm_mix_pre", "norm_mix_post", "w_in", "w_out", "gmlp_v_gain", "w_spatial", "b_spatial", "w_pool",
           "s_pool", "w_dw", "b_dw", "conv_ln_g", "conv_ln_b", "norm_xattn_pre", "norm_mem", "norm_xattn_post",
           "w_q", "w_k", "w_v", "w_o", "norm_ffn_pre", "norm_ffn_post", "w_up", "w_down")
SMALL = tuple(n for n in WEIGHTS if n not in BIG)
REPLICATED = tuple(n for n in SMALL if n != "w_dw")
GATHER_GROUPS = (("w_in", "w_dw"), ("w_out", "w_q", "w_k", "w_v", "w_o"), ("w_up",), ("w_down",))


def _relu2(acc):
    r = jnp.maximum(acc, 0.0)
    return acc, r * r


def _relu2_bwd(acc, up):
    return (acc * (2.0 * jnp.maximum(up, 0.0)),)


def _pack(arrays):
    flat = jnp.concatenate([a.reshape(-1) for a in arrays])
    tile = 8 * V7X_LANES
    pad = (-flat.shape[0]) % tile
    return jnp.pad(flat, (0, pad)).reshape(-1, V7X_LANES)


def _pack_layers(arrays):
    n_l = arrays[0].shape[0]
    flat = jnp.concatenate([a.reshape(n_l, -1) for a in arrays], axis=1)
    pad = (-flat.shape[1]) % (8 * V7X_LANES)
    return jnp.pad(flat, ((0, 0), (0, pad))).reshape(n_l, -1, V7X_LANES)


def _unpack_layers(packed, like):
    flat = packed.reshape(packed.shape[0], -1)
    out, pos = [], 0
    for a in like:
        out.append(flat[:, pos:pos + a[0].size].reshape(a.shape))
        pos += a[0].size
    return out


class _GradientReducer:
    def __init__(self, place, w, m, v):
        self.place, self.w, self.m, self.v = place, w, m, v
        self.flying = []
        self.done = {n: None for n in BIG}

    def add(self, tag, l, grads):
        names = list(grads)
        views = [_halves(grads[n]) for n in names]
        zones = [lax.empty((v.shape[0],) + v.shape[2:], v.dtype) for v in views]
        started = _start_copies("swap_start_" + tag, [(views + zones, _swap_plan, N_CHIPS * len(names))])[0]
        self.flying.append(dict(stage=0, tag=tag, l=l, names=names, started=started))
        return [started[2][0]]

    def advance(self, after):
        made = []
        after = list(after) if isinstance(after, (list, tuple)) else [after]
        for item in self.flying:
            item["stage"] += 1
        for item in self.flying:
            tag, names, k = item["tag"], item["names"], len(item["names"])
            if item["stage"] == 1:
                bufs = _wait_copies("swap_wait_" + tag, item["started"], _swap_plan, N_CHIPS * k, after)
                sums, parts = zip(*[_pair_sum("pair_sum_" + n, self.place, dv, got)
                                    for n, dv, got in zip(names, bufs[:k], bufs[k:])])
                item["started"] = _start_copies("scatter_start_" + tag,
                                                [(list(sums) + list(parts), _scatter_plan, 3 * k)])[0]
                made.append(item["started"][2][0])
        after = after + made
        for item in list(self.flying):
            tag, names, k = item["tag"], item["names"], len(item["names"])
            if item["stage"] == 3:
                bufs = _wait_copies("scatter_wait_" + tag, item["started"], _scatter_plan, 3 * k, after)
                halves = [_chip_sum("chip_sum_" + n, self.place, p) for n, p in zip(names, bufs[k:])]
                item["started"] = _start_copies("share_start_" + tag, [(halves, _share_plan, k)])[0]
                made.append(item["started"][2][0])
            elif item["stage"] == 4:
                halves = _wait_copies("share_wait_" + tag, item["started"], _share_plan, k, after)
                for n, h in zip(names, halves):
                    g = h.reshape(self.w[n].shape[1:])
                    self.done[n] = _adamw_layer("adamw_" + n, item["l"], self.w[n], g, self.m[n], self.v[n],
                                                self.done[n])
                    made.append(self.done[n][3])
                self.flying.remove(item)
        return made

    def drain(self, after):
        made = list(after)
        while self.flying:
            made = list(after) + self.advance(made)
        return made


def _step(x, mem, target, w, m, v):
    n_layers = w["w_in"].shape[0]
    seq, d = x.shape
    heads, hd = w["gmlp_v_gain"].shape[1:]
    gw = heads * hd
    groups, pgw = w["w_pool"].shape[1:3]
    pw = groups * pgw
    cw = w["b_dw"].shape[1]
    cb = cw // N_CHIPS
    taps = w["w_dw"].shape[1]
    cx, cy, cc = _me()
    chip = 2 * cx + cy
    place = jnp.stack([cc, chip, 2 * chip + cc]).astype(jnp.int32)
    vec = lambda name, l: w[name][l].reshape(1, -1)

    taps_padded = jnp.pad(w["w_dw"], ((0, 0), (0, (-taps) % 16), (0, 0)))
    gathering = []

    def send_layer(l, last):
        for g, names in enumerate(GATHER_GROUPS):
            bufs = [_halves(_place_shard("place_" + n, place, taps_padded if n == "w_dw" else w[n], l,
                                         F32 if n == "w_dw" else BF16, after=last)) for n in names]
            gathering.extend(_start_copies("gather_start_%d%d" % (l, g), [(bufs, _direct_plan, len(names))],
                                           after=last))
            last = [gathering[-1][2][0]]
        return last

    def relay(l, g, after):
        names = GATHER_GROUPS[g]
        at = l * len(GATHER_GROUPS) + g
        bufs = _wait_copies("gather_wait_%d%d" % (l, g), gathering[at], _direct_plan, len(names), after)
        gathering[at] = _start_copies("relay_start_%d%d" % (l, g), [(bufs, _relay_plan, 2 * len(names))])[0]
        return [gathering[at][2][0]]

    def arrive(l, g, after):
        names = GATHER_GROUPS[g]
        tag = "%d%d" % (l, g)
        bufs = _wait_copies("relay_wait_" + tag, gathering[l * len(GATHER_GROUPS) + g], _relay_plan,
                            2 * len(names), after)
        bufs = _exchange("gather_last_" + tag, bufs, _last_plan, len(names))
        out = {}
        for n, b in zip(names, bufs):
            full = b.reshape(N_CHIPS, 2 * b.shape[2], b.shape[3])
            out[n] = full if n in COLUMN_SPLIT + ("w_dw",) else full.reshape(-1, full.shape[2])
        return out

    saved = []
    sent = send_layer(0, [])
    _, h1 = _norm_fwd("norm_first", x, None, None, vec("norm_mix_pre", 0))
    relay(0, 0, [h1] + sent)
    for l in range(n_layers):
        gv = w["gmlp_v_gain"][l].reshape(heads, 1, hd)
        ws = w["w_spatial"][l]
        bcol = w["b_spatial"][l].reshape(heads, -1, 1)
        wl = arrive(l, 0, [h1])
        z = _mm_nn_col("mm_in", h1, wl["w_in"], [F32])[0]
        sent = send_layer(l + 1, [z]) if l + 1 < n_layers else []
        ya = _gmlp_fwd(z, gv, ws, bcol)
        yb = _pool_fwd(z, w["w_pool"][l], vec("s_pool", l), (2 * gw) // pw)
        hc = _conv_fwd(z, wl["w_dw"], taps, vec("b_dw", l), (2 * gw + pw) // cb, (2 * gw + pw + cw) // cb,
                       after=relay(l, 1, [ya]))
        yc = _ln_swish_fwd(hc, vec("conv_ln_g", l), vec("conv_ln_b", l))
        y = jnp.concatenate([ya, yb, yc], axis=1)
        wl.update(arrive(l, 1, [y] + sent))
        o = _mm_nn_row("mm_out", y, wl["w_out"], [F32])[0]
        x1, h2 = _norm_fwd("norm_mix", x, o, vec("norm_mix_post", l), vec("norm_xattn_pre", l),
                           after=relay(l, 2, [o]))
        _, mn = _norm_fwd("norm_mem", mem, None, None, vec("norm_mem", l))
        q = _mm_nn_row("mm_q", h2, wl["w_q"], [BF16])[0]
        k = _mm_nn_row("mm_k", mn, wl["w_k"], [BF16])[0]
        vv = _mm_nn_row("mm_v", mn, wl["w_v"], [BF16])[0]
        a = _attn_fwd(q, k, vv)
        o2 = _mm_nn_row("mm_o", a, wl["w_o"], [F32], after=relay(l, 3, [a]))[0]
        x2, h3 = _norm_fwd("norm_xattn", x1, o2, vec("norm_xattn_post", l), vec("norm_ffn_pre", l))
        wl.update(arrive(l, 2, h3))
        up, r = _mm_nn_col("mm_up", h3, wl["w_up"], [BF16, BF16], epi=_relu2)
        wl.update(arrive(l, 3, r))
        o3 = _mm_nn_row("mm_down", r, wl["w_down"], [F32],
                        after=relay(l + 1, 0, [r]) if l + 1 < n_layers else [])[0]
        g_next = vec("norm_mix_pre", l + 1) if l + 1 < n_layers else None
        x3, h_next = _norm_fwd("norm_ffn", x2, o3, vec("norm_ffn_post", l), g_next)
        saved.append(dict(x=x, h1=h1, z=z, hc=hc, y=y, o=o, x1=x1, h2=h2, mn=mn, q=q, k=k, v=vv, a=a, o2=o2, x2=x2,
                          h3=h3, up=up, r=r, o3=o3, x3=x3, gv=gv, ws=ws, bcol=bcol, w=wl))
        x, h1 = x3, h_next

    dx, loss_parts = _loss_head(x, target)
    loss = lax.psum(jnp.sum(loss_parts), ("x", "y", "c"))

    reducer = _GradientReducer(place, w, m, v)
    small = {n: [None] * n_layers for n in SMALL}
    by_chip = lambda g: g.reshape(N_CHIPS, g.shape[0] // N_CHIPS, g.shape[1])
    small_sent = [None] * n_layers

    def small_layer(l):
        return [small[n][l].reshape(w[n].shape[1:]) for n in REPLICATED] + [small["w_dw"][l]]

    def send_small(l):
        landing = _place_flat(place, _pack(small_layer(l)))
        small_sent[l] = _start_copies("small_start_%d" % l, [([landing], _broadcast_plan, N_DEVICES - 1)])[0]
        return [small_sent[l][2][0]]

    dh = None
    made = []
    for l in reversed(range(n_layers)):
        t = saved[l]
        wl = t["w"]
        g_next = vec("norm_mix_pre", l + 1) if l + 1 < n_layers else None
        dx, do3, dgp, dgn = _norm_bwd("norm_ffn_bwd", dx, dh, t["x3"], t["o3"], vec("norm_ffn_post", l), g_next,
                                      after=made)
        small["norm_ffn_post"][l] = dgp
        if dgn is not None:
            small["norm_mix_pre"][l + 1] = dgn
        made = reducer.advance(dx)
        if l + 1 < n_layers:
            made += send_small(l + 1)
        d_down = _mm_tn_row("mm_down_dw", t["r"], do3)
        made += reducer.add("%d0" % l, l, {"w_down": by_chip(d_down)})
        dup = _mm_nt_row("mm_down_dx", do3, wl["w_down"], BF16, epi=_relu2_bwd, extra=t["up"], after=made)
        d_up = _mm_tn_col("mm_up_dw", t["h3"], dup)
        made = reducer.add("%d3" % l, l, {"w_up": d_up})
        dh3 = _mm_nt_col("mm_up_dx", dup, wl["w_up"], F32, after=made)
        made = []
        dx, do2, dgp, dgn = _norm_bwd("norm_xattn_bwd", dx, dh3, t["x2"], t["o2"], vec("norm_xattn_post", l),
                                      vec("norm_ffn_pre", l), after=made)
        small["norm_xattn_post"][l], small["norm_ffn_pre"][l] = dgp, dgn
        made = reducer.advance(dx)
        d_o = _mm_tn_row("mm_o_dw", t["a"], do2)
        da = _mm_nt_row("mm_o_dx", do2, wl["w_o"], BF16, after=made)
        dq, dk, dv = _attn_bwd(t["q"], t["k"], t["v"], da)
        d_q = _mm_tn_row("mm_q_dw", t["h2"], dq)
        d_k = _mm_tn_row("mm_k_dw", t["mn"], dk)
        d_v = _mm_tn_row("mm_v_dw", t["mn"], dv)
        dh2 = _mm_nt_row("mm_q_dx", dq, wl["w_q"], F32)
        dmn = _mm_nt_row("mm_k_dx", dk, wl["w_k"], F32) + _mm_nt_row("mm_v_dx", dv, wl["w_v"], F32)
        _, _, _, small["norm_mem"][l] = _norm_bwd("norm_mem_bwd", None, dmn, mem, None, None, vec("norm_mem", l))
        dx, do, dgp, dgn = _norm_bwd("norm_mix_bwd", dx, dh2, t["x1"], t["o"], vec("norm_mix_post", l),
                                     vec("norm_xattn_pre", l))
        small["norm_mix_post"][l], small["norm_xattn_pre"][l] = dgp, dgn
        made = reducer.advance(dx)
        d_out = _mm_tn_row("mm_out_dw", t["y"], do)
        made += reducer.add("%d1" % l, l, {"w_o": by_chip(d_o), "w_q": by_chip(d_q), "w_k": by_chip(d_k),
                                    "w_v": by_chip(d_v), "w_out": by_chip(d_out)})
        dy = _mm_nt_row("mm_out_dx", do, wl["w_out"], F32, after=made)
        dzu, dzv, dgv, dws, dbcol = _gmlp_bwd(t["z"], dy, t["gv"], t["ws"], t["bcol"])
        small["gmlp_v_gain"][l] = dgv.reshape(heads, hd)
        small["w_spatial"][l] = dws
        small["b_spatial"][l] = dbcol.reshape(heads, -1)
        dzp, dwp, dsp = _pool_bwd(t["z"], dy, w["w_pool"][l], vec("s_pool", l), (2 * gw) // pw, gw // pw)
        small["w_pool"][l], small["s_pool"][l] = dwp, dsp.reshape(-1)
        dhc, dlg, dlb = _ln_swish_bwd(t["hc"], dy, vec("conv_ln_g", l), vec("conv_ln_b", l), (gw + pw) // cw)
        small["conv_ln_g"][l], small["conv_ln_b"][l] = dlg.reshape(-1), dlb.reshape(-1)
        dval, dgate, dwd, dbd = _conv_bwd(t["z"], dhc, wl["w_dw"], taps, (2 * gw + pw) // cb,
                                          (2 * gw + pw + cw) // cb)
        small["w_dw"][l], small["b_dw"][l] = dwd, dbd.reshape(-1)
        dz = jnp.concatenate([dzu, dzv, dzp, dval, dgate], axis=1)
        made = reducer.advance(dz)
        d_in = _mm_tn_col("mm_in_dw", t["h1"], dz)
        made += reducer.add("%d2" % l, l, {"w_in": d_in})
        dh = _mm_nt_col("mm_in_dx", dz, wl["w_in"], F32, after=made)
        made = []
    grad_x, _, _, dgn = _norm_bwd("norm_first_bwd", dx, dh, saved[0]["x"], None, None, vec("norm_mix_pre", 0))
    small["norm_mix_pre"][0] = dgn
    drained = reducer.drain([grad_x] + send_small(0))
    drained = [a for a in drained if all(a is not sent[2][0] for sent in small_sent)]

    like = [w[n] for n in REPLICATED]
    packed = [_pack_layers([p[n] for n in REPLICATED]) for p in (w, m, v)]
    updated, taps_grad = None, []
    for l in range(n_layers):
        landed = _wait_copies("small_wait_%d" % l, small_sent[l], _broadcast_plan, N_DEVICES - 1, drained)[0]
        total = _device_sum(landed)
        updated = _adamw_layer("adamw_small", l, packed[0], total, packed[1], packed[2], updated)
        start = sum(a[0].size for a in like)
        taps_grad.append(total.reshape(-1)[start:start + small["w_dw"][l].size].reshape(small["w_dw"][l].shape))
    grad, delta, new_m, new_v = (dict(zip(REPLICATED, _unpack_layers(u, like))) for u in updated)
    view = lambda a: a.reshape(-1, a.shape[-1])
    taps_grad = lax.dynamic_slice_in_dim(jnp.stack(taps_grad), chip * cb, cb, axis=2)
    updated = _adamw_flat("adamw_taps", view(w["w_dw"]), view(taps_grad), view(m["w_dw"]), view(v["w_dw"]))
    grad["w_dw"] = taps_grad
    delta["w_dw"], new_m["w_dw"], new_v["w_dw"] = (u.reshape(w["w_dw"].shape) for u in updated)
    for n in BIG:
        grad[n], delta[n], new_m[n], new_v[n] = reducer.done[n]

    return (loss, grad_x[None], *[grad[n] for n in WEIGHTS], *[delta[n] for n in WEIGHTS],
            *[new_m[n] for n in WEIGHTS], *[new_v[n] for n in WEIGHTS])


def kernel(x, mem, norm_mix_pre, norm_mix_post, w_in, w_out, gmlp_v_gain, w_spatial, b_spatial, w_pool, s_pool, w_dw, b_dw, conv_ln_g, conv_ln_b, norm_xattn_pre, norm_mem, norm_xattn_post, w_q, w_k, w_v, w_o, norm_ffn_pre, norm_ffn_post, w_up, w_down, loss_target, m_norm_mix_pre, m_norm_mix_post, m_w_in, m_w_out, m_gmlp_v_gain, m_w_spatial, m_b_spatial, m_w_pool, m_s_pool, m_w_dw, m_b_dw, m_conv_ln_g, m_conv_ln_b, m_norm_xattn_pre, m_norm_mem, m_norm_xattn_post, m_w_q, m_w_k, m_w_v, m_w_o, m_norm_ffn_pre, m_norm_ffn_post, m_w_up, m_w_down, v_norm_mix_pre, v_norm_mix_post, v_w_in, v_w_out, v_gmlp_v_gain, v_w_spatial, v_b_spatial, v_w_pool, v_s_pool, v_w_dw, v_b_dw, v_conv_ln_g, v_conv_ln_b, v_norm_xattn_pre, v_norm_mem, v_norm_xattn_post, v_w_q, v_w_k, v_w_v, v_w_o, v_norm_ffn_pre, v_norm_ffn_post, v_w_up, v_w_down):
    given = dict(locals())
    w = {n: given[n] for n in WEIGHTS}
    m = {n: given["m_" + n] for n in WEIGHTS}
    v = {n: given["v_" + n] for n in WEIGHTS}
    return _step(x[0], mem[0], loss_target[0], w, m, v)
```

```python
import functools

import jax
import jax.numpy as jnp
from jax import lax
from jax.experimental import pallas as pl
from jax.experimental.pallas import tpu as pltpu

F32 = jnp.float32
BF16 = jnp.bfloat16
MESH = pl.DeviceIdType.MESH

N_CHIPS = 4
N_DEVICES = 8
XATTN_HEADS = 4
POOL_WINDOWS = (2, 4, 8, 16)
RMS_EPS = 1e-6
LN_EPS = 1e-5
ADAM_LR, ADAM_B1, ADAM_B2, ADAM_EPS, ADAM_WD, ADAM_STEP = 0.001, 0.9, 0.999, 1e-08, 0.01, 10

V7X_LANES = 128
V7X_VMEM_LIMIT = 56 * 1024 * 1024
ROW_TILE = 256
GMLP_UNROLL = 4
MM_TILE_M, MM_TILE_N, MM_TILE_K = 1024, 1024, 2048

ANY = pl.BlockSpec(memory_space=pl.ANY)
HBM = pl.BlockSpec(memory_space=pltpu.HBM)
SEM = pl.BlockSpec(memory_space=pltpu.SEMAPHORE)


def _tile(dim, pref):
    if dim <= pref:
        return dim
    t = (pref // V7X_LANES) * V7X_LANES
    while t >= V7X_LANES:
        if dim % t == 0:
            return t
        t -= V7X_LANES
    return dim


def _params(sem=None):
    return pltpu.CompilerParams(dimension_semantics=sem, vmem_limit_bytes=V7X_VMEM_LIMIT)


NN = (((1,), (0,)), ((), ()))
NT = (((1,), (1,)), ((), ()))
TN = (((0,), (0,)), ((), ()))


def _mm(name, a, b, *, dn, grid, a_spec, b_spec, o_specs, out_shapes, acc_shape, epi=None, extra=None,
        extra_spec=None, after=()):
    nk = grid[2]
    n_out = len(out_shapes)
    has_extra = extra is not None
    after = list(after)

    def body(*refs):
        a_ref, b_ref = refs[0], refs[1]
        pos = 2
        e_ref = None
        if has_extra:
            e_ref = refs[pos]
            pos += 1
        pos += len(after)
        o_refs = refs[pos:pos + n_out]
        part = lax.dot_general(a_ref[...].astype(BF16), b_ref[...].astype(BF16), dn, preferred_element_type=F32)

        def finish(total):
            if epi is None:
                vals = (total,)
            elif has_extra:
                vals = epi(total, e_ref[...])
            else:
                vals = epi(total)
            for o, v in zip(o_refs, vals):
                o[...] = v.astype(o.dtype)

        if nk == 1:
            finish(part)
            return
        acc = refs[pos + n_out]
        k = pl.program_id(2)

        @pl.when(k == 0)
        def _():
            acc[...] = part

        @pl.when(jnp.logical_and(k > 0, k < nk - 1))
        def _():
            acc[...] += part

        @pl.when(k == nk - 1)
        def _():
            finish(acc[...] + part)

    ins, in_specs = [a, b], [a_spec, b_spec]
    if has_extra:
        ins.append(extra)
        in_specs.append(extra_spec)
    ins += after
    in_specs += [ANY] * len(after)
    outs = pl.pallas_call(
        body, name=name, grid=grid, in_specs=in_specs, out_specs=list(o_specs), out_shape=list(out_shapes),
        scratch_shapes=[pltpu.VMEM(acc_shape, F32)] if nk > 1 else [],
        compiler_params=_params(("parallel", "parallel", "arbitrary")))(*ins)
    return outs


def _mm_nn_row(name, a, w, out_dtypes, epi=None, after=()):
    m, k = a.shape
    n = w.shape[1]
    tm, tn, tk = _tile(m, MM_TILE_M), _tile(n, MM_TILE_N), _tile(k, MM_TILE_K)
    o_spec = pl.BlockSpec((tm, tn), lambda i, j, kk: (i, j))
    return _mm(name, a, w, dn=NN, grid=(m // tm, n // tn, k // tk),
               a_spec=pl.BlockSpec((tm, tk), lambda i, j, kk: (i, kk)),
               b_spec=pl.BlockSpec((tk, tn), lambda i, j, kk: (kk, j)),
               o_specs=[o_spec] * len(out_dtypes),
               out_shapes=[jax.ShapeDtypeStruct((m, n), d) for d in out_dtypes], acc_shape=(tm, tn), epi=epi,
               after=after)


def _mm_nn_col(name, a, w, out_dtypes, epi=None, after=()):
    m, k = a.shape
    c = w.shape[2]
    tm, tn, tk = _tile(m, MM_TILE_M), _tile(c, MM_TILE_N), _tile(k, MM_TILE_K)
    nb = c // tn
    o_spec = pl.BlockSpec((tm, tn), lambda i, j, kk: (i, j))
    return _mm(name, a, w, dn=NN, grid=(m // tm, N_CHIPS * nb, k // tk),
               a_spec=pl.BlockSpec((tm, tk), lambda i, j, kk: (i, kk)),
               b_spec=pl.BlockSpec((None, tk, tn), lambda i, j, kk: (j // nb, kk, j % nb)),
               o_specs=[o_spec] * len(out_dtypes),
               out_shapes=[jax.ShapeDtypeStruct((m, N_CHIPS * c), d) for d in out_dtypes], acc_shape=(tm, tn),
               epi=epi, after=after)


def _mm_nt_row(name, dy, w, out_dtype, epi=None, extra=None, after=()):
    m, n = dy.shape
    k = w.shape[0]
    tm, tn, tk = _tile(m, MM_TILE_M), _tile(k, MM_TILE_N), _tile(n, MM_TILE_K)
    o_spec = pl.BlockSpec((tm, tn), lambda i, j, kk: (i, j))
    return _mm(name, dy, w, dn=NT, grid=(m // tm, k // tn, n // tk),
               a_spec=pl.BlockSpec((tm, tk), lambda i, j, kk: (i, kk)),
               b_spec=pl.BlockSpec((tn, tk), lambda i, j, kk: (j, kk)),
               o_specs=[o_spec], out_shapes=[jax.ShapeDtypeStruct((m, k), out_dtype)], acc_shape=(tm, tn),
               epi=epi, extra=extra, extra_spec=o_spec, after=after)[0]


def _mm_nt_col(name, dy, w, out_dtype, after=()):
    m = dy.shape[0]
    k, c = w.shape[1], w.shape[2]
    tm, tn, tk = _tile(m, MM_TILE_M), _tile(k, MM_TILE_N), _tile(c, MM_TILE_K)
    kb = c // tk
    return _mm(name, dy, w, dn=NT, grid=(m // tm, k // tn, N_CHIPS * kb),
               a_spec=pl.BlockSpec((tm, tk), lambda i, j, kk: (i, kk)),
               b_spec=pl.BlockSpec((None, tn, tk), lambda i, j, kk: (kk // kb, j, kk % kb)),
               o_specs=[pl.BlockSpec((tm, tn), lambda i, j, kk: (i, j))],
               out_shapes=[jax.ShapeDtypeStruct((m, k), out_dtype)], acc_shape=(tm, tn), after=after)[0]


def _mm_tn_row(name, a, dy):
    t, m = a.shape
    n = dy.shape[1]
    tm, tn, tk = _tile(m, MM_TILE_M), _tile(n, MM_TILE_N), _tile(t, MM_TILE_K)
    return _mm(name, a, dy, dn=TN, grid=(m // tm, n // tn, t // tk),
               a_spec=pl.BlockSpec((tk, tm), lambda i, j, kk: (kk, i)),
               b_spec=pl.BlockSpec((tk, tn), lambda i, j, kk: (kk, j)),
               o_specs=[pl.BlockSpec((tm, tn), lambda i, j, kk: (i, j))],
               out_shapes=[jax.ShapeDtypeStruct((m, n), BF16)], acc_shape=(tm, tn))[0]


def _mm_tn_col(name, a, dy):
    t, m = a.shape
    c = dy.shape[1] // N_CHIPS
    tm, tn, tk = _tile(m, MM_TILE_M), _tile(c, MM_TILE_N), _tile(t, MM_TILE_K)
    nb = c // tn
    return _mm(name, a, dy, dn=TN, grid=(m // tm, N_CHIPS * nb, t // tk),
               a_spec=pl.BlockSpec((tk, tm), lambda i, j, kk: (kk, i)),
               b_spec=pl.BlockSpec((tk, tn), lambda i, j, kk: (kk, j)),
               o_specs=[pl.BlockSpec((None, tm, tn), lambda i, j, kk: (j // nb, i, j % nb))],
               out_shapes=[jax.ShapeDtypeStruct((N_CHIPS, m, c), BF16)], acc_shape=(tm, tn))[0]


def _rms(x, g):
    r = lax.rsqrt(jnp.mean(x * x, axis=-1, keepdims=True) + RMS_EPS)
    return x * r * g


def _rms_bwd(x, g, dy):
    r = lax.rsqrt(jnp.mean(x * x, axis=-1, keepdims=True) + RMS_EPS)
    xr = x * r
    dyg = dy * g
    dx = r * (dyg - xr * jnp.mean(dyg * xr, axis=-1, keepdims=True))
    return dx, jnp.sum(dy * xr, axis=0, keepdims=True)


def _norm_fwd(name, x, o, g_post, g_next, after=()):
    after = list(after)
    s, d = x.shape
    tr = _tile(s, ROW_TILE)
    has_prev, has_next = o is not None, g_next is not None
    row = pl.BlockSpec((tr, d), lambda i: (i, 0))
    vec = pl.BlockSpec((1, d), lambda i: (0, 0))

    def body(*refs):
        refs = list(refs)
        xn = refs.pop(0)[...]
        if has_prev:
            o_ref, gp_ref = refs.pop(0), refs.pop(0)
            xn = xn + _rms(o_ref[...], gp_ref[...])
        gn_ref = refs.pop(0) if has_next else None
        del refs[:len(after)]
        if has_prev:
            refs.pop(0)[...] = xn
        if has_next:
            refs.pop(0)[...] = _rms(xn, gn_ref[...]).astype(BF16)

    ins, in_specs = [x], [row]
    if has_prev:
        ins += [o, g_post]
        in_specs += [row, vec]
    if has_next:
        ins.append(g_next)
        in_specs.append(vec)
    ins += after
    in_specs += [ANY] * len(after)
    out_shapes, out_specs = [], []
    if has_prev:
        out_shapes.append(jax.ShapeDtypeStruct((s, d), F32))
        out_specs.append(row)
    if has_next:
        out_shapes.append(jax.ShapeDtypeStruct((s, d), BF16))
        out_specs.append(row)
    outs = pl.pallas_call(body, name=name, grid=(s // tr,), in_specs=in_specs, out_specs=out_specs,
                          out_shape=out_shapes, compiler_params=_params(("parallel",)))(*ins)
    outs = list(outs)
    x_new = outs.pop(0) if has_prev else x
    h = outs.pop(0) if has_next else None
    return x_new, h


def _norm_bwd(name, dxn, dh, xn, o, g_post, g_next, after=()):
    after = list(after)
    s, d = xn.shape
    tr = _tile(s, ROW_TILE)
    has_prev, has_next, has_dxn = o is not None, dh is not None, dxn is not None
    row = pl.BlockSpec((tr, d), lambda i: (i, 0))
    vec = pl.BlockSpec((1, d), lambda i: (0, 0))

    def body(*refs):
        refs = list(refs)
        first = pl.program_id(0) == 0
        dxn_ref = refs.pop(0) if has_dxn else None
        dh_ref = refs.pop(0) if has_next else None
        xn_ref = refs.pop(0)
        if has_prev:
            o_ref, gp_ref = refs.pop(0), refs.pop(0)
        gn_ref = refs.pop(0) if has_next else None
        del refs[:len(after)]
        dx_ref = refs.pop(0)
        if has_prev:
            do_ref, dgp_ref = refs.pop(0), refs.pop(0)
        dgn_ref = refs.pop(0) if has_next else None

        def accumulate(ref, val):
            @pl.when(first)
            def _():
                ref[...] = val

            @pl.when(jnp.logical_not(first))
            def _():
                ref[...] += val

        dx = dxn_ref[...] if has_dxn else None
        if has_next:
            dxh, dgn = _rms_bwd(xn_ref[...], gn_ref[...], dh_ref[...].astype(F32))
            dx = dxh if dx is None else dx + dxh
            accumulate(dgn_ref, dgn)
        dx_ref[...] = dx
        if has_prev:
            do, dgp = _rms_bwd(o_ref[...], gp_ref[...], dx)
            do_ref[...] = do.astype(BF16)
            accumulate(dgp_ref, dgp)

    ins, in_specs = [], []
    if has_dxn:
        ins.append(dxn)
        in_specs.append(row)
    if has_next:
        ins.append(dh)
        in_specs.append(row)
    ins.append(xn)
    in_specs.append(row)
    if has_prev:
        ins += [o, g_post]
        in_specs += [row, vec]
    if has_next:
        ins.append(g_next)
        in_specs.append(vec)
    ins += after
    in_specs += [ANY] * len(after)
    out_shapes, out_specs = [jax.ShapeDtypeStruct((s, d), F32)], [row]
    if has_prev:
        out_shapes += [jax.ShapeDtypeStruct((s, d), BF16), jax.ShapeDtypeStruct((1, d), F32)]
        out_specs += [row, vec]
    if has_next:
        out_shapes.append(jax.ShapeDtypeStruct((1, d), F32))
        out_specs.append(vec)
    outs = list(pl.pallas_call(body, name=name, grid=(s // tr,), in_specs=in_specs, out_specs=out_specs,
                               out_shape=out_shapes, compiler_params=_params(("arbitrary",)))(*ins))
    dx = outs.pop(0)
    do, dgp = (outs.pop(0), outs.pop(0)) if has_prev else (None, None)
    dgn = outs.pop(0) if has_next else None
    return dx, do, dgp, dgn


def _loss_head(y, target):
    s, d = y.shape
    tr = _tile(s, ROW_TILE)
    row = pl.BlockSpec((tr, d), lambda i: (i, 0))
    vec = pl.BlockSpec((1, d), lambda i: (0, 0))

    def body(y_ref, t_ref, dy_ref, l_ref):
        err = y_ref[...] - t_ref[...]
        dy_ref[...] = err * (1.0 / d)
        part = jnp.sum(err * err, axis=0, keepdims=True) * (0.5 / d)

        @pl.when(pl.program_id(0) == 0)
        def _():
            l_ref[...] = part

        @pl.when(pl.program_id(0) != 0)
        def _():
            l_ref[...] += part

    return pl.pallas_call(body, name="loss_head", grid=(s // tr,), in_specs=[row, row], out_specs=[row, vec],
                          out_shape=[jax.ShapeDtypeStruct((s, d), F32), jax.ShapeDtypeStruct((1, d), F32)],
                          compiler_params=_params(("arbitrary",)))(y, target)


@jax.custom_vjp
def _bdot(a, b):
    return jnp.dot(a.astype(BF16), b.astype(BF16), preferred_element_type=F32)


def _bdot_fwd(a, b):
    return _bdot(a, b), (a, b)


def _bdot_bwd(res, ct):
    a, b = res
    ctb = ct.astype(BF16)
    da = lax.dot_general(ctb, b.astype(BF16), NT, preferred_element_type=F32)
    db = lax.dot_general(a.astype(BF16), ctb, TN, preferred_element_type=F32)
    return da, db


_bdot.defvjp(_bdot_fwd, _bdot_bwd)


@functools.partial(jax.custom_vjp, nondiff_argnums=(1,))
def _shift(x, k):
    n = x.shape[0]
    if k == 0:
        return x
    rolled = pltpu.roll(x, k % n, 0)
    t = lax.broadcasted_iota(jnp.int32, x.shape, 0)
    keep = (t >= k) if k > 0 else (t < n + k)
    return jnp.where(keep, rolled, 0.0)


def _shift_fwd(x, k):
    return _shift(x, k), None


def _shift_bwd(k, _, ct):
    return (_shift(ct, -k),)


_shift.defvjp(_shift_fwd, _shift_bwd)


def _sigmoid(x):
    return 1.0 / (1.0 + jnp.exp(-x))


def _layer_norm(x, g, b=None):
    mu = jnp.mean(x, axis=-1, keepdims=True)
    xc = x - mu
    var = jnp.mean(xc * xc, axis=-1, keepdims=True)
    y = xc * lax.rsqrt(var + LN_EPS) * g
    return y if b is None else y + b


def _gmlp_chunk(zu, zv, gv, w, bcol):
    ch = w.shape[0]
    u = jax.nn.gelu(zu)
    vn = _layer_norm(jax.nn.gelu(zv), gv)
    t = lax.broadcasted_iota(jnp.int32, (ch, ch), 0)
    s = lax.broadcasted_iota(jnp.int32, (ch, ch), 1)
    wm = jnp.where(t >= s, w, 0.0)
    return u * (_bdot(wm, vn) + bcol)


def _gmlp_specs(seq, heads, hd, ch, u_off, v_off):
    col = lambda off: pl.BlockSpec((seq, hd), lambda h: (0, off + h))
    return (col(u_off), col(v_off), pl.BlockSpec((None, 1, hd), lambda h: (h, 0, 0)),
            pl.BlockSpec((None, ch, ch), lambda h: (h, 0, 0)), pl.BlockSpec((None, ch, 1), lambda h: (h, 0, 0)))


def _gmlp_fwd(z, gv, ws, bcol):
    seq = z.shape[0]
    heads, _, hd = gv.shape
    ch = ws.shape[-1]
    zu_s, zv_s, gv_s, w_s, b_s = _gmlp_specs(seq, heads, hd, ch, 0, heads)

    def body(zu_ref, zv_ref, gv_ref, w_ref, b_ref, y_ref):
        gvv, w, bc = gv_ref[...], w_ref[...], b_ref[...]

        def step(c, carry):
            rows = pl.ds(pl.multiple_of(c * ch, ch), ch)
            y_ref[rows, :] = _gmlp_chunk(zu_ref[rows, :], zv_ref[rows, :], gvv, w, bc).astype(BF16)
            return carry

        lax.fori_loop(0, seq // ch, step, 0, unroll=GMLP_UNROLL)

    return pl.pallas_call(body, name="gmlp_fwd", grid=(heads,), in_specs=[zu_s, zv_s, gv_s, w_s, b_s],
                          out_specs=pl.BlockSpec((seq, hd), lambda h: (0, h)),
                          out_shape=jax.ShapeDtypeStruct((seq, heads * hd), BF16),
                          compiler_params=_params(("parallel",)))(z, z, gv, ws, bcol)


def _gmlp_bwd(z, dy, gv, ws, bcol):
    seq = z.shape[0]
    heads, _, hd = gv.shape
    ch = ws.shape[-1]
    zu_s, zv_s, gv_s, w_s, b_s = _gmlp_specs(seq, heads, hd, ch, 0, heads)
    col = pl.BlockSpec((seq, hd), lambda h: (0, h))

    def body(zu_ref, zv_ref, dy_ref, gv_ref, w_ref, b_ref, dzu_ref, dzv_ref, dgv_ref, dw_ref, db_ref):
        gvv, w, bc = gv_ref[...], w_ref[...], b_ref[...]

        together = GMLP_UNROLL if (seq // ch) % GMLP_UNROLL == 0 else 1

        def step(c, carry):
            dgv, dw, db = carry
            for u in range(together):
                rows = pl.ds(pl.multiple_of((c * together + u) * ch, ch), ch)
                _, vjp = jax.vjp(_gmlp_chunk, zu_ref[rows, :], zv_ref[rows, :], gvv, w, bc)
                dzu, dzv, dgv_c, dw_c, db_c = vjp(dy_ref[rows, :])
                dzu_ref[rows, :] = dzu.astype(BF16)
                dzv_ref[rows, :] = dzv.astype(BF16)
                dgv, dw, db = dgv + dgv_c, dw + dw_c, db + db_c
            return dgv, dw, db

        zero = (jnp.zeros((1, hd), F32), jnp.zeros((ch, ch), F32), jnp.zeros((ch, 1), F32))
        dgv, dw, db = lax.fori_loop(0, seq // ch // together, step, zero)
        dgv_ref[...] = dgv
        dw_ref[...] = dw
        db_ref[...] = db

    return pl.pallas_call(
        body, name="gmlp_bwd", grid=(heads,), in_specs=[zu_s, zv_s, col, gv_s, w_s, b_s],
        out_specs=[col, col, gv_s, w_s, b_s],
        out_shape=[jax.ShapeDtypeStruct((seq, heads * hd), BF16), jax.ShapeDtypeStruct((seq, heads * hd), BF16),
                   jax.ShapeDtypeStruct(gv.shape, F32), jax.ShapeDtypeStruct(ws.shape, F32),
                   jax.ShapeDtypeStruct(bcol.shape, F32)],
        compiler_params=_params(("parallel",)))(z, z, dy, gv, ws, bcol)


def _pool_group(p, w, s, window):
    win, span = p, 1
    while span < window:
        win = win + _shift(win, span)
        span *= 2
    t = lax.broadcasted_iota(jnp.int32, (p.shape[0], 1), 0).astype(F32)
    cnt = jnp.minimum(t + 1.0, float(window))
    return _bdot(win / cnt - p, w) * s


def _pool_fwd(z, w_pool, s_pool, col_block):
    seq = z.shape[0]
    groups, gw, _ = w_pool.shape
    pw = groups * gw

    def body(p_ref, w_ref, s_ref, y_ref):
        for g in range(groups):
            cols = slice(g * gw, (g + 1) * gw)
            y_ref[:, cols] = _pool_group(p_ref[:, cols], w_ref[g], s_ref[:, cols], POOL_WINDOWS[g]).astype(BF16)

    return pl.pallas_call(
        body, name="pool_fwd", grid=(1,),
        in_specs=[pl.BlockSpec((seq, pw), lambda i: (0, col_block)),
                  pl.BlockSpec((groups, gw, gw), lambda i: (0, 0, 0)), pl.BlockSpec((1, pw), lambda i: (0, 0))],
        out_specs=pl.BlockSpec((seq, pw), lambda i: (0, 0)), out_shape=jax.ShapeDtypeStruct((seq, pw), BF16),
        compiler_params=_params(("arbitrary",)))(z, w_pool, s_pool)


def _pool_bwd(z, dy, w_pool, s_pool, col_block, dy_block):
    seq = z.shape[0]
    groups, gw, _ = w_pool.shape
    pw = groups * gw

    def body(p_ref, dy_ref, w_ref, s_ref, dp_ref, dw_ref, ds_ref):
        for g in range(groups):
            cols = slice(g * gw, (g + 1) * gw)
            _, vjp = jax.vjp(functools.partial(_pool_group, window=POOL_WINDOWS[g]), p_ref[:, cols], w_ref[g],
                             s_ref[:, cols])
            dp, dw, ds = vjp(dy_ref[:, cols])
            dp_ref[:, cols] = dp.astype(BF16)
            dw_ref[g] = dw
            ds_ref[:, cols] = ds

    return pl.pallas_call(
        body, name="pool_bwd", grid=(1,),
        in_specs=[pl.BlockSpec((seq, pw), lambda i: (0, col_block)),
                  pl.BlockSpec((seq, pw), lambda i: (0, dy_block)),
                  pl.BlockSpec((groups, gw, gw), lambda i: (0, 0, 0)), pl.BlockSpec((1, pw), lambda i: (0, 0))],
        out_specs=[pl.BlockSpec((seq, pw), lambda i: (0, 0)), pl.BlockSpec((groups, gw, gw), lambda i: (0, 0, 0)),
                   pl.BlockSpec((1, pw), lambda i: (0, 0))],
        out_shape=[jax.ShapeDtypeStruct((seq, pw), BF16), jax.ShapeDtypeStruct(w_pool.shape, F32),
                   jax.ShapeDtypeStruct((1, pw), F32)],
        compiler_params=_params(("arbitrary",)))(z, dy, w_pool, s_pool)


def _conv_fwd(z, w_dw, taps, b_dw, val_block, gate_block, after=()):
    seq = z.shape[0]
    rows, cb = w_dw.shape[1], w_dw.shape[2]
    after = list(after)

    def body(val_ref, gate_ref, w_ref, b_ref, *refs):
        out_ref = refs[-1]
        h = val_ref[...] * _sigmoid(gate_ref[...])
        acc = jnp.broadcast_to(b_ref[...], h.shape)
        for d in range(taps):
            acc = acc + w_ref[pl.ds(taps - 1 - d, 1), :] * _shift(h, d)
        out_ref[...] = acc

    return pl.pallas_call(
        body, name="conv_fwd", grid=(N_CHIPS,),
        in_specs=[pl.BlockSpec((seq, cb), lambda j: (0, val_block + j)),
                  pl.BlockSpec((seq, cb), lambda j: (0, gate_block + j)),
                  pl.BlockSpec((None, rows, cb), lambda j: (j, 0, 0)),
                  pl.BlockSpec((1, cb), lambda j: (0, j))] + [ANY] * len(after),
        out_specs=pl.BlockSpec((seq, cb), lambda j: (0, j)),
        out_shape=jax.ShapeDtypeStruct((seq, N_CHIPS * cb), F32),
        compiler_params=_params(("parallel",)))(z, z, w_dw, b_dw, *after)


def _conv_bwd(z, dout, w_dw, taps, val_block, gate_block):
    seq = z.shape[0]
    rows, cb = w_dw.shape[1], w_dw.shape[2]
    col = pl.BlockSpec((seq, cb), lambda j: (0, j))

    def body(val_ref, gate_ref, do_ref, w_ref, dval_ref, dgate_ref, dw_ref, db_ref):
        val, sg, do = val_ref[...], _sigmoid(gate_ref[...]), do_ref[...]
        h = val * sg
        db_ref[...] = jnp.sum(do, axis=0, keepdims=True)
        dh = jnp.zeros_like(h)
        for d in range(taps):
            k = taps - 1 - d
            dw_ref[pl.ds(k, 1), :] = jnp.sum(do * _shift(h, d), axis=0, keepdims=True)
            dh = dh + w_ref[pl.ds(k, 1), :] * _shift(do, -d)
        dval_ref[...] = (dh * sg).astype(BF16)
        dgate_ref[...] = (dh * val * sg * (1.0 - sg)).astype(BF16)

    return pl.pallas_call(
        body, name="conv_bwd", grid=(N_CHIPS,),
        in_specs=[pl.BlockSpec((seq, cb), lambda j: (0, val_block + j)),
                  pl.BlockSpec((seq, cb), lambda j: (0, gate_block + j)), col,
                  pl.BlockSpec((None, rows, cb), lambda j: (j, 0, 0))],
        out_specs=[col, col, pl.BlockSpec((taps, cb), lambda j: (0, j)), pl.BlockSpec((1, cb), lambda j: (0, j))],
        out_shape=[jax.ShapeDtypeStruct((seq, N_CHIPS * cb), BF16), jax.ShapeDtypeStruct((seq, N_CHIPS * cb), BF16),
                   jax.ShapeDtypeStruct((taps, N_CHIPS * cb), F32), jax.ShapeDtypeStruct((1, N_CHIPS * cb), F32)],
        compiler_params=_params(("parallel",)))(z, z, dout, w_dw)


def _ln_swish(hc, g, b):
    y = _layer_norm(hc, g, b)
    return y * _sigmoid(y)


def _ln_swish_fwd(hc, g, b):
    s, cw = hc.shape
    tr = _tile(s, ROW_TILE)
    row = pl.BlockSpec((tr, cw), lambda i: (i, 0))
    vec = pl.BlockSpec((1, cw), lambda i: (0, 0))

    def body(h_ref, g_ref, b_ref, y_ref):
        y_ref[...] = _ln_swish(h_ref[...], g_ref[...], b_ref[...]).astype(BF16)

    return pl.pallas_call(body, name="ln_swish_fwd", grid=(s // tr,), in_specs=[row, vec, vec], out_specs=row,
                          out_shape=jax.ShapeDtypeStruct((s, cw), BF16),
                          compiler_params=_params(("parallel",)))(hc, g, b)


def _ln_swish_bwd(hc, dy, g, b, dy_block):
    s, cw = hc.shape
    tr = _tile(s, ROW_TILE)
    row = pl.BlockSpec((tr, cw), lambda i: (i, 0))
    vec = pl.BlockSpec((1, cw), lambda i: (0, 0))

    def body(h_ref, dy_ref, g_ref, b_ref, dh_ref, dg_ref, db_ref):
        _, vjp = jax.vjp(_ln_swish, h_ref[...], g_ref[...], b_ref[...])
        dh, dg, db = vjp(dy_ref[...])
        dh_ref[...] = dh

        @pl.when(pl.program_id(0) == 0)
        def _():
            dg_ref[...] = dg
            db_ref[...] = db

        @pl.when(pl.program_id(0) != 0)
        def _():
            dg_ref[...] += dg
            db_ref[...] += db

    return pl.pallas_call(
        body, name="ln_swish_bwd", grid=(s // tr,),
        in_specs=[row, pl.BlockSpec((tr, cw), lambda i: (i, dy_block)), vec, vec], out_specs=[row, vec, vec],
        out_shape=[jax.ShapeDtypeStruct((s, cw), F32), jax.ShapeDtypeStruct((1, cw), F32),
                   jax.ShapeDtypeStruct((1, cw), F32)],
        compiler_params=_params(("arbitrary",)))(hc, dy, g, b)


def _attn_probs(q, k, scale):
    s = lax.dot_general(q, k, NT, preferred_element_type=F32) * scale
    e = jnp.exp(s - jnp.max(s, axis=-1, keepdims=True))
    return e / jnp.sum(e, axis=-1, keepdims=True)


def _attn_fwd(q, k, v):
    seq, d = q.shape
    mem = k.shape[0]
    hd = d // XATTN_HEADS
    scale = hd ** -0.5
    qs = pl.BlockSpec((seq, hd), lambda h: (0, h))
    ms = pl.BlockSpec((mem, hd), lambda h: (0, h))

    def body(q_ref, k_ref, v_ref, a_ref):
        p = _attn_probs(q_ref[...], k_ref[...], scale)
        a_ref[...] = jnp.dot(p.astype(BF16), v_ref[...], preferred_element_type=F32).astype(BF16)

    return pl.pallas_call(body, name="attn_fwd", grid=(XATTN_HEADS,), in_specs=[qs, ms, ms], out_specs=qs,
                          out_shape=jax.ShapeDtypeStruct((seq, d), BF16),
                          compiler_params=_params(("parallel",)))(q, k, v)


def _attn_bwd(q, k, v, da):
    seq, d = q.shape
    mem = k.shape[0]
    hd = d // XATTN_HEADS
    scale = hd ** -0.5
    qs = pl.BlockSpec((seq, hd), lambda h: (0, h))
    ms = pl.BlockSpec((mem, hd), lambda h: (0, h))

    def body(q_ref, k_ref, v_ref, da_ref, dq_ref, dk_ref, dv_ref):
        q_, k_, v_, da_ = q_ref[...], k_ref[...], v_ref[...], da_ref[...]
        p = _attn_probs(q_, k_, scale)
        dv_ref[...] = lax.dot_general(p.astype(BF16), da_, TN, preferred_element_type=F32).astype(BF16)
        dp = lax.dot_general(da_, v_, NT, preferred_element_type=F32)
        ds = (p * (dp - jnp.sum(dp * p, axis=-1, keepdims=True)) * scale).astype(BF16)
        dq_ref[...] = jnp.dot(ds, k_, preferred_element_type=F32).astype(BF16)
        dk_ref[...] = lax.dot_general(ds, q_, TN, preferred_element_type=F32).astype(BF16)

    return pl.pallas_call(
        body, name="attn_bwd", grid=(XATTN_HEADS,), in_specs=[qs, ms, ms, qs], out_specs=[qs, ms, ms],
        out_shape=[jax.ShapeDtypeStruct((seq, d), BF16), jax.ShapeDtypeStruct((mem, d), BF16),
                   jax.ShapeDtypeStruct((mem, d), BF16)],
        compiler_params=_params(("parallel",)))(q, k, v, da)


def _place_shard(name, place, w, l, dtype, after=()):
    _, r, c = w.shape
    tr = _tile(r, 2 * ROW_TILE) if r % 16 == 0 else r
    after = list(after)

    def body(place_ref, w_ref, *refs):
        refs[-1][...] = w_ref[...].astype(dtype)

    return pl.pallas_call(
        body, name=name,
        grid_spec=pltpu.PrefetchScalarGridSpec(
            num_scalar_prefetch=1, grid=(r // tr,),
            in_specs=[pl.BlockSpec((None, tr, c), lambda i, p: (l, i, 0))] + [ANY] * len(after),
            out_specs=pl.BlockSpec((None, tr, c), lambda i, p: (p[1], i, 0))),
        out_shape=jax.ShapeDtypeStruct((N_CHIPS, r, c), dtype),
        compiler_params=_params(("parallel",)))(place, w, *after)


def _place_flat(place, flat):
    rows, lanes = flat.shape

    def body(place_ref, f_ref, o_ref):
        o_ref[...] = f_ref[...]

    return pl.pallas_call(
        body, name="place_flat",
        grid_spec=pltpu.PrefetchScalarGridSpec(
            num_scalar_prefetch=1, grid=(1,), in_specs=[pl.BlockSpec((rows, lanes), lambda i, p: (0, 0))],
            out_specs=pl.BlockSpec((None, rows, lanes), lambda i, p: (p[2], 0, 0))),
        out_shape=jax.ShapeDtypeStruct((N_DEVICES, rows, lanes), flat.dtype),
        compiler_params=_params(("arbitrary",)))(place, flat)


def _pair_sum(name, place, dw, got):
    n, _, r2, c = dw.shape
    tr = _tile(r2, 2 * ROW_TILE)

    def body(place_ref, own_ref, got_ref, s_ref, t_ref):
        val = (own_ref[...].astype(F32) + got_ref[...].astype(F32)).astype(BF16)
        s_ref[...] = val

        @pl.when(pl.program_id(1) == place_ref[1])
        def _():
            t_ref[...] = val

    slab = pl.BlockSpec((None, tr, c), lambda i, j, p: (j, i, 0))
    sds = jax.ShapeDtypeStruct((n, r2, c), BF16)
    return pl.pallas_call(
        body, name=name,
        grid_spec=pltpu.PrefetchScalarGridSpec(
            num_scalar_prefetch=1, grid=(r2 // tr, n),
            in_specs=[pl.BlockSpec((None, None, tr, c), lambda i, j, p: (j, p[0], i, 0)), slab],
            out_specs=[slab, pl.BlockSpec((None, tr, c), lambda i, j, p: (p[1], i, 0))]),
        out_shape=[sds, sds], compiler_params=_params(("parallel", "arbitrary")))(place, dw, got)


def _chip_sum(name, place, parts):
    n, r2, c = parts.shape
    tr = _tile(r2, ROW_TILE)

    def body(place_ref, *refs):
        o_ref = refs[n]
        acc = refs[0][...].astype(F32)
        for j in range(1, n):
            acc = acc + refs[j][...].astype(F32)
        o_ref[...] = acc

    part = lambda j: pl.BlockSpec((None, tr, c), lambda i, p: (j, i, 0))
    return pl.pallas_call(
        body, name=name,
        grid_spec=pltpu.PrefetchScalarGridSpec(
            num_scalar_prefetch=1, grid=(r2 // tr,), in_specs=[part(j) for j in range(n)],
            out_specs=pl.BlockSpec((None, tr, c), lambda i, p: (p[0], i, 0))),
        out_shape=jax.ShapeDtypeStruct((2, r2, c), F32),
        compiler_params=_params(("parallel",)))(place, *([parts] * n))


def _device_sum(parts):
    n, rows, lanes = parts.shape
    tr = _tile(rows, 4 * ROW_TILE) if rows % 8 == 0 else rows

    def body(p_ref, o_ref):
        acc = p_ref[0]
        for j in range(1, n):
            acc = acc + p_ref[j]
        o_ref[...] = acc

    return pl.pallas_call(
        body, name="device_sum", grid=(rows // tr,), in_specs=[pl.BlockSpec((n, tr, lanes), lambda i: (0, i, 0))],
        out_specs=pl.BlockSpec((tr, lanes), lambda i: (i, 0)), out_shape=jax.ShapeDtypeStruct((rows, lanes), F32),
        compiler_params=_params(("parallel",)))(parts)


def _adam_update(w, g, m, v):
    nm = ADAM_B1 * m + (1.0 - ADAM_B1) * g
    nv = ADAM_B2 * v + (1.0 - ADAM_B2) * (g * g)
    c1 = 1.0 - ADAM_B1 ** ADAM_STEP
    c2 = 1.0 - ADAM_B2 ** ADAM_STEP
    return -ADAM_LR * ((nm / c1) / (jnp.sqrt(nv / c2) + ADAM_EPS) + ADAM_WD * w), nm, nv


def _adamw_layer(name, l, w, g, m, v, prev):
    n_l, r, c = w.shape
    tr = _tile(r, ROW_TILE)
    slab = pl.BlockSpec((None, tr, c), lambda i: (l, i, 0))

    def body(w_ref, g_ref, m_ref, v_ref, *refs):
        go_ref, d_ref, nm_ref, nv_ref = refs[-4:]
        g_ = g_ref[...]
        delta, nm, nv = _adam_update(w_ref[...], g_, m_ref[...], v_ref[...])
        go_ref[...] = g_
        d_ref[...] = delta
        nm_ref[...] = nm
        nv_ref[...] = nv

    ins = [w, g, m, v]
    in_specs = [slab, pl.BlockSpec((tr, c), lambda i: (i, 0)), slab, slab]
    aliases = {}
    if prev is not None:
        aliases = {len(ins) + i: i for i in range(4)}
        ins += list(prev)
        in_specs += [ANY] * 4
    sds = jax.ShapeDtypeStruct((n_l, r, c), F32)
    return pl.pallas_call(body, name=name, grid=(r // tr,), in_specs=in_specs, out_specs=[slab] * 4,
                          out_shape=[sds] * 4, input_output_aliases=aliases,
                          compiler_params=_params(("parallel",)))(*ins)


def _adamw_flat(name, w, g, m, v):
    rows, cols = w.shape
    tr = _tile(rows, ROW_TILE) if rows % 8 == 0 else rows
    spec = pl.BlockSpec((tr, cols), lambda i: (i, 0))

    def body(w_ref, g_ref, m_ref, v_ref, d_ref, nm_ref, nv_ref):
        d_ref[...], nm_ref[...], nv_ref[...] = _adam_update(w_ref[...], g_ref[...], m_ref[...], v_ref[...])

    sds = jax.ShapeDtypeStruct((rows, cols), F32)
    return pl.pallas_call(body, name=name, grid=(rows // tr,), in_specs=[spec] * 4, out_specs=[spec] * 3,
                          out_shape=[sds] * 3, compiler_params=_params(("parallel",)))(w, g, m, v)


def _me():
    return lax.axis_index("x"), lax.axis_index("y"), lax.axis_index("c")


def _other_chips(x, y):
    return [(1 - x, y, 2 * (1 - x) + y), (x, 1 - y, 2 * x + 1 - y), (1 - x, 1 - y, 2 * (1 - x) + 1 - y)]


def _remote(src, dst, send_sem, recv_sem, target):
    return pltpu.make_async_remote_copy(src_ref=src, dst_ref=dst, send_sem=send_sem, recv_sem=recv_sem,
                                        device_id=target, device_id_type=MESH)


def _exchange(name, bufs, plan, n_copies):
    n = len(bufs)

    def body(*refs):
        send_sems, recv_sems = refs[2 * n:]
        copies = []
        for i, (src, dst, target) in enumerate(plan(refs[n:2 * n], _me())):
            if target is None:
                cp = pltpu.make_async_copy(src, dst, send_sems.at[i])
            else:
                cp = _remote(src, dst, send_sems.at[i], recv_sems.at[i], target)
            cp.start()
            copies.append((cp, target))
        assert len(copies) == n_copies
        for cp, target in copies:
            if target is None:
                cp.wait()
            else:
                cp.wait_recv()
        for cp, target in copies:
            if target is not None:
                cp.wait_send()

    return pl.pallas_call(
        body, name=name, in_specs=[ANY] * n, out_specs=[ANY] * n,
        out_shape=[jax.ShapeDtypeStruct(b.shape, b.dtype) for b in bufs],
        scratch_shapes=[pltpu.SemaphoreType.DMA((n_copies,)), pltpu.SemaphoreType.DMA((n_copies,))],
        input_output_aliases={i: i for i in range(n)},
        compiler_params=pltpu.CompilerParams(has_side_effects=True))(*bufs)


def _start_copies(name, groups, after=()):
    all_bufs = [b for bufs, _, _ in groups for b in bufs]
    after = list(after)
    n = len(all_bufs)
    n_g = len(groups)

    def body(*refs):
        in_refs, sem_refs = refs[:n], refs[n + len(after):n + len(after) + 2 * n_g]
        pos = 0
        for g, (bufs, plan, n_copies) in enumerate(groups):
            copies = plan(in_refs[pos:pos + len(bufs)], _me())
            assert len(copies) == n_copies
            for i, (src, dst, target) in enumerate(copies):
                _remote(src, dst, sem_refs[2 * g].at[i], sem_refs[2 * g + 1].at[i], target).start()
            pos += len(bufs)

    sems = []
    for _, _, n_copies in groups:
        sems += [pltpu.SemaphoreType.DMA((n_copies,))] * 2
    outs = pl.pallas_call(
        body, name=name, in_specs=[HBM] * n + [ANY] * len(after), out_specs=[SEM] * (2 * n_g) + [HBM] * n,
        out_shape=sems + [pltpu.HBM(b.shape, b.dtype) for b in all_bufs],
        input_output_aliases={i: 2 * n_g + i for i in range(n)},
        compiler_params=pltpu.CompilerParams(has_side_effects=pltpu.SideEffectType.DATAFLOW_SIDE_EFFECTING))(
            *[pltpu.with_memory_space_constraint(b, pltpu.HBM) for b in all_bufs], *after)
    result, pos = [], 2 * n_g
    for g, (bufs, _, _) in enumerate(groups):
        result.append((outs[2 * g], outs[2 * g + 1], list(outs[pos:pos + len(bufs)])))
        pos += len(bufs)
    return result


def _wait_copies(name, started, plan, n_copies, after):
    send_sems, recv_sems, bufs = started
    n = len(bufs)
    after = list(after) if isinstance(after, (list, tuple)) else [after]
    after = [a for a in after if all(a is not b for b in bufs)]

    def body(*refs):
        copies = plan(refs[:n], _me())
        assert len(copies) == n_copies
        for i, (src, dst, target) in enumerate(copies):
            cp = _remote(src, dst, refs[n].at[i], refs[n + 1].at[i], target)
            cp.wait_send()
            cp.wait_recv()

    return list(pl.pallas_call(
        body, name=name, in_specs=[HBM] * n + [SEM, SEM] + [ANY] * len(after), out_specs=[HBM] * n,
        out_shape=[pltpu.HBM(b.shape, b.dtype) for b in bufs], input_output_aliases={i: i for i in range(n)},
        compiler_params=pltpu.CompilerParams(has_side_effects=pltpu.SideEffectType.DATAFLOW_SIDE_EFFECTING))(
            *bufs, send_sems, recv_sems, *after))


def _halves(a):
    return a.reshape(a.shape[0], 2, a.shape[1] // 2, a.shape[2])


def _neighbour_slab(x, y, c):
    across_x, across_y = 2 * (1 - x) + y, 2 * x + 1 - y
    return across_x + c * (across_y - across_x)


def _direct_plan(refs, me):
    x, y, c = me
    mine = 2 * x + y
    target = (x + (1 - c) * (1 - 2 * x), y + c * (1 - 2 * y), c)
    return [(g.at[mine], g.at[mine], target) for g in refs]


def _relay_plan(refs, me):
    x, y, c = me
    src = _neighbour_slab(x, y, c)
    onward = (x + c * (1 - 2 * x), y + (1 - c) * (1 - 2 * y), c)
    copies = []
    for g in refs:
        copies += [(g.at[src, c], g.at[src, c], onward), (g.at[src], g.at[src], (x, y, 1 - c))]
    return copies


def _last_plan(refs, me):
    x, y, c = me
    far = 2 * (1 - x) + 1 - y
    return [(g.at[far, c], g.at[far, c], (x, y, 1 - c)) for g in refs]


def _swap_plan(refs, me):
    x, y, c = me
    k = len(refs) // 2
    return [(refs[i].at[j, 1 - c], refs[k + i].at[j], (x, y, 1 - c)) for i in range(k) for j in range(N_CHIPS)]


def _scatter_plan(refs, me):
    x, y, c = me
    mine = 2 * x + y
    k = len(refs) // 2
    return [(refs[i].at[chip], refs[k + i].at[mine], (px, py, c))
            for i in range(k) for px, py, chip in _other_chips(x, y)]


def _share_plan(refs, me):
    x, y, c = me
    return [(g.at[c], g.at[c], (x, y, 1 - c)) for g in refs]


def _broadcast_plan(refs, me):
    x, y, c = me
    mine = 4 * x + 2 * y + c
    copies = []
    for fx, fy, fc in [(0, 0, 1), (0, 1, 0), (0, 1, 1), (1, 0, 0), (1, 0, 1), (1, 1, 0), (1, 1, 1)]:
        peer = (x + fx - 2 * fx * x, y + fy - 2 * fy * y, c + fc - 2 * fc * c)
        copies.append((refs[0].at[mine], refs[0].at[mine], peer))
    return copies


BIG = ("w_in", "w_out", "w_q", "w_k", "w_v", "w_o", "w_up", "w_down")
COLUMN_SPLIT = ("w_in", "w_up")
WEIGHTS = ("norm_mix_pre", "norm_mix_post", "w_in", "w_out", "gmlp_v_gain", "w_spatial", "b_spatial", "w_pool",
           "s_pool", "w_dw", "b_dw", "conv_ln_g", "conv_ln_b", "norm_xattn_pre", "norm_mem", "norm_xattn_post",
           "w_q", "w_k", "w_v", "w_o", "norm_ffn_pre", "norm_ffn_post", "w_up", "w_down")
SMALL = tuple(n for n in WEIGHTS if n not in BIG)
REPLICATED = tuple(n for n in SMALL if n != "w_dw")
GATHER_GROUPS = (("w_in", "w_dw"), ("w_out", "w_q", "w_k", "w_v", "w_o"), ("w_up",), ("w_down",))


def _relu2(acc):
    r = jnp.maximum(acc, 0.0)
    return acc, r * r


def _relu2_bwd(acc, up):
    return (acc * (2.0 * jnp.maximum(up, 0.0)),)


def _pack(arrays):
    flat = jnp.concatenate([a.reshape(-1) for a in arrays])
    tile = 8 * V7X_LANES
    pad = (-flat.shape[0]) % tile
    return jnp.pad(flat, (0, pad)).reshape(-1, V7X_LANES)


def _pack_layers(arrays):
    n_l = arrays[0].shape[0]
    flat = jnp.concatenate([a.reshape(n_l, -1) for a in arrays], axis=1)
    pad = (-flat.shape[1]) % (8 * V7X_LANES)
    return jnp.pad(flat, ((0, 0), (0, pad))).reshape(n_l, -1, V7X_LANES)


def _unpack_layers(packed, like):
    flat = packed.reshape(packed.shape[0], -1)
    out, pos = [], 0
    for a in like:
        out.append(flat[:, pos:pos + a[0].size].reshape(a.shape))
        pos += a[0].size
    return out


class _GradientReducer:
    def __init__(self, place, w, m, v):
        self.place, self.w, self.m, self.v = place, w, m, v
        self.flying = []
        self.done = {n: None for n in BIG}

    def add(self, tag, l, grads):
        names = list(grads)
        views = [_halves(grads[n]) for n in names]
        zones = [lax.empty((v.shape[0],) + v.shape[2:], v.dtype) for v in views]
        started = _start_copies("swap_start_" + tag, [(views + zones, _swap_plan, N_CHIPS * len(names))])[0]
        self.flying.append(dict(stage=0, tag=tag, l=l, names=names, started=started))
        return [started[2][0]]

    def advance(self, after):
        made = []
        after = list(after) if isinstance(after, (list, tuple)) else [after]
        for item in self.flying:
            item["stage"] += 1
        for item in self.flying:
            tag, names, k = item["tag"], item["names"], len(item["names"])
            if item["stage"] == 1:
                bufs = _wait_copies("swap_wait_" + tag, item["started"], _swap_plan, N_CHIPS * k, after)
                sums, parts = zip(*[_pair_sum("pair_sum_" + n, self.place, dv, got)
                                    for n, dv, got in zip(names, bufs[:k], bufs[k:])])
                item["started"] = _start_copies("scatter_start_" + tag,
                                                [(list(sums) + list(parts), _scatter_plan, 3 * k)])[0]
                made.append(item["started"][2][0])
        after = after + made
        for item in list(self.flying):
            tag, names, k = item["tag"], item["names"], len(item["names"])
            if item["stage"] == 3:
                bufs = _wait_copies("scatter_wait_" + tag, item["started"], _scatter_plan, 3 * k, after)
                halves = [_chip_sum("chip_sum_" + n, self.place, p) for n, p in zip(names, bufs[k:])]
                item["started"] = _start_copies("share_start_" + tag, [(halves, _share_plan, k)])[0]
                made.append(item["started"][2][0])
            elif item["stage"] == 4:
                halves = _wait_copies("share_wait_" + tag, item["started"], _share_plan, k, after)
                for n, h in zip(names, halves):
                    g = h.reshape(self.w[n].shape[1:])
                    self.done[n] = _adamw_layer("adamw_" + n, item["l"], self.w[n], g, self.m[n], self.v[n],
                                                self.done[n])
                    made.append(self.done[n][3])
                self.flying.remove(item)
        return made

    def drain(self, after):
        made = list(after)
        while self.flying:
            made = list(after) + self.advance(made)
        return made


def _step(x, mem, target, w, m, v):
    n_layers = w["w_in"].shape[0]
    seq, d = x.shape
    heads, hd = w["gmlp_v_gain"].shape[1:]
    gw = heads * hd
    groups, pgw = w["w_pool"].shape[1:3]
    pw = groups * pgw
    cw = w["b_dw"].shape[1]
    cb = cw // N_CHIPS
    taps = w["w_dw"].shape[1]
    cx, cy, cc = _me()
    chip = 2 * cx + cy
    place = jnp.stack([cc, chip, 2 * chip + cc]).astype(jnp.int32)
    vec = lambda name, l: w[name][l].reshape(1, -1)

    taps_padded = jnp.pad(w["w_dw"], ((0, 0), (0, (-taps) % 16), (0, 0)))
    gathering = []

    def send_layer(l, last):
        for g, names in enumerate(GATHER_GROUPS):
            bufs = [_halves(_place_shard("place_" + n, place, taps_padded if n == "w_dw" else w[n], l,
                                         F32 if n == "w_dw" else BF16, after=last)) for n in names]
            gathering.extend(_start_copies("gather_start_%d%d" % (l, g), [(bufs, _direct_plan, len(names))],
                                           after=last))
            last = [gathering[-1][2][0]]
        return last

    def relay(l, g, after):
        names = GATHER_GROUPS[g]
        at = l * len(GATHER_GROUPS) + g
        bufs = _wait_copies("gather_wait_%d%d" % (l, g), gathering[at], _direct_plan, len(names), after)
        gathering[at] = _start_copies("relay_start_%d%d" % (l, g), [(bufs, _relay_plan, 2 * len(names))])[0]
        return [gathering[at][2][0]]

    def arrive(l, g, after):
        names = GATHER_GROUPS[g]
        tag = "%d%d" % (l, g)
        bufs = _wait_copies("relay_wait_" + tag, gathering[l * len(GATHER_GROUPS) + g], _relay_plan,
                            2 * len(names), after)
        bufs = _exchange("gather_last_" + tag, bufs, _last_plan, len(names))
        out = {}
        for n, b in zip(names, bufs):
            full = b.reshape(N_CHIPS, 2 * b.shape[2], b.shape[3])
            out[n] = full if n in COLUMN_SPLIT + ("w_dw",) else full.reshape(-1, full.shape[2])
        return out

    saved = []
    sent = send_layer(0, [])
    _, h1 = _norm_fwd("norm_first", x, None, None, vec("norm_mix_pre", 0))
    relay(0, 0, [h1] + sent)
    for l in range(n_layers):
        gv = w["gmlp_v_gain"][l].reshape(heads, 1, hd)
        ws = w["w_spatial"][l]
        bcol = w["b_spatial"][l].reshape(heads, -1, 1)
        wl = arrive(l, 0, [h1])
        early = relay(l, 1, [h1]) if l > 0 else []
        z = _mm_nn_col("mm_in", h1, wl["w_in"], [F32], after=early)[0]
        sent = send_layer(l + 1, [z]) if l + 1 < n_layers else []
        ya = _gmlp_fwd(z, gv, ws, bcol)
        yb = _pool_fwd(z, w["w_pool"][l], vec("s_pool", l), (2 * gw) // pw)
        hc = _conv_fwd(z, wl["w_dw"], taps, vec("b_dw", l), (2 * gw + pw) // cb, (2 * gw + pw + cw) // cb,
                       after=relay(l, 1, [ya]) if l == 0 else [])
        yc = _ln_swish_fwd(hc, vec("conv_ln_g", l), vec("conv_ln_b", l))
        y = jnp.concatenate([ya, yb, yc], axis=1)
        wl.update(arrive(l, 1, [y] + sent))
        o = _mm_nn_row("mm_out", y, wl["w_out"], [F32])[0]
        x1, h2 = _norm_fwd("norm_mix", x, o, vec("norm_mix_post", l), vec("norm_xattn_pre", l),
                           after=relay(l, 2, [o]))
        _, mn = _norm_fwd("norm_mem", mem, None, None, vec("norm_mem", l))
        q = _mm_nn_row("mm_q", h2, wl["w_q"], [BF16])[0]
        k = _mm_nn_row("mm_k", mn, wl["w_k"], [BF16])[0]
        vv = _mm_nn_row("mm_v", mn, wl["w_v"], [BF16])[0]
        a = _attn_fwd(q, k, vv)
        o2 = _mm_nn_row("mm_o", a, wl["w_o"], [F32], after=relay(l, 3, [a]))[0]
        x2, h3 = _norm_fwd("norm_xattn", x1, o2, vec("norm_xattn_post", l), vec("norm_ffn_pre", l))
        wl.update(arrive(l, 2, h3))
        up, r = _mm_nn_col("mm_up", h3, wl["w_up"], [BF16, BF16], epi=_relu2)
        wl.update(arrive(l, 3, r))
        o3 = _mm_nn_row("mm_down", r, wl["w_down"], [F32],
                        after=relay(l + 1, 0, [r]) if l + 1 < n_layers else [])[0]
        g_next = vec("norm_mix_pre", l + 1) if l + 1 < n_layers else None
        x3, h_next = _norm_fwd("norm_ffn", x2, o3, vec("norm_ffn_post", l), g_next)
        saved.append(dict(x=x, h1=h1, z=z, hc=hc, y=y, o=o, x1=x1, h2=h2, mn=mn, q=q, k=k, v=vv, a=a, o2=o2, x2=x2,
                          h3=h3, up=up, r=r, o3=o3, x3=x3, gv=gv, ws=ws, bcol=bcol, w=wl))
        x, h1 = x3, h_next

    dx, loss_parts = _loss_head(x, target)
    loss = lax.psum(jnp.sum(loss_parts), ("x", "y", "c"))

    reducer = _GradientReducer(place, w, m, v)
    small = {n: [None] * n_layers for n in SMALL}
    by_chip = lambda g: g.reshape(N_CHIPS, g.shape[0] // N_CHIPS, g.shape[1])
    small_sent = [None] * n_layers

    def small_layer(l):
        return [small[n][l].reshape(w[n].shape[1:]) for n in REPLICATED] + [small["w_dw"][l]]

    def send_small(l):
        landing = _place_flat(place, _pack(small_layer(l)))
        small_sent[l] = _start_copies("small_start_%d" % l, [([landing], _broadcast_plan, N_DEVICES - 1)])[0]
        return [small_sent[l][2][0]]

    dh = None
    made = []
    for l in reversed(range(n_layers)):
        t = saved[l]
        wl = t["w"]
        g_next = vec("norm_mix_pre", l + 1) if l + 1 < n_layers else None
        dx, do3, dgp, dgn = _norm_bwd("norm_ffn_bwd", dx, dh, t["x3"], t["o3"], vec("norm_ffn_post", l), g_next,
                                      after=made)
        small["norm_ffn_post"][l] = dgp
        if dgn is not None:
            small["norm_mix_pre"][l + 1] = dgn
        made = reducer.advance(dx)
        if l + 1 < n_layers:
            made += send_small(l + 1)
        d_down = _mm_tn_row("mm_down_dw", t["r"], do3)
        made += reducer.add("%d0" % l, l, {"w_down": by_chip(d_down)})
        dup = _mm_nt_row("mm_down_dx", do3, wl["w_down"], BF16, epi=_relu2_bwd, extra=t["up"], after=made)
        d_up = _mm_tn_col("mm_up_dw", t["h3"], dup)
        made = reducer.add("%d3" % l, l, {"w_up": d_up})
        dh3 = _mm_nt_col("mm_up_dx", dup, wl["w_up"], F32, after=made)
        made = []
        dx, do2, dgp, dgn = _norm_bwd("norm_xattn_bwd", dx, dh3, t["x2"], t["o2"], vec("norm_xattn_post", l),
                                      vec("norm_ffn_pre", l), after=made)
        small["norm_xattn_post"][l], small["norm_ffn_pre"][l] = dgp, dgn
        made = reducer.advance(dx)
        d_o = _mm_tn_row("mm_o_dw", t["a"], do2)
        da = _mm_nt_row("mm_o_dx", do2, wl["w_o"], BF16, after=made)
        dq, dk, dv = _attn_bwd(t["q"], t["k"], t["v"], da)
        d_q = _mm_tn_row("mm_q_dw", t["h2"], dq)
        d_k = _mm_tn_row("mm_k_dw", t["mn"], dk)
        d_v = _mm_tn_row("mm_v_dw", t["mn"], dv)
        dh2 = _mm_nt_row("mm_q_dx", dq, wl["w_q"], F32)
        dmn = _mm_nt_row("mm_k_dx", dk, wl["w_k"], F32) + _mm_nt_row("mm_v_dx", dv, wl["w_v"], F32)
        _, _, _, small["norm_mem"][l] = _norm_bwd("norm_mem_bwd", None, dmn, mem, None, None, vec("norm_mem", l))
        dx, do, dgp, dgn = _norm_bwd("norm_mix_bwd", dx, dh2, t["x1"], t["o"], vec("norm_mix_post", l),
                                     vec("norm_xattn_pre", l))
        small["norm_mix_post"][l], small["norm_xattn_pre"][l] = dgp, dgn
        made = reducer.advance(dx)
        d_out = _mm_tn_row("mm_out_dw", t["y"], do)
        made += reducer.add("%d1" % l, l, {"w_o": by_chip(d_o), "w_q": by_chip(d_q), "w_k": by_chip(d_k),
                                    "w_v": by_chip(d_v), "w_out": by_chip(d_out)})
        dy = _mm_nt_row("mm_out_dx", do, wl["w_out"], F32, after=made)
        dzu, dzv, dgv, dws, dbcol = _gmlp_bwd(t["z"], dy, t["gv"], t["ws"], t["bcol"])
        small["gmlp_v_gain"][l] = dgv.reshape(heads, hd)
        small["w_spatial"][l] = dws
        small["b_spatial"][l] = dbcol.reshape(heads, -1)
        dzp, dwp, dsp = _pool_bwd(t["z"], dy, w["w_pool"][l], vec("s_pool", l), (2 * gw) // pw, gw // pw)
        small["w_pool"][l], small["s_pool"][l] = dwp, dsp.reshape(-1)
        dhc, dlg, dlb = _ln_swish_bwd(t["hc"], dy, vec("conv_ln_g", l), vec("conv_ln_b", l), (gw + pw) // cw)
        small["conv_ln_g"][l], small["conv_ln_b"][l] = dlg.reshape(-1), dlb.reshape(-1)
        dval, dgate, dwd, dbd = _conv_bwd(t["z"], dhc, wl["w_dw"], taps, (2 * gw + pw) // cb,
                                          (2 * gw + pw + cw) // cb)
        small["w_dw"][l], small["b_dw"][l] = dwd, dbd.reshape(-1)
        dz = jnp.concatenate([dzu, dzv, dzp, dval, dgate], axis=1)
        made = reducer.advance(dz)
        d_in = _mm_tn_col("mm_in_dw", t["h1"], dz)
        made += reducer.add("%d2" % l, l, {"w_in": d_in})
        dh = _mm_nt_col("mm_in_dx", dz, wl["w_in"], F32, after=made)
        made = []
    grad_x, _, _, dgn = _norm_bwd("norm_first_bwd", dx, dh, saved[0]["x"], None, None, vec("norm_mix_pre", 0))
    small["norm_mix_pre"][0] = dgn
    drained = reducer.drain([grad_x] + send_small(0))
    drained = [a for a in drained if all(a is not sent[2][0] for sent in small_sent)]

    like = [w[n] for n in REPLICATED]
    packed = [_pack_layers([p[n] for n in REPLICATED]) for p in (w, m, v)]
    updated, taps_grad = None, []
    for l in range(n_layers):
        landed = _wait_copies("small_wait_%d" % l, small_sent[l], _broadcast_plan, N_DEVICES - 1, drained)[0]
        total = _device_sum(landed)
        updated = _adamw_layer("adamw_small", l, packed[0], total, packed[1], packed[2], updated)
        start = sum(a[0].size for a in like)
        taps_grad.append(total.reshape(-1)[start:start + small["w_dw"][l].size].reshape(small["w_dw"][l].shape))
    grad, delta, new_m, new_v = (dict(zip(REPLICATED, _unpack_layers(u, like))) for u in updated)
    view = lambda a: a.reshape(-1, a.shape[-1])
    taps_grad = lax.dynamic_slice_in_dim(jnp.stack(taps_grad), chip * cb, cb, axis=2)
    updated = _adamw_flat("adamw_taps", view(w["w_dw"]), view(taps_grad), view(m["w_dw"]), view(v["w_dw"]))
    grad["w_dw"] = taps_grad
    delta["w_dw"], new_m["w_dw"], new_v["w_dw"] = (u.reshape(w["w_dw"].shape) for u in updated)
    for n in BIG:
        grad[n], delta[n], new_m[n], new_v[n] = reducer.done[n]

    return (loss, grad_x[None], *[grad[n] for n in WEIGHTS], *[delta[n] for n in WEIGHTS],
            *[new_m[n] for n in WEIGHTS], *[new_v[n] for n in WEIGHTS])


def kernel(x, mem, norm_mix_pre, norm_mix_post, w_in, w_out, gmlp_v_gain, w_spatial, b_spatial, w_pool, s_pool, w_dw, b_dw, conv_ln_g, conv_ln_b, norm_xattn_pre, norm_mem, norm_xattn_post, w_q, w_k, w_v, w_o, norm_ffn_pre, norm_ffn_post, w_up, w_down, loss_target, m_norm_mix_pre, m_norm_mix_post, m_w_in, m_w_out, m_gmlp_v_gain, m_w_spatial, m_b_spatial, m_w_pool, m_s_pool, m_w_dw, m_b_dw, m_conv_ln_g, m_conv_ln_b, m_norm_xattn_pre, m_norm_mem, m_norm_xattn_post, m_w_q, m_w_k, m_w_v, m_w_o, m_norm_ffn_pre, m_norm_ffn_post, m_w_up, m_w_down, v_norm_mix_pre, v_norm_mix_post, v_w_in, v_w_out, v_gmlp_v_gain, v_w_spatial, v_b_spatial, v_w_pool, v_s_pool, v_w_dw, v_b_dw, v_conv_ln_g, v_conv_ln_b, v_norm_xattn_pre, v_norm_mem, v_norm_xattn_post, v_w_q, v_w_k, v_w_v, v_w_o, v_norm_ffn_pre, v_norm_ffn_post, v_w_up, v_w_down):
    given = dict(locals())
    w = {n: given[n] for n in WEIGHTS}
    m = {n: given["m_" + n] for n in WEIGHTS}
    v = {n: given["v_" + n] for n in WEIGHTS}
    return _step(x[0], mem[0], loss_target[0], w, m, v)
```

```python
import functools

import jax
import jax.numpy as jnp
from jax import lax
from jax.experimental import pallas as pl
from jax.experimental.pallas import tpu as pltpu

F32 = jnp.float32
BF16 = jnp.bfloat16
MESH = pl.DeviceIdType.MESH

N_CHIPS = 4
N_DEVICES = 8
XATTN_HEADS = 4
POOL_WINDOWS = (2, 4, 8, 16)
RMS_EPS = 1e-6
LN_EPS = 1e-5
ADAM_LR, ADAM_B1, ADAM_B2, ADAM_EPS, ADAM_WD, ADAM_STEP = 0.001, 0.9, 0.999, 1e-08, 0.01, 10

V7X_LANES = 128
V7X_VMEM_LIMIT = 56 * 1024 * 1024
ROW_TILE = 256
NORM_ROW_TILE = 256
GMLP_UNROLL = 4
MM_TILE_M, MM_TILE_N, MM_TILE_K = 1024, 1024, 2048
MM_TILE_M_ONE_STEP = 1024

ANY = pl.BlockSpec(memory_space=pl.ANY)
HBM = pl.BlockSpec(memory_space=pltpu.HBM)
SEM = pl.BlockSpec(memory_space=pltpu.SEMAPHORE)


def _tile(dim, pref):
    if dim <= pref:
        return dim
    t = (pref // V7X_LANES) * V7X_LANES
    while t >= V7X_LANES:
        if dim % t == 0:
            return t
        t -= V7X_LANES
    return dim


def _mm_tiles(m, n, k, chunks=1):
    tk = _tile(k, MM_TILE_K)
    one_step = chunks * (k // tk) == 1
    return _tile(m, MM_TILE_M_ONE_STEP if one_step else MM_TILE_M), _tile(n, MM_TILE_N), tk


def _params(sem=None):
    return pltpu.CompilerParams(dimension_semantics=sem, vmem_limit_bytes=V7X_VMEM_LIMIT)


NN = (((1,), (0,)), ((), ()))
NT = (((1,), (1,)), ((), ()))
TN = (((0,), (0,)), ((), ()))


def _mm(name, a, b, *, dn, grid, a_spec, b_spec, o_specs, out_shapes, acc_shape, epi=None, extra=None,
        extra_spec=None, after=()):
    nk = grid[2]
    n_out = len(out_shapes)
    has_extra = extra is not None
    after = list(after)

    def body(*refs):
        a_ref, b_ref = refs[0], refs[1]
        pos = 2
        e_ref = None
        if has_extra:
            e_ref = refs[pos]
            pos += 1
        pos += len(after)
        o_refs = refs[pos:pos + n_out]
        part = lax.dot_general(a_ref[...].astype(BF16), b_ref[...].astype(BF16), dn, preferred_element_type=F32)

        def finish(total):
            if epi is None:
                vals = (total,)
            elif has_extra:
                vals = epi(total, e_ref[...])
            else:
                vals = epi(total)
            for o, v in zip(o_refs, vals):
                o[...] = v.astype(o.dtype)

        if nk == 1:
            finish(part)
            return
        acc = refs[pos + n_out]
        k = pl.program_id(2)

        @pl.when(k == 0)
        def _():
            acc[...] = part

        @pl.when(jnp.logical_and(k > 0, k < nk - 1))
        def _():
            acc[...] += part

        @pl.when(k == nk - 1)
        def _():
            finish(acc[...] + part)

    ins, in_specs = [a, b], [a_spec, b_spec]
    if has_extra:
        ins.append(extra)
        in_specs.append(extra_spec)
    ins += after
    in_specs += [ANY] * len(after)
    outs = pl.pallas_call(
        body, name=name, grid=grid, in_specs=in_specs, out_specs=list(o_specs), out_shape=list(out_shapes),
        scratch_shapes=[pltpu.VMEM(acc_shape, F32)] if nk > 1 else [],
        compiler_params=_params(("parallel", "parallel", "arbitrary")))(*ins)
    return outs


def _mm_nn_row(name, a, w, out_dtypes, epi=None, after=()):
    m, k = a.shape
    n = w.shape[1]
    tm, tn, tk = _mm_tiles(m, n, k)
    o_spec = pl.BlockSpec((tm, tn), lambda i, j, kk: (i, j))
    return _mm(name, a, w, dn=NN, grid=(m // tm, n // tn, k // tk),
               a_spec=pl.BlockSpec((tm, tk), lambda i, j, kk: (i, kk)),
               b_spec=pl.BlockSpec((tk, tn), lambda i, j, kk: (kk, j)),
               o_specs=[o_spec] * len(out_dtypes),
               out_shapes=[jax.ShapeDtypeStruct((m, n), d) for d in out_dtypes], acc_shape=(tm, tn), epi=epi,
               after=after)


def _mm_nn_col(name, a, w, out_dtypes, epi=None, after=()):
    m, k = a.shape
    c = w.shape[2]
    tm, tn, tk = _mm_tiles(m, c, k)
    nb = c // tn
    o_spec = pl.BlockSpec((tm, tn), lambda i, j, kk: (i, j))
    return _mm(name, a, w, dn=NN, grid=(m // tm, N_CHIPS * nb, k // tk),
               a_spec=pl.BlockSpec((tm, tk), lambda i, j, kk: (i, kk)),
               b_spec=pl.BlockSpec((None, tk, tn), lambda i, j, kk: (j // nb, kk, j % nb)),
               o_specs=[o_spec] * len(out_dtypes),
               out_shapes=[jax.ShapeDtypeStruct((m, N_CHIPS * c), d) for d in out_dtypes], acc_shape=(tm, tn),
               epi=epi, after=after)


def _mm_nt_row(name, dy, w, out_dtype, epi=None, extra=None, after=()):
    m, n = dy.shape
    k = w.shape[0]
    tm, tn, tk = _mm_tiles(m, k, n)
    o_spec = pl.BlockSpec((tm, tn), lambda i, j, kk: (i, j))
    return _mm(name, dy, w, dn=NT, grid=(m // tm, k // tn, n // tk),
               a_spec=pl.BlockSpec((tm, tk), lambda i, j, kk: (i, kk)),
               b_spec=pl.BlockSpec((tn, tk), lambda i, j, kk: (j, kk)),
               o_specs=[o_spec], out_shapes=[jax.ShapeDtypeStruct((m, k), out_dtype)], acc_shape=(tm, tn),
               epi=epi, extra=extra, extra_spec=o_spec, after=after)[0]


def _mm_nt_col(name, dy, w, out_dtype, after=()):
    m = dy.shape[0]
    k, c = w.shape[1], w.shape[2]
    tm, tn, tk = _mm_tiles(m, k, c, chunks=N_CHIPS)
    kb = c // tk
    return _mm(name, dy, w, dn=NT, grid=(m // tm, k // tn, N_CHIPS * kb),
               a_spec=pl.BlockSpec((tm, tk), lambda i, j, kk: (i, kk)),
               b_spec=pl.BlockSpec((None, tn, tk), lambda i, j, kk: (kk // kb, j, kk % kb)),
               o_specs=[pl.BlockSpec((tm, tn), lambda i, j, kk: (i, j))],
               out_shapes=[jax.ShapeDtypeStruct((m, k), out_dtype)], acc_shape=(tm, tn), after=after)[0]


def _mm_tn_row(name, a, dy):
    t, m = a.shape
    n = dy.shape[1]
    tm, tn, tk = _mm_tiles(m, n, t)
    return _mm(name, a, dy, dn=TN, grid=(m // tm, n // tn, t // tk),
               a_spec=pl.BlockSpec((tk, tm), lambda i, j, kk: (kk, i)),
               b_spec=pl.BlockSpec((tk, tn), lambda i, j, kk: (kk, j)),
               o_specs=[pl.BlockSpec((tm, tn), lambda i, j, kk: (i, j))],
               out_shapes=[jax.ShapeDtypeStruct((m, n), BF16)], acc_shape=(tm, tn))[0]


def _mm_tn_col(name, a, dy):
    t, m = a.shape
    c = dy.shape[1] // N_CHIPS
    tm, tn, tk = _mm_tiles(m, c, t)
    nb = c // tn
    return _mm(name, a, dy, dn=TN, grid=(m // tm, N_CHIPS * nb, t // tk),
               a_spec=pl.BlockSpec((tk, tm), lambda i, j, kk: (kk, i)),
               b_spec=pl.BlockSpec((tk, tn), lambda i, j, kk: (kk, j)),
               o_specs=[pl.BlockSpec((None, tm, tn), lambda i, j, kk: (j // nb, i, j % nb))],
               out_shapes=[jax.ShapeDtypeStruct((N_CHIPS, m, c), BF16)], acc_shape=(tm, tn))[0]


def _rms(x, g):
    r = lax.rsqrt(jnp.mean(x * x, axis=-1, keepdims=True) + RMS_EPS)
    return x * r * g


def _rms_bwd(x, g, dy):
    r = lax.rsqrt(jnp.mean(x * x, axis=-1, keepdims=True) + RMS_EPS)
    xr = x * r
    dyg = dy * g
    dx = r * (dyg - xr * jnp.mean(dyg * xr, axis=-1, keepdims=True))
    return dx, jnp.sum(dy * xr, axis=0, keepdims=True)


def _norm_fwd(name, x, o, g_post, g_next, after=()):
    after = list(after)
    s, d = x.shape
    tr = _tile(s, NORM_ROW_TILE)
    has_prev, has_next = o is not None, g_next is not None
    row = pl.BlockSpec((tr, d), lambda i: (i, 0))
    vec = pl.BlockSpec((1, d), lambda i: (0, 0))

    def body(*refs):
        refs = list(refs)
        xn = refs.pop(0)[...]
        if has_prev:
            o_ref, gp_ref = refs.pop(0), refs.pop(0)
            xn = xn + _rms(o_ref[...], gp_ref[...])
        gn_ref = refs.pop(0) if has_next else None
        del refs[:len(after)]
        if has_prev:
            refs.pop(0)[...] = xn
        if has_next:
            refs.pop(0)[...] = _rms(xn, gn_ref[...]).astype(BF16)

    ins, in_specs = [x], [row]
    if has_prev:
        ins += [o, g_post]
        in_specs += [row, vec]
    if has_next:
        ins.append(g_next)
        in_specs.append(vec)
    ins += after
    in_specs += [ANY] * len(after)
    out_shapes, out_specs = [], []
    if has_prev:
        out_shapes.append(jax.ShapeDtypeStruct((s, d), F32))
        out_specs.append(row)
    if has_next:
        out_shapes.append(jax.ShapeDtypeStruct((s, d), BF16))
        out_specs.append(row)
    outs = pl.pallas_call(body, name=name, grid=(s // tr,), in_specs=in_specs, out_specs=out_specs,
                          out_shape=out_shapes, compiler_params=_params(("parallel",)))(*ins)
    outs = list(outs)
    x_new = outs.pop(0) if has_prev else x
    h = outs.pop(0) if has_next else None
    return x_new, h


def _norm_bwd(name, dxn, dh, xn, o, g_post, g_next, after=()):
    after = list(after)
    s, d = xn.shape
    tr = _tile(s, NORM_ROW_TILE)
    has_prev, has_next, has_dxn = o is not None, dh is not None, dxn is not None
    row = pl.BlockSpec((tr, d), lambda i: (i, 0))
    vec = pl.BlockSpec((1, d), lambda i: (0, 0))

    def body(*refs):
        refs = list(refs)
        first = pl.program_id(0) == 0
        dxn_ref = refs.pop(0) if has_dxn else None
        dh_ref = refs.pop(0) if has_next else None
        xn_ref = refs.pop(0)
        if has_prev:
            o_ref, gp_ref = refs.pop(0), refs.pop(0)
        gn_ref = refs.pop(0) if has_next else None
        del refs[:len(after)]
        dx_ref = refs.pop(0)
        if has_prev:
            do_ref, dgp_ref = refs.pop(0), refs.pop(0)
        dgn_ref = refs.pop(0) if has_next else None

        def accumulate(ref, val):
            @pl.when(first)
            def _():
                ref[...] = val

            @pl.when(jnp.logical_not(first))
            def _():
                ref[...] += val

        dx = dxn_ref[...] if has_dxn else None
        if has_next:
            dxh, dgn = _rms_bwd(xn_ref[...], gn_ref[...], dh_ref[...].astype(F32))
            dx = dxh if dx is None else dx + dxh
            accumulate(dgn_ref, dgn)
        dx_ref[...] = dx
        if has_prev:
            do, dgp = _rms_bwd(o_ref[...], gp_ref[...], dx)
            do_ref[...] = do.astype(BF16)
            accumulate(dgp_ref, dgp)

    ins, in_specs = [], []
    if has_dxn:
        ins.append(dxn)
        in_specs.append(row)
    if has_next:
        ins.append(dh)
        in_specs.append(row)
    ins.append(xn)
    in_specs.append(row)
    if has_prev:
        ins += [o, g_post]
        in_specs += [row, vec]
    if has_next:
        ins.append(g_next)
        in_specs.append(vec)
    ins += after
    in_specs += [ANY] * len(after)
    out_shapes, out_specs = [jax.ShapeDtypeStruct((s, d), F32)], [row]
    if has_prev:
        out_shapes += [jax.ShapeDtypeStruct((s, d), BF16), jax.ShapeDtypeStruct((1, d), F32)]
        out_specs += [row, vec]
    if has_next:
        out_shapes.append(jax.ShapeDtypeStruct((1, d), F32))
        out_specs.append(vec)
    outs = list(pl.pallas_call(body, name=name, grid=(s // tr,), in_specs=in_specs, out_specs=out_specs,
                               out_shape=out_shapes, compiler_params=_params(("arbitrary",)))(*ins))
    dx = outs.pop(0)
    do, dgp = (outs.pop(0), outs.pop(0)) if has_prev else (None, None)
    dgn = outs.pop(0) if has_next else None
    return dx, do, dgp, dgn


def _loss_head(y, target):
    s, d = y.shape
    tr = _tile(s, ROW_TILE)
    row = pl.BlockSpec((tr, d), lambda i: (i, 0))
    vec = pl.BlockSpec((1, d), lambda i: (0, 0))

    def body(y_ref, t_ref, dy_ref, l_ref):
        err = y_ref[...] - t_ref[...]
        dy_ref[...] = err * (1.0 / d)
        part = jnp.sum(err * err, axis=0, keepdims=True) * (0.5 / d)

        @pl.when(pl.program_id(0) == 0)
        def _():
            l_ref[...] = part

        @pl.when(pl.program_id(0) != 0)
        def _():
            l_ref[...] += part

    return pl.pallas_call(body, name="loss_head", grid=(s // tr,), in_specs=[row, row], out_specs=[row, vec],
                          out_shape=[jax.ShapeDtypeStruct((s, d), F32), jax.ShapeDtypeStruct((1, d), F32)],
                          compiler_params=_params(("arbitrary",)))(y, target)


@jax.custom_vjp
def _bdot(a, b):
    return jnp.dot(a.astype(BF16), b.astype(BF16), preferred_element_type=F32)


def _bdot_fwd(a, b):
    return _bdot(a, b), (a, b)


def _bdot_bwd(res, ct):
    a, b = res
    ctb = ct.astype(BF16)
    da = lax.dot_general(ctb, b.astype(BF16), NT, preferred_element_type=F32)
    db = lax.dot_general(a.astype(BF16), ctb, TN, preferred_element_type=F32)
    return da, db


_bdot.defvjp(_bdot_fwd, _bdot_bwd)


@functools.partial(jax.custom_vjp, nondiff_argnums=(1,))
def _shift(x, k):
    n = x.shape[0]
    if k == 0:
        return x
    rolled = pltpu.roll(x, k % n, 0)
    t = lax.broadcasted_iota(jnp.int32, x.shape, 0)
    keep = (t >= k) if k > 0 else (t < n + k)
    return jnp.where(keep, rolled, 0.0)


def _shift_fwd(x, k):
    return _shift(x, k), None


def _shift_bwd(k, _, ct):
    return (_shift(ct, -k),)


_shift.defvjp(_shift_fwd, _shift_bwd)


def _sigmoid(x):
    return 1.0 / (1.0 + jnp.exp(-x))


def _layer_norm(x, g, b=None):
    mu = jnp.mean(x, axis=-1, keepdims=True)
    xc = x - mu
    var = jnp.mean(xc * xc, axis=-1, keepdims=True)
    y = xc * lax.rsqrt(var + LN_EPS) * g
    return y if b is None else y + b


def _gmlp_chunk(zu, zv, gv, w, bcol):
    ch = w.shape[0]
    u = jax.nn.gelu(zu)
    vn = _layer_norm(jax.nn.gelu(zv), gv)
    t = lax.broadcasted_iota(jnp.int32, (ch, ch), 0)
    s = lax.broadcasted_iota(jnp.int32, (ch, ch), 1)
    wm = jnp.where(t >= s, w, 0.0)
    return u * (_bdot(wm, vn) + bcol)


def _gmlp_specs(seq, heads, hd, ch, u_off, v_off):
    col = lambda off: pl.BlockSpec((seq, hd), lambda h: (0, off + h))
    return (col(u_off), col(v_off), pl.BlockSpec((None, 1, hd), lambda h: (h, 0, 0)),
            pl.BlockSpec((None, ch, ch), lambda h: (h, 0, 0)), pl.BlockSpec((None, ch, 1), lambda h: (h, 0, 0)))


def _gmlp_fwd(z, gv, ws, bcol):
    seq = z.shape[0]
    heads, _, hd = gv.shape
    ch = ws.shape[-1]
    zu_s, zv_s, gv_s, w_s, b_s = _gmlp_specs(seq, heads, hd, ch, 0, heads)

    def body(zu_ref, zv_ref, gv_ref, w_ref, b_ref, y_ref):
        gvv, w, bc = gv_ref[...], w_ref[...], b_ref[...]

        def step(c, carry):
            rows = pl.ds(pl.multiple_of(c * ch, ch), ch)
            y_ref[rows, :] = _gmlp_chunk(zu_ref[rows, :], zv_ref[rows, :], gvv, w, bc).astype(BF16)
            return carry

        lax.fori_loop(0, seq // ch, step, 0, unroll=GMLP_UNROLL)

    return pl.pallas_call(body, name="gmlp_fwd", grid=(heads,), in_specs=[zu_s, zv_s, gv_s, w_s, b_s],
                          out_specs=pl.BlockSpec((seq, hd), lambda h: (0, h)),
                          out_shape=jax.ShapeDtypeStruct((seq, heads * hd), BF16),
                          compiler_params=_params(("parallel",)))(z, z, gv, ws, bcol)


def _gmlp_bwd(z, dy, gv, ws, bcol):
    seq = z.shape[0]
    heads, _, hd = gv.shape
    ch = ws.shape[-1]
    zu_s, zv_s, gv_s, w_s, b_s = _gmlp_specs(seq, heads, hd, ch, 0, heads)
    col = pl.BlockSpec((seq, hd), lambda h: (0, h))

    def body(zu_ref, zv_ref, dy_ref, gv_ref, w_ref, b_ref, dzu_ref, dzv_ref, dgv_ref, dw_ref, db_ref):
        gvv, w, bc = gv_ref[...], w_ref[...], b_ref[...]

        together = GMLP_UNROLL if (seq // ch) % GMLP_UNROLL == 0 else 1

        def step(c, carry):
            dgv, dw, db = carry
            for u in range(together):
                rows = pl.ds(pl.multiple_of((c * together + u) * ch, ch), ch)
                _, vjp = jax.vjp(_gmlp_chunk, zu_ref[rows, :], zv_ref[rows, :], gvv, w, bc)
                dzu, dzv, dgv_c, dw_c, db_c = vjp(dy_ref[rows, :])
                dzu_ref[rows, :] = dzu.astype(BF16)
                dzv_ref[rows, :] = dzv.astype(BF16)
                dgv, dw, db = dgv + dgv_c, dw + dw_c, db + db_c
            return dgv, dw, db

        zero = (jnp.zeros((1, hd), F32), jnp.zeros((ch, ch), F32), jnp.zeros((ch, 1), F32))
        dgv, dw, db = lax.fori_loop(0, seq // ch // together, step, zero)
        dgv_ref[...] = dgv
        dw_ref[...] = dw
        db_ref[...] = db

    return pl.pallas_call(
        body, name="gmlp_bwd", grid=(heads,), in_specs=[zu_s, zv_s, col, gv_s, w_s, b_s],
        out_specs=[col, col, gv_s, w_s, b_s],
        out_shape=[jax.ShapeDtypeStruct((seq, heads * hd), BF16), jax.ShapeDtypeStruct((seq, heads * hd), BF16),
                   jax.ShapeDtypeStruct(gv.shape, F32), jax.ShapeDtypeStruct(ws.shape, F32),
                   jax.ShapeDtypeStruct(bcol.shape, F32)],
        compiler_params=_params(("parallel",)))(z, z, dy, gv, ws, bcol)


def _pool_group(p, w, s, window):
    win, span = p, 1
    while span < window:
        win = win + _shift(win, span)
        span *= 2
    t = lax.broadcasted_iota(jnp.int32, (p.shape[0], 1), 0).astype(F32)
    cnt = jnp.minimum(t + 1.0, float(window))
    return _bdot(win / cnt - p, w) * s


def _pool_fwd(z, w_pool, s_pool, col_block):
    seq = z.shape[0]
    groups, gw, _ = w_pool.shape
    pw = groups * gw

    def body(p_ref, w_ref, s_ref, y_ref):
        for g in range(groups):
            cols = slice(g * gw, (g + 1) * gw)
            y_ref[:, cols] = _pool_group(p_ref[:, cols], w_ref[g], s_ref[:, cols], POOL_WINDOWS[g]).astype(BF16)

    return pl.pallas_call(
        body, name="pool_fwd", grid=(1,),
        in_specs=[pl.BlockSpec((seq, pw), lambda i: (0, col_block)),
                  pl.BlockSpec((groups, gw, gw), lambda i: (0, 0, 0)), pl.BlockSpec((1, pw), lambda i: (0, 0))],
        out_specs=pl.BlockSpec((seq, pw), lambda i: (0, 0)), out_shape=jax.ShapeDtypeStruct((seq, pw), BF16),
        compiler_params=_params(("arbitrary",)))(z, w_pool, s_pool)


def _pool_bwd(z, dy, w_pool, s_pool, col_block, dy_block):
    seq = z.shape[0]
    groups, gw, _ = w_pool.shape
    pw = groups * gw

    def body(p_ref, dy_ref, w_ref, s_ref, dp_ref, dw_ref, ds_ref):
        for g in range(groups):
            cols = slice(g * gw, (g + 1) * gw)
            _, vjp = jax.vjp(functools.partial(_pool_group, window=POOL_WINDOWS[g]), p_ref[:, cols], w_ref[g],
                             s_ref[:, cols])
            dp, dw, ds = vjp(dy_ref[:, cols])
            dp_ref[:, cols] = dp.astype(BF16)
            dw_ref[g] = dw
            ds_ref[:, cols] = ds

    return pl.pallas_call(
        body, name="pool_bwd", grid=(1,),
        in_specs=[pl.BlockSpec((seq, pw), lambda i: (0, col_block)),
                  pl.BlockSpec((seq, pw), lambda i: (0, dy_block)),
                  pl.BlockSpec((groups, gw, gw), lambda i: (0, 0, 0)), pl.BlockSpec((1, pw), lambda i: (0, 0))],
        out_specs=[pl.BlockSpec((seq, pw), lambda i: (0, 0)), pl.BlockSpec((groups, gw, gw), lambda i: (0, 0, 0)),
                   pl.BlockSpec((1, pw), lambda i: (0, 0))],
        out_shape=[jax.ShapeDtypeStruct((seq, pw), BF16), jax.ShapeDtypeStruct(w_pool.shape, F32),
                   jax.ShapeDtypeStruct((1, pw), F32)],
        compiler_params=_params(("arbitrary",)))(z, dy, w_pool, s_pool)


def _conv_fwd(z, w_dw, taps, b_dw, val_block, gate_block, after=()):
    seq = z.shape[0]
    rows, cb = w_dw.shape[1], w_dw.shape[2]
    after = list(after)

    def body(val_ref, gate_ref, w_ref, b_ref, *refs):
        out_ref = refs[-1]
        h = val_ref[...] * _sigmoid(gate_ref[...])
        acc = jnp.broadcast_to(b_ref[...], h.shape)
        for d in range(taps):
            acc = acc + w_ref[pl.ds(taps - 1 - d, 1), :] * _shift(h, d)
        out_ref[...] = acc

    return pl.pallas_call(
        body, name="conv_fwd", grid=(N_CHIPS,),
        in_specs=[pl.BlockSpec((seq, cb), lambda j: (0, val_block + j)),
                  pl.BlockSpec((seq, cb), lambda j: (0, gate_block + j)),
                  pl.BlockSpec((None, rows, cb), lambda j: (j, 0, 0)),
                  pl.BlockSpec((1, cb), lambda j: (0, j))] + [ANY] * len(after),
        out_specs=pl.BlockSpec((seq, cb), lambda j: (0, j)),
        out_shape=jax.ShapeDtypeStruct((seq, N_CHIPS * cb), F32),
        compiler_params=_params(("parallel",)))(z, z, w_dw, b_dw, *after)


def _conv_bwd(z, dout, w_dw, taps, val_block, gate_block):
    seq = z.shape[0]
    rows, cb = w_dw.shape[1], w_dw.shape[2]
    col = pl.BlockSpec((seq, cb), lambda j: (0, j))

    def body(val_ref, gate_ref, do_ref, w_ref, dval_ref, dgate_ref, dw_ref, db_ref):
        val, sg, do = val_ref[...], _sigmoid(gate_ref[...]), do_ref[...]
        h = val * sg
        db_ref[...] = jnp.sum(do, axis=0, keepdims=True)
        dh = jnp.zeros_like(h)
        for d in range(taps):
            k = taps - 1 - d
            dw_ref[pl.ds(k, 1), :] = jnp.sum(do * _shift(h, d), axis=0, keepdims=True)
            dh = dh + w_ref[pl.ds(k, 1), :] * _shift(do, -d)
        dval_ref[...] = (dh * sg).astype(BF16)
        dgate_ref[...] = (dh * val * sg * (1.0 - sg)).astype(BF16)

    return pl.pallas_call(
        body, name="conv_bwd", grid=(N_CHIPS,),
        in_specs=[pl.BlockSpec((seq, cb), lambda j: (0, val_block + j)),
                  pl.BlockSpec((seq, cb), lambda j: (0, gate_block + j)), col,
                  pl.BlockSpec((None, rows, cb), lambda j: (j, 0, 0))],
        out_specs=[col, col, pl.BlockSpec((taps, cb), lambda j: (0, j)), pl.BlockSpec((1, cb), lambda j: (0, j))],
        out_shape=[jax.ShapeDtypeStruct((seq, N_CHIPS * cb), BF16), jax.ShapeDtypeStruct((seq, N_CHIPS * cb), BF16),
                   jax.ShapeDtypeStruct((taps, N_CHIPS * cb), F32), jax.ShapeDtypeStruct((1, N_CHIPS * cb), F32)],
        compiler_params=_params(("parallel",)))(z, z, dout, w_dw)


def _ln_swish(hc, g, b):
    y = _layer_norm(hc, g, b)
    return y * _sigmoid(y)


def _ln_swish_fwd(hc, g, b):
    s, cw = hc.shape
    tr = _tile(s, ROW_TILE)
    row = pl.BlockSpec((tr, cw), lambda i: (i, 0))
    vec = pl.BlockSpec((1, cw), lambda i: (0, 0))

    def body(h_ref, g_ref, b_ref, y_ref):
        y_ref[...] = _ln_swish(h_ref[...], g_ref[...], b_ref[...]).astype(BF16)

    return pl.pallas_call(body, name="ln_swish_fwd", grid=(s // tr,), in_specs=[row, vec, vec], out_specs=row,
                          out_shape=jax.ShapeDtypeStruct((s, cw), BF16),
                          compiler_params=_params(("parallel",)))(hc, g, b)


def _ln_swish_bwd(hc, dy, g, b, dy_block):
    s, cw = hc.shape
    tr = _tile(s, ROW_TILE)
    row = pl.BlockSpec((tr, cw), lambda i: (i, 0))
    vec = pl.BlockSpec((1, cw), lambda i: (0, 0))

    def body(h_ref, dy_ref, g_ref, b_ref, dh_ref, dg_ref, db_ref):
        _, vjp = jax.vjp(_ln_swish, h_ref[...], g_ref[...], b_ref[...])
        dh, dg, db = vjp(dy_ref[...])
        dh_ref[...] = dh

        @pl.when(pl.program_id(0) == 0)
        def _():
            dg_ref[...] = dg
            db_ref[...] = db

        @pl.when(pl.program_id(0) != 0)
        def _():
            dg_ref[...] += dg
            db_ref[...] += db

    return pl.pallas_call(
        body, name="ln_swish_bwd", grid=(s // tr,),
        in_specs=[row, pl.BlockSpec((tr, cw), lambda i: (i, dy_block)), vec, vec], out_specs=[row, vec, vec],
        out_shape=[jax.ShapeDtypeStruct((s, cw), F32), jax.ShapeDtypeStruct((1, cw), F32),
                   jax.ShapeDtypeStruct((1, cw), F32)],
        compiler_params=_params(("arbitrary",)))(hc, dy, g, b)


def _attn_probs(q, k, scale):
    s = lax.dot_general(q, k, NT, preferred_element_type=F32) * scale
    e = jnp.exp(s - jnp.max(s, axis=-1, keepdims=True))
    return e / jnp.sum(e, axis=-1, keepdims=True)


def _attn_fwd(q, k, v):
    seq, d = q.shape
    mem = k.shape[0]
    hd = d // XATTN_HEADS
    scale = hd ** -0.5
    qs = pl.BlockSpec((seq, hd), lambda h: (0, h))
    ms = pl.BlockSpec((mem, hd), lambda h: (0, h))

    def body(q_ref, k_ref, v_ref, a_ref):
        p = _attn_probs(q_ref[...], k_ref[...], scale)
        a_ref[...] = jnp.dot(p.astype(BF16), v_ref[...], preferred_element_type=F32).astype(BF16)

    return pl.pallas_call(body, name="attn_fwd", grid=(XATTN_HEADS,), in_specs=[qs, ms, ms], out_specs=qs,
                          out_shape=jax.ShapeDtypeStruct((seq, d), BF16),
                          compiler_params=_params(("parallel",)))(q, k, v)


def _attn_bwd(q, k, v, da):
    seq, d = q.shape
    mem = k.shape[0]
    hd = d // XATTN_HEADS
    scale = hd ** -0.5
    qs = pl.BlockSpec((seq, hd), lambda h: (0, h))
    ms = pl.BlockSpec((mem, hd), lambda h: (0, h))

    def body(q_ref, k_ref, v_ref, da_ref, dq_ref, dk_ref, dv_ref):
        q_, k_, v_, da_ = q_ref[...], k_ref[...], v_ref[...], da_ref[...]
        p = _attn_probs(q_, k_, scale)
        dv_ref[...] = lax.dot_general(p.astype(BF16), da_, TN, preferred_element_type=F32).astype(BF16)
        dp = lax.dot_general(da_, v_, NT, preferred_element_type=F32)
        ds = (p * (dp - jnp.sum(dp * p, axis=-1, keepdims=True)) * scale).astype(BF16)
        dq_ref[...] = jnp.dot(ds, k_, preferred_element_type=F32).astype(BF16)
        dk_ref[...] = lax.dot_general(ds, q_, TN, preferred_element_type=F32).astype(BF16)

    return pl.pallas_call(
        body, name="attn_bwd", grid=(XATTN_HEADS,), in_specs=[qs, ms, ms, qs], out_specs=[qs, ms, ms],
        out_shape=[jax.ShapeDtypeStruct((seq, d), BF16), jax.ShapeDtypeStruct((mem, d), BF16),
                   jax.ShapeDtypeStruct((mem, d), BF16)],
        compiler_params=_params(("parallel",)))(q, k, v, da)


def _place_shard(name, place, w, l, dtype, after=()):
    _, r, c = w.shape
    tr = _tile(r, 2 * ROW_TILE) if r % 16 == 0 else r
    after = list(after)

    def body(place_ref, w_ref, *refs):
        refs[-1][...] = w_ref[...].astype(dtype)

    return pl.pallas_call(
        body, name=name,
        grid_spec=pltpu.PrefetchScalarGridSpec(
            num_scalar_prefetch=1, grid=(r // tr,),
            in_specs=[pl.BlockSpec((None, tr, c), lambda i, p: (l, i, 0))] + [ANY] * len(after),
            out_specs=pl.BlockSpec((None, tr, c), lambda i, p: (p[1], i, 0))),
        out_shape=jax.ShapeDtypeStruct((N_CHIPS, r, c), dtype),
        compiler_params=_params(("parallel",)))(place, w, *after)


def _place_flat(place, flat):
    rows, lanes = flat.shape

    def body(place_ref, f_ref, o_ref):
        o_ref[...] = f_ref[...]

    return pl.pallas_call(
        body, name="place_flat",
        grid_spec=pltpu.PrefetchScalarGridSpec(
            num_scalar_prefetch=1, grid=(1,), in_specs=[pl.BlockSpec((rows, lanes), lambda i, p: (0, 0))],
            out_specs=pl.BlockSpec((None, rows, lanes), lambda i, p: (p[2], 0, 0))),
        out_shape=jax.ShapeDtypeStruct((N_DEVICES, rows, lanes), flat.dtype),
        compiler_params=_params(("arbitrary",)))(place, flat)


def _pair_sum(name, place, dw, got):
    n, _, r2, c = dw.shape
    tr = _tile(r2, 4 * ROW_TILE)

    def body(place_ref, own_ref, got_ref, s_ref, t_ref):
        val = (own_ref[...].astype(F32) + got_ref[...].astype(F32)).astype(BF16)
        s_ref[...] = val

        @pl.when(pl.program_id(1) == place_ref[1])
        def _():
            t_ref[...] = val

    slab = pl.BlockSpec((None, tr, c), lambda i, j, p: (j, i, 0))
    sds = jax.ShapeDtypeStruct((n, r2, c), BF16)
    return pl.pallas_call(
        body, name=name,
        grid_spec=pltpu.PrefetchScalarGridSpec(
            num_scalar_prefetch=1, grid=(r2 // tr, n),
            in_specs=[pl.BlockSpec((None, None, tr, c), lambda i, j, p: (j, p[0], i, 0)), slab],
            out_specs=[slab, pl.BlockSpec((None, tr, c), lambda i, j, p: (p[1], i, 0))]),
        out_shape=[sds, sds], compiler_params=_params(("parallel", "arbitrary")))(place, dw, got)


def _chip_sum(name, place, parts):
    n, r2, c = parts.shape
    tr = _tile(r2, ROW_TILE)

    def body(place_ref, *refs):
        o_ref = refs[n]
        acc = refs[0][...].astype(F32)
        for j in range(1, n):
            acc = acc + refs[j][...].astype(F32)
        o_ref[...] = acc

    part = lambda j: pl.BlockSpec((None, tr, c), lambda i, p: (j, i, 0))
    return pl.pallas_call(
        body, name=name,
        grid_spec=pltpu.PrefetchScalarGridSpec(
            num_scalar_prefetch=1, grid=(r2 // tr,), in_specs=[part(j) for j in range(n)],
            out_specs=pl.BlockSpec((None, tr, c), lambda i, p: (p[0], i, 0))),
        out_shape=jax.ShapeDtypeStruct((2, r2, c), F32),
        compiler_params=_params(("parallel",)))(place, *([parts] * n))


def _device_sum(parts):
    n, rows, lanes = parts.shape
    tr = _tile(rows, 4 * ROW_TILE) if rows % 8 == 0 else rows

    def body(p_ref, o_ref):
        acc = p_ref[0]
        for j in range(1, n):
            acc = acc + p_ref[j]
        o_ref[...] = acc

    return pl.pallas_call(
        body, name="device_sum", grid=(rows // tr,), in_specs=[pl.BlockSpec((n, tr, lanes), lambda i: (0, i, 0))],
        out_specs=pl.BlockSpec((tr, lanes), lambda i: (i, 0)), out_shape=jax.ShapeDtypeStruct((rows, lanes), F32),
        compiler_params=_params(("parallel",)))(parts)


def _adam_update(w, g, m, v):
    nm = ADAM_B1 * m + (1.0 - ADAM_B1) * g
    nv = ADAM_B2 * v + (1.0 - ADAM_B2) * (g * g)
    c1 = 1.0 - ADAM_B1 ** ADAM_STEP
    c2 = 1.0 - ADAM_B2 ** ADAM_STEP
    return -ADAM_LR * ((nm / c1) / (jnp.sqrt(nv / c2) + ADAM_EPS) + ADAM_WD * w), nm, nv


def _adamw_layer(name, l, w, g, m, v, prev):
    n_l, r, c = w.shape
    tr = _tile(r, ROW_TILE)
    slab = pl.BlockSpec((None, tr, c), lambda i: (l, i, 0))

    def body(w_ref, g_ref, m_ref, v_ref, *refs):
        go_ref, d_ref, nm_ref, nv_ref = refs[-4:]
        g_ = g_ref[...]
        delta, nm, nv = _adam_update(w_ref[...], g_, m_ref[...], v_ref[...])
        go_ref[...] = g_
        d_ref[...] = delta
        nm_ref[...] = nm
        nv_ref[...] = nv

    ins = [w, g, m, v]
    in_specs = [slab, pl.BlockSpec((tr, c), lambda i: (i, 0)), slab, slab]
    aliases = {}
    if prev is not None:
        aliases = {len(ins) + i: i for i in range(4)}
        ins += list(prev)
        in_specs += [ANY] * 4
    sds = jax.ShapeDtypeStruct((n_l, r, c), F32)
    return pl.pallas_call(body, name=name, grid=(r // tr,), in_specs=in_specs, out_specs=[slab] * 4,
                          out_shape=[sds] * 4, input_output_aliases=aliases,
                          compiler_params=_params(("parallel",)))(*ins)


def _adamw_flat(name, w, g, m, v):
    rows, cols = w.shape
    tr = _tile(rows, ROW_TILE) if rows % 8 == 0 else rows
    spec = pl.BlockSpec((tr, cols), lambda i: (i, 0))

    def body(w_ref, g_ref, m_ref, v_ref, d_ref, nm_ref, nv_ref):
        d_ref[...], nm_ref[...], nv_ref[...] = _adam_update(w_ref[...], g_ref[...], m_ref[...], v_ref[...])

    sds = jax.ShapeDtypeStruct((rows, cols), F32)
    return pl.pallas_call(body, name=name, grid=(rows // tr,), in_specs=[spec] * 4, out_specs=[spec] * 3,
                          out_shape=[sds] * 3, compiler_params=_params(("parallel",)))(w, g, m, v)


def _me():
    return lax.axis_index("x"), lax.axis_index("y"), lax.axis_index("c")


def _other_chips(x, y):
    return [(1 - x, y, 2 * (1 - x) + y), (x, 1 - y, 2 * x + 1 - y), (1 - x, 1 - y, 2 * (1 - x) + 1 - y)]


def _remote(src, dst, send_sem, recv_sem, target):
    return pltpu.make_async_remote_copy(src_ref=src, dst_ref=dst, send_sem=send_sem, recv_sem=recv_sem,
                                        device_id=target, device_id_type=MESH)


def _exchange(name, bufs, plan, n_copies):
    n = len(bufs)

    def body(*refs):
        send_sems, recv_sems = refs[2 * n:]
        copies = []
        for i, (src, dst, target) in enumerate(plan(refs[n:2 * n], _me())):
            if target is None:
                cp = pltpu.make_async_copy(src, dst, send_sems.at[i])
            else:
                cp = _remote(src, dst, send_sems.at[i], recv_sems.at[i], target)
            cp.start()
            copies.append((cp, target))
        assert len(copies) == n_copies
        for cp, target in copies:
            if target is None:
                cp.wait()
            else:
                cp.wait_recv()
        for cp, target in copies:
            if target is not None:
                cp.wait_send()

    return pl.pallas_call(
        body, name=name, in_specs=[ANY] * n, out_specs=[ANY] * n,
        out_shape=[jax.ShapeDtypeStruct(b.shape, b.dtype) for b in bufs],
        scratch_shapes=[pltpu.SemaphoreType.DMA((n_copies,)), pltpu.SemaphoreType.DMA((n_copies,))],
        input_output_aliases={i: i for i in range(n)},
        compiler_params=pltpu.CompilerParams(has_side_effects=True))(*bufs)


def _start_copies(name, groups, after=()):
    all_bufs = [b for bufs, _, _ in groups for b in bufs]
    after = list(after)
    n = len(all_bufs)
    n_g = len(groups)

    def body(*refs):
        in_refs, sem_refs = refs[:n], refs[n + len(after):n + len(after) + 2 * n_g]
        pos = 0
        for g, (bufs, plan, n_copies) in enumerate(groups):
            copies = plan(in_refs[pos:pos + len(bufs)], _me())
            assert len(copies) == n_copies
            for i, (src, dst, target) in enumerate(copies):
                _remote(src, dst, sem_refs[2 * g].at[i], sem_refs[2 * g + 1].at[i], target).start()
            pos += len(bufs)

    sems = []
    for _, _, n_copies in groups:
        sems += [pltpu.SemaphoreType.DMA((n_copies,))] * 2
    outs = pl.pallas_call(
        body, name=name, in_specs=[HBM] * n + [ANY] * len(after), out_specs=[SEM] * (2 * n_g) + [HBM] * n,
        out_shape=sems + [pltpu.HBM(b.shape, b.dtype) for b in all_bufs],
        input_output_aliases={i: 2 * n_g + i for i in range(n)},
        compiler_params=pltpu.CompilerParams(has_side_effects=pltpu.SideEffectType.DATAFLOW_SIDE_EFFECTING))(
            *[pltpu.with_memory_space_constraint(b, pltpu.HBM) for b in all_bufs], *after)
    result, pos = [], 2 * n_g
    for g, (bufs, _, _) in enumerate(groups):
        result.append((outs[2 * g], outs[2 * g + 1], list(outs[pos:pos + len(bufs)])))
        pos += len(bufs)
    return result


def _wait_copies(name, started, plan, n_copies, after):
    send_sems, recv_sems, bufs = started
    n = len(bufs)
    after = list(after) if isinstance(after, (list, tuple)) else [after]
    after = [a for a in after if all(a is not b for b in bufs)]

    def body(*refs):
        copies = plan(refs[:n], _me())
        assert len(copies) == n_copies
        for i, (src, dst, target) in enumerate(copies):
            cp = _remote(src, dst, refs[n].at[i], refs[n + 1].at[i], target)
            cp.wait_send()
            cp.wait_recv()

    return list(pl.pallas_call(
        body, name=name, in_specs=[HBM] * n + [SEM, SEM] + [ANY] * len(after), out_specs=[HBM] * n,
        out_shape=[pltpu.HBM(b.shape, b.dtype) for b in bufs], input_output_aliases={i: i for i in range(n)},
        compiler_params=pltpu.CompilerParams(has_side_effects=pltpu.SideEffectType.DATAFLOW_SIDE_EFFECTING))(
            *bufs, send_sems, recv_sems, *after))


def _halves(a):
    return a.reshape(a.shape[0], 2, a.shape[1] // 2, a.shape[2])


def _neighbour_slab(x, y, c):
    across_x, across_y = 2 * (1 - x) + y, 2 * x + 1 - y
    return across_x + c * (across_y - across_x)


def _direct_plan(refs, me):
    x, y, c = me
    mine = 2 * x + y
    target = (x + (1 - c) * (1 - 2 * x), y + c * (1 - 2 * y), c)
    return [(g.at[mine], g.at[mine], target) for g in refs]


def _relay_plan(refs, me):
    x, y, c = me
    src = _neighbour_slab(x, y, c)
    onward = (x + c * (1 - 2 * x), y + (1 - c) * (1 - 2 * y), c)
    copies = []
    for g in refs:
        copies += [(g.at[src, c], g.at[src, c], onward), (g.at[src], g.at[src], (x, y, 1 - c))]
    return copies


def _last_plan(refs, me):
    x, y, c = me
    far = 2 * (1 - x) + 1 - y
    return [(g.at[far, c], g.at[far, c], (x, y, 1 - c)) for g in refs]


def _swap_plan(refs, me):
    x, y, c = me
    k = len(refs) // 2
    return [(refs[i].at[j, 1 - c], refs[k + i].at[j], (x, y, 1 - c)) for i in range(k) for j in range(N_CHIPS)]


def _scatter_plan(refs, me):
    x, y, c = me
    mine = 2 * x + y
    k = len(refs) // 2
    return [(refs[i].at[chip], refs[k + i].at[mine], (px, py, c))
            for i in range(k) for px, py, chip in _other_chips(x, y)]


def _share_plan(refs, me):
    x, y, c = me
    return [(g.at[c], g.at[c], (x, y, 1 - c)) for g in refs]


def _broadcast_plan(refs, me):
    x, y, c = me
    mine = 4 * x + 2 * y + c
    copies = []
    for fx, fy, fc in [(0, 0, 1), (0, 1, 0), (0, 1, 1), (1, 0, 0), (1, 0, 1), (1, 1, 0), (1, 1, 1)]:
        peer = (x + fx - 2 * fx * x, y + fy - 2 * fy * y, c + fc - 2 * fc * c)
        copies.append((refs[0].at[mine], refs[0].at[mine], peer))
    return copies


BIG = ("w_in", "w_out", "w_q", "w_k", "w_v", "w_o", "w_up", "w_down")
COLUMN_SPLIT = ("w_in", "w_up")
WEIGHTS = ("norm_mix_pre", "norm_mix_post", "w_in", "w_out", "gmlp_v_gain", "w_spatial", "b_spatial", "w_pool",
           "s_pool", "w_dw", "b_dw", "conv_ln_g", "conv_ln_b", "norm_xattn_pre", "norm_mem", "norm_xattn_post",
           "w_q", "w_k", "w_v", "w_o", "norm_ffn_pre", "norm_ffn_post", "w_up", "w_down")
SMALL = tuple(n for n in WEIGHTS if n not in BIG)
REPLICATED = tuple(n for n in SMALL if n != "w_dw")
GATHER_GROUPS = (("w_in", "w_dw"), ("w_out", "w_q", "w_k", "w_v", "w_o"), ("w_up",), ("w_down",))


def _relu2(acc):
    r = jnp.maximum(acc, 0.0)
    return acc, r * r


def _relu2_bwd(acc, up):
    return (acc * (2.0 * jnp.maximum(up, 0.0)),)


def _pack(arrays):
    flat = jnp.concatenate([a.reshape(-1) for a in arrays])
    tile = 8 * V7X_LANES
    pad = (-flat.shape[0]) % tile
    return jnp.pad(flat, (0, pad)).reshape(-1, V7X_LANES)


def _pack_layers(arrays):
    n_l = arrays[0].shape[0]
    flat = jnp.concatenate([a.reshape(n_l, -1) for a in arrays], axis=1)
    pad = (-flat.shape[1]) % (8 * V7X_LANES)
    return jnp.pad(flat, ((0, 0), (0, pad))).reshape(n_l, -1, V7X_LANES)


def _unpack_layers(packed, like):
    flat = packed.reshape(packed.shape[0], -1)
    out, pos = [], 0
    for a in like:
        out.append(flat[:, pos:pos + a[0].size].reshape(a.shape))
        pos += a[0].size
    return out


class _GradientReducer:
    def __init__(self, place, w, m, v):
        self.place, self.w, self.m, self.v = place, w, m, v
        self.flying = []
        self.done = {n: None for n in BIG}

    def add(self, tag, l, grads):
        names = list(grads)
        views = [_halves(grads[n]) for n in names]
        zones = [lax.empty((v.shape[0],) + v.shape[2:], v.dtype) for v in views]
        started = _start_copies("swap_start_" + tag, [(views + zones, _swap_plan, N_CHIPS * len(names))])[0]
        self.flying.append(dict(stage=0, tag=tag, l=l, names=names, started=started))
        return [started[2][0]]

    def advance(self, after):
        made = []
        after = list(after) if isinstance(after, (list, tuple)) else [after]
        for item in self.flying:
            item["stage"] += 1
        for item in self.flying:
            tag, names, k = item["tag"], item["names"], len(item["names"])
            if item["stage"] == 1:
                bufs = _wait_copies("swap_wait_" + tag, item["started"], _swap_plan, N_CHIPS * k, after)
                sums, parts = zip(*[_pair_sum("pair_sum_" + n, self.place, dv, got)
                                    for n, dv, got in zip(names, bufs[:k], bufs[k:])])
                item["started"] = _start_copies("scatter_start_" + tag,
                                                [(list(sums) + list(parts), _scatter_plan, 3 * k)])[0]
                made.append(item["started"][2][0])
        after = after + made
        for item in list(self.flying):
            tag, names, k = item["tag"], item["names"], len(item["names"])
            if item["stage"] == 3:
                bufs = _wait_copies("scatter_wait_" + tag, item["started"], _scatter_plan, 3 * k, after)
                halves = [_chip_sum("chip_sum_" + n, self.place, p) for n, p in zip(names, bufs[k:])]
                item["started"] = _start_copies("share_start_" + tag, [(halves, _share_plan, k)])[0]
                made.append(item["started"][2][0])
            elif item["stage"] == 4:
                halves = _wait_copies("share_wait_" + tag, item["started"], _share_plan, k, after)
                for n, h in zip(names, halves):
                    g = h.reshape(self.w[n].shape[1:])
                    self.done[n] = _adamw_layer("adamw_" + n, item["l"], self.w[n], g, self.m[n], self.v[n],
                                                self.done[n])
                    made.append(self.done[n][3])
                self.flying.remove(item)
        return made

    def drain(self, after):
        made = list(after)
        while self.flying:
            made = list(after) + self.advance(made)
        return made


def _step(x, mem, target, w, m, v):
    n_layers = w["w_in"].shape[0]
    seq, d = x.shape
    heads, hd = w["gmlp_v_gain"].shape[1:]
    gw = heads * hd
    groups, pgw = w["w_pool"].shape[1:3]
    pw = groups * pgw
    cw = w["b_dw"].shape[1]
    cb = cw // N_CHIPS
    taps = w["w_dw"].shape[1]
    cx, cy, cc = _me()
    chip = 2 * cx + cy
    place = jnp.stack([cc, chip, 2 * chip + cc]).astype(jnp.int32)
    vec = lambda name, l: w[name][l].reshape(1, -1)

    taps_padded = jnp.pad(w["w_dw"], ((0, 0), (0, (-taps) % 16), (0, 0)))
    gathering = []

    def send_layer(l, last):
        for g, names in enumerate(GATHER_GROUPS):
            bufs = [_halves(_place_shard("place_" + n, place, taps_padded if n == "w_dw" else w[n], l,
                                         F32 if n == "w_dw" else BF16, after=last)) for n in names]
            gathering.extend(_start_copies("gather_start_%d%d" % (l, g), [(bufs, _direct_plan, len(names))],
                                           after=last))
            last = [gathering[-1][2][0]]
        return last

    def relay(l, g, after):
        names = GATHER_GROUPS[g]
        at = l * len(GATHER_GROUPS) + g
        bufs = _wait_copies("gather_wait_%d%d" % (l, g), gathering[at], _direct_plan, len(names), after)
        gathering[at] = _start_copies("relay_start_%d%d" % (l, g), [(bufs, _relay_plan, 2 * len(names))])[0]
        return [gathering[at][2][0]]

    def arrive(l, g, after):
        names = GATHER_GROUPS[g]
        tag = "%d%d" % (l, g)
        bufs = _wait_copies("relay_wait_" + tag, gathering[l * len(GATHER_GROUPS) + g], _relay_plan,
                            2 * len(names), after)
        bufs = _exchange("gather_last_" + tag, bufs, _last_plan, len(names))
        out = {}
        for n, b in zip(names, bufs):
            full = b.reshape(N_CHIPS, 2 * b.shape[2], b.shape[3])
            out[n] = full if n in COLUMN_SPLIT + ("w_dw",) else full.reshape(-1, full.shape[2])
        return out

    saved = []
    sent = send_layer(0, [])
    _, h1 = _norm_fwd("norm_first", x, None, None, vec("norm_mix_pre", 0))
    relay(0, 0, [h1] + sent)
    for l in range(n_layers):
        gv = w["gmlp_v_gain"][l].reshape(heads, 1, hd)
        ws = w["w_spatial"][l]
        bcol = w["b_spatial"][l].reshape(heads, -1, 1)
        wl = arrive(l, 0, [h1])
        early = relay(l, 1, [h1]) if l > 0 else []
        z = _mm_nn_col("mm_in", h1, wl["w_in"], [F32], after=early)[0]
        sent = send_layer(l + 1, [z]) if l + 1 < n_layers else []
        ya = _gmlp_fwd(z, gv, ws, bcol)
        yb = _pool_fwd(z, w["w_pool"][l], vec("s_pool", l), (2 * gw) // pw)
        hc = _conv_fwd(z, wl["w_dw"], taps, vec("b_dw", l), (2 * gw + pw) // cb, (2 * gw + pw + cw) // cb,
                       after=relay(l, 1, [ya]) if l == 0 else [])
        yc = _ln_swish_fwd(hc, vec("conv_ln_g", l), vec("conv_ln_b", l))
        y = jnp.concatenate([ya, yb, yc], axis=1)
        wl.update(arrive(l, 1, [y] + sent))
        o = _mm_nn_row("mm_out", y, wl["w_out"], [F32])[0]
        x1, h2 = _norm_fwd("norm_mix", x, o, vec("norm_mix_post", l), vec("norm_xattn_pre", l),
                           after=relay(l, 2, [o]))
        _, mn = _norm_fwd("norm_mem", mem, None, None, vec("norm_mem", l))
        q = _mm_nn_row("mm_q", h2, wl["w_q"], [BF16])[0]
        k = _mm_nn_row("mm_k", mn, wl["w_k"], [BF16])[0]
        vv = _mm_nn_row("mm_v", mn, wl["w_v"], [BF16])[0]
        a = _attn_fwd(q, k, vv)
        o2 = _mm_nn_row("mm_o", a, wl["w_o"], [F32], after=relay(l, 3, [a]))[0]
        x2, h3 = _norm_fwd("norm_xattn", x1, o2, vec("norm_xattn_post", l), vec("norm_ffn_pre", l))
        wl.update(arrive(l, 2, h3))
        up, r = _mm_nn_col("mm_up", h3, wl["w_up"], [BF16, BF16], epi=_relu2)
        wl.update(arrive(l, 3, r))
        o3 = _mm_nn_row("mm_down", r, wl["w_down"], [F32],
                        after=relay(l + 1, 0, [r]) if l + 1 < n_layers else [])[0]
        g_next = vec("norm_mix_pre", l + 1) if l + 1 < n_layers else None
        x3, h_next = _norm_fwd("norm_ffn", x2, o3, vec("norm_ffn_post", l), g_next)
        saved.append(dict(x=x, h1=h1, z=z, hc=hc, y=y, o=o, x1=x1, h2=h2, mn=mn, q=q, k=k, v=vv, a=a, o2=o2, x2=x2,
                          h3=h3, up=up, r=r, o3=o3, x3=x3, gv=gv, ws=ws, bcol=bcol, w=wl))
        x, h1 = x3, h_next

    dx, loss_parts = _loss_head(x, target)
    loss = lax.psum(jnp.sum(loss_parts), ("x", "y", "c"))

    reducer = _GradientReducer(place, w, m, v)
    small = {n: [None] * n_layers for n in SMALL}
    by_chip = lambda g: g.reshape(N_CHIPS, g.shape[0] // N_CHIPS, g.shape[1])
    small_sent = [None] * n_layers

    def small_layer(l):
        return [small[n][l].reshape(w[n].shape[1:]) for n in REPLICATED] + [small["w_dw"][l]]

    def send_small(l):
        landing = _place_flat(place, _pack(small_layer(l)))
        small_sent[l] = _start_copies("small_start_%d" % l, [([landing], _broadcast_plan, N_DEVICES - 1)])[0]
        return [small_sent[l][2][0]]

    dh = None
    made = []
    for l in reversed(range(n_layers)):
        t = saved[l]
        wl = t["w"]
        g_next = vec("norm_mix_pre", l + 1) if l + 1 < n_layers else None
        dx, do3, dgp, dgn = _norm_bwd("norm_ffn_bwd", dx, dh, t["x3"], t["o3"], vec("norm_ffn_post", l), g_next,
                                      after=made)
        small["norm_ffn_post"][l] = dgp
        if dgn is not None:
            small["norm_mix_pre"][l + 1] = dgn
        made = reducer.advance(dx)
        if l + 1 < n_layers:
            made += send_small(l + 1)
        d_down = _mm_tn_row("mm_down_dw", t["r"], do3)
        made += reducer.add("%d0" % l, l, {"w_down": by_chip(d_down)})
        dup = _mm_nt_row("mm_down_dx", do3, wl["w_down"], BF16, epi=_relu2_bwd, extra=t["up"], after=made)
        d_up = _mm_tn_col("mm_up_dw", t["h3"], dup)
        made = reducer.add("%d3" % l, l, {"w_up": d_up})
        dh3 = _mm_nt_col("mm_up_dx", dup, wl["w_up"], F32, after=made)
        made = []
        dx, do2, dgp, dgn = _norm_bwd("norm_xattn_bwd", dx, dh3, t["x2"], t["o2"], vec("norm_xattn_post", l),
                                      vec("norm_ffn_pre", l), after=made)
        small["norm_xattn_post"][l], small["norm_ffn_pre"][l] = dgp, dgn
        made = reducer.advance(dx)
        d_o = _mm_tn_row("mm_o_dw", t["a"], do2)
        da = _mm_nt_row("mm_o_dx", do2, wl["w_o"], BF16, after=made)
        dq, dk, dv = _attn_bwd(t["q"], t["k"], t["v"], da)
        d_q = _mm_tn_row("mm_q_dw", t["h2"], dq)
        d_k = _mm_tn_row("mm_k_dw", t["mn"], dk)
        d_v = _mm_tn_row("mm_v_dw", t["mn"], dv)
        dh2 = _mm_nt_row("mm_q_dx", dq, wl["w_q"], F32)
        dmn = _mm_nt_row("mm_k_dx", dk, wl["w_k"], F32) + _mm_nt_row("mm_v_dx", dv, wl["w_v"], F32)
        _, _, _, small["norm_mem"][l] = _norm_bwd("norm_mem_bwd", None, dmn, mem, None, None, vec("norm_mem", l))
        dx, do, dgp, dgn = _norm_bwd("norm_mix_bwd", dx, dh2, t["x1"], t["o"], vec("norm_mix_post", l),
                                     vec("norm_xattn_pre", l))
        small["norm_mix_post"][l], small["norm_xattn_pre"][l] = dgp, dgn
        made = reducer.advance(dx)
        d_out = _mm_tn_row("mm_out_dw", t["y"], do)
        made += reducer.add("%d1" % l, l, {"w_o": by_chip(d_o), "w_q": by_chip(d_q), "w_k": by_chip(d_k),
                                    "w_v": by_chip(d_v), "w_out": by_chip(d_out)})
        dy = _mm_nt_row("mm_out_dx", do, wl["w_out"], F32, after=made)
        dzu, dzv, dgv, dws, dbcol = _gmlp_bwd(t["z"], dy, t["gv"], t["ws"], t["bcol"])
        small["gmlp_v_gain"][l] = dgv.reshape(heads, hd)
        small["w_spatial"][l] = dws
        small["b_spatial"][l] = dbcol.reshape(heads, -1)
        dzp, dwp, dsp = _pool_bwd(t["z"], dy, w["w_pool"][l], vec("s_pool", l), (2 * gw) // pw, gw // pw)
        small["w_pool"][l], small["s_pool"][l] = dwp, dsp.reshape(-1)
        dhc, dlg, dlb = _ln_swish_bwd(t["hc"], dy, vec("conv_ln_g", l), vec("conv_ln_b", l), (gw + pw) // cw)
        small["conv_ln_g"][l], small["conv_ln_b"][l] = dlg.reshape(-1), dlb.reshape(-1)
        dval, dgate, dwd, dbd = _conv_bwd(t["z"], dhc, wl["w_dw"], taps, (2 * gw + pw) // cb,
                                          (2 * gw + pw + cw) // cb)
        small["w_dw"][l], small["b_dw"][l] = dwd, dbd.reshape(-1)
        dz = jnp.concatenate([dzu, dzv, dzp, dval, dgate], axis=1)
        made = reducer.advance(dz)
        d_in = _mm_tn_col("mm_in_dw", t["h1"], dz)
        made += reducer.add("%d2" % l, l, {"w_in": d_in})
        dh = _mm_nt_col("mm_in_dx", dz, wl["w_in"], F32, after=made)
        made = []
    grad_x, _, _, dgn = _norm_bwd("norm_first_bwd", dx, dh, saved[0]["x"], None, None, vec("norm_mix_pre", 0))
    small["norm_mix_pre"][0] = dgn
    drained = reducer.drain([grad_x] + send_small(0))
    drained = [a for a in drained if all(a is not sent[2][0] for sent in small_sent)]

    like = [w[n] for n in REPLICATED]
    packed = [_pack_layers([p[n] for n in REPLICATED]) for p in (w, m, v)]
    updated, taps_grad = None, []
    for l in range(n_layers):
        landed = _wait_copies("small_wait_%d" % l, small_sent[l], _broadcast_plan, N_DEVICES - 1, drained)[0]
        total = _device_sum(landed)
        updated = _adamw_layer("adamw_small", l, packed[0], total, packed[1], packed[2], updated)
        start = sum(a[0].size for a in like)
        taps_grad.append(total.reshape(-1)[start:start + small["w_dw"][l].size].reshape(small["w_dw"][l].shape))
    grad, delta, new_m, new_v = (dict(zip(REPLICATED, _unpack_layers(u, like))) for u in updated)
    view = lambda a: a.reshape(-1, a.shape[-1])
    taps_grad = lax.dynamic_slice_in_dim(jnp.stack(taps_grad), chip * cb, cb, axis=2)
    updated = _adamw_flat("adamw_taps", view(w["w_dw"]), view(taps_grad), view(m["w_dw"]), view(v["w_dw"]))
    grad["w_dw"] = taps_grad
    delta["w_dw"], new_m["w_dw"], new_v["w_dw"] = (u.reshape(w["w_dw"].shape) for u in updated)
    for n in BIG:
        grad[n], delta[n], new_m[n], new_v[n] = reducer.done[n]

    return (loss, grad_x[None], *[grad[n] for n in WEIGHTS], *[delta[n] for n in WEIGHTS],
            *[new_m[n] for n in WEIGHTS], *[new_v[n] for n in WEIGHTS])


def kernel(x, mem, norm_mix_pre, norm_mix_post, w_in, w_out, gmlp_v_gain, w_spatial, b_spatial, w_pool, s_pool, w_dw, b_dw, conv_ln_g, conv_ln_b, norm_xattn_pre, norm_mem, norm_xattn_post, w_q, w_k, w_v, w_o, norm_ffn_pre, norm_ffn_post, w_up, w_down, loss_target, m_norm_mix_pre, m_norm_mix_post, m_w_in, m_w_out, m_gmlp_v_gain, m_w_spatial, m_b_spatial, m_w_pool, m_s_pool, m_w_dw, m_b_dw, m_conv_ln_g, m_conv_ln_b, m_norm_xattn_pre, m_norm_mem, m_norm_xattn_post, m_w_q, m_w_k, m_w_v, m_w_o, m_norm_ffn_pre, m_norm_ffn_post, m_w_up, m_w_down, v_norm_mix_pre, v_norm_mix_post, v_w_in, v_w_out, v_gmlp_v_gain, v_w_spatial, v_b_spatial, v_w_pool, v_s_pool, v_w_dw, v_b_dw, v_conv_ln_g, v_conv_ln_b, v_norm_xattn_pre, v_norm_mem, v_norm_xattn_post, v_w_q, v_w_k, v_w_v, v_w_o, v_norm_ffn_pre, v_norm_ffn_post, v_w_up, v_w_down):
    given = dict(locals())
    w = {n: given[n] for n in WEIGHTS}
    m = {n: given["m_" + n] for n in WEIGHTS}
    v = {n: given["v_" + n] for n in WEIGHTS}
    return _step(x[0], mem[0], loss_target[0], w, m, v)
```

```python
import functools

import jax
import jax.numpy as jnp
from jax import lax
from jax.experimental import pallas as pl
from jax.experimental.pallas import tpu as pltpu

F32 = jnp.float32
BF16 = jnp.bfloat16
MESH = pl.DeviceIdType.MESH

N_CHIPS = 4
N_DEVICES = 8
XATTN_HEADS = 4
POOL_WINDOWS = (2, 4, 8, 16)
RMS_EPS = 1e-6
LN_EPS = 1e-5
ADAM_LR, ADAM_B1, ADAM_B2, ADAM_EPS, ADAM_WD, ADAM_STEP = 0.001, 0.9, 0.999, 1e-08, 0.01, 10

V7X_LANES = 128
V7X_VMEM_LIMIT = 56 * 1024 * 1024
ROW_TILE = 256
NORM_ROW_TILE = 256
GMLP_UNROLL = 4
MM_TILE_M, MM_TILE_N, MM_TILE_K = 1024, 1024, 2048
MM_TILE_M_ONE_STEP = 1024

ANY = pl.BlockSpec(memory_space=pl.ANY)
HBM = pl.BlockSpec(memory_space=pltpu.HBM)
SEM = pl.BlockSpec(memory_space=pltpu.SEMAPHORE)


def _tile(dim, pref):
    if dim <= pref:
        return dim
    t = (pref // V7X_LANES) * V7X_LANES
    while t >= V7X_LANES:
        if dim % t == 0:
            return t
        t -= V7X_LANES
    return dim


def _mm_tiles(m, n, k, chunks=1):
    tk = _tile(k, MM_TILE_K)
    one_step = chunks * (k // tk) == 1
    return _tile(m, MM_TILE_M_ONE_STEP if one_step else MM_TILE_M), _tile(n, MM_TILE_N), tk


def _params(sem=None):
    return pltpu.CompilerParams(dimension_semantics=sem, vmem_limit_bytes=V7X_VMEM_LIMIT)


NN = (((1,), (0,)), ((), ()))
NT = (((1,), (1,)), ((), ()))
TN = (((0,), (0,)), ((), ()))


def _mm(name, a, b, *, dn, grid, a_spec, b_spec, o_specs, out_shapes, acc_shape, epi=None, extra=None,
        extra_spec=None, after=()):
    nk = grid[2]
    n_out = len(out_shapes)
    has_extra = extra is not None
    after = list(after)

    def body(*refs):
        a_ref, b_ref = refs[0], refs[1]
        pos = 2
        e_ref = None
        if has_extra:
            e_ref = refs[pos]
            pos += 1
        pos += len(after)
        o_refs = refs[pos:pos + n_out]
        part = lax.dot_general(a_ref[...].astype(BF16), b_ref[...].astype(BF16), dn, preferred_element_type=F32)

        def finish(total):
            if epi is None:
                vals = (total,)
            elif has_extra:
                vals = epi(total, e_ref[...])
            else:
                vals = epi(total)
            for o, v in zip(o_refs, vals):
                o[...] = v.astype(o.dtype)

        if nk == 1:
            finish(part)
            return
        acc = refs[pos + n_out]
        k = pl.program_id(2)

        @pl.when(k == 0)
        def _():
            acc[...] = part

        @pl.when(jnp.logical_and(k > 0, k < nk - 1))
        def _():
            acc[...] += part

        @pl.when(k == nk - 1)
        def _():
            finish(acc[...] + part)

    ins, in_specs = [a, b], [a_spec, b_spec]
    if has_extra:
        ins.append(extra)
        in_specs.append(extra_spec)
    ins += after
    in_specs += [ANY] * len(after)
    outs = pl.pallas_call(
        body, name=name, grid=grid, in_specs=in_specs, out_specs=list(o_specs), out_shape=list(out_shapes),
        scratch_shapes=[pltpu.VMEM(acc_shape, F32)] if nk > 1 else [],
        compiler_params=_params(("parallel", "parallel", "arbitrary")))(*ins)
    return outs


def _mm_nn_row(name, a, w, out_dtypes, epi=None, after=()):
    m, k = a.shape
    n = w.shape[1]
    tm, tn, tk = _mm_tiles(m, n, k)
    o_spec = pl.BlockSpec((tm, tn), lambda i, j, kk: (i, j))
    return _mm(name, a, w, dn=NN, grid=(m // tm, n // tn, k // tk),
               a_spec=pl.BlockSpec((tm, tk), lambda i, j, kk: (i, kk)),
               b_spec=pl.BlockSpec((tk, tn), lambda i, j, kk: (kk, j)),
               o_specs=[o_spec] * len(out_dtypes),
               out_shapes=[jax.ShapeDtypeStruct((m, n), d) for d in out_dtypes], acc_shape=(tm, tn), epi=epi,
               after=after)


def _mm_nn_col(name, a, w, out_dtypes, epi=None, after=()):
    m, k = a.shape
    c = w.shape[2]
    tm, tn, tk = _mm_tiles(m, c, k)
    nb = c // tn
    o_spec = pl.BlockSpec((tm, tn), lambda i, j, kk: (i, j))
    return _mm(name, a, w, dn=NN, grid=(m // tm, N_CHIPS * nb, k // tk),
               a_spec=pl.BlockSpec((tm, tk), lambda i, j, kk: (i, kk)),
               b_spec=pl.BlockSpec((None, tk, tn), lambda i, j, kk: (j // nb, kk, j % nb)),
               o_specs=[o_spec] * len(out_dtypes),
               out_shapes=[jax.ShapeDtypeStruct((m, N_CHIPS * c), d) for d in out_dtypes], acc_shape=(tm, tn),
               epi=epi, after=after)


def _mm_nt_row(name, dy, w, out_dtype, epi=None, extra=None, after=()):
    m, n = dy.shape
    k = w.shape[0]
    tm, tn, tk = _mm_tiles(m, k, n)
    o_spec = pl.BlockSpec((tm, tn), lambda i, j, kk: (i, j))
    return _mm(name, dy, w, dn=NT, grid=(m // tm, k // tn, n // tk),
               a_spec=pl.BlockSpec((tm, tk), lambda i, j, kk: (i, kk)),
               b_spec=pl.BlockSpec((tn, tk), lambda i, j, kk: (j, kk)),
               o_specs=[o_spec], out_shapes=[jax.ShapeDtypeStruct((m, k), out_dtype)], acc_shape=(tm, tn),
               epi=epi, extra=extra, extra_spec=o_spec, after=after)[0]


def _mm_nt_col(name, dy, w, out_dtype, after=()):
    m = dy.shape[0]
    k, c = w.shape[1], w.shape[2]
    tm, tn, tk = _mm_tiles(m, k, c, chunks=N_CHIPS)
    kb = c // tk
    return _mm(name, dy, w, dn=NT, grid=(m // tm, k // tn, N_CHIPS * kb),
               a_spec=pl.BlockSpec((tm, tk), lambda i, j, kk: (i, kk)),
               b_spec=pl.BlockSpec((None, tn, tk), lambda i, j, kk: (kk // kb, j, kk % kb)),
               o_specs=[pl.BlockSpec((tm, tn), lambda i, j, kk: (i, j))],
               out_shapes=[jax.ShapeDtypeStruct((m, k), out_dtype)], acc_shape=(tm, tn), after=after)[0]


def _mm_tn_row(name, a, dy):
    t, m = a.shape
    n = dy.shape[1]
    tm, tn, tk = _mm_tiles(m, n, t)
    return _mm(name, a, dy, dn=TN, grid=(m // tm, n // tn, t // tk),
               a_spec=pl.BlockSpec((tk, tm), lambda i, j, kk: (kk, i)),
               b_spec=pl.BlockSpec((tk, tn), lambda i, j, kk: (kk, j)),
               o_specs=[pl.BlockSpec((tm, tn), lambda i, j, kk: (i, j))],
               out_shapes=[jax.ShapeDtypeStruct((m, n), BF16)], acc_shape=(tm, tn))[0]


def _mm_tn_col(name, a, dy):
    t, m = a.shape
    c = dy.shape[1] // N_CHIPS
    tm, tn, tk = _mm_tiles(m, c, t)
    nb = c // tn
    return _mm(name, a, dy, dn=TN, grid=(m // tm, N_CHIPS * nb, t // tk),
               a_spec=pl.BlockSpec((tk, tm), lambda i, j, kk: (kk, i)),
               b_spec=pl.BlockSpec((tk, tn), lambda i, j, kk: (kk, j)),
               o_specs=[pl.BlockSpec((None, tm, tn), lambda i, j, kk: (j // nb, i, j % nb))],
               out_shapes=[jax.ShapeDtypeStruct((N_CHIPS, m, c), BF16)], acc_shape=(tm, tn))[0]


def _rms(x, g):
    r = lax.rsqrt(jnp.mean(x * x, axis=-1, keepdims=True) + RMS_EPS)
    return x * r * g


def _rms_bwd(x, g, dy):
    r = lax.rsqrt(jnp.mean(x * x, axis=-1, keepdims=True) + RMS_EPS)
    xr = x * r
    dyg = dy * g
    dx = r * (dyg - xr * jnp.mean(dyg * xr, axis=-1, keepdims=True))
    return dx, jnp.sum(dy * xr, axis=0, keepdims=True)


def _norm_fwd(name, x, o, g_post, g_next, after=()):
    after = list(after)
    s, d = x.shape
    tr = _tile(s, NORM_ROW_TILE)
    has_prev, has_next = o is not None, g_next is not None
    row = pl.BlockSpec((tr, d), lambda i: (i, 0))
    vec = pl.BlockSpec((1, d), lambda i: (0, 0))

    def body(*refs):
        refs = list(refs)
        xn = refs.pop(0)[...]
        if has_prev:
            o_ref, gp_ref = refs.pop(0), refs.pop(0)
            xn = xn + _rms(o_ref[...], gp_ref[...])
        gn_ref = refs.pop(0) if has_next else None
        del refs[:len(after)]
        if has_prev:
            refs.pop(0)[...] = xn
        if has_next:
            refs.pop(0)[...] = _rms(xn, gn_ref[...]).astype(BF16)

    ins, in_specs = [x], [row]
    if has_prev:
        ins += [o, g_post]
        in_specs += [row, vec]
    if has_next:
        ins.append(g_next)
        in_specs.append(vec)
    ins += after
    in_specs += [ANY] * len(after)
    out_shapes, out_specs = [], []
    if has_prev:
        out_shapes.append(jax.ShapeDtypeStruct((s, d), F32))
        out_specs.append(row)
    if has_next:
        out_shapes.append(jax.ShapeDtypeStruct((s, d), BF16))
        out_specs.append(row)
    outs = pl.pallas_call(body, name=name, grid=(s // tr,), in_specs=in_specs, out_specs=out_specs,
                          out_shape=out_shapes, compiler_params=_params(("parallel",)))(*ins)
    outs = list(outs)
    x_new = outs.pop(0) if has_prev else x
    h = outs.pop(0) if has_next else None
    return x_new, h


def _norm_bwd(name, dxn, dh, xn, o, g_post, g_next, after=()):
    after = list(after)
    s, d = xn.shape
    tr = _tile(s, NORM_ROW_TILE)
    has_prev, has_next, has_dxn = o is not None, dh is not None, dxn is not None
    row = pl.BlockSpec((tr, d), lambda i: (i, 0))
    vec = pl.BlockSpec((1, d), lambda i: (0, 0))

    def body(*refs):
        refs = list(refs)
        first = pl.program_id(0) == 0
        dxn_ref = refs.pop(0) if has_dxn else None
        dh_ref = refs.pop(0) if has_next else None
        xn_ref = refs.pop(0)
        if has_prev:
            o_ref, gp_ref = refs.pop(0), refs.pop(0)
        gn_ref = refs.pop(0) if has_next else None
        del refs[:len(after)]
        dx_ref = refs.pop(0)
        if has_prev:
            do_ref, dgp_ref = refs.pop(0), refs.pop(0)
        dgn_ref = refs.pop(0) if has_next else None

        def accumulate(ref, val):
            @pl.when(first)
            def _():
                ref[...] = val

            @pl.when(jnp.logical_not(first))
            def _():
                ref[...] += val

        dx = dxn_ref[...] if has_dxn else None
        if has_next:
            dxh, dgn = _rms_bwd(xn_ref[...], gn_ref[...], dh_ref[...].astype(F32))
            dx = dxh if dx is None else dx + dxh
            accumulate(dgn_ref, dgn)
        dx_ref[...] = dx
        if has_prev:
            do, dgp = _rms_bwd(o_ref[...], gp_ref[...], dx)
            do_ref[...] = do.astype(BF16)
            accumulate(dgp_ref, dgp)

    ins, in_specs = [], []
    if has_dxn:
        ins.append(dxn)
        in_specs.append(row)
    if has_next:
        ins.append(dh)
        in_specs.append(row)
    ins.append(xn)
    in_specs.append(row)
    if has_prev:
        ins += [o, g_post]
        in_specs += [row, vec]
    if has_next:
        ins.append(g_next)
        in_specs.append(vec)
    ins += after
    in_specs += [ANY] * len(after)
    out_shapes, out_specs = [jax.ShapeDtypeStruct((s, d), F32)], [row]
    if has_prev:
        out_shapes += [jax.ShapeDtypeStruct((s, d), BF16), jax.ShapeDtypeStruct((1, d), F32)]
        out_specs += [row, vec]
    if has_next:
        out_shapes.append(jax.ShapeDtypeStruct((1, d), F32))
        out_specs.append(vec)
    outs = list(pl.pallas_call(body, name=name, grid=(s // tr,), in_specs=in_specs, out_specs=out_specs,
                               out_shape=out_shapes, compiler_params=_params(("arbitrary",)))(*ins))
    dx = outs.pop(0)
    do, dgp = (outs.pop(0), outs.pop(0)) if has_prev else (None, None)
    dgn = outs.pop(0) if has_next else None
    return dx, do, dgp, dgn


def _loss_head(y, target):
    s, d = y.shape
    tr = _tile(s, ROW_TILE)
    row = pl.BlockSpec((tr, d), lambda i: (i, 0))
    vec = pl.BlockSpec((1, d), lambda i: (0, 0))

    def body(y_ref, t_ref, dy_ref, l_ref):
        err = y_ref[...] - t_ref[...]
        dy_ref[...] = err * (1.0 / d)
        part = jnp.sum(err * err, axis=0, keepdims=True) * (0.5 / d)

        @pl.when(pl.program_id(0) == 0)
        def _():
            l_ref[...] = part

        @pl.when(pl.program_id(0) != 0)
        def _():
            l_ref[...] += part

    return pl.pallas_call(body, name="loss_head", grid=(s // tr,), in_specs=[row, row], out_specs=[row, vec],
                          out_shape=[jax.ShapeDtypeStruct((s, d), F32), jax.ShapeDtypeStruct((1, d), F32)],
                          compiler_params=_params(("arbitrary",)))(y, target)


@jax.custom_vjp
def _bdot(a, b):
    return jnp.dot(a.astype(BF16), b.astype(BF16), preferred_element_type=F32)


def _bdot_fwd(a, b):
    return _bdot(a, b), (a, b)


def _bdot_bwd(res, ct):
    a, b = res
    ctb = ct.astype(BF16)
    da = lax.dot_general(ctb, b.astype(BF16), NT, preferred_element_type=F32)
    db = lax.dot_general(a.astype(BF16), ctb, TN, preferred_element_type=F32)
    return da, db


_bdot.defvjp(_bdot_fwd, _bdot_bwd)


@functools.partial(jax.custom_vjp, nondiff_argnums=(1,))
def _shift(x, k):
    n = x.shape[0]
    if k == 0:
        return x
    rolled = pltpu.roll(x, k % n, 0)
    t = lax.broadcasted_iota(jnp.int32, x.shape, 0)
    keep = (t >= k) if k > 0 else (t < n + k)
    return jnp.where(keep, rolled, 0.0)


def _shift_fwd(x, k):
    return _shift(x, k), None


def _shift_bwd(k, _, ct):
    return (_shift(ct, -k),)


_shift.defvjp(_shift_fwd, _shift_bwd)


def _sigmoid(x):
    return 1.0 / (1.0 + jnp.exp(-x))


def _layer_norm(x, g, b=None):
    mu = jnp.mean(x, axis=-1, keepdims=True)
    xc = x - mu
    var = jnp.mean(xc * xc, axis=-1, keepdims=True)
    y = xc * lax.rsqrt(var + LN_EPS) * g
    return y if b is None else y + b


def _gmlp_chunk(zu, zv, gv, w, bcol):
    ch = w.shape[0]
    u = jax.nn.gelu(zu)
    vn = _layer_norm(jax.nn.gelu(zv), gv)
    t = lax.broadcasted_iota(jnp.int32, (ch, ch), 0)
    s = lax.broadcasted_iota(jnp.int32, (ch, ch), 1)
    wm = jnp.where(t >= s, w, 0.0)
    return u * (_bdot(wm, vn) + bcol)


def _gmlp_specs(seq, heads, hd, ch, u_off, v_off):
    col = lambda off: pl.BlockSpec((seq, hd), lambda h: (0, off + h))
    return (col(u_off), col(v_off), pl.BlockSpec((None, 1, hd), lambda h: (h, 0, 0)),
            pl.BlockSpec((None, ch, ch), lambda h: (h, 0, 0)), pl.BlockSpec((None, ch, 1), lambda h: (h, 0, 0)))


def _gmlp_fwd(z, gv, ws, bcol):
    seq = z.shape[0]
    heads, _, hd = gv.shape
    ch = ws.shape[-1]
    zu_s, zv_s, gv_s, w_s, b_s = _gmlp_specs(seq, heads, hd, ch, 0, heads)

    def body(zu_ref, zv_ref, gv_ref, w_ref, b_ref, y_ref):
        gvv, w, bc = gv_ref[...], w_ref[...], b_ref[...]

        def step(c, carry):
            rows = pl.ds(pl.multiple_of(c * ch, ch), ch)
            y_ref[rows, :] = _gmlp_chunk(zu_ref[rows, :], zv_ref[rows, :], gvv, w, bc).astype(BF16)
            return carry

        lax.fori_loop(0, seq // ch, step, 0, unroll=GMLP_UNROLL)

    return pl.pallas_call(body, name="gmlp_fwd", grid=(heads,), in_specs=[zu_s, zv_s, gv_s, w_s, b_s],
                          out_specs=pl.BlockSpec((seq, hd), lambda h: (0, h)),
                          out_shape=jax.ShapeDtypeStruct((seq, heads * hd), BF16),
                          compiler_params=_params(("parallel",)))(z, z, gv, ws, bcol)


def _gmlp_bwd(z, dy, gv, ws, bcol):
    seq = z.shape[0]
    heads, _, hd = gv.shape
    ch = ws.shape[-1]
    zu_s, zv_s, gv_s, w_s, b_s = _gmlp_specs(seq, heads, hd, ch, 0, heads)
    col = pl.BlockSpec((seq, hd), lambda h: (0, h))

    def body(zu_ref, zv_ref, dy_ref, gv_ref, w_ref, b_ref, dzu_ref, dzv_ref, dgv_ref, dw_ref, db_ref):
        gvv, w, bc = gv_ref[...], w_ref[...], b_ref[...]

        together = GMLP_UNROLL if (seq // ch) % GMLP_UNROLL == 0 else 1

        def step(c, carry):
            dgv, dw, db = carry
            for u in range(together):
                rows = pl.ds(pl.multiple_of((c * together + u) * ch, ch), ch)
                _, vjp = jax.vjp(_gmlp_chunk, zu_ref[rows, :], zv_ref[rows, :], gvv, w, bc)
                dzu, dzv, dgv_c, dw_c, db_c = vjp(dy_ref[rows, :])
                dzu_ref[rows, :] = dzu.astype(BF16)
                dzv_ref[rows, :] = dzv.astype(BF16)
                dgv, dw, db = dgv + dgv_c, dw + dw_c, db + db_c
            return dgv, dw, db

        zero = (jnp.zeros((1, hd), F32), jnp.zeros((ch, ch), F32), jnp.zeros((ch, 1), F32))
        dgv, dw, db = lax.fori_loop(0, seq // ch // together, step, zero)
        dgv_ref[...] = dgv
        dw_ref[...] = dw
        db_ref[...] = db

    return pl.pallas_call(
        body, name="gmlp_bwd", grid=(heads,), in_specs=[zu_s, zv_s, col, gv_s, w_s, b_s],
        out_specs=[col, col, gv_s, w_s, b_s],
        out_shape=[jax.ShapeDtypeStruct((seq, heads * hd), BF16), jax.ShapeDtypeStruct((seq, heads * hd), BF16),
                   jax.ShapeDtypeStruct(gv.shape, F32), jax.ShapeDtypeStruct(ws.shape, F32),
                   jax.ShapeDtypeStruct(bcol.shape, F32)],
        compiler_params=_params(("parallel",)))(z, z, dy, gv, ws, bcol)


def _pool_group(p, w, s, window):
    win, span = p, 1
    while span < window:
        win = win + _shift(win, span)
        span *= 2
    t = lax.broadcasted_iota(jnp.int32, (p.shape[0], 1), 0).astype(F32)
    cnt = jnp.minimum(t + 1.0, float(window))
    return _bdot(win / cnt - p, w) * s


def _pool_fwd(z, w_pool, s_pool, col_block):
    seq = z.shape[0]
    groups, gw, _ = w_pool.shape
    pw = groups * gw

    def body(p_ref, w_ref, s_ref, y_ref):
        for g in range(groups):
            cols = slice(g * gw, (g + 1) * gw)
            y_ref[:, cols] = _pool_group(p_ref[:, cols], w_ref[g], s_ref[:, cols], POOL_WINDOWS[g]).astype(BF16)

    return pl.pallas_call(
        body, name="pool_fwd", grid=(1,),
        in_specs=[pl.BlockSpec((seq, pw), lambda i: (0, col_block)),
                  pl.BlockSpec((groups, gw, gw), lambda i: (0, 0, 0)), pl.BlockSpec((1, pw), lambda i: (0, 0))],
        out_specs=pl.BlockSpec((seq, pw), lambda i: (0, 0)), out_shape=jax.ShapeDtypeStruct((seq, pw), BF16),
        compiler_params=_params(("arbitrary",)))(z, w_pool, s_pool)


def _pool_bwd(z, dy, w_pool, s_pool, col_block, dy_block):
    seq = z.shape[0]
    groups, gw, _ = w_pool.shape
    pw = groups * gw

    def body(p_ref, dy_ref, w_ref, s_ref, dp_ref, dw_ref, ds_ref):
        for g in range(groups):
            cols = slice(g * gw, (g + 1) * gw)
            _, vjp = jax.vjp(functools.partial(_pool_group, window=POOL_WINDOWS[g]), p_ref[:, cols], w_ref[g],
                             s_ref[:, cols])
            dp, dw, ds = vjp(dy_ref[:, cols])
            dp_ref[:, cols] = dp.astype(BF16)
            dw_ref[g] = dw
            ds_ref[:, cols] = ds

    return pl.pallas_call(
        body, name="pool_bwd", grid=(1,),
        in_specs=[pl.BlockSpec((seq, pw), lambda i: (0, col_block)),
                  pl.BlockSpec((seq, pw), lambda i: (0, dy_block)),
                  pl.BlockSpec((groups, gw, gw), lambda i: (0, 0, 0)), pl.BlockSpec((1, pw), lambda i: (0, 0))],
        out_specs=[pl.BlockSpec((seq, pw), lambda i: (0, 0)), pl.BlockSpec((groups, gw, gw), lambda i: (0, 0, 0)),
                   pl.BlockSpec((1, pw), lambda i: (0, 0))],
        out_shape=[jax.ShapeDtypeStruct((seq, pw), BF16), jax.ShapeDtypeStruct(w_pool.shape, F32),
                   jax.ShapeDtypeStruct((1, pw), F32)],
        compiler_params=_params(("arbitrary",)))(z, dy, w_pool, s_pool)


def _conv_fwd(z, w_dw, taps, b_dw, val_block, gate_block, after=()):
    seq = z.shape[0]
    rows, cb = w_dw.shape[1], w_dw.shape[2]
    after = list(after)

    def body(val_ref, gate_ref, w_ref, b_ref, *refs):
        out_ref = refs[-1]
        h = val_ref[...] * _sigmoid(gate_ref[...])
        acc = jnp.broadcast_to(b_ref[...], h.shape)
        for d in range(taps):
            acc = acc + w_ref[pl.ds(taps - 1 - d, 1), :] * _shift(h, d)
        out_ref[...] = acc

    return pl.pallas_call(
        body, name="conv_fwd", grid=(N_CHIPS,),
        in_specs=[pl.BlockSpec((seq, cb), lambda j: (0, val_block + j)),
                  pl.BlockSpec((seq, cb), lambda j: (0, gate_block + j)),
                  pl.BlockSpec((None, rows, cb), lambda j: (j, 0, 0)),
                  pl.BlockSpec((1, cb), lambda j: (0, j))] + [ANY] * len(after),
        out_specs=pl.BlockSpec((seq, cb), lambda j: (0, j)),
        out_shape=jax.ShapeDtypeStruct((seq, N_CHIPS * cb), F32),
        compiler_params=_params(("parallel",)))(z, z, w_dw, b_dw, *after)


def _conv_bwd(z, dout, w_dw, taps, val_block, gate_block):
    seq = z.shape[0]
    rows, cb = w_dw.shape[1], w_dw.shape[2]
    col = pl.BlockSpec((seq, cb), lambda j: (0, j))

    def body(val_ref, gate_ref, do_ref, w_ref, dval_ref, dgate_ref, dw_ref, db_ref):
        val, sg, do = val_ref[...], _sigmoid(gate_ref[...]), do_ref[...]
        h = val * sg
        db_ref[...] = jnp.sum(do, axis=0, keepdims=True)
        dh = jnp.zeros_like(h)
        for d in range(taps):
            k = taps - 1 - d
            dw_ref[pl.ds(k, 1), :] = jnp.sum(do * _shift(h, d), axis=0, keepdims=True)
            dh = dh + w_ref[pl.ds(k, 1), :] * _shift(do, -d)
        dval_ref[...] = (dh * sg).astype(BF16)
        dgate_ref[...] = (dh * val * sg * (1.0 - sg)).astype(BF16)

    return pl.pallas_call(
        body, name="conv_bwd", grid=(N_CHIPS,),
        in_specs=[pl.BlockSpec((seq, cb), lambda j: (0, val_block + j)),
                  pl.BlockSpec((seq, cb), lambda j: (0, gate_block + j)), col,
                  pl.BlockSpec((None, rows, cb), lambda j: (j, 0, 0))],
        out_specs=[col, col, pl.BlockSpec((taps, cb), lambda j: (0, j)), pl.BlockSpec((1, cb), lambda j: (0, j))],
        out_shape=[jax.ShapeDtypeStruct((seq, N_CHIPS * cb), BF16), jax.ShapeDtypeStruct((seq, N_CHIPS * cb), BF16),
                   jax.ShapeDtypeStruct((taps, N_CHIPS * cb), F32), jax.ShapeDtypeStruct((1, N_CHIPS * cb), F32)],
        compiler_params=_params(("parallel",)))(z, z, dout, w_dw)


def _ln_swish(hc, g, b):
    y = _layer_norm(hc, g, b)
    return y * _sigmoid(y)


def _ln_swish_fwd(hc, g, b):
    s, cw = hc.shape
    tr = _tile(s, ROW_TILE)
    row = pl.BlockSpec((tr, cw), lambda i: (i, 0))
    vec = pl.BlockSpec((1, cw), lambda i: (0, 0))

    def body(h_ref, g_ref, b_ref, y_ref):
        y_ref[...] = _ln_swish(h_ref[...], g_ref[...], b_ref[...]).astype(BF16)

    return pl.pallas_call(body, name="ln_swish_fwd", grid=(s // tr,), in_specs=[row, vec, vec], out_specs=row,
                          out_shape=jax.ShapeDtypeStruct((s, cw), BF16),
                          compiler_params=_params(("parallel",)))(hc, g, b)


def _ln_swish_bwd(hc, dy, g, b, dy_block):
    s, cw = hc.shape
    tr = _tile(s, ROW_TILE)
    row = pl.BlockSpec((tr, cw), lambda i: (i, 0))
    vec = pl.BlockSpec((1, cw), lambda i: (0, 0))

    def body(h_ref, dy_ref, g_ref, b_ref, dh_ref, dg_ref, db_ref):
        _, vjp = jax.vjp(_ln_swish, h_ref[...], g_ref[...], b_ref[...])
        dh, dg, db = vjp(dy_ref[...])
        dh_ref[...] = dh

        @pl.when(pl.program_id(0) == 0)
        def _():
            dg_ref[...] = dg
            db_ref[...] = db

        @pl.when(pl.program_id(0) != 0)
        def _():
            dg_ref[...] += dg
            db_ref[...] += db

    return pl.pallas_call(
        body, name="ln_swish_bwd", grid=(s // tr,),
        in_specs=[row, pl.BlockSpec((tr, cw), lambda i: (i, dy_block)), vec, vec], out_specs=[row, vec, vec],
        out_shape=[jax.ShapeDtypeStruct((s, cw), F32), jax.ShapeDtypeStruct((1, cw), F32),
                   jax.ShapeDtypeStruct((1, cw), F32)],
        compiler_params=_params(("arbitrary",)))(hc, dy, g, b)


def _attn_probs(q, k, scale):
    s = lax.dot_general(q, k, NT, preferred_element_type=F32) * scale
    e = jnp.exp(s - jnp.max(s, axis=-1, keepdims=True))
    return e / jnp.sum(e, axis=-1, keepdims=True)


def _attn_fwd(q, k, v):
    seq, d = q.shape
    mem = k.shape[0]
    hd = d // XATTN_HEADS
    scale = hd ** -0.5
    qs = pl.BlockSpec((seq, hd), lambda h: (0, h))
    ms = pl.BlockSpec((mem, hd), lambda h: (0, h))

    def body(q_ref, k_ref, v_ref, a_ref):
        p = _attn_probs(q_ref[...], k_ref[...], scale)
        a_ref[...] = jnp.dot(p.astype(BF16), v_ref[...], preferred_element_type=F32).astype(BF16)

    return pl.pallas_call(body, name="attn_fwd", grid=(XATTN_HEADS,), in_specs=[qs, ms, ms], out_specs=qs,
                          out_shape=jax.ShapeDtypeStruct((seq, d), BF16),
                          compiler_params=_params(("parallel",)))(q, k, v)


def _attn_bwd(q, k, v, da):
    seq, d = q.shape
    mem = k.shape[0]
    hd = d // XATTN_HEADS
    scale = hd ** -0.5
    qs = pl.BlockSpec((seq, hd), lambda h: (0, h))
    ms = pl.BlockSpec((mem, hd), lambda h: (0, h))

    def body(q_ref, k_ref, v_ref, da_ref, dq_ref, dk_ref, dv_ref):
        q_, k_, v_, da_ = q_ref[...], k_ref[...], v_ref[...], da_ref[...]
        p = _attn_probs(q_, k_, scale)
        dv_ref[...] = lax.dot_general(p.astype(BF16), da_, TN, preferred_element_type=F32).astype(BF16)
        dp = lax.dot_general(da_, v_, NT, preferred_element_type=F32)
        ds = (p * (dp - jnp.sum(dp * p, axis=-1, keepdims=True)) * scale).astype(BF16)
        dq_ref[...] = jnp.dot(ds, k_, preferred_element_type=F32).astype(BF16)
        dk_ref[...] = lax.dot_general(ds, q_, TN, preferred_element_type=F32).astype(BF16)

    return pl.pallas_call(
        body, name="attn_bwd", grid=(XATTN_HEADS,), in_specs=[qs, ms, ms, qs], out_specs=[qs, ms, ms],
        out_shape=[jax.ShapeDtypeStruct((seq, d), BF16), jax.ShapeDtypeStruct((mem, d), BF16),
                   jax.ShapeDtypeStruct((mem, d), BF16)],
        compiler_params=_params(("parallel",)))(q, k, v, da)


def _place_shard(name, place, w, l, dtype, after=()):
    _, r, c = w.shape
    tr = _tile(r, 2 * ROW_TILE) if r % 16 == 0 else r
    after = list(after)

    def body(place_ref, w_ref, *refs):
        refs[-1][...] = w_ref[...].astype(dtype)

    return pl.pallas_call(
        body, name=name,
        grid_spec=pltpu.PrefetchScalarGridSpec(
            num_scalar_prefetch=1, grid=(r // tr,),
            in_specs=[pl.BlockSpec((None, tr, c), lambda i, p: (l, i, 0))] + [ANY] * len(after),
            out_specs=pl.BlockSpec((None, tr, c), lambda i, p: (p[1], i, 0))),
        out_shape=jax.ShapeDtypeStruct((N_CHIPS, r, c), dtype),
        compiler_params=_params(("parallel",)))(place, w, *after)


def _place_flat(place, flat):
    rows, lanes = flat.shape

    def body(place_ref, f_ref, o_ref):
        o_ref[...] = f_ref[...]

    return pl.pallas_call(
        body, name="place_flat",
        grid_spec=pltpu.PrefetchScalarGridSpec(
            num_scalar_prefetch=1, grid=(1,), in_specs=[pl.BlockSpec((rows, lanes), lambda i, p: (0, 0))],
            out_specs=pl.BlockSpec((None, rows, lanes), lambda i, p: (p[2], 0, 0))),
        out_shape=jax.ShapeDtypeStruct((N_DEVICES, rows, lanes), flat.dtype),
        compiler_params=_params(("arbitrary",)))(place, flat)


def _pair_sum(name, place, dw, got):
    n, _, r2, c = dw.shape
    tr = _tile(r2, 4 * ROW_TILE)

    def body(place_ref, own_ref, got_ref, s_ref, t_ref):
        val = (own_ref[...].astype(F32) + got_ref[...].astype(F32)).astype(BF16)
        s_ref[...] = val

        @pl.when(pl.program_id(1) == place_ref[1])
        def _():
            t_ref[...] = val

    slab = pl.BlockSpec((None, tr, c), lambda i, j, p: (j, i, 0))
    sds = jax.ShapeDtypeStruct((n, r2, c), BF16)
    return pl.pallas_call(
        body, name=name,
        grid_spec=pltpu.PrefetchScalarGridSpec(
            num_scalar_prefetch=1, grid=(r2 // tr, n),
            in_specs=[pl.BlockSpec((None, None, tr, c), lambda i, j, p: (j, p[0], i, 0)), slab],
            out_specs=[slab, pl.BlockSpec((None, tr, c), lambda i, j, p: (p[1], i, 0))]),
        out_shape=[sds, sds], compiler_params=_params(("parallel", "arbitrary")))(place, dw, got)


def _chip_sum(name, place, parts):
    n, r2, c = parts.shape
    tr = _tile(r2, ROW_TILE)

    def body(place_ref, *refs):
        o_ref = refs[n]
        acc = refs[0][...].astype(F32)
        for j in range(1, n):
            acc = acc + refs[j][...].astype(F32)
        o_ref[...] = acc

    part = lambda j: pl.BlockSpec((None, tr, c), lambda i, p: (j, i, 0))
    return pl.pallas_call(
        body, name=name,
        grid_spec=pltpu.PrefetchScalarGridSpec(
            num_scalar_prefetch=1, grid=(r2 // tr,), in_specs=[part(j) for j in range(n)],
            out_specs=pl.BlockSpec((None, tr, c), lambda i, p: (p[0], i, 0))),
        out_shape=jax.ShapeDtypeStruct((2, r2, c), F32),
        compiler_params=_params(("parallel",)))(place, *([parts] * n))


def _device_sum(parts):
    n, rows, lanes = parts.shape
    tr = _tile(rows, 4 * ROW_TILE) if rows % 8 == 0 else rows

    def body(p_ref, o_ref):
        acc = p_ref[0]
        for j in range(1, n):
            acc = acc + p_ref[j]
        o_ref[...] = acc

    return pl.pallas_call(
        body, name="device_sum", grid=(rows // tr,), in_specs=[pl.BlockSpec((n, tr, lanes), lambda i: (0, i, 0))],
        out_specs=pl.BlockSpec((tr, lanes), lambda i: (i, 0)), out_shape=jax.ShapeDtypeStruct((rows, lanes), F32),
        compiler_params=_params(("parallel",)))(parts)


def _adam_update(w, g, m, v):
    nm = ADAM_B1 * m + (1.0 - ADAM_B1) * g
    nv = ADAM_B2 * v + (1.0 - ADAM_B2) * (g * g)
    c1 = 1.0 - ADAM_B1 ** ADAM_STEP
    c2 = 1.0 - ADAM_B2 ** ADAM_STEP
    return -ADAM_LR * ((nm / c1) / (jnp.sqrt(nv / c2) + ADAM_EPS) + ADAM_WD * w), nm, nv


def _adamw_layer(name, l, w, g, m, v, prev):
    n_l, r, c = w.shape
    tr = _tile(r, ROW_TILE)
    slab = pl.BlockSpec((None, tr, c), lambda i: (l, i, 0))

    def body(w_ref, g_ref, m_ref, v_ref, *refs):
        go_ref, d_ref, nm_ref, nv_ref = refs[-4:]
        g_ = g_ref[...]
        delta, nm, nv = _adam_update(w_ref[...], g_, m_ref[...], v_ref[...])
        go_ref[...] = g_
        d_ref[...] = delta
        nm_ref[...] = nm
        nv_ref[...] = nv

    ins = [w, g, m, v]
    in_specs = [slab, pl.BlockSpec((tr, c), lambda i: (i, 0)), slab, slab]
    aliases = {}
    if prev is not None:
        aliases = {len(ins) + i: i for i in range(4)}
        ins += list(prev)
        in_specs += [ANY] * 4
    sds = jax.ShapeDtypeStruct((n_l, r, c), F32)
    return pl.pallas_call(body, name=name, grid=(r // tr,), in_specs=in_specs, out_specs=[slab] * 4,
                          out_shape=[sds] * 4, input_output_aliases=aliases,
                          compiler_params=_params(("parallel",)))(*ins)


def _adamw_flat(name, w, g, m, v):
    rows, cols = w.shape
    tr = _tile(rows, ROW_TILE) if rows % 8 == 0 else rows
    spec = pl.BlockSpec((tr, cols), lambda i: (i, 0))

    def body(w_ref, g_ref, m_ref, v_ref, d_ref, nm_ref, nv_ref):
        d_ref[...], nm_ref[...], nv_ref[...] = _adam_update(w_ref[...], g_ref[...], m_ref[...], v_ref[...])

    sds = jax.ShapeDtypeStruct((rows, cols), F32)
    return pl.pallas_call(body, name=name, grid=(rows // tr,), in_specs=[spec] * 4, out_specs=[spec] * 3,
                          out_shape=[sds] * 3, compiler_params=_params(("parallel",)))(w, g, m, v)


def _me():
    return lax.axis_index("x"), lax.axis_index("y"), lax.axis_index("c")


def _other_chips(x, y):
    return [(1 - x, y, 2 * (1 - x) + y), (x, 1 - y, 2 * x + 1 - y), (1 - x, 1 - y, 2 * (1 - x) + 1 - y)]


def _remote(src, dst, send_sem, recv_sem, target):
    return pltpu.make_async_remote_copy(src_ref=src, dst_ref=dst, send_sem=send_sem, recv_sem=recv_sem,
                                        device_id=target, device_id_type=MESH)


def _exchange(name, bufs, plan, n_copies):
    n = len(bufs)

    def body(*refs):
        send_sems, recv_sems = refs[2 * n:]
        copies = []
        for i, (src, dst, target) in enumerate(plan(refs[n:2 * n], _me())):
            if target is None:
                cp = pltpu.make_async_copy(src, dst, send_sems.at[i])
            else:
                cp = _remote(src, dst, send_sems.at[i], recv_sems.at[i], target)
            cp.start()
            copies.append((cp, target))
        assert len(copies) == n_copies
        for cp, target in copies:
            if target is None:
                cp.wait()
            else:
                cp.wait_recv()
        for cp, target in copies:
            if target is not None:
                cp.wait_send()

    return pl.pallas_call(
        body, name=name, in_specs=[ANY] * n, out_specs=[ANY] * n,
        out_shape=[jax.ShapeDtypeStruct(b.shape, b.dtype) for b in bufs],
        scratch_shapes=[pltpu.SemaphoreType.DMA((n_copies,)), pltpu.SemaphoreType.DMA((n_copies,))],
        input_output_aliases={i: i for i in range(n)},
        compiler_params=pltpu.CompilerParams(has_side_effects=True))(*bufs)


def _start_copies(name, groups, after=(), source_token=False):
    all_bufs = [b for bufs, _, _ in groups for b in bufs]
    after = list(after)
    n = len(all_bufs)
    n_g = len(groups)

    def body(*refs):
        in_refs, sem_refs = refs[:n], refs[n + len(after):n + len(after) + 2 * n_g]
        if not source_token:
            refs[-1][...] = jnp.zeros_like(refs[-1])
        pos = 0
        for g, (bufs, plan, n_copies) in enumerate(groups):
            copies = plan(in_refs[pos:pos + len(bufs)], _me())
            assert len(copies) == n_copies
            for i, (src, dst, target) in enumerate(copies):
                _remote(src, dst, sem_refs[2 * g].at[i], sem_refs[2 * g + 1].at[i], target).start()
            pos += len(bufs)

    sems = []
    for _, _, n_copies in groups:
        sems += [pltpu.SemaphoreType.DMA((n_copies,))] * 2
    outs = pl.pallas_call(
        body, name=name, in_specs=[HBM] * n + [ANY] * len(after),
        out_specs=[SEM] * (2 * n_g) + [HBM] * n + [pl.BlockSpec(memory_space=pltpu.VMEM)] * (not source_token),
        out_shape=sems + [pltpu.HBM(b.shape, b.dtype) for b in all_bufs] + [
            jax.ShapeDtypeStruct((8, V7X_LANES), F32)] * (not source_token),
        input_output_aliases={i: 2 * n_g + i for i in range(n)},
        compiler_params=pltpu.CompilerParams(has_side_effects=pltpu.SideEffectType.DATAFLOW_SIDE_EFFECTING))(
            *[pltpu.with_memory_space_constraint(b, pltpu.HBM) for b in all_bufs], *after)
    result, pos = [], 2 * n_g
    for g, (bufs, _, _) in enumerate(groups):
        passed = list(outs[pos:pos + len(bufs)])
        result.append((outs[2 * g], outs[2 * g + 1], passed, passed[0] if source_token else outs[-1]))
        pos += len(bufs)
    return result


def _wait_copies(name, started, plan, n_copies, after):
    send_sems, recv_sems, bufs = started[:3]
    n = len(bufs)
    after = list(after) if isinstance(after, (list, tuple)) else [after]
    after = [a for a in after if all(a is not b for b in bufs)]

    def body(*refs):
        copies = plan(refs[:n], _me())
        assert len(copies) == n_copies
        for i, (src, dst, target) in enumerate(copies):
            cp = _remote(src, dst, refs[n].at[i], refs[n + 1].at[i], target)
            cp.wait_send()
            cp.wait_recv()

    return list(pl.pallas_call(
        body, name=name, in_specs=[HBM] * n + [SEM, SEM] + [ANY] * len(after), out_specs=[HBM] * n,
        out_shape=[pltpu.HBM(b.shape, b.dtype) for b in bufs], input_output_aliases={i: i for i in range(n)},
        compiler_params=pltpu.CompilerParams(has_side_effects=pltpu.SideEffectType.DATAFLOW_SIDE_EFFECTING))(
            *bufs, send_sems, recv_sems, *after))


def _halves(a):
    return a.reshape(a.shape[0], 2, a.shape[1] // 2, a.shape[2])


def _neighbour_slab(x, y, c):
    across_x, across_y = 2 * (1 - x) + y, 2 * x + 1 - y
    return across_x + c * (across_y - across_x)


def _direct_plan(refs, me):
    x, y, c = me
    mine = 2 * x + y
    target = (x + (1 - c) * (1 - 2 * x), y + c * (1 - 2 * y), c)
    return [(g.at[mine], g.at[mine], target) for g in refs]


def _relay_plan(refs, me):
    x, y, c = me
    src = _neighbour_slab(x, y, c)
    onward = (x + c * (1 - 2 * x), y + (1 - c) * (1 - 2 * y), c)
    copies = []
    for g in refs:
        copies += [(g.at[src, c], g.at[src, c], onward), (g.at[src], g.at[src], (x, y, 1 - c))]
    return copies


def _last_plan(refs, me):
    x, y, c = me
    far = 2 * (1 - x) + 1 - y
    return [(g.at[far, c], g.at[far, c], (x, y, 1 - c)) for g in refs]


def _swap_plan(refs, me):
    x, y, c = me
    k = len(refs) // 2
    return [(refs[i].at[j, 1 - c], refs[k + i].at[j], (x, y, 1 - c)) for i in range(k) for j in range(N_CHIPS)]


def _scatter_plan(refs, me):
    x, y, c = me
    mine = 2 * x + y
    k = len(refs) // 2
    return [(refs[i].at[chip], refs[k + i].at[mine], (px, py, c))
            for i in range(k) for px, py, chip in _other_chips(x, y)]


def _share_plan(refs, me):
    x, y, c = me
    return [(g.at[c], g.at[c], (x, y, 1 - c)) for g in refs]


def _broadcast_plan(refs, me):
    x, y, c = me
    mine = 4 * x + 2 * y + c
    copies = []
    for fx, fy, fc in [(0, 0, 1), (0, 1, 0), (0, 1, 1), (1, 0, 0), (1, 0, 1), (1, 1, 0), (1, 1, 1)]:
        peer = (x + fx - 2 * fx * x, y + fy - 2 * fy * y, c + fc - 2 * fc * c)
        copies.append((refs[0].at[mine], refs[0].at[mine], peer))
    return copies


BIG = ("w_in", "w_out", "w_q", "w_k", "w_v", "w_o", "w_up", "w_down")
COLUMN_SPLIT = ("w_in", "w_up")
WEIGHTS = ("norm_mix_pre", "norm_mix_post", "w_in", "w_out", "gmlp_v_gain", "w_spatial", "b_spatial", "w_pool",
           "s_pool", "w_dw", "b_dw", "conv_ln_g", "conv_ln_b", "norm_xattn_pre", "norm_mem", "norm_xattn_post",
           "w_q", "w_k", "w_v", "w_o", "norm_ffn_pre", "norm_ffn_post", "w_up", "w_down")
SMALL = tuple(n for n in WEIGHTS if n not in BIG)
REPLICATED = tuple(n for n in SMALL if n != "w_dw")
GATHER_GROUPS = (("w_in", "w_dw"), ("w_out", "w_q", "w_k", "w_v", "w_o"), ("w_up",), ("w_down",))


def _relu2(acc):
    r = jnp.maximum(acc, 0.0)
    return acc, r * r


def _relu2_bwd(acc, up):
    return (acc * (2.0 * jnp.maximum(up, 0.0)),)


def _pack(arrays):
    flat = jnp.concatenate([a.reshape(-1) for a in arrays])
    tile = 8 * V7X_LANES
    pad = (-flat.shape[0]) % tile
    return jnp.pad(flat, (0, pad)).reshape(-1, V7X_LANES)


def _pack_layers(arrays):
    n_l = arrays[0].shape[0]
    flat = jnp.concatenate([a.reshape(n_l, -1) for a in arrays], axis=1)
    pad = (-flat.shape[1]) % (8 * V7X_LANES)
    return jnp.pad(flat, ((0, 0), (0, pad))).reshape(n_l, -1, V7X_LANES)


def _unpack_layers(packed, like):
    flat = packed.reshape(packed.shape[0], -1)
    out, pos = [], 0
    for a in like:
        out.append(flat[:, pos:pos + a[0].size].reshape(a.shape))
        pos += a[0].size
    return out


class _GradientReducer:
    def __init__(self, place, w, m, v):
        self.place, self.w, self.m, self.v = place, w, m, v
        self.flying = []
        self.done = {n: None for n in BIG}

    def add(self, tag, l, grads):
        names = list(grads)
        views = [_halves(grads[n]) for n in names]
        zones = [lax.empty((v.shape[0],) + v.shape[2:], v.dtype) for v in views]
        started = _start_copies("swap_start_" + tag, [(views + zones, _swap_plan, N_CHIPS * len(names))],
                                source_token=True)[0]
        self.flying.append(dict(stage=0, tag=tag, l=l, names=names, started=started))
        return [started[3]]

    def advance(self, after):
        made = []
        after = list(after) if isinstance(after, (list, tuple)) else [after]
        for item in self.flying:
            item["stage"] += 1
        for item in self.flying:
            tag, names, k = item["tag"], item["names"], len(item["names"])
            if item["stage"] == 1:
                bufs = _wait_copies("swap_wait_" + tag, item["started"], _swap_plan, N_CHIPS * k, after)
                sums, parts = zip(*[_pair_sum("pair_sum_" + n, self.place, dv, got)
                                    for n, dv, got in zip(names, bufs[:k], bufs[k:])])
                item["started"] = _start_copies("scatter_start_" + tag,
                                                [(list(sums) + list(parts), _scatter_plan, 3 * k)],
                                                source_token=True)[0]
                made.append(item["started"][3])
        after = after + made
        for item in list(self.flying):
            tag, names, k = item["tag"], item["names"], len(item["names"])
            if item["stage"] == 4:
                halves = _wait_copies("share_wait_" + tag, item["started"], _share_plan, k, after)
                for n, h in zip(names, halves):
                    g = h.reshape(self.w[n].shape[1:])
                    self.done[n] = _adamw_layer("adamw_" + n, item["l"], self.w[n], g, self.m[n], self.v[n],
                                                self.done[n])
                    made.append(self.done[n][3])
                self.flying.remove(item)
        after = after + [a for a in made if all(a is not b for b in after)]
        for item in self.flying:
            tag, names, k = item["tag"], item["names"], len(item["names"])
            if item["stage"] == 3:
                bufs = _wait_copies("scatter_wait_" + tag, item["started"], _scatter_plan, 3 * k, after)
                halves = [_chip_sum("chip_sum_" + n, self.place, p) for n, p in zip(names, bufs[k:])]
                item["started"] = _start_copies("share_start_" + tag, [(halves, _share_plan, k)])[0]
                made.append(item["started"][3])
        return made

    def drain(self, after):
        made = list(after)
        while self.flying:
            made = list(after) + self.advance(made)
        return made


def _step(x, mem, target, w, m, v):
    n_layers = w["w_in"].shape[0]
    seq, d = x.shape
    heads, hd = w["gmlp_v_gain"].shape[1:]
    gw = heads * hd
    groups, pgw = w["w_pool"].shape[1:3]
    pw = groups * pgw
    cw = w["b_dw"].shape[1]
    cb = cw // N_CHIPS
    taps = w["w_dw"].shape[1]
    cx, cy, cc = _me()
    chip = 2 * cx + cy
    place = jnp.stack([cc, chip, 2 * chip + cc]).astype(jnp.int32)
    vec = lambda name, l: w[name][l].reshape(1, -1)

    taps_padded = jnp.pad(w["w_dw"], ((0, 0), (0, (-taps) % 16), (0, 0)))
    gathering = []

    def send_layer(l, last):
        for g, names in enumerate(GATHER_GROUPS):
            bufs = [_halves(_place_shard("place_" + n, place, taps_padded if n == "w_dw" else w[n], l,
                                         F32 if n == "w_dw" else BF16, after=last)) for n in names]
            gathering.extend(_start_copies("gather_start_%d%d" % (l, g), [(bufs, _direct_plan, len(names))],
                                           after=last))
            last = [gathering[-1][3]]
        return last

    def relay(l, g, after):
        names = GATHER_GROUPS[g]
        at = l * len(GATHER_GROUPS) + g
        bufs = _wait_copies("gather_wait_%d%d" % (l, g), gathering[at], _direct_plan, len(names), after)
        gathering[at] = _start_copies("relay_start_%d%d" % (l, g), [(bufs, _relay_plan, 2 * len(names))])[0]
        return [gathering[at][3]]

    def arrive(l, g, after):
        names = GATHER_GROUPS[g]
        tag = "%d%d" % (l, g)
        bufs = _wait_copies("relay_wait_" + tag, gathering[l * len(GATHER_GROUPS) + g], _relay_plan,
                            2 * len(names), after)
        bufs = _exchange("gather_last_" + tag, bufs, _last_plan, len(names))
        out = {}
        for n, b in zip(names, bufs):
            full = b.reshape(N_CHIPS, 2 * b.shape[2], b.shape[3])
            out[n] = full if n in COLUMN_SPLIT + ("w_dw",) else full.reshape(-1, full.shape[2])
        return out

    saved = []
    sent = send_layer(0, [])
    _, h1 = _norm_fwd("norm_first", x, None, None, vec("norm_mix_pre", 0))
    relay(0, 0, [h1] + sent)
    for l in range(n_layers):
        gv = w["gmlp_v_gain"][l].reshape(heads, 1, hd)
        ws = w["w_spatial"][l]
        bcol = w["b_spatial"][l].reshape(heads, -1, 1)
        wl = arrive(l, 0, [h1])
        early = relay(l, 1, [h1]) if l > 0 else []
        z = _mm_nn_col("mm_in", h1, wl["w_in"], [F32], after=early)[0]
        sent = send_layer(l + 1, [z]) if l + 1 < n_layers else []
        ya = _gmlp_fwd(z, gv, ws, bcol)
        yb = _pool_fwd(z, w["w_pool"][l], vec("s_pool", l), (2 * gw) // pw)
        hc = _conv_fwd(z, wl["w_dw"], taps, vec("b_dw", l), (2 * gw + pw) // cb, (2 * gw + pw + cw) // cb,
                       after=relay(l, 1, [ya]) if l == 0 else [])
        yc = _ln_swish_fwd(hc, vec("conv_ln_g", l), vec("conv_ln_b", l))
        y = jnp.concatenate([ya, yb, yc], axis=1)
        wl.update(arrive(l, 1, [y] + sent))
        o = _mm_nn_row("mm_out", y, wl["w_out"], [F32])[0]
        x1, h2 = _norm_fwd("norm_mix", x, o, vec("norm_mix_post", l), vec("norm_xattn_pre", l),
                           after=relay(l, 2, [o]))
        _, mn = _norm_fwd("norm_mem", mem, None, None, vec("norm_mem", l))
        q = _mm_nn_row("mm_q", h2, wl["w_q"], [BF16])[0]
        k = _mm_nn_row("mm_k", mn, wl["w_k"], [BF16])[0]
        vv = _mm_nn_row("mm_v", mn, wl["w_v"], [BF16])[0]
        a = _attn_fwd(q, k, vv)
        o2 = _mm_nn_row("mm_o", a, wl["w_o"], [F32], after=relay(l, 3, [a]))[0]
        x2, h3 = _norm_fwd("norm_xattn", x1, o2, vec("norm_xattn_post", l), vec("norm_ffn_pre", l))
        wl.update(arrive(l, 2, h3))
        up, r = _mm_nn_col("mm_up", h3, wl["w_up"], [BF16, BF16], epi=_relu2)
        wl.update(arrive(l, 3, r))
        o3 = _mm_nn_row("mm_down", r, wl["w_down"], [F32],
                        after=relay(l + 1, 0, [r]) if l + 1 < n_layers else [])[0]
        g_next = vec("norm_mix_pre", l + 1) if l + 1 < n_layers else None
        x3, h_next = _norm_fwd("norm_ffn", x2, o3, vec("norm_ffn_post", l), g_next)
        saved.append(dict(x=x, h1=h1, z=z, hc=hc, y=y, o=o, x1=x1, h2=h2, mn=mn, q=q, k=k, v=vv, a=a, o2=o2, x2=x2,
                          h3=h3, up=up, r=r, o3=o3, x3=x3, gv=gv, ws=ws, bcol=bcol, w=wl))
        x, h1 = x3, h_next

    dx, loss_parts = _loss_head(x, target)
    loss = lax.psum(jnp.sum(loss_parts), ("x", "y", "c"))

    reducer = _GradientReducer(place, w, m, v)
    small = {n: [None] * n_layers for n in SMALL}
    by_chip = lambda g: g.reshape(N_CHIPS, g.shape[0] // N_CHIPS, g.shape[1])
    small_sent = [None] * n_layers

    def small_layer(l):
        return [small[n][l].reshape(w[n].shape[1:]) for n in REPLICATED] + [small["w_dw"][l]]

    def send_small(l):
        landing = _place_flat(place, _pack(small_layer(l)))
        small_sent[l] = _start_copies("small_start_%d" % l, [([landing], _broadcast_plan, N_DEVICES - 1)])[0]
        return [small_sent[l][3]]

    dh = None
    made = []
    for l in reversed(range(n_layers)):
        t = saved[l]
        wl = t["w"]
        g_next = vec("norm_mix_pre", l + 1) if l + 1 < n_layers else None
        dx, do3, dgp, dgn = _norm_bwd("norm_ffn_bwd", dx, dh, t["x3"], t["o3"], vec("norm_ffn_post", l), g_next,
                                      after=made)
        small["norm_ffn_post"][l] = dgp
        if dgn is not None:
            small["norm_mix_pre"][l + 1] = dgn
        made = reducer.advance(dx)
        if l + 1 < n_layers:
            made += send_small(l + 1)
        d_down = _mm_tn_row("mm_down_dw", t["r"], do3)
        made += reducer.add("%d0" % l, l, {"w_down": by_chip(d_down)})
        dup = _mm_nt_row("mm_down_dx", do3, wl["w_down"], BF16, epi=_relu2_bwd, extra=t["up"], after=made)
        d_up = _mm_tn_col("mm_up_dw", t["h3"], dup)
        made = reducer.add("%d3" % l, l, {"w_up": d_up})
        dh3 = _mm_nt_col("mm_up_dx", dup, wl["w_up"], F32, after=made)
        made = []
        dx, do2, dgp, dgn = _norm_bwd("norm_xattn_bwd", dx, dh3, t["x2"], t["o2"], vec("norm_xattn_post", l),
                                      vec("norm_ffn_pre", l), after=made)
        small["norm_xattn_post"][l], small["norm_ffn_pre"][l] = dgp, dgn
        made = reducer.advance(dx)
        d_o = _mm_tn_row("mm_o_dw", t["a"], do2)
        da = _mm_nt_row("mm_o_dx", do2, wl["w_o"], BF16, after=made)
        dq, dk, dv = _attn_bwd(t["q"], t["k"], t["v"], da)
        d_q = _mm_tn_row("mm_q_dw", t["h2"], dq)
        d_k = _mm_tn_row("mm_k_dw", t["mn"], dk)
        d_v = _mm_tn_row("mm_v_dw", t["mn"], dv)
        dh2 = _mm_nt_row("mm_q_dx", dq, wl["w_q"], F32)
        dmn = _mm_nt_row("mm_k_dx", dk, wl["w_k"], F32) + _mm_nt_row("mm_v_dx", dv, wl["w_v"], F32)
        _, _, _, small["norm_mem"][l] = _norm_bwd("norm_mem_bwd", None, dmn, mem, None, None, vec("norm_mem", l))
        dx, do, dgp, dgn = _norm_bwd("norm_mix_bwd", dx, dh2, t["x1"], t["o"], vec("norm_mix_post", l),
                                     vec("norm_xattn_pre", l))
        small["norm_mix_post"][l], small["norm_xattn_pre"][l] = dgp, dgn
        made = reducer.advance(dx)
        d_out = _mm_tn_row("mm_out_dw", t["y"], do)
        made += reducer.add("%d1" % l, l, {"w_o": by_chip(d_o), "w_q": by_chip(d_q), "w_k": by_chip(d_k),
                                    "w_v": by_chip(d_v), "w_out": by_chip(d_out)})
        dy = _mm_nt_row("mm_out_dx", do, wl["w_out"], F32, after=made)
        dzu, dzv, dgv, dws, dbcol = _gmlp_bwd(t["z"], dy, t["gv"], t["ws"], t["bcol"])
        small["gmlp_v_gain"][l] = dgv.reshape(heads, hd)
        small["w_spatial"][l] = dws
        small["b_spatial"][l] = dbcol.reshape(heads, -1)
        dzp, dwp, dsp = _pool_bwd(t["z"], dy, w["w_pool"][l], vec("s_pool", l), (2 * gw) // pw, gw // pw)
        small["w_pool"][l], small["s_pool"][l] = dwp, dsp.reshape(-1)
        dhc, dlg, dlb = _ln_swish_bwd(t["hc"], dy, vec("conv_ln_g", l), vec("conv_ln_b", l), (gw + pw) // cw)
        small["conv_ln_g"][l], small["conv_ln_b"][l] = dlg.reshape(-1), dlb.reshape(-1)
        dval, dgate, dwd, dbd = _conv_bwd(t["z"], dhc, wl["w_dw"], taps, (2 * gw + pw) // cb,
                                          (2 * gw + pw + cw) // cb)
        small["w_dw"][l], small["b_dw"][l] = dwd, dbd.reshape(-1)
        dz = jnp.concatenate([dzu, dzv, dzp, dval, dgate], axis=1)
        made = reducer.advance(dz)
        d_in = _mm_tn_col("mm_in_dw", t["h1"], dz)
        made += reducer.add("%d2" % l, l, {"w_in": d_in})
        dh = _mm_nt_col("mm_in_dx", dz, wl["w_in"], F32, after=made)
        made = []
    grad_x, _, _, dgn = _norm_bwd("norm_first_bwd", dx, dh, saved[0]["x"], None, None, vec("norm_mix_pre", 0))
    small["norm_mix_pre"][0] = dgn
    drained = reducer.drain([grad_x] + send_small(0))

    like = [w[n] for n in REPLICATED]
    packed = [_pack_layers([p[n] for n in REPLICATED]) for p in (w, m, v)]
    updated, taps_grad = None, []
    for l in range(n_layers):
        landed = _wait_copies("small_wait_%d" % l, small_sent[l], _broadcast_plan, N_DEVICES - 1, drained)[0]
        total = _device_sum(landed)
        updated = _adamw_layer("adamw_small", l, packed[0], total, packed[1], packed[2], updated)
        start = sum(a[0].size for a in like)
        taps_grad.append(total.reshape(-1)[start:start + small["w_dw"][l].size].reshape(small["w_dw"][l].shape))
    grad, delta, new_m, new_v = (dict(zip(REPLICATED, _unpack_layers(u, like))) for u in updated)
    view = lambda a: a.reshape(-1, a.shape[-1])
    taps_grad = lax.dynamic_slice_in_dim(jnp.stack(taps_grad), chip * cb, cb, axis=2)
    updated = _adamw_flat("adamw_taps", view(w["w_dw"]), view(taps_grad), view(m["w_dw"]), view(v["w_dw"]))
    grad["w_dw"] = taps_grad
    delta["w_dw"], new_m["w_dw"], new_v["w_dw"] = (u.reshape(w["w_dw"].shape) for u in updated)
    for n in BIG:
        grad[n], delta[n], new_m[n], new_v[n] = reducer.done[n]

    return (loss, grad_x[None], *[grad[n] for n in WEIGHTS], *[delta[n] for n in WEIGHTS],
            *[new_m[n] for n in WEIGHTS], *[new_v[n] for n in WEIGHTS])


def kernel(x, mem, norm_mix_pre, norm_mix_post, w_in, w_out, gmlp_v_gain, w_spatial, b_spatial, w_pool, s_pool, w_dw, b_dw, conv_ln_g, conv_ln_b, norm_xattn_pre, norm_mem, norm_xattn_post, w_q, w_k, w_v, w_o, norm_ffn_pre, norm_ffn_post, w_up, w_down, loss_target, m_norm_mix_pre, m_norm_mix_post, m_w_in, m_w_out, m_gmlp_v_gain, m_w_spatial, m_b_spatial, m_w_pool, m_s_pool, m_w_dw, m_b_dw, m_conv_ln_g, m_conv_ln_b, m_norm_xattn_pre, m_norm_mem, m_norm_xattn_post, m_w_q, m_w_k, m_w_v, m_w_o, m_norm_ffn_pre, m_norm_ffn_post, m_w_up, m_w_down, v_norm_mix_pre, v_norm_mix_post, v_w_in, v_w_out, v_gmlp_v_gain, v_w_spatial, v_b_spatial, v_w_pool, v_s_pool, v_w_dw, v_b_dw, v_conv_ln_g, v_conv_ln_b, v_norm_xattn_pre, v_norm_mem, v_norm_xattn_post, v_w_q, v_w_k, v_w_v, v_w_o, v_norm_ffn_pre, v_norm_ffn_post, v_w_up, v_w_down):
    given = dict(locals())
    w = {n: given[n] for n in WEIGHTS}
    m = {n: given["m_" + n] for n in WEIGHTS}
    v = {n: given["v_" + n] for n in WEIGHTS}
    return _step(x[0], mem[0], loss_target[0], w, m, v)
```

```python
import functools

import jax
import jax.numpy as jnp
from jax import lax
from jax.experimental import pallas as pl
from jax.experimental.pallas import tpu as pltpu

F32 = jnp.float32
BF16 = jnp.bfloat16
MESH = pl.DeviceIdType.MESH

N_CHIPS = 4
N_DEVICES = 8
XATTN_HEADS = 4
POOL_WINDOWS = (2, 4, 8, 16)
RMS_EPS = 1e-6
LN_EPS = 1e-5
ADAM_LR, ADAM_B1, ADAM_B2, ADAM_EPS, ADAM_WD, ADAM_STEP = 0.001, 0.9, 0.999, 1e-08, 0.01, 10

V7X_LANES = 128
V7X_VMEM_LIMIT = 56 * 1024 * 1024
ROW_TILE = 256
NORM_ROW_TILE = 256
GMLP_UNROLL = 4
MM_TILE_M, MM_TILE_N, MM_TILE_K = 1024, 1024, 4096
MM_TILE_M_ONE_STEP = 1024

ANY = pl.BlockSpec(memory_space=pl.ANY)
HBM = pl.BlockSpec(memory_space=pltpu.HBM)
SEM = pl.BlockSpec(memory_space=pltpu.SEMAPHORE)


def _tile(dim, pref):
    if dim <= pref:
        return dim
    t = (pref // V7X_LANES) * V7X_LANES
    while t >= V7X_LANES:
        if dim % t == 0:
            return t
        t -= V7X_LANES
    return dim


def _mm_tiles(m, n, k, chunks=1):
    tk = _tile(k, MM_TILE_K)
    one_step = chunks * (k // tk) == 1
    return _tile(m, MM_TILE_M_ONE_STEP if one_step else MM_TILE_M), _tile(n, MM_TILE_N), tk


def _params(sem=None):
    return pltpu.CompilerParams(dimension_semantics=sem, vmem_limit_bytes=V7X_VMEM_LIMIT)


NN = (((1,), (0,)), ((), ()))
NT = (((1,), (1,)), ((), ()))
TN = (((0,), (0,)), ((), ()))


def _mm(name, a, b, *, dn, grid, a_spec, b_spec, o_specs, out_shapes, acc_shape, epi=None, extra=None,
        extra_spec=None, after=()):
    nk = grid[2]
    n_out = len(out_shapes)
    has_extra = extra is not None
    after = list(after)

    def body(*refs):
        a_ref, b_ref = refs[0], refs[1]
        pos = 2
        e_ref = None
        if has_extra:
            e_ref = refs[pos]
            pos += 1
        pos += len(after)
        o_refs = refs[pos:pos + n_out]
        part = lax.dot_general(a_ref[...].astype(BF16), b_ref[...].astype(BF16), dn, preferred_element_type=F32)

        def finish(total):
            if epi is None:
                vals = (total,)
            elif has_extra:
                vals = epi(total, e_ref[...])
            else:
                vals = epi(total)
            for o, v in zip(o_refs, vals):
                o[...] = v.astype(o.dtype)

        if nk == 1:
            finish(part)
            return
        acc = refs[pos + n_out]
        k = pl.program_id(2)

        @pl.when(k == 0)
        def _():
            acc[...] = part

        @pl.when(jnp.logical_and(k > 0, k < nk - 1))
        def _():
            acc[...] += part

        @pl.when(k == nk - 1)
        def _():
            finish(acc[...] + part)

    ins, in_specs = [a, b], [a_spec, b_spec]
    if has_extra:
        ins.append(extra)
        in_specs.append(extra_spec)
    ins += after
    in_specs += [ANY] * len(after)
    outs = pl.pallas_call(
        body, name=name, grid=grid, in_specs=in_specs, out_specs=list(o_specs), out_shape=list(out_shapes),
        scratch_shapes=[pltpu.VMEM(acc_shape, F32)] if nk > 1 else [],
        compiler_params=_params(("parallel", "parallel", "arbitrary")))(*ins)
    return outs


def _mm_nn_row(name, a, w, out_dtypes, epi=None, after=()):
    m, k = a.shape
    n = w.shape[1]
    tm, tn, tk = _mm_tiles(m, n, k)
    o_spec = pl.BlockSpec((tm, tn), lambda i, j, kk: (i, j))
    return _mm(name, a, w, dn=NN, grid=(m // tm, n // tn, k // tk),
               a_spec=pl.BlockSpec((tm, tk), lambda i, j, kk: (i, kk)),
               b_spec=pl.BlockSpec((tk, tn), lambda i, j, kk: (kk, j)),
               o_specs=[o_spec] * len(out_dtypes),
               out_shapes=[jax.ShapeDtypeStruct((m, n), d) for d in out_dtypes], acc_shape=(tm, tn), epi=epi,
               after=after)


def _mm_nn_col(name, a, w, out_dtypes, epi=None, after=()):
    m, k = a.shape
    c = w.shape[2]
    tm, tn, tk = _mm_tiles(m, c, k)
    nb = c // tn
    o_spec = pl.BlockSpec((tm, tn), lambda i, j, kk: (i, j))
    return _mm(name, a, w, dn=NN, grid=(m // tm, N_CHIPS * nb, k // tk),
               a_spec=pl.BlockSpec((tm, tk), lambda i, j, kk: (i, kk)),
               b_spec=pl.BlockSpec((None, tk, tn), lambda i, j, kk: (j // nb, kk, j % nb)),
               o_specs=[o_spec] * len(out_dtypes),
               out_shapes=[jax.ShapeDtypeStruct((m, N_CHIPS * c), d) for d in out_dtypes], acc_shape=(tm, tn),
               epi=epi, after=after)


def _mm_nt_row(name, dy, w, out_dtype, epi=None, extra=None, after=()):
    m, n = dy.shape
    k = w.shape[0]
    tm, tn, tk = _mm_tiles(m, k, n)
    o_spec = pl.BlockSpec((tm, tn), lambda i, j, kk: (i, j))
    return _mm(name, dy, w, dn=NT, grid=(m // tm, k // tn, n // tk),
               a_spec=pl.BlockSpec((tm, tk), lambda i, j, kk: (i, kk)),
               b_spec=pl.BlockSpec((tn, tk), lambda i, j, kk: (j, kk)),
               o_specs=[o_spec], out_shapes=[jax.ShapeDtypeStruct((m, k), out_dtype)], acc_shape=(tm, tn),
               epi=epi, extra=extra, extra_spec=o_spec, after=after)[0]


def _mm_nt_col(name, dy, w, out_dtype, after=()):
    m = dy.shape[0]
    k, c = w.shape[1], w.shape[2]
    tm, tn, tk = _mm_tiles(m, k, c, chunks=N_CHIPS)
    kb = c // tk
    return _mm(name, dy, w, dn=NT, grid=(m // tm, k // tn, N_CHIPS * kb),
               a_spec=pl.BlockSpec((tm, tk), lambda i, j, kk: (i, kk)),
               b_spec=pl.BlockSpec((None, tn, tk), lambda i, j, kk: (kk // kb, j, kk % kb)),
               o_specs=[pl.BlockSpec((tm, tn), lambda i, j, kk: (i, j))],
               out_shapes=[jax.ShapeDtypeStruct((m, k), out_dtype)], acc_shape=(tm, tn), after=after)[0]


def _mm_tn_row(name, a, dy):
    t, m = a.shape
    n = dy.shape[1]
    tm, tn, tk = _mm_tiles(m, n, t)
    return _mm(name, a, dy, dn=TN, grid=(m // tm, n // tn, t // tk),
               a_spec=pl.BlockSpec((tk, tm), lambda i, j, kk: (kk, i)),
               b_spec=pl.BlockSpec((tk, tn), lambda i, j, kk: (kk, j)),
               o_specs=[pl.BlockSpec((tm, tn), lambda i, j, kk: (i, j))],
               out_shapes=[jax.ShapeDtypeStruct((m, n), BF16)], acc_shape=(tm, tn))[0]


def _mm_tn_col(name, a, dy):
    t, m = a.shape
    c = dy.shape[1] // N_CHIPS
    tm, tn, tk = _mm_tiles(m, c, t)
    nb = c // tn
    return _mm(name, a, dy, dn=TN, grid=(m // tm, N_CHIPS * nb, t // tk),
               a_spec=pl.BlockSpec((tk, tm), lambda i, j, kk: (kk, i)),
               b_spec=pl.BlockSpec((tk, tn), lambda i, j, kk: (kk, j)),
               o_specs=[pl.BlockSpec((None, tm, tn), lambda i, j, kk: (j // nb, i, j % nb))],
               out_shapes=[jax.ShapeDtypeStruct((N_CHIPS, m, c), BF16)], acc_shape=(tm, tn))[0]


def _rms(x, g):
    r = lax.rsqrt(jnp.mean(x * x, axis=-1, keepdims=True) + RMS_EPS)
    return x * r * g


def _rms_bwd(x, g, dy):
    r = lax.rsqrt(jnp.mean(x * x, axis=-1, keepdims=True) + RMS_EPS)
    xr = x * r
    dyg = dy * g
    dx = r * (dyg - xr * jnp.mean(dyg * xr, axis=-1, keepdims=True))
    return dx, jnp.sum(dy * xr, axis=0, keepdims=True)


def _norm_fwd(name, x, o, g_post, g_next, after=()):
    after = list(after)
    s, d = x.shape
    tr = _tile(s, NORM_ROW_TILE)
    has_prev, has_next = o is not None, g_next is not None
    row = pl.BlockSpec((tr, d), lambda i: (i, 0))
    vec = pl.BlockSpec((1, d), lambda i: (0, 0))

    def body(*refs):
        refs = list(refs)
        xn = refs.pop(0)[...]
        if has_prev:
            o_ref, gp_ref = refs.pop(0), refs.pop(0)
            xn = xn + _rms(o_ref[...], gp_ref[...])
        gn_ref = refs.pop(0) if has_next else None
        del refs[:len(after)]
        if has_prev:
            refs.pop(0)[...] = xn
        if has_next:
            refs.pop(0)[...] = _rms(xn, gn_ref[...]).astype(BF16)

    ins, in_specs = [x], [row]
    if has_prev:
        ins += [o, g_post]
        in_specs += [row, vec]
    if has_next:
        ins.append(g_next)
        in_specs.append(vec)
    ins += after
    in_specs += [ANY] * len(after)
    out_shapes, out_specs = [], []
    if has_prev:
        out_shapes.append(jax.ShapeDtypeStruct((s, d), F32))
        out_specs.append(row)
    if has_next:
        out_shapes.append(jax.ShapeDtypeStruct((s, d), BF16))
        out_specs.append(row)
    outs = pl.pallas_call(body, name=name, grid=(s // tr,), in_specs=in_specs, out_specs=out_specs,
                          out_shape=out_shapes, compiler_params=_params(("parallel",)))(*ins)
    outs = list(outs)
    x_new = outs.pop(0) if has_prev else x
    h = outs.pop(0) if has_next else None
    return x_new, h


def _norm_bwd(name, dxn, dh, xn, o, g_post, g_next, after=()):
    after = list(after)
    s, d = xn.shape
    tr = _tile(s, NORM_ROW_TILE)
    has_prev, has_next, has_dxn = o is not None, dh is not None, dxn is not None
    row = pl.BlockSpec((tr, d), lambda i: (i, 0))
    vec = pl.BlockSpec((1, d), lambda i: (0, 0))

    def body(*refs):
        refs = list(refs)
        first = pl.program_id(0) == 0
        dxn_ref = refs.pop(0) if has_dxn else None
        dh_ref = refs.pop(0) if has_next else None
        xn_ref = refs.pop(0)
        if has_prev:
            o_ref, gp_ref = refs.pop(0), refs.pop(0)
        gn_ref = refs.pop(0) if has_next else None
        del refs[:len(after)]
        dx_ref = refs.pop(0)
        if has_prev:
            do_ref, dgp_ref = refs.pop(0), refs.pop(0)
        dgn_ref = refs.pop(0) if has_next else None

        def accumulate(ref, val):
            @pl.when(first)
            def _():
                ref[...] = val

            @pl.when(jnp.logical_not(first))
            def _():
                ref[...] += val

        dx = dxn_ref[...] if has_dxn else None
        if has_next:
            dxh, dgn = _rms_bwd(xn_ref[...], gn_ref[...], dh_ref[...].astype(F32))
            dx = dxh if dx is None else dx + dxh
            accumulate(dgn_ref, dgn)
        dx_ref[...] = dx
        if has_prev:
            do, dgp = _rms_bwd(o_ref[...], gp_ref[...], dx)
            do_ref[...] = do.astype(BF16)
            accumulate(dgp_ref, dgp)

    ins, in_specs = [], []
    if has_dxn:
        ins.append(dxn)
        in_specs.append(row)
    if has_next:
        ins.append(dh)
        in_specs.append(row)
    ins.append(xn)
    in_specs.append(row)
    if has_prev:
        ins += [o, g_post]
        in_specs += [row, vec]
    if has_next:
        ins.append(g_next)
        in_specs.append(vec)
    ins += after
    in_specs += [ANY] * len(after)
    out_shapes, out_specs = [jax.ShapeDtypeStruct((s, d), F32)], [row]
    if has_prev:
        out_shapes += [jax.ShapeDtypeStruct((s, d), BF16), jax.ShapeDtypeStruct((1, d), F32)]
        out_specs += [row, vec]
    if has_next:
        out_shapes.append(jax.ShapeDtypeStruct((1, d), F32))
        out_specs.append(vec)
    outs = list(pl.pallas_call(body, name=name, grid=(s // tr,), in_specs=in_specs, out_specs=out_specs,
                               out_shape=out_shapes, compiler_params=_params(("arbitrary",)))(*ins))
    dx = outs.pop(0)
    do, dgp = (outs.pop(0), outs.pop(0)) if has_prev else (None, None)
    dgn = outs.pop(0) if has_next else None
    return dx, do, dgp, dgn


def _loss_head(y, target):
    s, d = y.shape
    tr = _tile(s, ROW_TILE)
    row = pl.BlockSpec((tr, d), lambda i: (i, 0))
    vec = pl.BlockSpec((1, d), lambda i: (0, 0))

    def body(y_ref, t_ref, dy_ref, l_ref):
        err = y_ref[...] - t_ref[...]
        dy_ref[...] = err * (1.0 / d)
        part = jnp.sum(err * err, axis=0, keepdims=True) * (0.5 / d)

        @pl.when(pl.program_id(0) == 0)
        def _():
            l_ref[...] = part

        @pl.when(pl.program_id(0) != 0)
        def _():
            l_ref[...] += part

    return pl.pallas_call(body, name="loss_head", grid=(s // tr,), in_specs=[row, row], out_specs=[row, vec],
                          out_shape=[jax.ShapeDtypeStruct((s, d), F32), jax.ShapeDtypeStruct((1, d), F32)],
                          compiler_params=_params(("arbitrary",)))(y, target)


@jax.custom_vjp
def _bdot(a, b):
    return jnp.dot(a.astype(BF16), b.astype(BF16), preferred_element_type=F32)


def _bdot_fwd(a, b):
    return _bdot(a, b), (a, b)


def _bdot_bwd(res, ct):
    a, b = res
    ctb = ct.astype(BF16)
    da = lax.dot_general(ctb, b.astype(BF16), NT, preferred_element_type=F32)
    db = lax.dot_general(a.astype(BF16), ctb, TN, preferred_element_type=F32)
    return da, db


_bdot.defvjp(_bdot_fwd, _bdot_bwd)


@functools.partial(jax.custom_vjp, nondiff_argnums=(1,))
def _shift(x, k):
    n = x.shape[0]
    if k == 0:
        return x
    rolled = pltpu.roll(x, k % n, 0)
    t = lax.broadcasted_iota(jnp.int32, x.shape, 0)
    keep = (t >= k) if k > 0 else (t < n + k)
    return jnp.where(keep, rolled, 0.0)


def _shift_fwd(x, k):
    return _shift(x, k), None


def _shift_bwd(k, _, ct):
    return (_shift(ct, -k),)


_shift.defvjp(_shift_fwd, _shift_bwd)


def _sigmoid(x):
    return 1.0 / (1.0 + jnp.exp(-x))


def _layer_norm(x, g, b=None):
    mu = jnp.mean(x, axis=-1, keepdims=True)
    xc = x - mu
    var = jnp.mean(xc * xc, axis=-1, keepdims=True)
    y = xc * lax.rsqrt(var + LN_EPS) * g
    return y if b is None else y + b


def _gmlp_chunk(zu, zv, gv, w, bcol):
    ch = w.shape[0]
    u = jax.nn.gelu(zu)
    vn = _layer_norm(jax.nn.gelu(zv), gv)
    t = lax.broadcasted_iota(jnp.int32, (ch, ch), 0)
    s = lax.broadcasted_iota(jnp.int32, (ch, ch), 1)
    wm = jnp.where(t >= s, w, 0.0)
    return u * (_bdot(wm, vn) + bcol)


def _gmlp_specs(seq, heads, hd, ch, u_off, v_off):
    col = lambda off: pl.BlockSpec((seq, hd), lambda h: (0, off + h))
    return (col(u_off), col(v_off), pl.BlockSpec((None, 1, hd), lambda h: (h, 0, 0)),
            pl.BlockSpec((None, ch, ch), lambda h: (h, 0, 0)), pl.BlockSpec((None, ch, 1), lambda h: (h, 0, 0)))


def _gmlp_fwd(z, gv, ws, bcol):
    seq = z.shape[0]
    heads, _, hd = gv.shape
    ch = ws.shape[-1]
    zu_s, zv_s, gv_s, w_s, b_s = _gmlp_specs(seq, heads, hd, ch, 0, heads)

    def body(zu_ref, zv_ref, gv_ref, w_ref, b_ref, y_ref):
        gvv, w, bc = gv_ref[...], w_ref[...], b_ref[...]

        def step(c, carry):
            rows = pl.ds(pl.multiple_of(c * ch, ch), ch)
            y_ref[rows, :] = _gmlp_chunk(zu_ref[rows, :], zv_ref[rows, :], gvv, w, bc).astype(BF16)
            return carry

        lax.fori_loop(0, seq // ch, step, 0, unroll=GMLP_UNROLL)

    return pl.pallas_call(body, name="gmlp_fwd", grid=(heads,), in_specs=[zu_s, zv_s, gv_s, w_s, b_s],
                          out_specs=pl.BlockSpec((seq, hd), lambda h: (0, h)),
                          out_shape=jax.ShapeDtypeStruct((seq, heads * hd), BF16),
                          compiler_params=_params(("parallel",)))(z, z, gv, ws, bcol)


def _gmlp_bwd(z, dy, gv, ws, bcol):
    seq = z.shape[0]
    heads, _, hd = gv.shape
    ch = ws.shape[-1]
    zu_s, zv_s, gv_s, w_s, b_s = _gmlp_specs(seq, heads, hd, ch, 0, heads)
    col = pl.BlockSpec((seq, hd), lambda h: (0, h))

    def body(zu_ref, zv_ref, dy_ref, gv_ref, w_ref, b_ref, dzu_ref, dzv_ref, dgv_ref, dw_ref, db_ref):
        gvv, w, bc = gv_ref[...], w_ref[...], b_ref[...]

        together = GMLP_UNROLL if (seq // ch) % GMLP_UNROLL == 0 else 1

        def step(c, carry):
            dgv, dw, db = carry
            for u in range(together):
                rows = pl.ds(pl.multiple_of((c * together + u) * ch, ch), ch)
                _, vjp = jax.vjp(_gmlp_chunk, zu_ref[rows, :], zv_ref[rows, :], gvv, w, bc)
                dzu, dzv, dgv_c, dw_c, db_c = vjp(dy_ref[rows, :])
                dzu_ref[rows, :] = dzu.astype(BF16)
                dzv_ref[rows, :] = dzv.astype(BF16)
                dgv, dw, db = dgv + dgv_c, dw + dw_c, db + db_c
            return dgv, dw, db

        zero = (jnp.zeros((1, hd), F32), jnp.zeros((ch, ch), F32), jnp.zeros((ch, 1), F32))
        dgv, dw, db = lax.fori_loop(0, seq // ch // together, step, zero)
        dgv_ref[...] = dgv
        dw_ref[...] = dw
        db_ref[...] = db

    return pl.pallas_call(
        body, name="gmlp_bwd", grid=(heads,), in_specs=[zu_s, zv_s, col, gv_s, w_s, b_s],
        out_specs=[col, col, gv_s, w_s, b_s],
        out_shape=[jax.ShapeDtypeStruct((seq, heads * hd), BF16), jax.ShapeDtypeStruct((seq, heads * hd), BF16),
                   jax.ShapeDtypeStruct(gv.shape, F32), jax.ShapeDtypeStruct(ws.shape, F32),
                   jax.ShapeDtypeStruct(bcol.shape, F32)],
        compiler_params=_params(("parallel",)))(z, z, dy, gv, ws, bcol)


def _pool_group(p, w, s, window):
    win, span = p, 1
    while span < window:
        win = win + _shift(win, span)
        span *= 2
    t = lax.broadcasted_iota(jnp.int32, (p.shape[0], 1), 0).astype(F32)
    cnt = jnp.minimum(t + 1.0, float(window))
    return _bdot(win / cnt - p, w) * s


def _pool_fwd(z, w_pool, s_pool, col_block):
    seq = z.shape[0]
    groups, gw, _ = w_pool.shape
    pw = groups * gw

    def body(p_ref, w_ref, s_ref, y_ref):
        for g in range(groups):
            cols = slice(g * gw, (g + 1) * gw)
            y_ref[:, cols] = _pool_group(p_ref[:, cols], w_ref[g], s_ref[:, cols], POOL_WINDOWS[g]).astype(BF16)

    return pl.pallas_call(
        body, name="pool_fwd", grid=(1,),
        in_specs=[pl.BlockSpec((seq, pw), lambda i: (0, col_block)),
                  pl.BlockSpec((groups, gw, gw), lambda i: (0, 0, 0)), pl.BlockSpec((1, pw), lambda i: (0, 0))],
        out_specs=pl.BlockSpec((seq, pw), lambda i: (0, 0)), out_shape=jax.ShapeDtypeStruct((seq, pw), BF16),
        compiler_params=_params(("arbitrary",)))(z, w_pool, s_pool)


def _pool_bwd(z, dy, w_pool, s_pool, col_block, dy_block):
    seq = z.shape[0]
    groups, gw, _ = w_pool.shape
    pw = groups * gw

    def body(p_ref, dy_ref, w_ref, s_ref, dp_ref, dw_ref, ds_ref):
        for g in range(groups):
            cols = slice(g * gw, (g + 1) * gw)
            _, vjp = jax.vjp(functools.partial(_pool_group, window=POOL_WINDOWS[g]), p_ref[:, cols], w_ref[g],
                             s_ref[:, cols])
            dp, dw, ds = vjp(dy_ref[:, cols])
            dp_ref[:, cols] = dp.astype(BF16)
            dw_ref[g] = dw
            ds_ref[:, cols] = ds

    return pl.pallas_call(
        body, name="pool_bwd", grid=(1,),
        in_specs=[pl.BlockSpec((seq, pw), lambda i: (0, col_block)),
                  pl.BlockSpec((seq, pw), lambda i: (0, dy_block)),
                  pl.BlockSpec((groups, gw, gw), lambda i: (0, 0, 0)), pl.BlockSpec((1, pw), lambda i: (0, 0))],
        out_specs=[pl.BlockSpec((seq, pw), lambda i: (0, 0)), pl.BlockSpec((groups, gw, gw), lambda i: (0, 0, 0)),
                   pl.BlockSpec((1, pw), lambda i: (0, 0))],
        out_shape=[jax.ShapeDtypeStruct((seq, pw), BF16), jax.ShapeDtypeStruct(w_pool.shape, F32),
                   jax.ShapeDtypeStruct((1, pw), F32)],
        compiler_params=_params(("arbitrary",)))(z, dy, w_pool, s_pool)


def _conv_fwd(z, w_dw, taps, b_dw, val_block, gate_block, after=()):
    seq = z.shape[0]
    rows, cb = w_dw.shape[1], w_dw.shape[2]
    after = list(after)

    def body(val_ref, gate_ref, w_ref, b_ref, *refs):
        out_ref = refs[-1]
        h = val_ref[...] * _sigmoid(gate_ref[...])
        acc = jnp.broadcast_to(b_ref[...], h.shape)
        for d in range(taps):
            acc = acc + w_ref[pl.ds(taps - 1 - d, 1), :] * _shift(h, d)
        out_ref[...] = acc

    return pl.pallas_call(
        body, name="conv_fwd", grid=(N_CHIPS,),
        in_specs=[pl.BlockSpec((seq, cb), lambda j: (0, val_block + j)),
                  pl.BlockSpec((seq, cb), lambda j: (0, gate_block + j)),
                  pl.BlockSpec((None, rows, cb), lambda j: (j, 0, 0)),
                  pl.BlockSpec((1, cb), lambda j: (0, j))] + [ANY] * len(after),
        out_specs=pl.BlockSpec((seq, cb), lambda j: (0, j)),
        out_shape=jax.ShapeDtypeStruct((seq, N_CHIPS * cb), F32),
        compiler_params=_params(("parallel",)))(z, z, w_dw, b_dw, *after)


def _conv_bwd(z, dout, w_dw, taps, val_block, gate_block):
    seq = z.shape[0]
    rows, cb = w_dw.shape[1], w_dw.shape[2]
    col = pl.BlockSpec((seq, cb), lambda j: (0, j))

    def body(val_ref, gate_ref, do_ref, w_ref, dval_ref, dgate_ref, dw_ref, db_ref):
        val, sg, do = val_ref[...], _sigmoid(gate_ref[...]), do_ref[...]
        h = val * sg
        db_ref[...] = jnp.sum(do, axis=0, keepdims=True)
        dh = jnp.zeros_like(h)
        for d in range(taps):
            k = taps - 1 - d
            dw_ref[pl.ds(k, 1), :] = jnp.sum(do * _shift(h, d), axis=0, keepdims=True)
            dh = dh + w_ref[pl.ds(k, 1), :] * _shift(do, -d)
        dval_ref[...] = (dh * sg).astype(BF16)
        dgate_ref[...] = (dh * val * sg * (1.0 - sg)).astype(BF16)

    return pl.pallas_call(
        body, name="conv_bwd", grid=(N_CHIPS,),
        in_specs=[pl.BlockSpec((seq, cb), lambda j: (0, val_block + j)),
                  pl.BlockSpec((seq, cb), lambda j: (0, gate_block + j)), col,
                  pl.BlockSpec((None, rows, cb), lambda j: (j, 0, 0))],
        out_specs=[col, col, pl.BlockSpec((taps, cb), lambda j: (0, j)), pl.BlockSpec((1, cb), lambda j: (0, j))],
        out_shape=[jax.ShapeDtypeStruct((seq, N_CHIPS * cb), BF16), jax.ShapeDtypeStruct((seq, N_CHIPS * cb), BF16),
                   jax.ShapeDtypeStruct((taps, N_CHIPS * cb), F32), jax.ShapeDtypeStruct((1, N_CHIPS * cb), F32)],
        compiler_params=_params(("parallel",)))(z, z, dout, w_dw)


def _ln_swish(hc, g, b):
    y = _layer_norm(hc, g, b)
    return y * _sigmoid(y)


def _ln_swish_fwd(hc, g, b):
    s, cw = hc.shape
    tr = _tile(s, ROW_TILE)
    row = pl.BlockSpec((tr, cw), lambda i: (i, 0))
    vec = pl.BlockSpec((1, cw), lambda i: (0, 0))

    def body(h_ref, g_ref, b_ref, y_ref):
        y_ref[...] = _ln_swish(h_ref[...], g_ref[...], b_ref[...]).astype(BF16)

    return pl.pallas_call(body, name="ln_swish_fwd", grid=(s // tr,), in_specs=[row, vec, vec], out_specs=row,
                          out_shape=jax.ShapeDtypeStruct((s, cw), BF16),
                          compiler_params=_params(("parallel",)))(hc, g, b)


def _ln_swish_bwd(hc, dy, g, b, dy_block):
    s, cw = hc.shape
    tr = _tile(s, ROW_TILE)
    row = pl.BlockSpec((tr, cw), lambda i: (i, 0))
    vec = pl.BlockSpec((1, cw), lambda i: (0, 0))

    def body(h_ref, dy_ref, g_ref, b_ref, dh_ref, dg_ref, db_ref):
        _, vjp = jax.vjp(_ln_swish, h_ref[...], g_ref[...], b_ref[...])
        dh, dg, db = vjp(dy_ref[...])
        dh_ref[...] = dh

        @pl.when(pl.program_id(0) == 0)
        def _():
            dg_ref[...] = dg
            db_ref[...] = db

        @pl.when(pl.program_id(0) != 0)
        def _():
            dg_ref[...] += dg
            db_ref[...] += db

    return pl.pallas_call(
        body, name="ln_swish_bwd", grid=(s // tr,),
        in_specs=[row, pl.BlockSpec((tr, cw), lambda i: (i, dy_block)), vec, vec], out_specs=[row, vec, vec],
        out_shape=[jax.ShapeDtypeStruct((s, cw), F32), jax.ShapeDtypeStruct((1, cw), F32),
                   jax.ShapeDtypeStruct((1, cw), F32)],
        compiler_params=_params(("arbitrary",)))(hc, dy, g, b)


def _attn_probs(q, k, scale):
    s = lax.dot_general(q, k, NT, preferred_element_type=F32) * scale
    e = jnp.exp(s - jnp.max(s, axis=-1, keepdims=True))
    return e / jnp.sum(e, axis=-1, keepdims=True)


def _attn_fwd(q, k, v):
    seq, d = q.shape
    mem = k.shape[0]
    hd = d // XATTN_HEADS
    scale = hd ** -0.5
    qs = pl.BlockSpec((seq, hd), lambda h: (0, h))
    ms = pl.BlockSpec((mem, hd), lambda h: (0, h))

    def body(q_ref, k_ref, v_ref, a_ref):
        p = _attn_probs(q_ref[...], k_ref[...], scale)
        a_ref[...] = jnp.dot(p.astype(BF16), v_ref[...], preferred_element_type=F32).astype(BF16)

    return pl.pallas_call(body, name="attn_fwd", grid=(XATTN_HEADS,), in_specs=[qs, ms, ms], out_specs=qs,
                          out_shape=jax.ShapeDtypeStruct((seq, d), BF16),
                          compiler_params=_params(("parallel",)))(q, k, v)


def _attn_bwd(q, k, v, da):
    seq, d = q.shape
    mem = k.shape[0]
    hd = d // XATTN_HEADS
    scale = hd ** -0.5
    qs = pl.BlockSpec((seq, hd), lambda h: (0, h))
    ms = pl.BlockSpec((mem, hd), lambda h: (0, h))

    def body(q_ref, k_ref, v_ref, da_ref, dq_ref, dk_ref, dv_ref):
        q_, k_, v_, da_ = q_ref[...], k_ref[...], v_ref[...], da_ref[...]
        p = _attn_probs(q_, k_, scale)
        dv_ref[...] = lax.dot_general(p.astype(BF16), da_, TN, preferred_element_type=F32).astype(BF16)
        dp = lax.dot_general(da_, v_, NT, preferred_element_type=F32)
        ds = (p * (dp - jnp.sum(dp * p, axis=-1, keepdims=True)) * scale).astype(BF16)
        dq_ref[...] = jnp.dot(ds, k_, preferred_element_type=F32).astype(BF16)
        dk_ref[...] = lax.dot_general(ds, q_, TN, preferred_element_type=F32).astype(BF16)

    return pl.pallas_call(
        body, name="attn_bwd", grid=(XATTN_HEADS,), in_specs=[qs, ms, ms, qs], out_specs=[qs, ms, ms],
        out_shape=[jax.ShapeDtypeStruct((seq, d), BF16), jax.ShapeDtypeStruct((mem, d), BF16),
                   jax.ShapeDtypeStruct((mem, d), BF16)],
        compiler_params=_params(("parallel",)))(q, k, v, da)


def _place_shard(name, place, w, l, dtype, after=()):
    _, r, c = w.shape
    tr = _tile(r, 2 * ROW_TILE) if r % 16 == 0 else r
    after = list(after)

    def body(place_ref, w_ref, *refs):
        refs[-1][...] = w_ref[...].astype(dtype)

    return pl.pallas_call(
        body, name=name,
        grid_spec=pltpu.PrefetchScalarGridSpec(
            num_scalar_prefetch=1, grid=(r // tr,),
            in_specs=[pl.BlockSpec((None, tr, c), lambda i, p: (l, i, 0))] + [ANY] * len(after),
            out_specs=pl.BlockSpec((None, tr, c), lambda i, p: (p[1], i, 0))),
        out_shape=jax.ShapeDtypeStruct((N_CHIPS, r, c), dtype),
        compiler_params=_params(("parallel",)))(place, w, *after)


def _place_flat(place, flat):
    rows, lanes = flat.shape

    def body(place_ref, f_ref, o_ref):
        o_ref[...] = f_ref[...]

    return pl.pallas_call(
        body, name="place_flat",
        grid_spec=pltpu.PrefetchScalarGridSpec(
            num_scalar_prefetch=1, grid=(1,), in_specs=[pl.BlockSpec((rows, lanes), lambda i, p: (0, 0))],
            out_specs=pl.BlockSpec((None, rows, lanes), lambda i, p: (p[2], 0, 0))),
        out_shape=jax.ShapeDtypeStruct((N_DEVICES, rows, lanes), flat.dtype),
        compiler_params=_params(("arbitrary",)))(place, flat)


def _pair_sum(name, place, dw, got):
    n, _, r2, c = dw.shape
    tr = _tile(r2, 4 * ROW_TILE)

    def body(place_ref, own_ref, got_ref, s_ref, t_ref):
        val = (own_ref[...].astype(F32) + got_ref[...].astype(F32)).astype(BF16)
        s_ref[...] = val

        @pl.when(pl.program_id(1) == place_ref[1])
        def _():
            t_ref[...] = val

    slab = pl.BlockSpec((None, tr, c), lambda i, j, p: (j, i, 0))
    sds = jax.ShapeDtypeStruct((n, r2, c), BF16)
    return pl.pallas_call(
        body, name=name,
        grid_spec=pltpu.PrefetchScalarGridSpec(
            num_scalar_prefetch=1, grid=(r2 // tr, n),
            in_specs=[pl.BlockSpec((None, None, tr, c), lambda i, j, p: (j, p[0], i, 0)), slab],
            out_specs=[slab, pl.BlockSpec((None, tr, c), lambda i, j, p: (p[1], i, 0))]),
        out_shape=[sds, sds], compiler_params=_params(("parallel", "arbitrary")))(place, dw, got)


def _chip_sum(name, place, parts):
    n, r2, c = parts.shape
    tr = _tile(r2, ROW_TILE)

    def body(place_ref, *refs):
        o_ref = refs[n]
        acc = refs[0][...].astype(F32)
        for j in range(1, n):
            acc = acc + refs[j][...].astype(F32)
        o_ref[...] = acc

    part = lambda j: pl.BlockSpec((None, tr, c), lambda i, p: (j, i, 0))
    return pl.pallas_call(
        body, name=name,
        grid_spec=pltpu.PrefetchScalarGridSpec(
            num_scalar_prefetch=1, grid=(r2 // tr,), in_specs=[part(j) for j in range(n)],
            out_specs=pl.BlockSpec((None, tr, c), lambda i, p: (p[0], i, 0))),
        out_shape=jax.ShapeDtypeStruct((2, r2, c), F32),
        compiler_params=_params(("parallel",)))(place, *([parts] * n))


def _device_sum(parts):
    n, rows, lanes = parts.shape
    tr = _tile(rows, 4 * ROW_TILE) if rows % 8 == 0 else rows

    def body(p_ref, o_ref):
        acc = p_ref[0]
        for j in range(1, n):
            acc = acc + p_ref[j]
        o_ref[...] = acc

    return pl.pallas_call(
        body, name="device_sum", grid=(rows // tr,), in_specs=[pl.BlockSpec((n, tr, lanes), lambda i: (0, i, 0))],
        out_specs=pl.BlockSpec((tr, lanes), lambda i: (i, 0)), out_shape=jax.ShapeDtypeStruct((rows, lanes), F32),
        compiler_params=_params(("parallel",)))(parts)


def _adam_update(w, g, m, v):
    nm = ADAM_B1 * m + (1.0 - ADAM_B1) * g
    nv = ADAM_B2 * v + (1.0 - ADAM_B2) * (g * g)
    c1 = 1.0 - ADAM_B1 ** ADAM_STEP
    c2 = 1.0 - ADAM_B2 ** ADAM_STEP
    return -ADAM_LR * ((nm / c1) / (jnp.sqrt(nv / c2) + ADAM_EPS) + ADAM_WD * w), nm, nv


def _adamw_layer(name, l, w, g, m, v, prev):
    n_l, r, c = w.shape
    tr = _tile(r, ROW_TILE)
    slab = pl.BlockSpec((None, tr, c), lambda i: (l, i, 0))

    def body(w_ref, g_ref, m_ref, v_ref, *refs):
        go_ref, d_ref, nm_ref, nv_ref = refs[-4:]
        g_ = g_ref[...]
        delta, nm, nv = _adam_update(w_ref[...], g_, m_ref[...], v_ref[...])
        go_ref[...] = g_
        d_ref[...] = delta
        nm_ref[...] = nm
        nv_ref[...] = nv

    ins = [w, g, m, v]
    in_specs = [slab, pl.BlockSpec((tr, c), lambda i: (i, 0)), slab, slab]
    aliases = {}
    if prev is not None:
        aliases = {len(ins) + i: i for i in range(4)}
        ins += list(prev)
        in_specs += [ANY] * 4
    sds = jax.ShapeDtypeStruct((n_l, r, c), F32)
    return pl.pallas_call(body, name=name, grid=(r // tr,), in_specs=in_specs, out_specs=[slab] * 4,
                          out_shape=[sds] * 4, input_output_aliases=aliases,
                          compiler_params=_params(("parallel",)))(*ins)


def _adamw_flat(name, w, g, m, v):
    rows, cols = w.shape
    tr = _tile(rows, ROW_TILE) if rows % 8 == 0 else rows
    spec = pl.BlockSpec((tr, cols), lambda i: (i, 0))

    def body(w_ref, g_ref, m_ref, v_ref, d_ref, nm_ref, nv_ref):
        d_ref[...], nm_ref[...], nv_ref[...] = _adam_update(w_ref[...], g_ref[...], m_ref[...], v_ref[...])

    sds = jax.ShapeDtypeStruct((rows, cols), F32)
    return pl.pallas_call(body, name=name, grid=(rows // tr,), in_specs=[spec] * 4, out_specs=[spec] * 3,
                          out_shape=[sds] * 3, compiler_params=_params(("parallel",)))(w, g, m, v)


def _me():
    return lax.axis_index("x"), lax.axis_index("y"), lax.axis_index("c")


def _other_chips(x, y):
    return [(1 - x, y, 2 * (1 - x) + y), (x, 1 - y, 2 * x + 1 - y), (1 - x, 1 - y, 2 * (1 - x) + 1 - y)]


def _remote(src, dst, send_sem, recv_sem, target):
    return pltpu.make_async_remote_copy(src_ref=src, dst_ref=dst, send_sem=send_sem, recv_sem=recv_sem,
                                        device_id=target, device_id_type=MESH)


def _exchange(name, bufs, plan, n_copies):
    n = len(bufs)

    def body(*refs):
        send_sems, recv_sems = refs[2 * n:]
        copies = []
        for i, (src, dst, target) in enumerate(plan(refs[n:2 * n], _me())):
            if target is None:
                cp = pltpu.make_async_copy(src, dst, send_sems.at[i])
            else:
                cp = _remote(src, dst, send_sems.at[i], recv_sems.at[i], target)
            cp.start()
            copies.append((cp, target))
        assert len(copies) == n_copies
        for cp, target in copies:
            if target is None:
                cp.wait()
            else:
                cp.wait_recv()
        for cp, target in copies:
            if target is not None:
                cp.wait_send()

    return pl.pallas_call(
        body, name=name, in_specs=[ANY] * n, out_specs=[ANY] * n,
        out_shape=[jax.ShapeDtypeStruct(b.shape, b.dtype) for b in bufs],
        scratch_shapes=[pltpu.SemaphoreType.DMA((n_copies,)), pltpu.SemaphoreType.DMA((n_copies,))],
        input_output_aliases={i: i for i in range(n)},
        compiler_params=pltpu.CompilerParams(has_side_effects=True))(*bufs)


def _start_copies(name, groups, after=(), source_token=False):
    all_bufs = [b for bufs, _, _ in groups for b in bufs]
    after = list(after)
    n = len(all_bufs)
    n_g = len(groups)

    def body(*refs):
        in_refs, sem_refs = refs[:n], refs[n + len(after):n + len(after) + 2 * n_g]
        if not source_token:
            refs[-1][...] = jnp.zeros_like(refs[-1])
        pos = 0
        for g, (bufs, plan, n_copies) in enumerate(groups):
            copies = plan(in_refs[pos:pos + len(bufs)], _me())
            assert len(copies) == n_copies
            for i, (src, dst, target) in enumerate(copies):
                _remote(src, dst, sem_refs[2 * g].at[i], sem_refs[2 * g + 1].at[i], target).start()
            pos += len(bufs)

    sems = []
    for _, _, n_copies in groups:
        sems += [pltpu.SemaphoreType.DMA((n_copies,))] * 2
    outs = pl.pallas_call(
        body, name=name, in_specs=[HBM] * n + [ANY] * len(after),
        out_specs=[SEM] * (2 * n_g) + [HBM] * n + [pl.BlockSpec(memory_space=pltpu.VMEM)] * (not source_token),
        out_shape=sems + [pltpu.HBM(b.shape, b.dtype) for b in all_bufs] + [
            jax.ShapeDtypeStruct((8, V7X_LANES), F32)] * (not source_token),
        input_output_aliases={i: 2 * n_g + i for i in range(n)},
        compiler_params=pltpu.CompilerParams(has_side_effects=pltpu.SideEffectType.DATAFLOW_SIDE_EFFECTING))(
            *[pltpu.with_memory_space_constraint(b, pltpu.HBM) for b in all_bufs], *after)
    result, pos = [], 2 * n_g
    for g, (bufs, _, _) in enumerate(groups):
        passed = list(outs[pos:pos + len(bufs)])
        result.append((outs[2 * g], outs[2 * g + 1], passed, passed[0] if source_token else outs[-1]))
        pos += len(bufs)
    return result


def _wait_copies(name, started, plan, n_copies, after):
    send_sems, recv_sems, bufs = started[:3]
    n = len(bufs)
    after = list(after) if isinstance(after, (list, tuple)) else [after]
    after = [a for a in after if all(a is not b for b in bufs)]

    def body(*refs):
        copies = plan(refs[:n], _me())
        assert len(copies) == n_copies
        for i, (src, dst, target) in enumerate(copies):
            cp = _remote(src, dst, refs[n].at[i], refs[n + 1].at[i], target)
            cp.wait_send()
            cp.wait_recv()

    return list(pl.pallas_call(
        body, name=name, in_specs=[HBM] * n + [SEM, SEM] + [ANY] * len(after), out_specs=[HBM] * n,
        out_shape=[pltpu.HBM(b.shape, b.dtype) for b in bufs], input_output_aliases={i: i for i in range(n)},
        compiler_params=pltpu.CompilerParams(has_side_effects=pltpu.SideEffectType.DATAFLOW_SIDE_EFFECTING))(
            *bufs, send_sems, recv_sems, *after))


def _halves(a):
    return a.reshape(a.shape[0], 2, a.shape[1] // 2, a.shape[2])


def _neighbour_slab(x, y, c):
    across_x, across_y = 2 * (1 - x) + y, 2 * x + 1 - y
    return across_x + c * (across_y - across_x)


def _direct_plan(refs, me):
    x, y, c = me
    mine = 2 * x + y
    target = (x + (1 - c) * (1 - 2 * x), y + c * (1 - 2 * y), c)
    return [(g.at[mine], g.at[mine], target) for g in refs]


def _relay_plan(refs, me):
    x, y, c = me
    src = _neighbour_slab(x, y, c)
    onward = (x + c * (1 - 2 * x), y + (1 - c) * (1 - 2 * y), c)
    copies = []
    for g in refs:
        copies += [(g.at[src, c], g.at[src, c], onward), (g.at[src], g.at[src], (x, y, 1 - c))]
    return copies


def _last_plan(refs, me):
    x, y, c = me
    far = 2 * (1 - x) + 1 - y
    return [(g.at[far, c], g.at[far, c], (x, y, 1 - c)) for g in refs]


def _swap_plan(refs, me):
    x, y, c = me
    k = len(refs) // 2
    return [(refs[i].at[j, 1 - c], refs[k + i].at[j], (x, y, 1 - c)) for i in range(k) for j in range(N_CHIPS)]


def _scatter_plan(refs, me):
    x, y, c = me
    mine = 2 * x + y
    k = len(refs) // 2
    return [(refs[i].at[chip], refs[k + i].at[mine], (px, py, c))
            for i in range(k) for px, py, chip in _other_chips(x, y)]


def _share_plan(refs, me):
    x, y, c = me
    return [(g.at[c], g.at[c], (x, y, 1 - c)) for g in refs]


def _broadcast_plan(refs, me):
    x, y, c = me
    mine = 4 * x + 2 * y + c
    copies = []
    for fx, fy, fc in [(0, 0, 1), (0, 1, 0), (0, 1, 1), (1, 0, 0), (1, 0, 1), (1, 1, 0), (1, 1, 1)]:
        peer = (x + fx - 2 * fx * x, y + fy - 2 * fy * y, c + fc - 2 * fc * c)
        copies.append((refs[0].at[mine], refs[0].at[mine], peer))
    return copies


BIG = ("w_in", "w_out", "w_q", "w_k", "w_v", "w_o", "w_up", "w_down")
COLUMN_SPLIT = ("w_in", "w_up")
WEIGHTS = ("norm_mix_pre", "norm_mix_post", "w_in", "w_out", "gmlp_v_gain", "w_spatial", "b_spatial", "w_pool",
           "s_pool", "w_dw", "b_dw", "conv_ln_g", "conv_ln_b", "norm_xattn_pre", "norm_mem", "norm_xattn_post",
           "w_q", "w_k", "w_v", "w_o", "norm_ffn_pre", "norm_ffn_post", "w_up", "w_down")
SMALL = tuple(n for n in WEIGHTS if n not in BIG)
REPLICATED = tuple(n for n in SMALL if n != "w_dw")
GATHER_GROUPS = (("w_in", "w_dw"), ("w_out", "w_q", "w_k", "w_v", "w_o"), ("w_up",), ("w_down",))


def _relu2(acc):
    r = jnp.maximum(acc, 0.0)
    return acc, r * r


def _relu2_bwd(acc, up):
    return (acc * (2.0 * jnp.maximum(up, 0.0)),)


def _pack(arrays):
    flat = jnp.concatenate([a.reshape(-1) for a in arrays])
    tile = 8 * V7X_LANES
    pad = (-flat.shape[0]) % tile
    return jnp.pad(flat, (0, pad)).reshape(-1, V7X_LANES)


def _pack_layers(arrays):
    n_l = arrays[0].shape[0]
    flat = jnp.concatenate([a.reshape(n_l, -1) for a in arrays], axis=1)
    pad = (-flat.shape[1]) % (8 * V7X_LANES)
    return jnp.pad(flat, ((0, 0), (0, pad))).reshape(n_l, -1, V7X_LANES)


def _unpack_layers(packed, like):
    flat = packed.reshape(packed.shape[0], -1)
    out, pos = [], 0
    for a in like:
        out.append(flat[:, pos:pos + a[0].size].reshape(a.shape))
        pos += a[0].size
    return out


class _GradientReducer:
    def __init__(self, place, w, m, v):
        self.place, self.w, self.m, self.v = place, w, m, v
        self.flying = []
        self.done = {n: None for n in BIG}

    def add(self, tag, l, grads):
        names = list(grads)
        views = [_halves(grads[n]) for n in names]
        zones = [lax.empty((v.shape[0],) + v.shape[2:], v.dtype) for v in views]
        started = _start_copies("swap_start_" + tag, [(views + zones, _swap_plan, N_CHIPS * len(names))],
                                source_token=True)[0]
        self.flying.append(dict(stage=0, tag=tag, l=l, names=names, started=started))
        return [started[3]]

    def advance(self, after):
        made = []
        after = list(after) if isinstance(after, (list, tuple)) else [after]
        for item in self.flying:
            item["stage"] += 1
        for item in self.flying:
            tag, names, k = item["tag"], item["names"], len(item["names"])
            if item["stage"] == 1:
                bufs = _wait_copies("swap_wait_" + tag, item["started"], _swap_plan, N_CHIPS * k, after)
                sums, parts = zip(*[_pair_sum("pair_sum_" + n, self.place, dv, got)
                                    for n, dv, got in zip(names, bufs[:k], bufs[k:])])
                item["started"] = _start_copies("scatter_start_" + tag,
                                                [(list(sums) + list(parts), _scatter_plan, 3 * k)],
                                                source_token=True)[0]
                made.append(item["started"][3])
        after = after + made
        for item in list(self.flying):
            tag, names, k = item["tag"], item["names"], len(item["names"])
            if item["stage"] == 4:
                halves = _wait_copies("share_wait_" + tag, item["started"], _share_plan, k, after)
                for n, h in zip(names, halves):
                    g = h.reshape(self.w[n].shape[1:])
                    self.done[n] = _adamw_layer("adamw_" + n, item["l"], self.w[n], g, self.m[n], self.v[n],
                                                self.done[n])
                    made.append(self.done[n][3])
                self.flying.remove(item)
        after = after + [a for a in made if all(a is not b for b in after)]
        for item in self.flying:
            tag, names, k = item["tag"], item["names"], len(item["names"])
            if item["stage"] == 3:
                bufs = _wait_copies("scatter_wait_" + tag, item["started"], _scatter_plan, 3 * k, after)
                halves = [_chip_sum("chip_sum_" + n, self.place, p) for n, p in zip(names, bufs[k:])]
                item["started"] = _start_copies("share_start_" + tag, [(halves, _share_plan, k)])[0]
                made.append(item["started"][3])
        return made

    def drain(self, after):
        made = list(after)
        while self.flying:
            made = list(after) + self.advance(made)
        return made


def _step(x, mem, target, w, m, v):
    n_layers = w["w_in"].shape[0]
    seq, d = x.shape
    heads, hd = w["gmlp_v_gain"].shape[1:]
    gw = heads * hd
    groups, pgw = w["w_pool"].shape[1:3]
    pw = groups * pgw
    cw = w["b_dw"].shape[1]
    cb = cw // N_CHIPS
    taps = w["w_dw"].shape[1]
    cx, cy, cc = _me()
    chip = 2 * cx + cy
    place = jnp.stack([cc, chip, 2 * chip + cc]).astype(jnp.int32)
    vec = lambda name, l: w[name][l].reshape(1, -1)

    taps_padded = jnp.pad(w["w_dw"], ((0, 0), (0, (-taps) % 16), (0, 0)))
    gathering = []

    def cast_group(l, names, last):
        return [_halves(_place_shard("place_" + n, place, taps_padded if n == "w_dw" else w[n], l,
                                     F32 if n == "w_dw" else BF16, after=last)) for n in names]

    def send_layer(l, last, cast=None):
        for g, names in enumerate(GATHER_GROUPS):
            bufs = cast[g] if cast else cast_group(l, names, last)
            gathering.extend(_start_copies("gather_start_%d%d" % (l, g), [(bufs, _direct_plan, len(names))],
                                           after=last))
            last = [gathering[-1][3]]
        return last

    def relay(l, g, after):
        names = GATHER_GROUPS[g]
        at = l * len(GATHER_GROUPS) + g
        bufs = _wait_copies("gather_wait_%d%d" % (l, g), gathering[at], _direct_plan, len(names), after)
        gathering[at] = _start_copies("relay_start_%d%d" % (l, g), [(bufs, _relay_plan, 2 * len(names))])[0]
        return [gathering[at][3]]

    def arrive(l, g, after):
        names = GATHER_GROUPS[g]
        tag = "%d%d" % (l, g)
        bufs = _wait_copies("relay_wait_" + tag, gathering[l * len(GATHER_GROUPS) + g], _relay_plan,
                            2 * len(names), after)
        bufs = _exchange("gather_last_" + tag, bufs, _last_plan, len(names))
        out = {}
        for n, b in zip(names, bufs):
            full = b.reshape(N_CHIPS, 2 * b.shape[2], b.shape[3])
            out[n] = full if n in COLUMN_SPLIT + ("w_dw",) else full.reshape(-1, full.shape[2])
        return out

    saved = []
    passing = relay(0, 0, send_layer(0, []))
    cast_ahead = [cast_group(1, names, passing) for names in GATHER_GROUPS] if n_layers > 1 else None
    _, h1 = _norm_fwd("norm_first", x, None, None, vec("norm_mix_pre", 0),
                      after=[slab for group in cast_ahead for slab in group] if cast_ahead else passing)
    for l in range(n_layers):
        gv = w["gmlp_v_gain"][l].reshape(heads, 1, hd)
        ws = w["w_spatial"][l]
        bcol = w["b_spatial"][l].reshape(heads, -1, 1)
        wl = arrive(l, 0, [h1])
        early = relay(l, 1, [h1]) if l > 0 else []
        z = _mm_nn_col("mm_in", h1, wl["w_in"], [F32], after=early)[0]
        sent = send_layer(l + 1, [z], cast_ahead if l == 0 else None) if l + 1 < n_layers else []
        ya = _gmlp_fwd(z, gv, ws, bcol)
        yb = _pool_fwd(z, w["w_pool"][l], vec("s_pool", l), (2 * gw) // pw)
        hc = _conv_fwd(z, wl["w_dw"], taps, vec("b_dw", l), (2 * gw + pw) // cb, (2 * gw + pw + cw) // cb,
                       after=relay(l, 1, [ya]) if l == 0 else [])
        yc = _ln_swish_fwd(hc, vec("conv_ln_g", l), vec("conv_ln_b", l))
        y = jnp.concatenate([ya, yb, yc], axis=1)
        wl.update(arrive(l, 1, [y] + sent))
        o = _mm_nn_row("mm_out", y, wl["w_out"], [F32])[0]
        x1, h2 = _norm_fwd("norm_mix", x, o, vec("norm_mix_post", l), vec("norm_xattn_pre", l),
                           after=relay(l, 2, [o]))
        _, mn = _norm_fwd("norm_mem", mem, None, None, vec("norm_mem", l))
        q = _mm_nn_row("mm_q", h2, wl["w_q"], [BF16])[0]
        k = _mm_nn_row("mm_k", mn, wl["w_k"], [BF16])[0]
        vv = _mm_nn_row("mm_v", mn, wl["w_v"], [BF16])[0]
        a = _attn_fwd(q, k, vv)
        o2 = _mm_nn_row("mm_o", a, wl["w_o"], [F32], after=relay(l, 3, [a]))[0]
        x2, h3 = _norm_fwd("norm_xattn", x1, o2, vec("norm_xattn_post", l), vec("norm_ffn_pre", l))
        wl.update(arrive(l, 2, h3))
        up, r = _mm_nn_col("mm_up", h3, wl["w_up"], [BF16, BF16], epi=_relu2)
        wl.update(arrive(l, 3, r))
        o3 = _mm_nn_row("mm_down", r, wl["w_down"], [F32],
                        after=relay(l + 1, 0, [r]) if l + 1 < n_layers else [])[0]
        g_next = vec("norm_mix_pre", l + 1) if l + 1 < n_layers else None
        x3, h_next = _norm_fwd("norm_ffn", x2, o3, vec("norm_ffn_post", l), g_next)
        saved.append(dict(x=x, h1=h1, z=z, hc=hc, y=y, o=o, x1=x1, h2=h2, mn=mn, q=q, k=k, v=vv, a=a, o2=o2, x2=x2,
                          h3=h3, up=up, r=r, o3=o3, x3=x3, gv=gv, ws=ws, bcol=bcol, w=wl))
        x, h1 = x3, h_next

    dx, loss_parts = _loss_head(x, target)
    loss = lax.psum(jnp.sum(loss_parts), ("x", "y", "c"))

    reducer = _GradientReducer(place, w, m, v)
    small = {n: [None] * n_layers for n in SMALL}
    by_chip = lambda g: g.reshape(N_CHIPS, g.shape[0] // N_CHIPS, g.shape[1])
    small_sent = [None] * n_layers

    def small_layer(l):
        return [small[n][l].reshape(w[n].shape[1:]) for n in REPLICATED] + [small["w_dw"][l]]

    def send_small(l):
        landing = _place_flat(place, _pack(small_layer(l)))
        small_sent[l] = _start_copies("small_start_%d" % l, [([landing], _broadcast_plan, N_DEVICES - 1)])[0]
        return [small_sent[l][3]]

    dh = None
    made = []
    for l in reversed(range(n_layers)):
        t = saved[l]
        wl = t["w"]
        g_next = vec("norm_mix_pre", l + 1) if l + 1 < n_layers else None
        dx, do3, dgp, dgn = _norm_bwd("norm_ffn_bwd", dx, dh, t["x3"], t["o3"], vec("norm_ffn_post", l), g_next,
                                      after=made)
        small["norm_ffn_post"][l] = dgp
        if dgn is not None:
            small["norm_mix_pre"][l + 1] = dgn
        made = reducer.advance(dx)
        if l + 1 < n_layers:
            made += send_small(l + 1)
        d_down = _mm_tn_row("mm_down_dw", t["r"], do3)
        made += reducer.add("%d0" % l, l, {"w_down": by_chip(d_down)})
        dup = _mm_nt_row("mm_down_dx", do3, wl["w_down"], BF16, epi=_relu2_bwd, extra=t["up"], after=made)
        d_up = _mm_tn_col("mm_up_dw", t["h3"], dup)
        made = reducer.add("%d3" % l, l, {"w_up": d_up})
        dh3 = _mm_nt_col("mm_up_dx", dup, wl["w_up"], F32, after=made)
        made = []
        dx, do2, dgp, dgn = _norm_bwd("norm_xattn_bwd", dx, dh3, t["x2"], t["o2"], vec("norm_xattn_post", l),
                                      vec("norm_ffn_pre", l), after=made)
        small["norm_xattn_post"][l], small["norm_ffn_pre"][l] = dgp, dgn
        made = reducer.advance(dx)
        d_o = _mm_tn_row("mm_o_dw", t["a"], do2)
        da = _mm_nt_row("mm_o_dx", do2, wl["w_o"], BF16, after=made)
        dq, dk, dv = _attn_bwd(t["q"], t["k"], t["v"], da)
        d_q = _mm_tn_row("mm_q_dw", t["h2"], dq)
        d_k = _mm_tn_row("mm_k_dw", t["mn"], dk)
        d_v = _mm_tn_row("mm_v_dw", t["mn"], dv)
        dh2 = _mm_nt_row("mm_q_dx", dq, wl["w_q"], F32)
        dmn = _mm_nt_row("mm_k_dx", dk, wl["w_k"], F32) + _mm_nt_row("mm_v_dx", dv, wl["w_v"], F32)
        _, _, _, small["norm_mem"][l] = _norm_bwd("norm_mem_bwd", None, dmn, mem, None, None, vec("norm_mem", l))
        dx, do, dgp, dgn = _norm_bwd("norm_mix_bwd", dx, dh2, t["x1"], t["o"], vec("norm_mix_post", l),
                                     vec("norm_xattn_pre", l))
        small["norm_mix_post"][l], small["norm_xattn_pre"][l] = dgp, dgn
        made = reducer.advance(dx)
        d_out = _mm_tn_row("mm_out_dw", t["y"], do)
        made += reducer.add("%d1" % l, l, {"w_o": by_chip(d_o), "w_q": by_chip(d_q), "w_k": by_chip(d_k),
                                    "w_v": by_chip(d_v), "w_out": by_chip(d_out)})
        dy = _mm_nt_row("mm_out_dx", do, wl["w_out"], F32, after=made)
        dzu, dzv, dgv, dws, dbcol = _gmlp_bwd(t["z"], dy, t["gv"], t["ws"], t["bcol"])
        small["gmlp_v_gain"][l] = dgv.reshape(heads, hd)
        small["w_spatial"][l] = dws
        small["b_spatial"][l] = dbcol.reshape(heads, -1)
        dzp, dwp, dsp = _pool_bwd(t["z"], dy, w["w_pool"][l], vec("s_pool", l), (2 * gw) // pw, gw // pw)
        small["w_pool"][l], small["s_pool"][l] = dwp, dsp.reshape(-1)
        dhc, dlg, dlb = _ln_swish_bwd(t["hc"], dy, vec("conv_ln_g", l), vec("conv_ln_b", l), (gw + pw) // cw)
        small["conv_ln_g"][l], small["conv_ln_b"][l] = dlg.reshape(-1), dlb.reshape(-1)
        dval, dgate, dwd, dbd = _conv_bwd(t["z"], dhc, wl["w_dw"], taps, (2 * gw + pw) // cb,
                                          (2 * gw + pw + cw) // cb)
        small["w_dw"][l], small["b_dw"][l] = dwd, dbd.reshape(-1)
        dz = jnp.concatenate([dzu, dzv, dzp, dval, dgate], axis=1)
        made = reducer.advance(dz)
        d_in = _mm_tn_col("mm_in_dw", t["h1"], dz)
        made += reducer.add("%d2" % l, l, {"w_in": d_in})
        dh = _mm_nt_col("mm_in_dx", dz, wl["w_in"], F32, after=made)
        made = []
    grad_x, _, _, dgn = _norm_bwd("norm_first_bwd", dx, dh, saved[0]["x"], None, None, vec("norm_mix_pre", 0))
    small["norm_mix_pre"][0] = dgn
    drained = reducer.drain([grad_x] + send_small(0))

    like = [w[n] for n in REPLICATED]
    packed = [_pack_layers([p[n] for n in REPLICATED]) for p in (w, m, v)]
    updated, taps_grad = None, []
    for l in range(n_layers):
        landed = _wait_copies("small_wait_%d" % l, small_sent[l], _broadcast_plan, N_DEVICES - 1, drained)[0]
        total = _device_sum(landed)
        updated = _adamw_layer("adamw_small", l, packed[0], total, packed[1], packed[2], updated)
        start = sum(a[0].size for a in like)
        taps_grad.append(total.reshape(-1)[start:start + small["w_dw"][l].size].reshape(small["w_dw"][l].shape))
    grad, delta, new_m, new_v = (dict(zip(REPLICATED, _unpack_layers(u, like))) for u in updated)
    view = lambda a: a.reshape(-1, a.shape[-1])
    taps_grad = lax.dynamic_slice_in_dim(jnp.stack(taps_grad), chip * cb, cb, axis=2)
    updated = _adamw_flat("adamw_taps", view(w["w_dw"]), view(taps_grad), view(m["w_dw"]), view(v["w_dw"]))
    grad["w_dw"] = taps_grad
    delta["w_dw"], new_m["w_dw"], new_v["w_dw"] = (u.reshape(w["w_dw"].shape) for u in updated)
    for n in BIG:
        grad[n], delta[n], new_m[n], new_v[n] = reducer.done[n]

    return (loss, grad_x[None], *[grad[n] for n in WEIGHTS], *[delta[n] for n in WEIGHTS],
            *[new_m[n] for n in WEIGHTS], *[new_v[n] for n in WEIGHTS])


def kernel(x, mem, norm_mix_pre, norm_mix_post, w_in, w_out, gmlp_v_gain, w_spatial, b_spatial, w_pool, s_pool, w_dw, b_dw, conv_ln_g, conv_ln_b, norm_xattn_pre, norm_mem, norm_xattn_post, w_q, w_k, w_v, w_o, norm_ffn_pre, norm_ffn_post, w_up, w_down, loss_target, m_norm_mix_pre, m_norm_mix_post, m_w_in, m_w_out, m_gmlp_v_gain, m_w_spatial, m_b_spatial, m_w_pool, m_s_pool, m_w_dw, m_b_dw, m_conv_ln_g, m_conv_ln_b, m_norm_xattn_pre, m_norm_mem, m_norm_xattn_post, m_w_q, m_w_k, m_w_v, m_w_o, m_norm_ffn_pre, m_norm_ffn_post, m_w_up, m_w_down, v_norm_mix_pre, v_norm_mix_post, v_w_in, v_w_out, v_gmlp_v_gain, v_w_spatial, v_b_spatial, v_w_pool, v_s_pool, v_w_dw, v_b_dw, v_conv_ln_g, v_conv_ln_b, v_norm_xattn_pre, v_norm_mem, v_norm_xattn_post, v_w_q, v_w_k, v_w_v, v_w_o, v_norm_ffn_pre, v_norm_ffn_post, v_w_up, v_w_down):
    given = dict(locals())
    w = {n: given[n] for n in WEIGHTS}
    m = {n: given["m_" + n] for n in WEIGHTS}
    v = {n: given["v_" + n] for n in WEIGHTS}
    return _step(x[0], mem[0], loss_target[0], w, m, v)
```
